```python
import math
import jax
import jax.numpy as jnp
from jax import lax
import numpy as np

D_MODEL = 1024
BATCH = 8
SEQ = 4096
DEPTH = 2

GRID_W = 64
CTX_LEN = 256
N_MIXERS = 4
MIX_WIDTH = D_MODEL
GROUP_WIDTH = MIX_WIDTH // N_MIXERS
HEAD_DIM = 64
GROUP_HEADS = GROUP_WIDTH // HEAD_DIM

CM_CH = GROUP_WIDTH
CM_KERNEL = 31
DA_HEADS = GROUP_HEADS
DA_QK_DIM = HEAD_DIM // 2
DA_V_DIM = HEAD_DIM
ROPE_THETA = 10000.0
DN_HEADS = GROUP_HEADS
DN_K_DIM = HEAD_DIM
DN_V_DIM = HEAD_DIM
DN_SHORT_CONV = 5
DN_QKV = DN_HEADS * (2 * DN_K_DIM + DN_V_DIM)
GLA_HEADS = GROUP_HEADS
GLA_K_DIM = HEAD_DIM // 2
GLA_V_DIM = HEAD_DIM
GLA_GATE_RANK = 16
GLA_TAU = 16.0

CHUNK = 64
Q_BLOCK = 128
D_FF = ((8 * D_MODEL // 3 + 255) // 256) * 256
FFN_KERNEL = 3
EPS = 1e-6

CM_COLS = 2 * CM_CH
DA_COLS = 2 * DA_HEADS * 2 * DA_QK_DIM + DA_HEADS * DA_V_DIM
DN_COLS = DN_QKV + 4 * DN_HEADS + DN_HEADS * DN_V_DIM
GLA_COLS = GLA_HEADS * (2 * GLA_K_DIM + GLA_V_DIM) + 2 * GLA_GATE_RANK + GLA_HEADS * GLA_V_DIM
IN_COLS = CM_COLS + DA_COLS + DN_COLS + GLA_COLS
IN_SPLITS = [CM_COLS, CM_COLS + DA_COLS, CM_COLS + DA_COLS + DN_COLS]

kernel_name = 'hybrid_parallel_group_flow_block'

F32 = jnp.float32


def rms_norm(x, g):
    xf = x.astype(F32)
    y = xf * lax.rsqrt(jnp.mean(xf * xf, axis=-1, keepdims=True) + EPS)
    return (y * g.astype(F32)).astype(x.dtype)


def layer_norm(x, g, b):
    xf = x.astype(F32)
    mu = jnp.mean(xf, axis=-1, keepdims=True)
    var = jnp.mean(jnp.square(xf - mu), axis=-1, keepdims=True)
    y = (xf - mu) * lax.rsqrt(var + EPS)
    return (y * g.astype(F32) + b.astype(F32)).astype(x.dtype)


def l2_norm(x):
    xf = x.astype(F32)
    return xf * lax.rsqrt(jnp.sum(xf * xf, axis=-1, keepdims=True) + EPS)


def modulate(x, shift, scale):
    return x * (1 + scale[..., None, :]) + shift[..., None, :]


def dwconv(x, w):
    pad = w.shape[0] // 2
    return lax.conv_general_dilated(
        x, w[:, None, :].astype(x.dtype), (1,), [(pad, pad)],
        dimension_numbers=('NWC', 'WIO', 'NWC'), feature_group_count=x.shape[-1])


def axial_rope(n):
    rows = n // GRID_W
    row = jnp.repeat(jnp.arange(rows, dtype=F32), GRID_W)
    col = jnp.tile(jnp.arange(GRID_W, dtype=F32), rows)
    nf = DA_QK_DIM // 4
    inv = ROPE_THETA ** (-jnp.arange(nf, dtype=F32) / nf)
    ang = jnp.concatenate([row[:, None] * inv, col[:, None] * inv], axis=-1)
    return jnp.cos(ang), jnp.sin(ang)


def apply_rope(x, cos, sin):
    c = cos[:, None, None, :].astype(x.dtype)
    s = sin[:, None, None, :].astype(x.dtype)
    x1, x2 = jnp.split(x, 2, axis=-1)
    return jnp.concatenate([x1 * c - x2 * s, x1 * s + x2 * c], axis=-1)


def _to_heads(t, n_heads):
    b, l = t.shape[:2]
    return jnp.moveaxis(t.reshape(b, l, n_heads, -1), 2, 1)


def _chunks(t):
    n = t.shape[2] // CHUNK
    return t.reshape(t.shape[:2] + (n, CHUNK) + t.shape[3:])


def _gated_head_out(o, gate, g, n_heads):
    o = jnp.moveaxis(o, 1, 2)
    b, l = o.shape[:2]
    y = rms_norm(o, g) * jax.nn.silu(gate.reshape(b, l, n_heads, -1).astype(F32))
    return y.reshape(b, l, -1).astype(gate.dtype)


def gated_delta_chunked(q, k, v, g, beta, s0, with_out):
    q = _chunks(q * (q.shape[-1] ** -0.5))
    k, v, g, beta = _chunks(k), _chunks(v), _chunks(g), _chunks(beta)
    gcum = jnp.cumsum(g, axis=-1)
    causal = jnp.tril(jnp.ones((CHUNK, CHUNK), bool))
    strict = jnp.tril(jnp.ones((CHUNK, CHUNK), bool), -1)
    diff = gcum[..., :, None] - gcum[..., None, :]
    decay = jnp.where(causal, jnp.exp(jnp.where(causal, diff, 0.0)), 0.0)
    kb = k * beta[..., None]
    a = jnp.where(strict, jnp.einsum('bhnid,bhnjd->bhnij', kb, k) * decay, 0.0)
    eye = jnp.eye(CHUNK, dtype=F32)
    t_inv = lax.linalg.triangular_solve(eye + a, jnp.broadcast_to(eye, a.shape),
                                        left_side=True, lower=True, unit_diagonal=True)
    u = t_inv @ (v * beta[..., None])
    w = t_inv @ (kb * jnp.exp(gcum)[..., None])
    k_end = k * jnp.exp(gcum[..., -1:] - gcum)[..., None]
    g_end = jnp.exp(gcum[..., -1])
    xs = [u, w, k_end, g_end]
    if with_out:
        q_in = q * jnp.exp(gcum)[..., None]
        a_qk = jnp.where(causal, jnp.einsum('bhnid,bhnjd->bhnij', q, k) * decay, 0.0)
        xs += [q_in, a_qk]
    xs = tuple(jnp.moveaxis(t, 2, 0) for t in xs)

    def step(s, xi):
        v_new = xi[0] - xi[1] @ s
        s_new = s * xi[3][..., None, None] + jnp.einsum('bhck,bhcv->bhkv', xi[2], v_new)
        if with_out:
            return s_new, xi[4] @ s + xi[5] @ v_new
        return s_new, None

    s_fin, o = lax.scan(step, s0, xs)
    if with_out:
        o = jnp.moveaxis(o, 0, 2)
        o = o.reshape(o.shape[:2] + (-1, o.shape[-1]))
    return s_fin, o


def gla_chunked(q, k, v, gk, s0, with_out):
    q = _chunks(q * (q.shape[-1] ** -0.5))
    k, v, gk = _chunks(k), _chunks(v), _chunks(gk)
    b = jnp.cumsum(gk, axis=3)
    b_end = b[..., -1:, :]
    k_end = k * jnp.exp(b_end - b)
    decay_end = jnp.exp(b_end[..., 0, :])
    xs = [k_end, v, decay_end]
    if with_out:
        causal = jnp.tril(jnp.ones((CHUNK, CHUNK), bool))
        q_in = q * jnp.exp(b)
        a_qk = jnp.where(causal, jnp.einsum('bhnid,bhnjd->bhnij', q_in, k * jnp.exp(-b)), 0.0)
        xs += [q_in, a_qk]
    xs = tuple(jnp.moveaxis(t, 2, 0) for t in xs)

    def step(s, xi):
        s_new = s * xi[2][..., :, None] + jnp.einsum('bhck,bhcv->bhkv', xi[0], xi[1])
        if with_out:
            return s_new, xi[3] @ s + xi[4] @ xi[1]
        return s_new, None

    s_fin, o = lax.scan(step, s0, xs)
    if with_out:
        o = jnp.moveaxis(o, 0, 2)
        o = o.reshape(o.shape[:2] + (-1, o.shape[-1]))
    return s_fin, o


def two_segment_scan(chunk_fn, ctx_in, lat_in, s0, reverse, with_ctx_out):
    flip = (lambda t: jnp.flip(t, axis=2)) if reverse else (lambda t: t)
    s_ctx, o_ctx = chunk_fn(*[flip(t) for t in ctx_in], s0, with_ctx_out)
    _, o_lat = chunk_fn(*[flip(t) for t in lat_in], s_ctx, True)
    return (flip(o_ctx) if with_ctx_out else None), flip(o_lat)


def conv_module_mixer(p_ctx, p_lat, conv_w, conv_b, ln_g, ln_b, with_ctx_out):
    def run(p):
        a, gate = jnp.split(p, 2, axis=-1)
        y = a * jax.nn.sigmoid(gate)
        y = dwconv(y, conv_w) + conv_b.astype(y.dtype)
        return jax.nn.silu(layer_norm(y, ln_g, ln_b))
    return (run(p_ctx) if with_ctx_out else None), run(p_lat)


def diff_attention_mixer(p_ctx, p_lat, cos, sin, qn_g, kn_g, lam_p, subln_g, layer_idx, with_ctx_out):
    nq = DA_HEADS * 2 * DA_QK_DIM

    def heads(p):
        b, l = p.shape[:2]
        q, k, v = jnp.split(p, [nq, 2 * nq], axis=-1)
        q = rms_norm(q.reshape(b, l, DA_HEADS, 2, DA_QK_DIM), qn_g)
        k = rms_norm(k.reshape(b, l, DA_HEADS, 2, DA_QK_DIM), kn_g)
        return q, k, v.reshape(b, l, DA_HEADS, DA_V_DIM)

    qc, kc, vc = heads(p_ctx)
    ql, kl, vl = heads(p_lat)
    ql = apply_rope(ql, cos, sin)
    kl = apply_rope(kl, cos, sin)
    lam_init = 0.8 - 0.6 * math.exp(-0.3 * layer_idx)
    lp = lam_p.astype(F32)
    lam = jnp.exp(jnp.sum(lp[0] * lp[1])) - jnp.exp(jnp.sum(lp[2] * lp[3])) + lam_init
    scale = DA_QK_DIM ** -0.5

    def attend(qb, kk, vv):
        s = jnp.einsum('bqhmd,bkhmd->bhmqk', qb, kk).astype(F32) * scale
        pr = jax.nn.softmax(s, axis=-1)
        a = pr[:, :, 0] - lam * pr[:, :, 1]
        return jnp.einsum('bhqk,bkhd->bqhd', a.astype(vv.dtype), vv)

    def finish(o):
        y = rms_norm(o, subln_g) * (1 - lam_init)
        return y.reshape(o.shape[0], o.shape[1], -1)

    k_all = jnp.concatenate([kc, kl], axis=1)
    v_all = jnp.concatenate([vc, vl], axis=1)
    b, n = ql.shape[:2]
    nb = n // Q_BLOCK
    q_blocks = jnp.moveaxis(ql.reshape(b, nb, Q_BLOCK, DA_HEADS, 2, DA_QK_DIM), 1, 0)
    ol = lax.map(lambda qb: attend(qb, k_all, v_all), q_blocks)
    ol = jnp.moveaxis(ol, 0, 1).reshape(b, n, DA_HEADS, DA_V_DIM)
    oc = finish(attend(qc, kc, vc)) if with_ctx_out else None
    return oc, finish(ol)


def gated_deltanet_mixer(p_ctx, p_lat, conv_w, a_log, dt_bias, onorm_g, with_ctx_out):
    nk = DN_HEADS * DN_K_DIM

    def prep(p):
        b, l = p.shape[:2]
        qkv, beta_raw, a_raw, gate = jnp.split(p, [DN_QKV, DN_QKV + 2 * DN_HEADS, DN_QKV + 4 * DN_HEADS], axis=-1)
        qkv = jax.nn.silu(dwconv(qkv, conv_w))
        q, k, v = jnp.split(qkv, [nk, 2 * nk], axis=-1)
        q = l2_norm(_to_heads(q, DN_HEADS))
        k = l2_norm(_to_heads(k, DN_HEADS))
        v = _to_heads(v, DN_HEADS).astype(F32)
        beta = jax.nn.sigmoid(beta_raw.astype(F32)).reshape(b, l, 2, DN_HEADS)
        g = -jnp.exp(a_log.astype(F32)) * jax.nn.softplus(
            a_raw.astype(F32).reshape(b, l, 2, DN_HEADS) + dt_bias.astype(F32))
        return q, k, v, jnp.transpose(g, (0, 2, 3, 1)), jnp.transpose(beta, (0, 2, 3, 1)), gate

    qc, kc, vc, gc, bc, gate_c = prep(p_ctx)
    ql, kl, vl, gl, bl, gate_l = prep(p_lat)
    s0 = jnp.zeros((ql.shape[0], DN_HEADS, DN_K_DIM, DN_V_DIM), F32)
    outs = [two_segment_scan(gated_delta_chunked, (qc, kc, vc, gc[:, d], bc[:, d]),
                             (ql, kl, vl, gl[:, d], bl[:, d]), s0, d == 1, with_ctx_out)
            for d in range(2)]
    ol = _gated_head_out(outs[0][1] + outs[1][1], gate_l, onorm_g, DN_HEADS)
    oc = _gated_head_out(outs[0][0] + outs[1][0], gate_c, onorm_g, DN_HEADS) if with_ctx_out else None
    return oc, ol


def gla_mixer(p_ctx, p_lat, w2, b2, onorm_g, with_ctx_out):
    nk = GLA_HEADS * GLA_K_DIM
    nv = GLA_HEADS * GLA_V_DIM

    def prep(p):
        b, l = p.shape[:2]
        q, k, v, lr, gate = jnp.split(p, [nk, 2 * nk, 2 * nk + nv, 2 * nk + nv + 2 * GLA_GATE_RANK], axis=-1)
        lr = lr.reshape(b, l, 2, GLA_GATE_RANK).astype(F32)
        z = jnp.einsum('blmr,mrk->bmlk', lr, w2.astype(F32)) + b2.astype(F32)[None, :, None, :]
        gk = jax.nn.log_sigmoid(z) / GLA_TAU
        gk = jnp.transpose(gk.reshape(b, 2, l, GLA_HEADS, GLA_K_DIM), (0, 1, 3, 2, 4))
        q = _to_heads(q, GLA_HEADS).astype(F32)
        k = _to_heads(k, GLA_HEADS).astype(F32)
        v = _to_heads(v, GLA_HEADS).astype(F32)
        return q, k, v, gk, gate

    qc, kc, vc, gkc, gate_c = prep(p_ctx)
    ql, kl, vl, gkl, gate_l = prep(p_lat)
    s0 = jnp.zeros((ql.shape[0], GLA_HEADS, GLA_K_DIM, GLA_V_DIM), F32)
    outs = [two_segment_scan(gla_chunked, (qc, kc, vc, gkc[:, d]), (ql, kl, vl, gkl[:, d]),
                             s0, d == 1, with_ctx_out)
            for d in range(2)]
    ol = _gated_head_out(outs[0][1] + outs[1][1], gate_l, onorm_g, GLA_HEADS)
    oc = _gated_head_out(outs[0][0] + outs[1][0], gate_c, onorm_g, GLA_HEADS) if with_ctx_out else None
    return oc, ol


def conv_ffn(h, w_up, conv_w, w_down):
    u = dwconv(h @ w_up, conv_w)
    a, g = jnp.split(u, 2, axis=-1)
    return (jax.nn.silu(g) * a) @ w_down


def trunk_layer(x, xc, c_act, cc_act, lp, cos, sin, layer_idx, with_ctx_out):
    mod = c_act @ lp['w_mod'] + lp['b_mod']
    modc = cc_act @ lp['w_mod'] + lp['b_mod']
    sh1, sc1, g1, sh2, sc2, g2 = jnp.split(mod, 6, axis=-1)
    sh1c, sc1c, g1c, sh2c, sc2c, g2c = jnp.split(modc, 6, axis=-1)

    h = modulate(rms_norm(x, lp['norm1_g']), sh1, sc1)
    hc = modulate(rms_norm(xc, lp['norm1_g']), sh1c, sc1c)
    pa, pb, pcn, pd = jnp.split(h @ lp['w_in'], IN_SPLITS, axis=-1)
    pac, pbc, pcc, pdc = jnp.split(hc @ lp['w_in'], IN_SPLITS, axis=-1)

    ya_c, ya = conv_module_mixer(pac, pa, lp['cm_conv_w'], lp['cm_conv_b'], lp['cm_ln_g'], lp['cm_ln_b'], with_ctx_out)
    yb_c, yb = diff_attention_mixer(pbc, pb, cos, sin, lp['da_qnorm_g'], lp['da_knorm_g'], lp['da_lambda'],
                                    lp['da_subln_g'], layer_idx, with_ctx_out)
    yc_c, yc = gated_deltanet_mixer(pcc, pcn, lp['dn_conv_w'], lp['dn_a_log'], lp['dn_dt_bias'],
                                    lp['dn_onorm_g'], with_ctx_out)
    yd_c, yd = gla_mixer(pdc, pd, lp['gla_w2'], lp['gla_b2'], lp['gla_onorm_g'], with_ctx_out)

    x = x + g1[:, None, :] * (jnp.concatenate([ya, yb, yc, yd], axis=-1) @ lp['w_out'])
    x = x + g2[:, None, :] * conv_ffn(modulate(rms_norm(x, lp['norm2_g']), sh2, sc2),
                                      lp['ffn_w_up'], lp['ffn_conv_w'], lp['ffn_w_down'])
    if with_ctx_out:
        xc = xc + g1c * (jnp.concatenate([ya_c, yb_c, yc_c, yd_c], axis=-1) @ lp['w_out'])
        xc = xc + g2c * conv_ffn(modulate(rms_norm(xc, lp['norm2_g']), sh2c, sc2c),
                                 lp['ffn_w_up'], lp['ffn_conv_w'], lp['ffn_w_down'])
    return x, xc


def setup_inputs(seed: int = 0) -> dict:
    key = jax.random.key(seed)
    k = jax.random.split(key, 28)
    D, L = D_MODEL, DEPTH

    def nrm(i, shape, scale):
        return jax.random.normal(k[i], shape, F32) * scale

    def gain(i, shape):
        return 1.0 + 0.02 * jax.random.normal(k[i], shape, F32)

    dt = jnp.exp(jax.random.uniform(k[20], (L, 2, DN_HEADS), F32, math.log(1e-3), math.log(1e-1)))
    return {
        'x': nrm(0, (BATCH, SEQ, D), 1.0),
        'c': nrm(1, (BATCH, D), 1.0),
        'ctx': nrm(2, (BATCH, CTX_LEN, D), 1.0),
        'c_ctx': nrm(3, (D,), 1.0),
        'w_mod': nrm(4, (L, D, 6 * D), 0.5 * D ** -0.5),
        'b_mod': nrm(5, (L, 6 * D), 0.01),
        'norm1_g': gain(6, (L, D)),
        'norm2_g': gain(7, (L, D)),
        'w_in': nrm(8, (L, D, IN_COLS), D ** -0.5),
        'w_out': nrm(9, (L, MIX_WIDTH, D), MIX_WIDTH ** -0.5),
        'cm_conv_w': nrm(10, (L, CM_KERNEL, CM_CH), CM_KERNEL ** -0.5),
        'cm_conv_b': nrm(11, (L, CM_CH), 0.01),
        'cm_ln_g': gain(12, (L, CM_CH)),
        'cm_ln_b': nrm(13, (L, CM_CH), 0.01),
        'da_qnorm_g': gain(14, (L, DA_QK_DIM)),
        'da_knorm_g': gain(15, (L, DA_QK_DIM)),
        'da_lambda': nrm(16, (L, 4, DA_QK_DIM), 0.1),
        'da_subln_g': gain(17, (L, DA_V_DIM)),
        'dn_conv_w': nrm(18, (L, DN_SHORT_CONV, DN_QKV), DN_SHORT_CONV ** -0.5),
        'dn_a_log': jnp.log(jax.random.uniform(k[19], (L, 2, DN_HEADS), F32, 1.0, 16.0)),
        'dn_dt_bias': dt + jnp.log(-jnp.expm1(-dt)),
        'dn_onorm_g': gain(21, (L, DN_V_DIM)),
        'gla_w2': nrm(22, (L, 2, GLA_GATE_RANK, GLA_HEADS * GLA_K_DIM), GLA_GATE_RANK ** -0.5),
        'gla_b2': nrm(23, (L, 2, GLA_HEADS * GLA_K_DIM), 0.01),
        'gla_onorm_g': gain(24, (L, GLA_V_DIM)),
        'ffn_w_up': nrm(25, (L, D, 2 * D_FF), D ** -0.5),
        'ffn_conv_w': nrm(26, (L, FFN_KERNEL, 2 * D_FF), FFN_KERNEL ** -0.5),
        'ffn_w_down': nrm(27, (L, D_FF, D), D_FF ** -0.5),
    }


def reference(x, c, ctx, c_ctx, w_mod, b_mod, norm1_g, norm2_g, w_in, w_out,
              cm_conv_w, cm_conv_b, cm_ln_g, cm_ln_b,
              da_qnorm_g, da_knorm_g, da_lambda, da_subln_g,
              dn_conv_w, dn_a_log, dn_dt_bias, dn_onorm_g,
              gla_w2, gla_b2, gla_onorm_g,
              ffn_w_up, ffn_conv_w, ffn_w_down):
    cos, sin = axial_rope(x.shape[1])
    c_act = jax.nn.silu(c)
    cc_act = jax.nn.silu(c_ctx)
    xc = ctx
    for l in range(DEPTH):
        lp = {
            'w_mod': w_mod[l], 'b_mod': b_mod[l], 'norm1_g': norm1_g[l], 'norm2_g': norm2_g[l],
            'w_in': w_in[l], 'w_out': w_out[l],
            'cm_conv_w': cm_conv_w[l], 'cm_conv_b': cm_conv_b[l], 'cm_ln_g': cm_ln_g[l], 'cm_ln_b': cm_ln_b[l],
            'da_qnorm_g': da_qnorm_g[l], 'da_knorm_g': da_knorm_g[l], 'da_lambda': da_lambda[l],
            'da_subln_g': da_subln_g[l],
            'dn_conv_w': dn_conv_w[l], 'dn_a_log': dn_a_log[l], 'dn_dt_bias': dn_dt_bias[l],
            'dn_onorm_g': dn_onorm_g[l],
            'gla_w2': gla_w2[l], 'gla_b2': gla_b2[l], 'gla_onorm_g': gla_onorm_g[l],
            'ffn_w_up': ffn_w_up[l], 'ffn_conv_w': ffn_conv_w[l], 'ffn_w_down': ffn_w_down[l],
        }
        x, xc = trunk_layer(x, xc, c_act, cc_act, lp, cos, sin, l, l < DEPTH - 1)
    return x
```

```python
import functools
import math

import jax
import jax.numpy as jnp
from jax import lax
from jax.experimental import pallas as pl
from jax.experimental.pallas import tpu as pltpu

F32 = jnp.float32
BF16 = jnp.bfloat16
HI = lax.Precision.HIGHEST
EPS = 1e-6

D_MODEL = 1024
GRID_W = 64
HEADS = 4
HEAD_DIM = 64
GW = 256
QK_DIM = 32
GLA_K = 32
GLA_RANK = 16
GLA_TAU = 16.0
CM_KERNEL = 31
DN_CONV = 5
ROPE_THETA = 10000.0
CH = 64
TM = 256
CPT = TM // CH
D_FF = 2816
FF_BLK = 256
N_FF_BLK = D_FF // FF_BLK
LANES = 128

IN_GROUPS = (("pa", 512), ("pb", 768), ("pc", 768), ("pcs", LANES), ("pd", 512), ("pdl", LANES), ("pg", 512))
IN_COLS_PAD = sum(w for _, w in IN_GROUPS)

VMEM_LIMIT = 56 * 1024 * 1024


def _cp(sem):
    return pltpu.CompilerParams(dimension_semantics=sem, vmem_limit_bytes=VMEM_LIMIT)


def _dot(a, b, prec=None):
    return jnp.dot(a, b, preferred_element_type=F32, precision=prec)


def _dot_nt(a, b, prec=None):
    return lax.dot_general(a, b, (((1,), (1,)), ((), ())), preferred_element_type=F32, precision=prec)


def _sigmoid(x):
    return 1.0 / (1.0 + jnp.exp(-x))


def _silu(x):
    return x * _sigmoid(x)


def _softplus(x):
    return jnp.maximum(x, 0.0) + jnp.log(1.0 + jnp.exp(-jnp.abs(x)))


def _iota(shape, dim):
    return lax.broadcasted_iota(jnp.int32, shape, dim)


def _group_ones(n, shift):
    return ((_iota((n, n), 0) >> shift) == (_iota((n, n), 1) >> shift)).astype(F32)


def _tile4(y):
    return jnp.concatenate([y, y, y, y], axis=0)


def _mod_kernel(c_ref, w_ref, b_ref, o_ref):
    o_ref[...] = _dot(_silu(c_ref[...]), w_ref[...], HI) + b_ref[...]


def _modulation(c_rows, w_mod, b_mod):
    r, d = c_rows.shape
    n = w_mod.shape[1] // d
    return pl.pallas_call(
        _mod_kernel,
        grid=(n,),
        in_specs=[pl.BlockSpec((r, d), lambda j: (0, 0)),
                  pl.BlockSpec((d, d), lambda j: (0, j)),
                  pl.BlockSpec((1, d), lambda j: (0, j))],
        out_specs=pl.BlockSpec((r, d), lambda j: (0, j)),
        out_shape=jax.ShapeDtypeStruct((r, n * d), F32),
        compiler_params=_cp(("arbitrary",)),
    )(c_rows, w_mod, b_mod.reshape(1, -1))


def _mod_spec(k, n_batch, first_tile):
    def imap(b, i):
        sel = jnp.where(i + first_tile == 0, n_batch, b)
        return (sel * 6 + k, 0, 0)
    return pl.BlockSpec((1, 1, D_MODEL), imap)


def _inproj_kernel(x_ref, sh_ref, sc_ref, g_ref, w_ref, *outs):
    x = x_ref[0]
    y = x * lax.rsqrt(jnp.mean(x * x, axis=-1, keepdims=True) + EPS) * g_ref[...]
    h = y * (1.0 + sc_ref[0]) + sh_ref[0]
    p = _dot(h.astype(BF16), w_ref[...])
    off = 0
    for o_ref, (_, width) in zip(outs, IN_GROUPS):
        o_ref[0] = p[:, off:off + width]
        off += width


def _inproj(xs, mod3, norm_g, w_in_r, n_batch):
    b, t, d = xs.shape
    row = lambda bb, i: (bb, i, 0)
    return pl.pallas_call(
        _inproj_kernel,
        grid=(b, t // TM),
        in_specs=[pl.BlockSpec((1, TM, d), row), _mod_spec(0, n_batch, 0), _mod_spec(1, n_batch, 0),
                  pl.BlockSpec((1, d), lambda bb, i: (0, 0)),
                  pl.BlockSpec((d, IN_COLS_PAD), lambda bb, i: (0, 0))],
        out_specs=[pl.BlockSpec((1, TM, w), row) for _, w in IN_GROUPS],
        out_shape=[jax.ShapeDtypeStruct((b, t, w), F32) for _, w in IN_GROUPS],
        compiler_params=_cp(("parallel", "arbitrary")),
    )(xs, mod3, mod3, norm_g.reshape(1, d), w_in_r)


def _halo_specs(width, halo, n_tiles):
    per = TM // halo
    left = pl.BlockSpec((1, halo, width), lambda b, i: (b, jnp.maximum(i * per - 1, 0), 0))
    right = pl.BlockSpec((1, halo, width), lambda b, i: (b, jnp.minimum((i + 1) * per, n_tiles * per - 1), 0))
    return left, right


def _halo_ok(i, n_tiles):
    return i >= 2, jnp.logical_and(i >= 1, i < n_tiles - 1)


CM_HALO = 16


def _convmod_kernel(pm_ref, pl_ref, pr_ref, cw_ref, cb_ref, lg_ref, lb_ref, o_ref, ext_ref):
    left_ok, right_ok = _halo_ok(pl.program_id(1), pl.num_programs(1))

    def glu(p):
        return p[:, :GW] * _sigmoid(p[:, GW:])

    ext_ref[0:CM_HALO] = jnp.where(left_ok, glu(pl_ref[0]), 0.0)
    ext_ref[CM_HALO:CM_HALO + TM] = glu(pm_ref[0])
    ext_ref[CM_HALO + TM:] = jnp.where(right_ok, glu(pr_ref[0]), 0.0)
    pad = CM_KERNEL // 2
    acc = jnp.zeros((TM, GW), F32)
    for j in range(CM_KERNEL):
        acc = acc + cw_ref[j:j + 1, :] * ext_ref[pl.ds(CM_HALO - pad + j, TM), :]
    y = acc + cb_ref[...]
    mu = jnp.mean(y, axis=-1, keepdims=True)
    yc = y - mu
    var = jnp.mean(yc * yc, axis=-1, keepdims=True)
    o_ref[0] = _silu(yc * lax.rsqrt(var + EPS) * lg_ref[...] + lb_ref[...])


def _conv_module(pa, conv_w, conv_b, ln_g, ln_b):
    b, t, w = pa.shape
    nt = t // TM
    left, right = _halo_specs(w, CM_HALO, nt)
    vec = pl.BlockSpec((1, GW), lambda bb, i: (0, 0))
    return pl.pallas_call(
        _convmod_kernel,
        grid=(b, nt),
        in_specs=[pl.BlockSpec((1, TM, w), lambda bb, i: (bb, i, 0)), left, right,
                  pl.BlockSpec((CM_KERNEL, GW), lambda bb, i: (0, 0)), vec, vec, vec],
        out_specs=pl.BlockSpec((1, TM, GW), lambda bb, i: (bb, i, 0)),
        out_shape=jax.ShapeDtypeStruct((b, t, GW), F32),
        scratch_shapes=[pltpu.VMEM((TM + 2 * CM_HALO, GW), F32)],
        compiler_params=_cp(("parallel", "arbitrary")),
    )(pa, pa, pa, conv_w, conv_b.reshape(1, GW), ln_g.reshape(1, GW), ln_b.reshape(1, GW))


def _attn_prep_kernel(pb_ref, qg_ref, kg_ref, cos_ref, sin_ref, q_out, kt_out, v_out):
    p = pb_ref[0]
    ones32 = _group_ones(GW, 5)
    first = (_iota((TM, GW), 1) & (QK_DIM - 1)) < QK_DIM // 2
    cos = cos_ref[...]
    sin = sin_ref[...]

    def norm_rope(t, g):
        ms = _dot(t * t, ones32, HI) * (1.0 / QK_DIM)
        tn = t * lax.rsqrt(ms + EPS) * g
        partner = jnp.where(first, pltpu.roll(tn, GW - QK_DIM // 2, 1), pltpu.roll(tn, QK_DIM // 2, 1))
        return tn * cos + partner * sin

    q = norm_rope(p[:, :GW], qg_ref[...]) * (QK_DIM ** -0.5)
    k = norm_rope(p[:, GW:2 * GW], kg_ref[...])
    q_out[0] = q.astype(BF16)
    kt_out[0, 0] = k.T.astype(BF16)
    v = p[:, 2 * GW:]
    for h in range(HEADS):
        v_out[0, 0, h] = v[:, h * HEAD_DIM:(h + 1) * HEAD_DIM].astype(BF16)


def _attn_prep(pb, qn_g, kn_g, cos_t, sin_t):
    b, t, w = pb.shape
    nt = t // TM
    vec = pl.BlockSpec((1, GW), lambda bb, i: (0, 0))
    tab = pl.BlockSpec((TM, GW), lambda bb, i: (i, 0))
    reps = GW // QK_DIM
    return pl.pallas_call(
        _attn_prep_kernel,
        grid=(b, nt),
        in_specs=[pl.BlockSpec((1, TM, w), lambda bb, i: (bb, i, 0)), vec, vec, tab, tab],
        out_specs=[pl.BlockSpec((1, TM, GW), lambda bb, i: (bb, i, 0)),
                   pl.BlockSpec((1, 1, GW, TM), lambda bb, i: (bb, i, 0, 0)),
                   pl.BlockSpec((1, 1, HEADS, TM, HEAD_DIM), lambda bb, i: (bb, i, 0, 0, 0))],
        out_shape=[jax.ShapeDtypeStruct((b, t, GW), BF16),
                   jax.ShapeDtypeStruct((b, nt, GW, TM), BF16),
                   jax.ShapeDtypeStruct((b, nt, HEADS, TM, HEAD_DIM), BF16)],
        compiler_params=_cp(("parallel", "arbitrary")),
    )(pb, jnp.tile(qn_g, reps).reshape(1, GW), jnp.tile(kn_g, reps).reshape(1, GW), cos_t, sin_t)


def _attn_kernel(q_ref, kt_ref, v_ref, lam_ref, sg_ref, o_ref, m_scr, l_scr, acc_scr, *, n_chunks, lam_init):
    lp = lam_ref[...]
    lam = (jnp.exp(jnp.sum(lp[0:1] * lp[1:2], axis=-1, keepdims=True))
           - jnp.exp(jnp.sum(lp[2:3] * lp[3:4], axis=-1, keepdims=True)) + lam_init)
    q_all = q_ref[0]
    for h in range(HEADS):
        m_scr[...] = jnp.full(m_scr.shape, -jnp.inf, F32)
        l_scr[...] = jnp.zeros(l_scr.shape, F32)
        acc_scr[...] = jnp.zeros(acc_scr.shape, F32)
        qs = [q_all[:, (2 * h + m) * QK_DIM:(2 * h + m + 1) * QK_DIM] for m in range(2)]

        def body(c, carry, h=h, qs=qs):
            v = v_ref[0, c, h]
            for m in range(2):
                g = 2 * h + m
                kt = kt_ref[0, c, g * QK_DIM:(g + 1) * QK_DIM, :]
                s = _dot(qs[m], kt)
                m_old = m_scr[m]
                m_new = jnp.maximum(m_old, jnp.max(s, axis=-1, keepdims=True))
                alpha = jnp.exp(m_old - m_new)
                p = jnp.exp(s - m_new)
                l_scr[m] = alpha * l_scr[m] + jnp.sum(p, axis=-1, keepdims=True)
                acc_scr[m] = alpha * acc_scr[m] + _dot(p.astype(BF16), v)
                m_scr[m] = m_new
            return carry

        lax.fori_loop(0, n_chunks, body, 0)
        o = acc_scr[0] / l_scr[0] - lam * (acc_scr[1] / l_scr[1])
        y = o * lax.rsqrt(jnp.mean(o * o, axis=-1, keepdims=True) + EPS) * sg_ref[...] * (1.0 - lam_init)
        o_ref[0, :, h * HEAD_DIM:(h + 1) * HEAD_DIM] = y


def _attention(q, kt, v, lam_p, subln_g, lam_init, first_tile, n_q, n_kv, t_out):
    b = q.shape[0]
    kern = functools.partial(_attn_kernel, n_chunks=n_kv, lam_init=lam_init)
    return pl.pallas_call(
        kern,
        grid=(b, n_q),
        in_specs=[pl.BlockSpec((1, TM, GW), lambda bb, i: (bb, i + first_tile, 0)),
                  pl.BlockSpec((1, n_kv, GW, TM), lambda bb, i: (bb, 0, 0, 0)),
                  pl.BlockSpec((1, n_kv, HEADS, TM, HEAD_DIM), lambda bb, i: (bb, 0, 0, 0, 0)),
                  pl.BlockSpec((4, QK_DIM), lambda bb, i: (0, 0)),
                  pl.BlockSpec((1, HEAD_DIM), lambda bb, i: (0, 0))],
        out_specs=pl.BlockSpec((1, TM, GW), lambda bb, i: (bb, i, 0)),
        out_shape=jax.ShapeDtypeStruct((b, t_out, GW), F32),
        scratch_shapes=[pltpu.VMEM((2, TM, 1), F32), pltpu.VMEM((2, TM, 1), F32),
                        pltpu.VMEM((2, TM, HEAD_DIM), F32)],
        compiler_params=_cp(("parallel", "arbitrary")),
    )(q, kt, v, lam_p, subln_g.reshape(1, HEAD_DIM))


def _chunk_masks():
    i = _iota((CH, GW), 0)
    j = _iota((CH, GW), 1) & (CH - 1)
    eye = (i == j).astype(F32)
    incl = (j <= i, j >= i)
    strict = (j < i, j > i)
    r = _iota((CH, CH), 0)
    c = _iota((CH, CH), 1)
    cum = ((c <= r).astype(F32), (c >= r).astype(F32))
    bd = (_iota((GW, GW), 0) >> 6) == (_iota((GW, GW), 1) >> 6)
    return eye, incl, strict, cum, bd


def _bd(y, bd):
    return jnp.where(bd, _tile4(y), 0.0).astype(BF16)


def _rev_tile(s, n_tiles):
    return jnp.where(s == 0, 0, n_tiles - s)


DN_HALO = 8


def _dn_prep_kernel(pm_ref, pl_ref, pr_ref, pcs_ref, cw_ref, alog_ref, dtb_ref, q_out, k_out, v_out, bg_out, ext_ref):
    left_ok, right_ok = _halo_ok(pl.program_id(1), pl.num_programs(1))
    ext_ref[0:DN_HALO] = jnp.where(left_ok, pl_ref[0], 0.0)
    ext_ref[DN_HALO:DN_HALO + TM] = pm_ref[0]
    ext_ref[DN_HALO + TM:] = jnp.where(right_ok, pr_ref[0], 0.0)
    pad = DN_CONV // 2
    acc = jnp.zeros((TM, 3 * GW), F32)
    for j in range(DN_CONV):
        acc = acc + cw_ref[j:j + 1, :] * ext_ref[pl.ds(DN_HALO - pad + j, TM), :]
    qkv = _silu(acc)
    ones64 = _group_ones(GW, 6)

    def l2n(t):
        return t * lax.rsqrt(_dot(t * t, ones64, HI) + EPS)

    q_out[0] = l2n(qkv[:, :GW]) * (HEAD_DIM ** -0.5)
    k_out[0] = l2n(qkv[:, GW:2 * GW])
    v_out[0] = qkv[:, 2 * GW:]
    s = pcs_ref[0]
    col = _iota(s.shape, 1)
    gate = -jnp.exp(alog_ref[...]) * _softplus(s + dtb_ref[...])
    bg_out[0] = jnp.where(col < 2 * HEADS, _sigmoid(s), jnp.where(col < 4 * HEADS, gate, 0.0))


def _dn_prep(pc, pcs, conv_w, a_log, dt_bias):
    b, t, w = pc.shape
    nt = t // TM
    left, right = _halo_specs(w, DN_HALO, nt)
    row = lambda bb, i: (bb, i, 0)
    pad_vec = lambda a: jnp.zeros((1, LANES), F32).at[0, 2 * HEADS:4 * HEADS].set(a.reshape(-1))
    vec = pl.BlockSpec((1, LANES), lambda bb, i: (0, 0))
    return pl.pallas_call(
        _dn_prep_kernel,
        grid=(b, nt),
        in_specs=[pl.BlockSpec((1, TM, w), row), left, right, pl.BlockSpec((1, TM, LANES), row),
                  pl.BlockSpec((DN_CONV, w), lambda bb, i: (0, 0)), vec, vec],
        out_specs=[pl.BlockSpec((1, TM, GW), row)] * 3 + [pl.BlockSpec((1, TM, LANES), row)],
        out_shape=[jax.ShapeDtypeStruct((b, t, GW), F32)] * 3 + [jax.ShapeDtypeStruct((b, t, LANES), F32)],
        scratch_shapes=[pltpu.VMEM((TM + 2 * DN_HALO, w), F32)],
        compiler_params=_cp(("parallel", "arbitrary")),
    )(pc, pc, pc, pcs, conv_w, pad_vec(a_log), pad_vec(dt_bias))


def _dn_local_kernel(q_ref, k_ref, v_ref, bg_ref, u_out, w_out, qi_out, a_out, ket_out, ge_out):
    eye, incl, strict, cum, bd = _chunk_masks()
    ones_cc = jnp.ones((CH, CH), F32)
    eye_bf = (_iota((GW, GW), 0) == _iota((GW, GW), 1)).astype(BF16)
    head_of_lane = _iota((LANES, GW), 1) >> 6
    src = _iota((LANES, GW), 0)
    for c in range(CPT):
        rows = slice(c * CH, (c + 1) * CH)
        q = q_ref[0, rows, :]
        k = k_ref[0, rows, :]
        v = v_ref[0, rows, :]
        bgc = bg_ref[0, rows, :]
        kbd = _bd(k, bd)
        for d in range(2):
            sel_b = (src == d * HEADS + head_of_lane).astype(F32)
            sel_g = (src == 2 * HEADS + d * HEADS + head_of_lane).astype(F32)
            beta = _dot(bgc, sel_b, HI)
            gcum = _dot(_dot(cum[d], bgc, HI), sel_g, HI)
            grow = _dot(ones_cc, gcum * eye, HI)
            gtot = gcum[CH - 1:CH] if d == 0 else gcum[0:1]
            decay = jnp.where(incl[d], jnp.exp(jnp.where(incl[d], gcum - grow, 0.0)), 0.0)
            kb = k * beta
            aq = _dot_nt(jnp.concatenate([kb, q], axis=0).astype(BF16), kbd)
            a = jnp.where(strict[d], aq[:CH] * decay, 0.0)
            a_qk = jnp.where(incl[d], aq[CH:] * decay, 0.0)
            t_inv = eye - a
            p = _dot(a.astype(BF16), _bd(a, bd))
            for it in range(5):
                pbd = _bd(p, bd)
                if it < 4:
                    r = _dot(jnp.concatenate([t_inv, p], axis=0).astype(BF16), pbd)
                    t_inv = t_inv + r[:CH]
                    p = r[CH:]
                else:
                    t_inv = t_inv + _dot(t_inv.astype(BF16), pbd)
            tb = t_inv.astype(BF16)
            egc = jnp.exp(gcum)
            u_out[0, d, rows, :] = _dot(tb, _bd(v * beta, bd))
            w_out[0, d, rows, :] = _dot(tb, _bd(kb * egc, bd)).astype(BF16)
            qi_out[0, d, rows, :] = (q * egc).astype(BF16)
            a_out[0, d, rows, :] = a_qk.astype(BF16)
            k_end = (k * jnp.exp(gtot - gcum)).astype(BF16)
            ket_out[0, d, c] = _dot_nt(eye_bf, k_end).astype(BF16)
            ge_out[0, d, c] = jnp.exp(gtot)


def _dn_local(q, k, v, bg):
    b, t, _ = q.shape
    nt = t // TM
    row = lambda bb, i: (bb, i, 0)
    drow = pl.BlockSpec((1, 2, TM, GW), lambda bb, i: (bb, 0, i, 0))
    return pl.pallas_call(
        _dn_local_kernel,
        grid=(b, nt),
        in_specs=[pl.BlockSpec((1, TM, GW), row)] * 3 + [pl.BlockSpec((1, TM, LANES), row)],
        out_specs=[drow, drow, drow, drow,
                   pl.BlockSpec((1, 2, CPT, GW, CH), lambda bb, i: (bb, 0, i, 0, 0)),
                   pl.BlockSpec((1, 2, CPT, 1, GW), lambda bb, i: (bb, 0, i, 0, 0))],
        out_shape=[jax.ShapeDtypeStruct((b, 2, t, GW), F32)] + [jax.ShapeDtypeStruct((b, 2, t, GW), BF16)] * 3
        + [jax.ShapeDtypeStruct((b, 2, t // CH, GW, CH), BF16), jax.ShapeDtypeStruct((b, 2, t // CH, 1, GW), F32)],
        compiler_params=_cp(("parallel", "arbitrary")),
    )(q, k, v, bg)


def _dn_scan_kernel(uf, wf, qf, af, kf, gf, ur, wr, qr, ar, kr, gr, of_ref, or_ref, s_scr):
    @pl.when(pl.program_id(1) == 0)
    def _():
        s_scr[...] = jnp.zeros(s_scr.shape, F32)

    bd = (_iota((GW, GW), 0) >> 6) == (_iota((GW, GW), 1) >> 6)
    dirs = ((uf, wf, qf, af, kf, gf, of_ref), (ur, wr, qr, ar, kr, gr, or_ref))
    for c in range(CPT):
        for d, (u, w, qi, a, ket, ge, o_ref) in enumerate(dirs):
            cc = c if d == 0 else CPT - 1 - c
            rows = slice(cc * CH, (cc + 1) * CH)
            s = s_scr[d]
            wq = _dot(jnp.concatenate([w[0, 0, rows, :], qi[0, 0, rows, :]], axis=0), s.astype(BF16))
            v_new = u[0, 0, rows, :] - wq[:CH]
            o_ref[0, rows, :] = wq[CH:] + _dot(a[0, 0, rows, :], _bd(v_new, bd))
            s_scr[d] = s * ge[0, 0, cc] + jnp.where(bd, _dot(ket[0, 0, cc], v_new.astype(BF16)), 0.0)


def _dir_specs(shape_tail, n_tiles, chunked):
    blk = (1, 1, CPT if chunked else TM) + shape_tail
    zeros = (0,) * len(shape_tail)
    fwd = pl.BlockSpec(blk, lambda b, s: (b, 0, s) + zeros)
    rev = pl.BlockSpec(blk, lambda b, s: (b, 1, _rev_tile(s, n_tiles)) + zeros)
    return fwd, rev


def _dn_scan(u, w, qi, a, ket, ge):
    b, _, t, _ = u.shape
    nt = t // TM
    rowf, rowr = _dir_specs((GW,), nt, False)
    ketf, ketr = _dir_specs((GW, CH), nt, True)
    gef, ger = _dir_specs((1, GW), nt, True)
    return pl.pallas_call(
        _dn_scan_kernel,
        grid=(b, nt),
        in_specs=[rowf, rowf, rowf, rowf, ketf, gef, rowr, rowr, rowr, rowr, ketr, ger],
        out_specs=[pl.BlockSpec((1, TM, GW), lambda bb, s: (bb, s, 0)),
                   pl.BlockSpec((1, TM, GW), lambda bb, s: (bb, _rev_tile(s, nt), 0))],
        out_shape=[jax.ShapeDtypeStruct((b, t, GW), F32)] * 2,
        scratch_shapes=[pltpu.VMEM((2, GW, GW), F32)],
        compiler_params=_cp(("parallel", "arbitrary")),
    )(u, w, qi, a, ket, ge, u, w, qi, a, ket, ge)


GLA_QK = HEADS * GLA_K


def _gla_local_kernel(pd_ref, pdl_ref, w2_ref, b2_ref, qi_out, a_out, ke_out, vt_out, vb_out, de_out):
    _, incl, _, cum, _ = _chunk_masks()
    eye_bf = (_iota((GW, GW), 0) == _iota((GW, GW), 1)).astype(BF16)
    bdk = (_iota((GW, GLA_QK), 0) >> 6) == (_iota((GW, GLA_QK), 1) >> 5)
    z = _dot(pdl_ref[0], w2_ref[...], HI) + b2_ref[...]
    gk_all = -_softplus(-z) * (1.0 / GLA_TAU)
    for c in range(CPT):
        rows = slice(c * CH, (c + 1) * CH)
        p = pd_ref[0, rows, :]
        q = p[:, :GLA_QK] * (GLA_K ** -0.5)
        k = p[:, GLA_QK:2 * GLA_QK]
        vb = p[:, 2 * GLA_QK:].astype(BF16)
        vb_out[0, rows, :] = vb
        vt_out[0, c] = _dot_nt(eye_bf, vb).astype(BF16)
        for d in range(2):
            bcs = _dot(cum[d], gk_all[rows, d * GLA_QK:(d + 1) * GLA_QK], HI)
            bend = bcs[CH - 1:CH] if d == 0 else bcs[0:1]
            q_in = (q * jnp.exp(bcs)).astype(BF16)
            kdec = jnp.where(bdk, _tile4(k * jnp.exp(-bcs)), 0.0).astype(BF16)
            a_out[0, d, rows, :] = jnp.where(incl[d], _dot_nt(q_in, kdec), 0.0).astype(BF16)
            qi_out[0, d, rows, :] = q_in
            ke_out[0, d, rows, :] = (k * jnp.exp(bend - bcs)).astype(BF16)
            de_out[0, d, c] = jnp.exp(bend)


def _gla_local(pd, pdl, w2bd, b2):
    b, t, w = pd.shape
    nt = t // TM
    row = lambda bb, i: (bb, i, 0)
    return pl.pallas_call(
        _gla_local_kernel,
        grid=(b, nt),
        in_specs=[pl.BlockSpec((1, TM, w), row), pl.BlockSpec((1, TM, LANES), row),
                  pl.BlockSpec((LANES, GW), lambda bb, i: (0, 0)), pl.BlockSpec((1, GW), lambda bb, i: (0, 0))],
        out_specs=[pl.BlockSpec((1, 2, TM, GLA_QK), lambda bb, i: (bb, 0, i, 0)),
                   pl.BlockSpec((1, 2, TM, GW), lambda bb, i: (bb, 0, i, 0)),
                   pl.BlockSpec((1, 2, TM, GLA_QK), lambda bb, i: (bb, 0, i, 0)),
                   pl.BlockSpec((1, CPT, GW, CH), lambda bb, i: (bb, i, 0, 0)),
                   pl.BlockSpec((1, TM, GW), row),
                   pl.BlockSpec((1, 2, CPT, 1, GLA_QK), lambda bb, i: (bb, 0, i, 0, 0))],
        out_shape=[jax.ShapeDtypeStruct((b, 2, t, GLA_QK), BF16), jax.ShapeDtypeStruct((b, 2, t, GW), BF16),
                   jax.ShapeDtypeStruct((b, 2, t, GLA_QK), BF16), jax.ShapeDtypeStruct((b, t // CH, GW, CH), BF16),
                   jax.ShapeDtypeStruct((b, t, GW), BF16), jax.ShapeDtypeStruct((b, 2, t // CH, 1, GLA_QK), F32)],
        compiler_params=_cp(("parallel", "arbitrary")),
    )(pd, pdl, w2bd, b2)


def _gla_scan_kernel(qf, af, kf, df, vtf, vf, qr, ar, kr, dr, vtr, vr, of_ref, or_ref, s_scr):
    @pl.when(pl.program_id(1) == 0)
    def _():
        s_scr[...] = jnp.zeros(s_scr.shape, F32)

    bd = (_iota((GW, GW), 0) >> 6) == (_iota((GW, GW), 1) >> 6)
    bdt = (_iota((GW, GLA_QK), 0) >> 6) == (_iota((GW, GLA_QK), 1) >> 5)
    dirs = ((qf, af, kf, df, vtf, vf, of_ref), (qr, ar, kr, dr, vtr, vr, or_ref))
    for c in range(CPT):
        for d, (qi, a, ke, de, vt, v, o_ref) in enumerate(dirs):
            cc = c if d == 0 else CPT - 1 - c
            rows = slice(cc * CH, (cc + 1) * CH)
            st = s_scr[d]
            vbd = jnp.where(bd, _tile4(v[0, rows, :]), jnp.zeros((), BF16))
            o_ref[0, rows, :] = _dot_nt(qi[0, 0, rows, :], st.astype(BF16)) + _dot(a[0, 0, rows, :], vbd)
            s_scr[d] = st * de[0, 0, cc] + jnp.where(bdt, _dot(vt[0, cc], ke[0, 0, rows, :]), 0.0)


def _gla_scan(qi, a, ke, de, vt, vb):
    b, _, t, _ = a.shape
    nt = t // TM
    qf, qr = _dir_specs((GLA_QK,), nt, False)
    af, ar = _dir_specs((GW,), nt, False)
    df, dr = _dir_specs((1, GLA_QK), nt, True)
    vtf = pl.BlockSpec((1, CPT, GW, CH), lambda bb, s: (bb, s, 0, 0))
    vtr = pl.BlockSpec((1, CPT, GW, CH), lambda bb, s: (bb, _rev_tile(s, nt), 0, 0))
    vf = pl.BlockSpec((1, TM, GW), lambda bb, s: (bb, s, 0))
    vr = pl.BlockSpec((1, TM, GW), lambda bb, s: (bb, _rev_tile(s, nt), 0))
    return pl.pallas_call(
        _gla_scan_kernel,
        grid=(b, nt),
        in_specs=[qf, af, qf, df, vtf, vf, qr, ar, qr, dr, vtr, vr],
        out_specs=[vf, vr],
        out_shape=[jax.ShapeDtypeStruct((b, t, GW), F32)] * 2,
        scratch_shapes=[pltpu.VMEM((2, GW, GLA_QK), F32)],
        compiler_params=_cp(("parallel", "arbitrary")),
    )(qi, a, ke, de, vt, vb, qi, a, ke, de, vt, vb)


def _outproj_kernel(x_ref, ya_ref, yb_ref, cf_ref, cr_ref, df_ref, dr_ref, pg_ref, g1_ref, gc_ref, gd_ref, w_ref, o_ref):
    ones64 = _group_ones(GW, 6)

    def fin(o, g, gate):
        ms = _dot(o * o, ones64, HI) * (1.0 / HEAD_DIM)
        return (o * lax.rsqrt(ms + EPS) * g * _silu(gate)).astype(BF16)

    pg = pg_ref[0]
    yc = fin(cf_ref[0] + cr_ref[0], gc_ref[...], pg[:, :GW])
    yd = fin(df_ref[0] + dr_ref[0], gd_ref[...], pg[:, GW:])
    res = (_dot(ya_ref[0].astype(BF16), w_ref[0:GW, :]) + _dot(yb_ref[0].astype(BF16), w_ref[GW:2 * GW, :])
           + _dot(yc, w_ref[2 * GW:3 * GW, :]) + _dot(yd, w_ref[3 * GW:, :]))
    o_ref[0] = x_ref[0] + g1_ref[0] * res


def _outproj(xs, ya, yb, ocf, ocr, odf, odr, pg, mod3, dn_g, gla_g, w_out_bf, n_batch, first_tile, yb_first):
    b, t, d = xs.shape
    n = t // TM - first_tile
    row = lambda bb, i: (bb, i + first_tile, 0)
    g256 = pl.BlockSpec((1, TM, GW), row)
    vec = pl.BlockSpec((1, GW), lambda bb, i: (0, 0))
    reps = GW // HEAD_DIM
    return pl.pallas_call(
        _outproj_kernel,
        grid=(b, n),
        in_specs=[pl.BlockSpec((1, TM, d), row), g256,
                  pl.BlockSpec((1, TM, GW), lambda bb, i: (bb, i + first_tile - yb_first, 0)),
                  g256, g256, g256, g256, pl.BlockSpec((1, TM, 2 * GW), row),
                  _mod_spec(2, n_batch, first_tile), vec, vec, pl.BlockSpec((d, d), lambda bb, i: (0, 0))],
        out_specs=pl.BlockSpec((1, TM, d), lambda bb, i: (bb, i, 0)),
        out_shape=jax.ShapeDtypeStruct((b, n * TM, d), F32),
        compiler_params=_cp(("parallel", "arbitrary")),
    )(xs, ya, yb, ocf, ocr, odf, odr, pg, mod3, jnp.tile(dn_g, reps).reshape(1, GW),
      jnp.tile(gla_g, reps).reshape(1, GW), w_out_bf)


FF_HALO = 8


def _ffn_kernel(xm_ref, xl_ref, xr_ref, sh_ref, sc_ref, g2_ref, ng_ref, wa_ref, wg_ref, cwa_ref, cwg_ref, wd_ref,
                o_ref, ext_ref, *, first_tile, n_tiles):
    left_ok, right_ok = _halo_ok(pl.program_id(1) + first_tile, n_tiles)
    rows_ext = _iota((TM + 2 * FF_HALO, 1), 0)
    keep = jnp.logical_and(jnp.logical_or(rows_ext >= FF_HALO, left_ok),
                           jnp.logical_or(rows_ext < FF_HALO + TM, right_ok)).astype(F32)
    x = jnp.concatenate([xl_ref[0], xm_ref[0], xr_ref[0]], axis=0)
    y = x * lax.rsqrt(jnp.mean(x * x, axis=-1, keepdims=True) + EPS) * ng_ref[...]
    h = ((y * (1.0 + sc_ref[0]) + sh_ref[0]) * keep).astype(BF16)
    acc = jnp.zeros((TM, D_MODEL), F32)
    for j in range(N_FF_BLK):
        def conv(w_ref, cw_ref, half):
            ext_ref[half] = _dot(h, w_ref[j])
            cw = cw_ref[j]
            return (cw[0:1] * ext_ref[half, pl.ds(FF_HALO - 1, TM), :] + cw[1:2] * ext_ref[half, pl.ds(FF_HALO, TM), :]
                    + cw[2:3] * ext_ref[half, pl.ds(FF_HALO + 1, TM), :])
        a = conv(wa_ref, cwa_ref, 0)
        g = conv(wg_ref, cwg_ref, 1)
        acc = acc + _dot((_silu(g) * a).astype(BF16), wd_ref[j])
    o_ref[0] = xm_ref[0] + g2_ref[0] * acc


def _ffn(x1, mod3, norm_g, wa, wg, cwa, cwg, wd, n_batch, first_tile, n_tiles_total):
    b, t, d = x1.shape
    n = t // TM
    per = TM // FF_HALO
    left = pl.BlockSpec((1, FF_HALO, d), lambda bb, i: (bb, jnp.maximum(i * per - 1, 0), 0))
    right = pl.BlockSpec((1, FF_HALO, d), lambda bb, i: (bb, jnp.minimum((i + 1) * per, n * per - 1), 0))
    const3 = lambda bb, i: (0, 0, 0)
    kern = functools.partial(_ffn_kernel, first_tile=first_tile, n_tiles=n_tiles_total)
    return pl.pallas_call(
        kern,
        grid=(b, n),
        in_specs=[pl.BlockSpec((1, TM, d), lambda bb, i: (bb, i, 0)), left, right,
                  _mod_spec(3, n_batch, first_tile), _mod_spec(4, n_batch, first_tile), _mod_spec(5, n_batch, first_tile),
                  pl.BlockSpec((1, d), lambda bb, i: (0, 0)),
                  pl.BlockSpec((N_FF_BLK, d, FF_BLK), const3), pl.BlockSpec((N_FF_BLK, d, FF_BLK), const3),
                  pl.BlockSpec((N_FF_BLK, 3, FF_BLK), const3), pl.BlockSpec((N_FF_BLK, 3, FF_BLK), const3),
                  pl.BlockSpec((N_FF_BLK, FF_BLK, d), const3)],
        out_specs=pl.BlockSpec((1, TM, d), lambda bb, i: (bb, i, 0)),
        out_shape=jax.ShapeDtypeStruct((b, t, d), F32),
        scratch_shapes=[pltpu.VMEM((2, TM + 2 * FF_HALO, FF_BLK), F32)],
        compiler_params=_cp(("parallel", "arbitrary")),
    )(x1, x1, x1, mod3, mod3, mod3, norm_g.reshape(1, d), wa, wg, cwa, cwg, wd)


def _rope_tables(seq, ctx_len):
    rows = seq // GRID_W
    row = jnp.repeat(jnp.arange(rows, dtype=F32), GRID_W)
    col = jnp.tile(jnp.arange(GRID_W, dtype=F32), rows)
    nf = QK_DIM // 4
    inv = ROPE_THETA ** (-jnp.arange(nf, dtype=F32) / nf)
    ang = jnp.concatenate([row[:, None] * inv, col[:, None] * inv], axis=-1)
    cos = jnp.concatenate([jnp.ones((ctx_len, QK_DIM // 2), F32), jnp.cos(ang)], axis=0)
    sin = jnp.concatenate([jnp.zeros((ctx_len, QK_DIM // 2), F32), jnp.sin(ang)], axis=0)
    reps = GW // QK_DIM
    return (jnp.tile(jnp.concatenate([cos, cos], axis=-1), (1, reps)),
            jnp.tile(jnp.concatenate([-sin, sin], axis=-1), (1, reps)))


def _regroup_w_in(w):
    d = w.shape[0]
    z = lambda n: jnp.zeros((d, n), w.dtype)
    return jnp.concatenate([w[:, :2048], w[:, 2048:2064], z(LANES - 16), w[:, 2320:2832], w[:, 2832:2864],
                            z(LANES - 32), w[:, 2064:2320], w[:, 2864:3120]], axis=1).astype(BF16)


def _gla_w2_blockdiag(w2):
    out = jnp.zeros((LANES, GW), F32)
    out = out.at[0:GLA_RANK, 0:GLA_QK].set(w2[0])
    return out.at[GLA_RANK:2 * GLA_RANK, GLA_QK:].set(w2[1])


def _layer(xs, mod3, lp, cos_t, sin_t, layer_idx, last, n_batch):
    b, t, d = xs.shape
    nt = t // TM
    pa, pb, pc, pcs, pd, pdl, pg = _inproj(xs, mod3, lp["norm1_g"], _regroup_w_in(lp["w_in"]), n_batch)

    ya = _conv_module(pa, lp["cm_conv_w"], lp["cm_conv_b"], lp["cm_ln_g"], lp["cm_ln_b"])

    lam_init = 0.8 - 0.6 * math.exp(-0.3 * layer_idx)
    q, kt, v = _attn_prep(pb, lp["da_qnorm_g"], lp["da_knorm_g"], cos_t, sin_t)
    yb = _attention(q, kt, v, lp["da_lambda"], lp["da_subln_g"], lam_init, 1, nt - 1, nt, t - TM)
    yb_first = 1
    if not last:
        yb_ctx = _attention(q, kt, v, lp["da_lambda"], lp["da_subln_g"], lam_init, 0, 1, 1, TM)
        yb = jnp.concatenate([yb_ctx, yb], axis=1)
        yb_first = 0

    dq, dk, dv, bg = _dn_prep(pc, pcs, lp["dn_conv_w"], lp["dn_a_log"], lp["dn_dt_bias"])
    ocf, ocr = _dn_scan(*_dn_local(dq, dk, dv, bg))

    qi, a, ke, vt, vb, de = _gla_local(pd, pdl, _gla_w2_blockdiag(lp["gla_w2"]), lp["gla_b2"].reshape(1, GW))
    odf, odr = _gla_scan(qi, a, ke, de, vt, vb)

    first_tile = 1 if last else 0
    x1 = _outproj(xs, ya, yb, ocf, ocr, odf, odr, pg, mod3, lp["dn_onorm_g"], lp["gla_onorm_g"],
                  lp["w_out"].astype(BF16), n_batch, first_tile, yb_first)

    w_up = lp["ffn_w_up"].astype(BF16)
    blocks = lambda m: jnp.transpose(m.reshape(m.shape[0], N_FF_BLK, FF_BLK), (1, 0, 2))
    cw = lp["ffn_conv_w"]
    return _ffn(x1, mod3, lp["norm2_g"], blocks(w_up[:, :D_FF]), blocks(w_up[:, D_FF:]),
                blocks(cw[:, :D_FF]), blocks(cw[:, D_FF:]),
                lp["ffn_w_down"].astype(BF16).reshape(N_FF_BLK, FF_BLK, d), n_batch, first_tile, nt)


def kernel(x, c, ctx, c_ctx, w_mod, b_mod, norm1_g, norm2_g, w_in, w_out, cm_conv_w, cm_conv_b, cm_ln_g, cm_ln_b, da_qnorm_g, da_knorm_g, da_lambda, da_subln_g, dn_conv_w, dn_a_log, dn_dt_bias, dn_onorm_g, gla_w2, gla_b2, gla_onorm_g, ffn_w_up, ffn_conv_w, ffn_w_down):
    n_batch, seq, d = x.shape
    ctx_len = ctx.shape[1]
    assert ctx_len == TM and seq % TM == 0 and d == D_MODEL
    depth = w_mod.shape[0]
    cos_t, sin_t = _rope_tables(seq, ctx_len)
    xs = jnp.concatenate([ctx, x], axis=1)
    mod_rows = 16
    c_rows = jnp.zeros((mod_rows, d), F32).at[:n_batch].set(c).at[n_batch].set(c_ctx)
    params = dict(w_mod=w_mod, b_mod=b_mod, norm1_g=norm1_g, norm2_g=norm2_g, w_in=w_in, w_out=w_out,
                  cm_conv_w=cm_conv_w, cm_conv_b=cm_conv_b, cm_ln_g=cm_ln_g, cm_ln_b=cm_ln_b,
                  da_qnorm_g=da_qnorm_g, da_knorm_g=da_knorm_g, da_lambda=da_lambda, da_subln_g=da_subln_g,
                  dn_conv_w=dn_conv_w, dn_a_log=dn_a_log, dn_dt_bias=dn_dt_bias, dn_onorm_g=dn_onorm_g,
                  gla_w2=gla_w2, gla_b2=gla_b2, gla_onorm_g=gla_onorm_g,
                  ffn_w_up=ffn_w_up, ffn_conv_w=ffn_conv_w, ffn_w_down=ffn_w_down)
    for l in range(depth):
        lp = {k: v[l] for k, v in params.items()}
        mod3 = _modulation(c_rows, lp["w_mod"], lp["b_mod"]).reshape(mod_rows * 6, 1, d)
        xs = _layer(xs, mod3, lp, cos_t, sin_t, l, l == depth - 1, n_batch)
    return xs
```

```python
import functools
import math

import jax
import jax.numpy as jnp
from jax import lax
from jax.experimental import pallas as pl
from jax.experimental.pallas import tpu as pltpu

F32 = jnp.float32
BF16 = jnp.bfloat16
HI = lax.Precision.HIGHEST
EPS = 1e-6

D_MODEL = 1024
GRID_W = 64
HEADS = 4
HEAD_DIM = 64
GW = 256
QK_DIM = 32
GLA_K = 32
GLA_RANK = 16
GLA_TAU = 16.0
CM_KERNEL = 31
DN_CONV = 5
ROPE_THETA = 10000.0
CH = 64
TM = 256
CPT = TM // CH
D_FF = 2816
FF_BLK = 256
N_FF_BLK = D_FF // FF_BLK
LANES = 128

IN_GROUPS = (("pa", 512), ("pb", 768), ("pc", 768), ("pcs", LANES), ("pd", 512), ("pdl", LANES), ("pg", 512))
IN_COLS_PAD = sum(w for _, w in IN_GROUPS)

VMEM_LIMIT = 56 * 1024 * 1024


def _cp(sem):
    return pltpu.CompilerParams(dimension_semantics=sem, vmem_limit_bytes=VMEM_LIMIT)


def _dot(a, b, prec=None):
    return jnp.dot(a, b, preferred_element_type=F32, precision=prec)


def _dot_nt(a, b, prec=None):
    return lax.dot_general(a, b, (((1,), (1,)), ((), ())), preferred_element_type=F32, precision=prec)


def _sigmoid(x):
    return 1.0 / (1.0 + jnp.exp(-x))


def _silu(x):
    return x * _sigmoid(x)


def _softplus(x):
    return jnp.maximum(x, 0.0) + jnp.log(1.0 + jnp.exp(-jnp.abs(x)))


def _iota(shape, dim):
    return lax.broadcasted_iota(jnp.int32, shape, dim)


def _group_ones(n, shift):
    return ((_iota((n, n), 0) >> shift) == (_iota((n, n), 1) >> shift)).astype(F32)


def _tile4(y):
    return jnp.concatenate([y, y, y, y], axis=0)


def _mod_kernel(c_ref, w_ref, b_ref, o_ref):
    o_ref[...] = _dot(_silu(c_ref[...]), w_ref[...], HI) + b_ref[...]


def _modulation(c_rows, w_mod, b_mod):
    r, d = c_rows.shape
    n = w_mod.shape[1] // d
    return pl.pallas_call(
        _mod_kernel,
        grid=(n,),
        in_specs=[pl.BlockSpec((r, d), lambda j: (0, 0)),
                  pl.BlockSpec((d, d), lambda j: (0, j)),
                  pl.BlockSpec((1, d), lambda j: (0, j))],
        out_specs=pl.BlockSpec((r, d), lambda j: (0, j)),
        out_shape=jax.ShapeDtypeStruct((r, n * d), F32),
        compiler_params=_cp(("arbitrary",)),
    )(c_rows, w_mod, b_mod.reshape(1, -1))


def _mod_spec(k, n_batch, first_tile):
    def imap(b, i):
        sel = jnp.where(i + first_tile == 0, n_batch, b)
        return (sel * 6 + k, 0, 0)
    return pl.BlockSpec((1, 1, D_MODEL), imap)


def _inproj_kernel(x_ref, sh_ref, sc_ref, g_ref, w_ref, *outs):
    x = x_ref[0]
    y = x * lax.rsqrt(jnp.mean(x * x, axis=-1, keepdims=True) + EPS) * g_ref[...]
    h = y * (1.0 + sc_ref[0]) + sh_ref[0]
    p = _dot(h.astype(BF16), w_ref[...])
    off = 0
    for o_ref, (_, width) in zip(outs, IN_GROUPS):
        o_ref[0] = p[:, off:off + width]
        off += width


def _inproj(xs, mod3, norm_g, w_in_r, n_batch):
    b, t, d = xs.shape
    row = lambda bb, i: (bb, i, 0)
    return pl.pallas_call(
        _inproj_kernel,
        grid=(b, t // TM),
        in_specs=[pl.BlockSpec((1, TM, d), row), _mod_spec(0, n_batch, 0), _mod_spec(1, n_batch, 0),
                  pl.BlockSpec((1, d), lambda bb, i: (0, 0)),
                  pl.BlockSpec((d, IN_COLS_PAD), lambda bb, i: (0, 0))],
        out_specs=[pl.BlockSpec((1, TM, w), row) for _, w in IN_GROUPS],
        out_shape=[jax.ShapeDtypeStruct((b, t, w), F32) for _, w in IN_GROUPS],
        compiler_params=_cp(("parallel", "arbitrary")),
    )(xs, mod3, mod3, norm_g.reshape(1, d), w_in_r)


def _halo_specs(width, halo, n_tiles):
    per = TM // halo
    left = pl.BlockSpec((1, halo, width), lambda b, i: (b, jnp.maximum(i * per - 1, 0), 0))
    right = pl.BlockSpec((1, halo, width), lambda b, i: (b, jnp.minimum((i + 1) * per, n_tiles * per - 1), 0))
    return left, right


def _halo_ok(i, n_tiles):
    return i >= 2, jnp.logical_and(i >= 1, i < n_tiles - 1)


CM_HALO = 16


def _convmod_kernel(pm_ref, pl_ref, pr_ref, cw_ref, cb_ref, lg_ref, lb_ref, o_ref, ext_ref):
    left_ok, right_ok = _halo_ok(pl.program_id(1), pl.num_programs(1))

    def glu(p):
        return p[:, :GW] * _sigmoid(p[:, GW:])

    ext_ref[0:CM_HALO] = jnp.where(left_ok, glu(pl_ref[0]), 0.0)
    ext_ref[CM_HALO:CM_HALO + TM] = glu(pm_ref[0])
    ext_ref[CM_HALO + TM:] = jnp.where(right_ok, glu(pr_ref[0]), 0.0)
    pad = CM_KERNEL // 2
    acc = jnp.zeros((TM, GW), F32)
    for j in range(CM_KERNEL):
        acc = acc + cw_ref[j:j + 1, :] * ext_ref[pl.ds(CM_HALO - pad + j, TM), :]
    y = acc + cb_ref[...]
    mu = jnp.mean(y, axis=-1, keepdims=True)
    yc = y - mu
    var = jnp.mean(yc * yc, axis=-1, keepdims=True)
    o_ref[0] = _silu(yc * lax.rsqrt(var + EPS) * lg_ref[...] + lb_ref[...])


def _conv_module(pa, conv_w, conv_b, ln_g, ln_b):
    b, t, w = pa.shape
    nt = t // TM
    left, right = _halo_specs(w, CM_HALO, nt)
    vec = pl.BlockSpec((1, GW), lambda bb, i: (0, 0))
    return pl.pallas_call(
        _convmod_kernel,
        grid=(b, nt),
        in_specs=[pl.BlockSpec((1, TM, w), lambda bb, i: (bb, i, 0)), left, right,
                  pl.BlockSpec((CM_KERNEL, GW), lambda bb, i: (0, 0)), vec, vec, vec],
        out_specs=pl.BlockSpec((1, TM, GW), lambda bb, i: (bb, i, 0)),
        out_shape=jax.ShapeDtypeStruct((b, t, GW), F32),
        scratch_shapes=[pltpu.VMEM((TM + 2 * CM_HALO, GW), F32)],
        compiler_params=_cp(("parallel", "arbitrary")),
    )(pa, pa, pa, conv_w, conv_b.reshape(1, GW), ln_g.reshape(1, GW), ln_b.reshape(1, GW))


N_MAPS = 2 * HEADS
V_EXT = 2 * HEAD_DIM


def _attn_prep_kernel(pb_ref, qg_ref, kg_ref, cos_ref, sin_ref, q_out, k_out, vt_out, qn_out, kn_out):
    p = pb_ref[0]
    ones32 = _group_ones(GW, 5)
    first = (_iota((TM, GW), 1) & (QK_DIM - 1)) < QK_DIM // 2
    cos = cos_ref[...]
    sin = sin_ref[...]

    def norm_rope(t, g):
        ms = _dot(t * t, ones32, HI) * (1.0 / QK_DIM)
        tn = t * lax.rsqrt(ms + EPS) * g
        partner = jnp.where(first, pltpu.roll(tn, GW - QK_DIM // 2, 1), pltpu.roll(tn, QK_DIM // 2, 1))
        return tn * cos + partner * sin

    qf = norm_rope(p[:, :GW], qg_ref[...]) * (QK_DIM ** -0.5)
    kf = norm_rope(p[:, GW:2 * GW], kg_ref[...])
    map_sel = ((_iota((N_MAPS, GW), 1) >> 5) == _iota((N_MAPS, GW), 0)).astype(F32)
    qn_out[0, 0] = _dot_nt(map_sel, qf * qf, HI)
    kn_out[0, 0] = _dot_nt(map_sel, kf * kf, HI)
    q = qf.astype(BF16)
    k = kf.astype(BF16)
    for g in range(N_MAPS):
        q_out[0, 0, g] = q[:, g * QK_DIM:(g + 1) * QK_DIM]
        k_out[0, 0, g] = k[:, g * QK_DIM:(g + 1) * QK_DIM]
    vt = p[:, 2 * GW:].T.astype(BF16)
    ones = jnp.ones((HEAD_DIM, TM), BF16)
    for h in range(HEADS):
        vt_out[0, 0, h, 0:HEAD_DIM, :] = vt[h * HEAD_DIM:(h + 1) * HEAD_DIM, :]
        vt_out[0, 0, h, HEAD_DIM:, :] = ones


def _attn_prep(pb, qn_g, kn_g, cos_t, sin_t):
    b, t, w = pb.shape
    nt = t // TM
    vec = pl.BlockSpec((1, GW), lambda bb, i: (0, 0))
    tab = pl.BlockSpec((TM, GW), lambda bb, i: (i, 0))
    reps = GW // QK_DIM
    qk_spec = pl.BlockSpec((1, 1, N_MAPS, TM, QK_DIM), lambda bb, i: (bb, i, 0, 0, 0))
    qk_shape = jax.ShapeDtypeStruct((b, nt, N_MAPS, TM, QK_DIM), BF16)
    n_spec = pl.BlockSpec((1, 1, N_MAPS, TM), lambda bb, i: (bb, i, 0, 0))
    n_shape = jax.ShapeDtypeStruct((b, nt, N_MAPS, TM), F32)
    return pl.pallas_call(
        _attn_prep_kernel,
        grid=(b, nt),
        in_specs=[pl.BlockSpec((1, TM, w), lambda bb, i: (bb, i, 0)), vec, vec, tab, tab],
        out_specs=[qk_spec, qk_spec,
                   pl.BlockSpec((1, 1, HEADS, V_EXT, TM), lambda bb, i: (bb, i, 0, 0, 0)), n_spec, n_spec],
        out_shape=[qk_shape, qk_shape, jax.ShapeDtypeStruct((b, nt, HEADS, V_EXT, TM), BF16), n_shape, n_shape],
        compiler_params=_cp(("parallel", "arbitrary")),
    )(pb, jnp.tile(qn_g, reps).reshape(1, GW), jnp.tile(kn_g, reps).reshape(1, GW), cos_t, sin_t)


SHIFT_MAX = 40.0


def _attn_kernel(q_ref, k_ref, vt_ref, qn_ref, kn_ref, lam_ref, sg_ref, o_ref, m_scr, acc_scr, s_scr, *, n_chunks, lam_init):
    acc_scr[...] = jnp.zeros(acc_scr.shape, F32)
    k2 = jnp.max(jnp.max(kn_ref[0], axis=0), axis=-1, keepdims=True)
    bound = jnp.sqrt(qn_ref[0, 0] * k2)
    safe = jnp.max(bound) <= SHIFT_MAX

    @pl.when(safe)
    def _():
        for g in range(N_MAPS):
            s_scr[0, g] = _dot_nt(k_ref[0, 0, g], q_ref[0, 0, g])

        def body(c, carry):
            cur = c & 1
            nxt = jnp.minimum(c + 1, n_chunks - 1)
            for g in range(N_MAPS):
                s_next = _dot_nt(k_ref[0, nxt, g], q_ref[0, 0, g])
                p = jnp.exp(s_scr[cur, g] - bound[g:g + 1]).astype(BF16)
                acc_scr[g] += _dot(vt_ref[0, c, g // 2], p)
                s_scr[1 - cur, g] = s_next
            return carry

        lax.fori_loop(0, n_chunks, body, 0)

    @pl.when(jnp.logical_not(safe))
    def _():
        m_scr[...] = jnp.full(m_scr.shape, -jnp.inf, F32)

        def body(c, carry):
            for g in range(N_MAPS):
                s = _dot_nt(k_ref[0, c, g], q_ref[0, 0, g])
                m_old = m_scr[g]
                m_new = jnp.maximum(m_old, jnp.max(s, axis=0, keepdims=True))
                p = jnp.exp(s - m_new).astype(BF16)
                acc_scr[g] = jnp.exp(m_old - m_new) * acc_scr[g] + _dot(vt_ref[0, c, g // 2], p)
                m_scr[g] = m_new
            return carry

        lax.fori_loop(0, n_chunks, body, 0)

    lp = lam_ref[...]
    lam = (jnp.exp(jnp.sum(lp[0:1] * lp[1:2], axis=-1, keepdims=True))
           - jnp.exp(jnp.sum(lp[2:3] * lp[3:4], axis=-1, keepdims=True)) + lam_init)
    outs = []
    for h in range(HEADS):
        a0 = acc_scr[2 * h]
        a1 = acc_scr[2 * h + 1]
        o = a0[:HEAD_DIM] / a0[HEAD_DIM:HEAD_DIM + 1] - lam * (a1[:HEAD_DIM] / a1[HEAD_DIM:HEAD_DIM + 1])
        outs.append(o * lax.rsqrt(jnp.mean(o * o, axis=0, keepdims=True) + EPS) * sg_ref[...] * (1.0 - lam_init))
    o_ref[0] = jnp.concatenate(outs, axis=0).T


def _attention(q, k, vt, qn, kn, lam_p, subln_g, lam_init, first_tile, n_q, n_kv, t_out):
    b = q.shape[0]
    kern = functools.partial(_attn_kernel, n_chunks=n_kv, lam_init=lam_init)
    return pl.pallas_call(
        kern,
        grid=(b, n_q),
        in_specs=[pl.BlockSpec((1, 1, N_MAPS, TM, QK_DIM), lambda bb, i: (bb, i + first_tile, 0, 0, 0)),
                  pl.BlockSpec((1, n_kv, N_MAPS, TM, QK_DIM), lambda bb, i: (bb, 0, 0, 0, 0)),
                  pl.BlockSpec((1, n_kv, HEADS, V_EXT, TM), lambda bb, i: (bb, 0, 0, 0, 0)),
                  pl.BlockSpec((1, 1, N_MAPS, TM), lambda bb, i: (bb, i + first_tile, 0, 0)),
                  pl.BlockSpec((1, n_kv, N_MAPS, TM), lambda bb, i: (bb, 0, 0, 0)),
                  pl.BlockSpec((4, QK_DIM), lambda bb, i: (0, 0)),
                  pl.BlockSpec((HEAD_DIM, 1), lambda bb, i: (0, 0))],
        out_specs=pl.BlockSpec((1, TM, GW), lambda bb, i: (bb, i, 0)),
        out_shape=jax.ShapeDtypeStruct((b, t_out, GW), F32),
        scratch_shapes=[pltpu.VMEM((N_MAPS, 1, TM), F32), pltpu.VMEM((N_MAPS, V_EXT, TM), F32),
                        pltpu.VMEM((2, N_MAPS, TM, TM), F32)],
        compiler_params=_cp(("parallel", "arbitrary")),
    )(q, k, vt, qn, kn, lam_p, subln_g.reshape(HEAD_DIM, 1))


def _chunk_masks():
    i = _iota((CH, GW), 0)
    j = _iota((CH, GW), 1) & (CH - 1)
    eye = (i == j).astype(F32)
    incl = (j <= i, j >= i)
    strict = (j < i, j > i)
    r = _iota((CH, CH), 0)
    c = _iota((CH, CH), 1)
    cum = ((c <= r).astype(F32), (c >= r).astype(F32))
    bd = (_iota((GW, GW), 0) >> 6) == (_iota((GW, GW), 1) >> 6)
    return eye, incl, strict, cum, bd


def _bd(y, bd):
    return jnp.where(bd, _tile4(y), 0.0).astype(BF16)


def _rev_tile(s, n_tiles):
    return jnp.where(s == 0, 0, n_tiles - s)


DN_HALO = 8


def _dn_prep_kernel(pm_ref, pl_ref, pr_ref, pcs_ref, cw_ref, alog_ref, dtb_ref, q_out, k_out, v_out, bg_out, ext_ref):
    left_ok, right_ok = _halo_ok(pl.program_id(1), pl.num_programs(1))
    ext_ref[0:DN_HALO] = jnp.where(left_ok, pl_ref[0], 0.0)
    ext_ref[DN_HALO:DN_HALO + TM] = pm_ref[0]
    ext_ref[DN_HALO + TM:] = jnp.where(right_ok, pr_ref[0], 0.0)
    pad = DN_CONV // 2
    acc = jnp.zeros((TM, 3 * GW), F32)
    for j in range(DN_CONV):
        acc = acc + cw_ref[j:j + 1, :] * ext_ref[pl.ds(DN_HALO - pad + j, TM), :]
    qkv = _silu(acc)
    ones64 = _group_ones(GW, 6)

    def l2n(t):
        return t * lax.rsqrt(_dot(t * t, ones64, HI) + EPS)

    q_out[0] = l2n(qkv[:, :GW]) * (HEAD_DIM ** -0.5)
    k_out[0] = l2n(qkv[:, GW:2 * GW])
    v_out[0] = qkv[:, 2 * GW:]
    s = pcs_ref[0]
    col = _iota(s.shape, 1)
    gate = -jnp.exp(alog_ref[...]) * _softplus(s + dtb_ref[...])
    bg_out[0] = jnp.where(col < 2 * HEADS, _sigmoid(s), jnp.where(col < 4 * HEADS, gate, 0.0))


def _dn_prep(pc, pcs, conv_w, a_log, dt_bias):
    b, t, w = pc.shape
    nt = t // TM
    left, right = _halo_specs(w, DN_HALO, nt)
    row = lambda bb, i: (bb, i, 0)
    pad_vec = lambda a: jnp.zeros((1, LANES), F32).at[0, 2 * HEADS:4 * HEADS].set(a.reshape(-1))
    vec = pl.BlockSpec((1, LANES), lambda bb, i: (0, 0))
    return pl.pallas_call(
        _dn_prep_kernel,
        grid=(b, nt),
        in_specs=[pl.BlockSpec((1, TM, w), row), left, right, pl.BlockSpec((1, TM, LANES), row),
                  pl.BlockSpec((DN_CONV, w), lambda bb, i: (0, 0)), vec, vec],
        out_specs=[pl.BlockSpec((1, TM, GW), row)] * 3 + [pl.BlockSpec((1, TM, LANES), row)],
        out_shape=[jax.ShapeDtypeStruct((b, t, GW), F32)] * 3 + [jax.ShapeDtypeStruct((b, t, LANES), F32)],
        scratch_shapes=[pltpu.VMEM((TM + 2 * DN_HALO, w), F32)],
        compiler_params=_cp(("parallel", "arbitrary")),
    )(pc, pc, pc, pcs, conv_w, pad_vec(a_log), pad_vec(dt_bias))


def _dn_local_kernel(q_ref, k_ref, v_ref, bg_ref, u_out, w_out, qi_out, a_out, ket_out, ge_out):
    eye, incl, strict, cum, bd = _chunk_masks()
    ones_cc = jnp.ones((CH, CH), F32)
    eye_bf = (_iota((GW, GW), 0) == _iota((GW, GW), 1)).astype(BF16)
    head_of_lane = _iota((LANES, GW), 1) >> 6
    src = _iota((LANES, GW), 0)
    for c in range(CPT):
        rows = slice(c * CH, (c + 1) * CH)
        q = q_ref[0, rows, :]
        k = k_ref[0, rows, :]
        v = v_ref[0, rows, :]
        bgc = bg_ref[0, rows, :]
        kbd = _bd(k, bd)
        for d in range(2):
            sel_b = (src == d * HEADS + head_of_lane).astype(F32)
            sel_g = (src == 2 * HEADS + d * HEADS + head_of_lane).astype(F32)
            beta = _dot(bgc, sel_b, HI)
            gcum = _dot(_dot(cum[d], bgc, HI), sel_g, HI)
            grow = _dot(ones_cc, gcum * eye, HI)
            gtot = gcum[CH - 1:CH] if d == 0 else gcum[0:1]
            decay = jnp.where(incl[d], jnp.exp(jnp.where(incl[d], gcum - grow, 0.0)), 0.0)
            kb = k * beta
            aq = _dot_nt(jnp.concatenate([kb, q], axis=0).astype(BF16), kbd)
            a = jnp.where(strict[d], aq[:CH] * decay, 0.0)
            a_qk = jnp.where(incl[d], aq[CH:] * decay, 0.0)
            t_inv = eye - a
            p = _dot(a.astype(BF16), _bd(a, bd))
            for it in range(5):
                pbd = _bd(p, bd)
                if it < 4:
                    r = _dot(jnp.concatenate([t_inv, p], axis=0).astype(BF16), pbd)
                    t_inv = t_inv + r[:CH]
                    p = r[CH:]
                else:
                    t_inv = t_inv + _dot(t_inv.astype(BF16), pbd)
            tb = t_inv.astype(BF16)
            egc = jnp.exp(gcum)
            u_out[0, d, rows, :] = _dot(tb, _bd(v * beta, bd))
            w_out[0, d, rows, :] = _dot(tb, _bd(kb * egc, bd)).astype(BF16)
            qi_out[0, d, rows, :] = (q * egc).astype(BF16)
            a_out[0, d, rows, :] = a_qk.astype(BF16)
            k_end = (k * jnp.exp(gtot - gcum)).astype(BF16)
            ket_out[0, d, c] = _dot_nt(eye_bf, k_end).astype(BF16)
            ge_out[0, d, c] = jnp.exp(gtot)


def _dn_local(q, k, v, bg):
    b, t, _ = q.shape
    nt = t // TM
    row = lambda bb, i: (bb, i, 0)
    drow = pl.BlockSpec((1, 2, TM, GW), lambda bb, i: (bb, 0, i, 0))
    return pl.pallas_call(
        _dn_local_kernel,
        grid=(b, nt),
        in_specs=[pl.BlockSpec((1, TM, GW), row)] * 3 + [pl.BlockSpec((1, TM, LANES), row)],
        out_specs=[drow, drow, drow, drow,
                   pl.BlockSpec((1, 2, CPT, GW, CH), lambda bb, i: (bb, 0, i, 0, 0)),
                   pl.BlockSpec((1, 2, CPT, 1, GW), lambda bb, i: (bb, 0, i, 0, 0))],
        out_shape=[jax.ShapeDtypeStruct((b, 2, t, GW), F32)] + [jax.ShapeDtypeStruct((b, 2, t, GW), BF16)] * 3
        + [jax.ShapeDtypeStruct((b, 2, t // CH, GW, CH), BF16), jax.ShapeDtypeStruct((b, 2, t // CH, 1, GW), F32)],
        compiler_params=_cp(("parallel", "arbitrary")),
    )(q, k, v, bg)


def _dn_scan_kernel(uf, wf, qf, af, kf, gf, ur, wr, qr, ar, kr, gr, of_ref, or_ref, s_scr):
    @pl.when(pl.program_id(1) == 0)
    def _():
        s_scr[...] = jnp.zeros(s_scr.shape, F32)

    bd = (_iota((GW, GW), 0) >> 6) == (_iota((GW, GW), 1) >> 6)
    dirs = ((uf, wf, qf, af, kf, gf, of_ref), (ur, wr, qr, ar, kr, gr, or_ref))
    for c in range(CPT):
        for d, (u, w, qi, a, ket, ge, o_ref) in enumerate(dirs):
            cc = c if d == 0 else CPT - 1 - c
            rows = slice(cc * CH, (cc + 1) * CH)
            s = s_scr[d]
            wq = _dot(jnp.concatenate([w[0, 0, rows, :], qi[0, 0, rows, :]], axis=0), s.astype(BF16))
            v_new = u[0, 0, rows, :] - wq[:CH]
            o_ref[0, rows, :] = wq[CH:] + _dot(a[0, 0, rows, :], _bd(v_new, bd))
            s_scr[d] = s * ge[0, 0, cc] + jnp.where(bd, _dot(ket[0, 0, cc], v_new.astype(BF16)), 0.0)


def _dir_specs(shape_tail, n_tiles, chunked):
    blk = (1, 1, CPT if chunked else TM) + shape_tail
    zeros = (0,) * len(shape_tail)
    fwd = pl.BlockSpec(blk, lambda b, s: (b, 0, s) + zeros)
    rev = pl.BlockSpec(blk, lambda b, s: (b, 1, _rev_tile(s, n_tiles)) + zeros)
    return fwd, rev


def _dn_scan(u, w, qi, a, ket, ge):
    b, _, t, _ = u.shape
    nt = t // TM
    rowf, rowr = _dir_specs((GW,), nt, False)
    ketf, ketr = _dir_specs((GW, CH), nt, True)
    gef, ger = _dir_specs((1, GW), nt, True)
    return pl.pallas_call(
        _dn_scan_kernel,
        grid=(b, nt),
        in_specs=[rowf, rowf, rowf, rowf, ketf, gef, rowr, rowr, rowr, rowr, ketr, ger],
        out_specs=[pl.BlockSpec((1, TM, GW), lambda bb, s: (bb, s, 0)),
                   pl.BlockSpec((1, TM, GW), lambda bb, s: (bb, _rev_tile(s, nt), 0))],
        out_shape=[jax.ShapeDtypeStruct((b, t, GW), F32)] * 2,
        scratch_shapes=[pltpu.VMEM((2, GW, GW), F32)],
        compiler_params=_cp(("parallel", "arbitrary")),
    )(u, w, qi, a, ket, ge, u, w, qi, a, ket, ge)


GLA_QK = HEADS * GLA_K


def _gla_local_kernel(pd_ref, pdl_ref, w2_ref, b2_ref, qi_out, a_out, ke_out, vt_out, vb_out, de_out):
    _, incl, _, cum, _ = _chunk_masks()
    eye_bf = (_iota((GW, GW), 0) == _iota((GW, GW), 1)).astype(BF16)
    bdk = (_iota((GW, GLA_QK), 0) >> 6) == (_iota((GW, GLA_QK), 1) >> 5)
    z = _dot(pdl_ref[0], w2_ref[...], HI) + b2_ref[...]
    gk_all = -_softplus(-z) * (1.0 / GLA_TAU)
    for c in range(CPT):
        rows = slice(c * CH, (c + 1) * CH)
        p = pd_ref[0, rows, :]
        q = p[:, :GLA_QK] * (GLA_K ** -0.5)
        k = p[:, GLA_QK:2 * GLA_QK]
        vb = p[:, 2 * GLA_QK:].astype(BF16)
        vb_out[0, rows, :] = vb
        vt_out[0, c] = _dot_nt(eye_bf, vb).astype(BF16)
        for d in range(2):
            bcs = _dot(cum[d], gk_all[rows, d * GLA_QK:(d + 1) * GLA_QK], HI)
            bend = bcs[CH - 1:CH] if d == 0 else bcs[0:1]
            q_in = (q * jnp.exp(bcs)).astype(BF16)
            kdec = jnp.where(bdk, _tile4(k * jnp.exp(-bcs)), 0.0).astype(BF16)
            a_out[0, d, rows, :] = jnp.where(incl[d], _dot_nt(q_in, kdec), 0.0).astype(BF16)
            qi_out[0, d, rows, :] = q_in
            ke_out[0, d, rows, :] = (k * jnp.exp(bend - bcs)).astype(BF16)
            de_out[0, d, c] = jnp.exp(bend)


def _gla_local(pd, pdl, w2bd, b2):
    b, t, w = pd.shape
    nt = t // TM
    row = lambda bb, i: (bb, i, 0)
    return pl.pallas_call(
        _gla_local_kernel,
        grid=(b, nt),
        in_specs=[pl.BlockSpec((1, TM, w), row), pl.BlockSpec((1, TM, LANES), row),
                  pl.BlockSpec((LANES, GW), lambda bb, i: (0, 0)), pl.BlockSpec((1, GW), lambda bb, i: (0, 0))],
        out_specs=[pl.BlockSpec((1, 2, TM, GLA_QK), lambda bb, i: (bb, 0, i, 0)),
                   pl.BlockSpec((1, 2, TM, GW), lambda bb, i: (bb, 0, i, 0)),
                   pl.BlockSpec((1, 2, TM, GLA_QK), lambda bb, i: (bb, 0, i, 0)),
                   pl.BlockSpec((1, CPT, GW, CH), lambda bb, i: (bb, i, 0, 0)),
                   pl.BlockSpec((1, TM, GW), row),
                   pl.BlockSpec((1, 2, CPT, 1, GLA_QK), lambda bb, i: (bb, 0, i, 0, 0))],
        out_shape=[jax.ShapeDtypeStruct((b, 2, t, GLA_QK), BF16), jax.ShapeDtypeStruct((b, 2, t, GW), BF16),
                   jax.ShapeDtypeStruct((b, 2, t, GLA_QK), BF16), jax.ShapeDtypeStruct((b, t // CH, GW, CH), BF16),
                   jax.ShapeDtypeStruct((b, t, GW), BF16), jax.ShapeDtypeStruct((b, 2, t // CH, 1, GLA_QK), F32)],
        compiler_params=_cp(("parallel", "arbitrary")),
    )(pd, pdl, w2bd, b2)


def _gla_scan_kernel(qf, af, kf, df, vtf, vf, qr, ar, kr, dr, vtr, vr, of_ref, or_ref, s_scr):
    @pl.when(pl.program_id(1) == 0)
    def _():
        s_scr[...] = jnp.zeros(s_scr.shape, F32)

    bd = (_iota((GW, GW), 0) >> 6) == (_iota((GW, GW), 1) >> 6)
    bdt = (_iota((GW, GLA_QK), 0) >> 6) == (_iota((GW, GLA_QK), 1) >> 5)
    dirs = ((qf, af, kf, df, vtf, vf, of_ref), (qr, ar, kr, dr, vtr, vr, or_ref))
    for c in range(CPT):
        for d, (qi, a, ke, de, vt, v, o_ref) in enumerate(dirs):
            cc = c if d == 0 else CPT - 1 - c
            rows = slice(cc * CH, (cc + 1) * CH)
            st = s_scr[d]
            vbd = jnp.where(bd, _tile4(v[0, rows, :]), jnp.zeros((), BF16))
            o_ref[0, rows, :] = _dot_nt(qi[0, 0, rows, :], st.astype(BF16)) + _dot(a[0, 0, rows, :], vbd)
            s_scr[d] = st * de[0, 0, cc] + jnp.where(bdt, _dot(vt[0, cc], ke[0, 0, rows, :]), 0.0)


def _gla_scan(qi, a, ke, de, vt, vb):
    b, _, t, _ = a.shape
    nt = t // TM
    qf, qr = _dir_specs((GLA_QK,), nt, False)
    af, ar = _dir_specs((GW,), nt, False)
    df, dr = _dir_specs((1, GLA_QK), nt, True)
    vtf = pl.BlockSpec((1, CPT, GW, CH), lambda bb, s: (bb, s, 0, 0))
    vtr = pl.BlockSpec((1, CPT, GW, CH), lambda bb, s: (bb, _rev_tile(s, nt), 0, 0))
    vf = pl.BlockSpec((1, TM, GW), lambda bb, s: (bb, s, 0))
    vr = pl.BlockSpec((1, TM, GW), lambda bb, s: (bb, _rev_tile(s, nt), 0))
    return pl.pallas_call(
        _gla_scan_kernel,
        grid=(b, nt),
        in_specs=[qf, af, qf, df, vtf, vf, qr, ar, qr, dr, vtr, vr],
        out_specs=[vf, vr],
        out_shape=[jax.ShapeDtypeStruct((b, t, GW), F32)] * 2,
        scratch_shapes=[pltpu.VMEM((2, GW, GLA_QK), F32)],
        compiler_params=_cp(("parallel", "arbitrary")),
    )(qi, a, ke, de, vt, vb, qi, a, ke, de, vt, vb)


def _outproj_kernel(x_ref, ya_ref, yb_ref, cf_ref, cr_ref, df_ref, dr_ref, pg_ref, g1_ref, gc_ref, gd_ref, w_ref, o_ref):
    ones64 = _group_ones(GW, 6)

    def fin(o, g, gate):
        ms = _dot(o * o, ones64, HI) * (1.0 / HEAD_DIM)
        return (o * lax.rsqrt(ms + EPS) * g * _silu(gate)).astype(BF16)

    pg = pg_ref[0]
    yc = fin(cf_ref[0] + cr_ref[0], gc_ref[...], pg[:, :GW])
    yd = fin(df_ref[0] + dr_ref[0], gd_ref[...], pg[:, GW:])
    res = (_dot(ya_ref[0].astype(BF16), w_ref[0:GW, :]) + _dot(yb_ref[0].astype(BF16), w_ref[GW:2 * GW, :])
           + _dot(yc, w_ref[2 * GW:3 * GW, :]) + _dot(yd, w_ref[3 * GW:, :]))
    o_ref[0] = x_ref[0] + g1_ref[0] * res


def _outproj(xs, ya, yb, ocf, ocr, odf, odr, pg, mod3, dn_g, gla_g, w_out_bf, n_batch, first_tile, yb_first):
    b, t, d = xs.shape
    n = t // TM - first_tile
    row = lambda bb, i: (bb, i + first_tile, 0)
    g256 = pl.BlockSpec((1, TM, GW), row)
    vec = pl.BlockSpec((1, GW), lambda bb, i: (0, 0))
    reps = GW // HEAD_DIM
    return pl.pallas_call(
        _outproj_kernel,
        grid=(b, n),
        in_specs=[pl.BlockSpec((1, TM, d), row), g256,
                  pl.BlockSpec((1, TM, GW), lambda bb, i: (bb, i + first_tile - yb_first, 0)),
                  g256, g256, g256, g256, pl.BlockSpec((1, TM, 2 * GW), row),
                  _mod_spec(2, n_batch, first_tile), vec, vec, pl.BlockSpec((d, d), lambda bb, i: (0, 0))],
        out_specs=pl.BlockSpec((1, TM, d), lambda bb, i: (bb, i, 0)),
        out_shape=jax.ShapeDtypeStruct((b, n * TM, d), F32),
        compiler_params=_cp(("parallel", "arbitrary")),
    )(xs, ya, yb, ocf, ocr, odf, odr, pg, mod3, jnp.tile(dn_g, reps).reshape(1, GW),
      jnp.tile(gla_g, reps).reshape(1, GW), w_out_bf)


FF_HALO = 8


def _ffn_kernel(xm_ref, xl_ref, xr_ref, sh_ref, sc_ref, g2_ref, ng_ref, wa_ref, wg_ref, cwa_ref, cwg_ref, wd_ref,
                o_ref, ext_ref, *, first_tile, n_tiles):
    left_ok, right_ok = _halo_ok(pl.program_id(1) + first_tile, n_tiles)
    rows_ext = _iota((TM + 2 * FF_HALO, 1), 0)
    keep = jnp.logical_and(jnp.logical_or(rows_ext >= FF_HALO, left_ok),
                           jnp.logical_or(rows_ext < FF_HALO + TM, right_ok)).astype(F32)
    x = jnp.concatenate([xl_ref[0], xm_ref[0], xr_ref[0]], axis=0)
    y = x * lax.rsqrt(jnp.mean(x * x, axis=-1, keepdims=True) + EPS) * ng_ref[...]
    h = ((y * (1.0 + sc_ref[0]) + sh_ref[0]) * keep).astype(BF16)
    acc = jnp.zeros((TM, D_MODEL), F32)
    for j in range(N_FF_BLK):
        def conv(w_ref, cw_ref, half):
            ext_ref[half] = _dot(h, w_ref[j])
            cw = cw_ref[j]
            return (cw[0:1] * ext_ref[half, pl.ds(FF_HALO - 1, TM), :] + cw[1:2] * ext_ref[half, pl.ds(FF_HALO, TM), :]
                    + cw[2:3] * ext_ref[half, pl.ds(FF_HALO + 1, TM), :])
        a = conv(wa_ref, cwa_ref, 0)
        g = conv(wg_ref, cwg_ref, 1)
        acc = acc + _dot((_silu(g) * a).astype(BF16), wd_ref[j])
    o_ref[0] = xm_ref[0] + g2_ref[0] * acc


def _ffn(x1, mod3, norm_g, wa, wg, cwa, cwg, wd, n_batch, first_tile, n_tiles_total):
    b, t, d = x1.shape
    n = t // TM
    per = TM // FF_HALO
    left = pl.BlockSpec((1, FF_HALO, d), lambda bb, i: (bb, jnp.maximum(i * per - 1, 0), 0))
    right = pl.BlockSpec((1, FF_HALO, d), lambda bb, i: (bb, jnp.minimum((i + 1) * per, n * per - 1), 0))
    const3 = lambda bb, i: (0, 0, 0)
    kern = functools.partial(_ffn_kernel, first_tile=first_tile, n_tiles=n_tiles_total)
    return pl.pallas_call(
        kern,
        grid=(b, n),
        in_specs=[pl.BlockSpec((1, TM, d), lambda bb, i: (bb, i, 0)), left, right,
                  _mod_spec(3, n_batch, first_tile), _mod_spec(4, n_batch, first_tile), _mod_spec(5, n_batch, first_tile),
                  pl.BlockSpec((1, d), lambda bb, i: (0, 0)),
                  pl.BlockSpec((N_FF_BLK, d, FF_BLK), const3), pl.BlockSpec((N_FF_BLK, d, FF_BLK), const3),
                  pl.BlockSpec((N_FF_BLK, 3, FF_BLK), const3), pl.BlockSpec((N_FF_BLK, 3, FF_BLK), const3),
                  pl.BlockSpec((N_FF_BLK, FF_BLK, d), const3)],
        out_specs=pl.BlockSpec((1, TM, d), lambda bb, i: (bb, i, 0)),
        out_shape=jax.ShapeDtypeStruct((b, t, d), F32),
        scratch_shapes=[pltpu.VMEM((2, TM + 2 * FF_HALO, FF_BLK), F32)],
        compiler_params=_cp(("parallel", "arbitrary")),
    )(x1, x1, x1, mod3, mod3, mod3, norm_g.reshape(1, d), wa, wg, cwa, cwg, wd)


def _rope_tables(seq, ctx_len):
    rows = seq // GRID_W
    row = jnp.repeat(jnp.arange(rows, dtype=F32), GRID_W)
    col = jnp.tile(jnp.arange(GRID_W, dtype=F32), rows)
    nf = QK_DIM // 4
    inv = ROPE_THETA ** (-jnp.arange(nf, dtype=F32) / nf)
    ang = jnp.concatenate([row[:, None] * inv, col[:, None] * inv], axis=-1)
    cos = jnp.concatenate([jnp.ones((ctx_len, QK_DIM // 2), F32), jnp.cos(ang)], axis=0)
    sin = jnp.concatenate([jnp.zeros((ctx_len, QK_DIM // 2), F32), jnp.sin(ang)], axis=0)
    reps = GW // QK_DIM
    return (jnp.tile(jnp.concatenate([cos, cos], axis=-1), (1, reps)),
            jnp.tile(jnp.concatenate([-sin, sin], axis=-1), (1, reps)))


def _regroup_w_in(w):
    d = w.shape[0]
    z = lambda n: jnp.zeros((d, n), w.dtype)
    return jnp.concatenate([w[:, :2048], w[:, 2048:2064], z(LANES - 16), w[:, 2320:2832], w[:, 2832:2864],
                            z(LANES - 32), w[:, 2064:2320], w[:, 2864:3120]], axis=1).astype(BF16)


def _gla_w2_blockdiag(w2):
    out = jnp.zeros((LANES, GW), F32)
    out = out.at[0:GLA_RANK, 0:GLA_QK].set(w2[0])
    return out.at[GLA_RANK:2 * GLA_RANK, GLA_QK:].set(w2[1])


def _layer(xs, mod3, lp, cos_t, sin_t, layer_idx, last, n_batch):
    b, t, d = xs.shape
    nt = t // TM
    pa, pb, pc, pcs, pd, pdl, pg = _inproj(xs, mod3, lp["norm1_g"], _regroup_w_in(lp["w_in"]), n_batch)

    ya = _conv_module(pa, lp["cm_conv_w"], lp["cm_conv_b"], lp["cm_ln_g"], lp["cm_ln_b"])

    lam_init = 0.8 - 0.6 * math.exp(-0.3 * layer_idx)
    q, k, vt, qn, kn = _attn_prep(pb, lp["da_qnorm_g"], lp["da_knorm_g"], cos_t, sin_t)
    yb = _attention(q, k, vt, qn, kn, lp["da_lambda"], lp["da_subln_g"], lam_init, 1, nt - 1, nt, t - TM)
    yb_first = 1
    if not last:
        yb_ctx = _attention(q, k, vt, qn, kn, lp["da_lambda"], lp["da_subln_g"], lam_init, 0, 1, 1, TM)
        yb = jnp.concatenate([yb_ctx, yb], axis=1)
        yb_first = 0

    dq, dk, dv, bg = _dn_prep(pc, pcs, lp["dn_conv_w"], lp["dn_a_log"], lp["dn_dt_bias"])
    ocf, ocr = _dn_scan(*_dn_local(dq, dk, dv, bg))

    qi, a, ke, vt, vb, de = _gla_local(pd, pdl, _gla_w2_blockdiag(lp["gla_w2"]), lp["gla_b2"].reshape(1, GW))
    odf, odr = _gla_scan(qi, a, ke, de, vt, vb)

    first_tile = 1 if last else 0
    x1 = _outproj(xs, ya, yb, ocf, ocr, odf, odr, pg, mod3, lp["dn_onorm_g"], lp["gla_onorm_g"],
                  lp["w_out"].astype(BF16), n_batch, first_tile, yb_first)

    w_up = lp["ffn_w_up"].astype(BF16)
    blocks = lambda m: jnp.transpose(m.reshape(m.shape[0], N_FF_BLK, FF_BLK), (1, 0, 2))
    cw = lp["ffn_conv_w"]
    return _ffn(x1, mod3, lp["norm2_g"], blocks(w_up[:, :D_FF]), blocks(w_up[:, D_FF:]),
                blocks(cw[:, :D_FF]), blocks(cw[:, D_FF:]),
                lp["ffn_w_down"].astype(BF16).reshape(N_FF_BLK, FF_BLK, d), n_batch, first_tile, nt)


def kernel(x, c, ctx, c_ctx, w_mod, b_mod, norm1_g, norm2_g, w_in, w_out, cm_conv_w, cm_conv_b, cm_ln_g, cm_ln_b, da_qnorm_g, da_knorm_g, da_lambda, da_subln_g, dn_conv_w, dn_a_log, dn_dt_bias, dn_onorm_g, gla_w2, gla_b2, gla_onorm_g, ffn_w_up, ffn_conv_w, ffn_w_down):
    n_batch, seq, d = x.shape
    ctx_len = ctx.shape[1]
    assert ctx_len == TM and seq % TM == 0 and d == D_MODEL
    depth = w_mod.shape[0]
    cos_t, sin_t = _rope_tables(seq, ctx_len)
    xs = jnp.concatenate([ctx, x], axis=1)
    mod_rows = 16
    c_rows = jnp.zeros((mod_rows, d), F32).at[:n_batch].set(c).at[n_batch].set(c_ctx)
    params = dict(w_mod=w_mod, b_mod=b_mod, norm1_g=norm1_g, norm2_g=norm2_g, w_in=w_in, w_out=w_out,
                  cm_conv_w=cm_conv_w, cm_conv_b=cm_conv_b, cm_ln_g=cm_ln_g, cm_ln_b=cm_ln_b,
                  da_qnorm_g=da_qnorm_g, da_knorm_g=da_knorm_g, da_lambda=da_lambda, da_subln_g=da_subln_g,
                  dn_conv_w=dn_conv_w, dn_a_log=dn_a_log, dn_dt_bias=dn_dt_bias, dn_onorm_g=dn_onorm_g,
                  gla_w2=gla_w2, gla_b2=gla_b2, gla_onorm_g=gla_onorm_g,
                  ffn_w_up=ffn_w_up, ffn_conv_w=ffn_conv_w, ffn_w_down=ffn_w_down)
    for l in range(depth):
        lp = {k: v[l] for k, v in params.items()}
        mod3 = _modulation(c_rows, lp["w_mod"], lp["b_mod"]).reshape(mod_rows * 6, 1, d)
        xs = _layer(xs, mod3, lp, cos_t, sin_t, l, l == depth - 1, n_batch)
    return xs
```

```python
import functools
import math

import jax
import jax.numpy as jnp
from jax import lax
from jax.experimental import pallas as pl
from jax.experimental.pallas import tpu as pltpu

F32 = jnp.float32
BF16 = jnp.bfloat16
HI = lax.Precision.HIGHEST
EPS = 1e-6

D_MODEL = 1024
GRID_W = 64
HEADS = 4
HEAD_DIM = 64
GW = 256
QK_DIM = 32
GLA_K = 32
GLA_RANK = 16
GLA_TAU = 16.0
CM_KERNEL = 31
DN_CONV = 5
ROPE_THETA = 10000.0
CH = 64
TM = 256
CPT = TM // CH
D_FF = 2816
FF_BLK = 256
N_FF_BLK = D_FF // FF_BLK
LANES = 128

IN_GROUPS = (("pa", 512), ("pb", 768), ("pc", 768), ("pcs", LANES), ("pd", 512), ("pdl", LANES), ("pg", 512))
IN_COLS_PAD = sum(w for _, w in IN_GROUPS)

VMEM_LIMIT = 56 * 1024 * 1024


def _cp(sem):
    return pltpu.CompilerParams(dimension_semantics=sem, vmem_limit_bytes=VMEM_LIMIT)


def _dot(a, b, prec=None):
    return jnp.dot(a, b, preferred_element_type=F32, precision=prec)


def _dot_nt(a, b, prec=None):
    return lax.dot_general(a, b, (((1,), (1,)), ((), ())), preferred_element_type=F32, precision=prec)


def _sigmoid(x):
    return 1.0 / (1.0 + jnp.exp(-x))


def _silu(x):
    return x * _sigmoid(x)


def _softplus(x):
    return jnp.maximum(x, 0.0) + jnp.log(1.0 + jnp.exp(-jnp.abs(x)))


def _iota(shape, dim):
    return lax.broadcasted_iota(jnp.int32, shape, dim)


def _group_ones(n, shift):
    return ((_iota((n, n), 0) >> shift) == (_iota((n, n), 1) >> shift)).astype(F32)


def _tile4(y):
    return jnp.concatenate([y, y, y, y], axis=0)


def _mod_kernel(c_ref, w_ref, b_ref, o_ref):
    o_ref[...] = _dot(_silu(c_ref[...]), w_ref[...], HI) + b_ref[...]


def _modulation(c_rows, w_mod, b_mod):
    r, d = c_rows.shape
    n = w_mod.shape[1] // d
    return pl.pallas_call(
        _mod_kernel,
        grid=(n,),
        in_specs=[pl.BlockSpec((r, d), lambda j: (0, 0)),
                  pl.BlockSpec((d, d), lambda j: (0, j)),
                  pl.BlockSpec((1, d), lambda j: (0, j))],
        out_specs=pl.BlockSpec((r, d), lambda j: (0, j)),
        out_shape=jax.ShapeDtypeStruct((r, n * d), F32),
        compiler_params=_cp(("arbitrary",)),
    )(c_rows, w_mod, b_mod.reshape(1, -1))


def _mod_spec(k, n_batch, first_tile):
    def imap(b, i):
        sel = jnp.where(i + first_tile == 0, n_batch, b)
        return (sel * 6 + k, 0, 0)
    return pl.BlockSpec((1, 1, D_MODEL), imap)


def _inproj_kernel(x_ref, sh_ref, sc_ref, g_ref, w_ref, *outs):
    x = x_ref[0]
    y = x * lax.rsqrt(jnp.mean(x * x, axis=-1, keepdims=True) + EPS) * g_ref[...]
    h = y * (1.0 + sc_ref[0]) + sh_ref[0]
    p = _dot(h.astype(BF16), w_ref[...])
    off = 0
    for o_ref, (_, width) in zip(outs, IN_GROUPS):
        o_ref[0] = p[:, off:off + width]
        off += width


def _inproj(xs, mod3, norm_g, w_in_r, n_batch):
    b, t, d = xs.shape
    row = lambda bb, i: (bb, i, 0)
    return pl.pallas_call(
        _inproj_kernel,
        grid=(b, t // TM),
        in_specs=[pl.BlockSpec((1, TM, d), row), _mod_spec(0, n_batch, 0), _mod_spec(1, n_batch, 0),
                  pl.BlockSpec((1, d), lambda bb, i: (0, 0)),
                  pl.BlockSpec((d, IN_COLS_PAD), lambda bb, i: (0, 0))],
        out_specs=[pl.BlockSpec((1, TM, w), row) for _, w in IN_GROUPS],
        out_shape=[jax.ShapeDtypeStruct((b, t, w), F32) for _, w in IN_GROUPS],
        compiler_params=_cp(("parallel", "arbitrary")),
    )(xs, mod3, mod3, norm_g.reshape(1, d), w_in_r)


def _halo_specs(width, halo, n_tiles):
    per = TM // halo
    left = pl.BlockSpec((1, halo, width), lambda b, i: (b, jnp.maximum(i * per - 1, 0), 0))
    right = pl.BlockSpec((1, halo, width), lambda b, i: (b, jnp.minimum((i + 1) * per, n_tiles * per - 1), 0))
    return left, right


def _halo_ok(i, n_tiles):
    return i >= 2, jnp.logical_and(i >= 1, i < n_tiles - 1)


CM_HALO = 16


def _convmod_kernel(pm_ref, pl_ref, pr_ref, cw_ref, cb_ref, lg_ref, lb_ref, o_ref, ext_ref):
    left_ok, right_ok = _halo_ok(pl.program_id(1), pl.num_programs(1))

    def glu(p):
        return p[:, :GW] * _sigmoid(p[:, GW:])

    ext_ref[0:CM_HALO] = jnp.where(left_ok, glu(pl_ref[0]), 0.0)
    ext_ref[CM_HALO:CM_HALO + TM] = glu(pm_ref[0])
    ext_ref[CM_HALO + TM:] = jnp.where(right_ok, glu(pr_ref[0]), 0.0)
    pad = CM_KERNEL // 2
    acc = jnp.zeros((TM, GW), F32)
    for j in range(CM_KERNEL):
        acc = acc + cw_ref[j:j + 1, :] * ext_ref[pl.ds(CM_HALO - pad + j, TM), :]
    y = acc + cb_ref[...]
    mu = jnp.mean(y, axis=-1, keepdims=True)
    yc = y - mu
    var = jnp.mean(yc * yc, axis=-1, keepdims=True)
    o_ref[0] = _silu(yc * lax.rsqrt(var + EPS) * lg_ref[...] + lb_ref[...])


def _conv_module(pa, conv_w, conv_b, ln_g, ln_b):
    b, t, w = pa.shape
    nt = t // TM
    left, right = _halo_specs(w, CM_HALO, nt)
    vec = pl.BlockSpec((1, GW), lambda bb, i: (0, 0))
    return pl.pallas_call(
        _convmod_kernel,
        grid=(b, nt),
        in_specs=[pl.BlockSpec((1, TM, w), lambda bb, i: (bb, i, 0)), left, right,
                  pl.BlockSpec((CM_KERNEL, GW), lambda bb, i: (0, 0)), vec, vec, vec],
        out_specs=pl.BlockSpec((1, TM, GW), lambda bb, i: (bb, i, 0)),
        out_shape=jax.ShapeDtypeStruct((b, t, GW), F32),
        scratch_shapes=[pltpu.VMEM((TM + 2 * CM_HALO, GW), F32)],
        compiler_params=_cp(("parallel", "arbitrary")),
    )(pa, pa, pa, conv_w, conv_b.reshape(1, GW), ln_g.reshape(1, GW), ln_b.reshape(1, GW))


N_MAPS = 2 * HEADS
V_EXT = 2 * HEAD_DIM


def _attn_prep_kernel(pb_ref, qg_ref, kg_ref, cos_ref, sin_ref, q_out, k_out, vt_out, qn_out, kn_out):
    p = pb_ref[0]
    ones32 = _group_ones(GW, 5)
    first = (_iota((TM, GW), 1) & (QK_DIM - 1)) < QK_DIM // 2
    cos = cos_ref[...]
    sin = sin_ref[...]

    def norm_rope(t, g):
        ms = _dot(t * t, ones32, HI) * (1.0 / QK_DIM)
        tn = t * lax.rsqrt(ms + EPS) * g
        partner = jnp.where(first, pltpu.roll(tn, GW - QK_DIM // 2, 1), pltpu.roll(tn, QK_DIM // 2, 1))
        return tn * cos + partner * sin

    qf = norm_rope(p[:, :GW], qg_ref[...]) * (QK_DIM ** -0.5)
    kf = norm_rope(p[:, GW:2 * GW], kg_ref[...])
    map_sel = ((_iota((N_MAPS, GW), 1) >> 5) == _iota((N_MAPS, GW), 0)).astype(F32)
    qn_out[0, 0] = _dot_nt(map_sel, qf * qf, HI)
    kn_out[0, 0] = _dot_nt(map_sel, kf * kf, HI)
    q = qf.astype(BF16)
    k = kf.astype(BF16)
    for g in range(N_MAPS):
        q_out[0, 0, g] = q[:, g * QK_DIM:(g + 1) * QK_DIM]
        k_out[0, 0, g] = k[:, g * QK_DIM:(g + 1) * QK_DIM]
    vt = p[:, 2 * GW:].T.astype(BF16)
    ones = jnp.ones((HEAD_DIM, TM), BF16)
    for h in range(HEADS):
        vt_out[0, 0, h, 0:HEAD_DIM, :] = vt[h * HEAD_DIM:(h + 1) * HEAD_DIM, :]
        vt_out[0, 0, h, HEAD_DIM:, :] = ones


def _attn_prep(pb, qn_g, kn_g, cos_t, sin_t):
    b, t, w = pb.shape
    nt = t // TM
    vec = pl.BlockSpec((1, GW), lambda bb, i: (0, 0))
    tab = pl.BlockSpec((TM, GW), lambda bb, i: (i, 0))
    reps = GW // QK_DIM
    qk_spec = pl.BlockSpec((1, 1, N_MAPS, TM, QK_DIM), lambda bb, i: (bb, i, 0, 0, 0))
    qk_shape = jax.ShapeDtypeStruct((b, nt, N_MAPS, TM, QK_DIM), BF16)
    n_spec = pl.BlockSpec((1, 1, N_MAPS, TM), lambda bb, i: (bb, i, 0, 0))
    n_shape = jax.ShapeDtypeStruct((b, nt, N_MAPS, TM), F32)
    return pl.pallas_call(
        _attn_prep_kernel,
        grid=(b, nt),
        in_specs=[pl.BlockSpec((1, TM, w), lambda bb, i: (bb, i, 0)), vec, vec, tab, tab],
        out_specs=[qk_spec, qk_spec,
                   pl.BlockSpec((1, 1, HEADS, V_EXT, TM), lambda bb, i: (bb, i, 0, 0, 0)), n_spec, n_spec],
        out_shape=[qk_shape, qk_shape, jax.ShapeDtypeStruct((b, nt, HEADS, V_EXT, TM), BF16), n_shape, n_shape],
        compiler_params=_cp(("parallel", "arbitrary")),
    )(pb, jnp.tile(qn_g, reps).reshape(1, GW), jnp.tile(kn_g, reps).reshape(1, GW), cos_t, sin_t)


SHIFT_MAX = 40.0


def _attn_kernel(q_ref, k_ref, vt_ref, qn_ref, kn_ref, lam_ref, sg_ref, o_ref, m_scr, acc_scr, s_scr, *, n_chunks, lam_init):
    acc_scr[...] = jnp.zeros(acc_scr.shape, F32)
    k2 = jnp.max(jnp.max(kn_ref[0], axis=0), axis=-1, keepdims=True)
    bound = jnp.sqrt(qn_ref[0, 0] * k2)
    safe = jnp.max(bound) <= SHIFT_MAX

    @pl.when(safe)
    def _():
        for g in range(N_MAPS):
            s_scr[0, g] = _dot_nt(k_ref[0, 0, g], q_ref[0, 0, g])

        def body(c, carry):
            cur = c & 1
            nxt = jnp.minimum(c + 1, n_chunks - 1)
            for g in range(N_MAPS):
                s_next = _dot_nt(k_ref[0, nxt, g], q_ref[0, 0, g])
                p = jnp.exp(s_scr[cur, g] - bound[g:g + 1]).astype(BF16)
                acc_scr[g] += _dot(vt_ref[0, c, g // 2], p)
                s_scr[1 - cur, g] = s_next
            return carry

        lax.fori_loop(0, n_chunks, body, 0)

    @pl.when(jnp.logical_not(safe))
    def _():
        m_scr[...] = jnp.full(m_scr.shape, -jnp.inf, F32)

        def body(c, carry):
            for g in range(N_MAPS):
                s = _dot_nt(k_ref[0, c, g], q_ref[0, 0, g])
                m_old = m_scr[g]
                m_new = jnp.maximum(m_old, jnp.max(s, axis=0, keepdims=True))
                p = jnp.exp(s - m_new).astype(BF16)
                acc_scr[g] = jnp.exp(m_old - m_new) * acc_scr[g] + _dot(vt_ref[0, c, g // 2], p)
                m_scr[g] = m_new
            return carry

        lax.fori_loop(0, n_chunks, body, 0)

    lp = lam_ref[...]
    lam = (jnp.exp(jnp.sum(lp[0:1] * lp[1:2], axis=-1, keepdims=True))
           - jnp.exp(jnp.sum(lp[2:3] * lp[3:4], axis=-1, keepdims=True)) + lam_init)
    outs = []
    for h in range(HEADS):
        a0 = acc_scr[2 * h]
        a1 = acc_scr[2 * h + 1]
        o = a0[:HEAD_DIM] / a0[HEAD_DIM:HEAD_DIM + 1] - lam * (a1[:HEAD_DIM] / a1[HEAD_DIM:HEAD_DIM + 1])
        outs.append(o * lax.rsqrt(jnp.mean(o * o, axis=0, keepdims=True) + EPS) * sg_ref[...] * (1.0 - lam_init))
    o_ref[0] = jnp.concatenate(outs, axis=0).T


def _attention(q, k, vt, qn, kn, lam_p, subln_g, lam_init, first_tile, n_q, n_kv, t_out):
    b = q.shape[0]
    kern = functools.partial(_attn_kernel, n_chunks=n_kv, lam_init=lam_init)
    return pl.pallas_call(
        kern,
        grid=(b, n_q),
        in_specs=[pl.BlockSpec((1, 1, N_MAPS, TM, QK_DIM), lambda bb, i: (bb, i + first_tile, 0, 0, 0)),
                  pl.BlockSpec((1, n_kv, N_MAPS, TM, QK_DIM), lambda bb, i: (bb, 0, 0, 0, 0)),
                  pl.BlockSpec((1, n_kv, HEADS, V_EXT, TM), lambda bb, i: (bb, 0, 0, 0, 0)),
                  pl.BlockSpec((1, 1, N_MAPS, TM), lambda bb, i: (bb, i + first_tile, 0, 0)),
                  pl.BlockSpec((1, n_kv, N_MAPS, TM), lambda bb, i: (bb, 0, 0, 0)),
                  pl.BlockSpec((4, QK_DIM), lambda bb, i: (0, 0)),
                  pl.BlockSpec((HEAD_DIM, 1), lambda bb, i: (0, 0))],
        out_specs=pl.BlockSpec((1, TM, GW), lambda bb, i: (bb, i, 0)),
        out_shape=jax.ShapeDtypeStruct((b, t_out, GW), F32),
        scratch_shapes=[pltpu.VMEM((N_MAPS, 1, TM), F32), pltpu.VMEM((N_MAPS, V_EXT, TM), F32),
                        pltpu.VMEM((2, N_MAPS, TM, TM), F32)],
        compiler_params=_cp(("parallel", "arbitrary")),
    )(q, k, vt, qn, kn, lam_p, subln_g.reshape(HEAD_DIM, 1))


def _chunk_masks():
    i = _iota((CH, GW), 0)
    j = _iota((CH, GW), 1) & (CH - 1)
    eye = (i == j).astype(F32)
    incl = (j <= i, j >= i)
    strict = (j < i, j > i)
    r = _iota((CH, CH), 0)
    c = _iota((CH, CH), 1)
    cum = ((c <= r).astype(F32), (c >= r).astype(F32))
    bd = (_iota((GW, GW), 0) >> 6) == (_iota((GW, GW), 1) >> 6)
    return eye, incl, strict, cum, bd


def _bd(y, bd):
    return jnp.where(bd, _tile4(y), 0.0).astype(BF16)


def _rev_tile(s, n_tiles):
    return jnp.where(s == 0, 0, n_tiles - s)


DN_HALO = 8


def _dn_prep_kernel(pm_ref, pl_ref, pr_ref, pcs_ref, cw_ref, alog_ref, dtb_ref, q_out, k_out, v_out, bg_out, ext_ref):
    left_ok, right_ok = _halo_ok(pl.program_id(1), pl.num_programs(1))
    ext_ref[0:DN_HALO] = jnp.where(left_ok, pl_ref[0], 0.0)
    ext_ref[DN_HALO:DN_HALO + TM] = pm_ref[0]
    ext_ref[DN_HALO + TM:] = jnp.where(right_ok, pr_ref[0], 0.0)
    pad = DN_CONV // 2
    acc = jnp.zeros((TM, 3 * GW), F32)
    for j in range(DN_CONV):
        acc = acc + cw_ref[j:j + 1, :] * ext_ref[pl.ds(DN_HALO - pad + j, TM), :]
    qkv = _silu(acc)
    ones64 = _group_ones(GW, 6)

    def l2n(t):
        return t * lax.rsqrt(_dot(t * t, ones64, HI) + EPS)

    q_out[0] = l2n(qkv[:, :GW]) * (HEAD_DIM ** -0.5)
    k_out[0] = l2n(qkv[:, GW:2 * GW])
    v_out[0] = qkv[:, 2 * GW:]
    s = pcs_ref[0]
    col = _iota(s.shape, 1)
    gate = -jnp.exp(alog_ref[...]) * _softplus(s + dtb_ref[...])
    bg_out[0] = jnp.where(col < 2 * HEADS, _sigmoid(s), jnp.where(col < 4 * HEADS, gate, 0.0))


def _dn_prep(pc, pcs, conv_w, a_log, dt_bias):
    b, t, w = pc.shape
    nt = t // TM
    left, right = _halo_specs(w, DN_HALO, nt)
    row = lambda bb, i: (bb, i, 0)
    pad_vec = lambda a: jnp.zeros((1, LANES), F32).at[0, 2 * HEADS:4 * HEADS].set(a.reshape(-1))
    vec = pl.BlockSpec((1, LANES), lambda bb, i: (0, 0))
    return pl.pallas_call(
        _dn_prep_kernel,
        grid=(b, nt),
        in_specs=[pl.BlockSpec((1, TM, w), row), left, right, pl.BlockSpec((1, TM, LANES), row),
                  pl.BlockSpec((DN_CONV, w), lambda bb, i: (0, 0)), vec, vec],
        out_specs=[pl.BlockSpec((1, TM, GW), row)] * 3 + [pl.BlockSpec((1, TM, LANES), row)],
        out_shape=[jax.ShapeDtypeStruct((b, t, GW), F32)] * 3 + [jax.ShapeDtypeStruct((b, t, LANES), F32)],
        scratch_shapes=[pltpu.VMEM((TM + 2 * DN_HALO, w), F32)],
        compiler_params=_cp(("parallel", "arbitrary")),
    )(pc, pc, pc, pcs, conv_w, pad_vec(a_log), pad_vec(dt_bias))


def _dn_local_kernel(q_ref, k_ref, v_ref, bg_ref, u_out, w_out, qi_out, a_out, ket_out, ge_out):
    rr = _iota((TM, GW), 0)
    cc = _iota((TM, GW), 1)
    i_in = rr & (CH - 1)
    j_in = cc & (CH - 1)
    bd = (rr >> 6) == (cc >> 6)
    eye_t = (i_in == j_in).astype(F32)
    incl = (j_in <= i_in, j_in >= i_in)
    strict = (j_in < i_in, j_in > i_in)
    ones_bd = bd.astype(F32)
    eye_bf = (rr == cc).astype(BF16)
    head_of_lane = _iota((LANES, GW), 1) >> 6
    src = _iota((LANES, GW), 0)
    q = q_ref[0]
    k = k_ref[0]
    v = v_ref[0]
    bg = bg_ref[0]
    beta, gcum, decay, kb, egc = [], [], [], [], []
    for d in range(2):
        sel_b = (src == d * HEADS + head_of_lane).astype(F32)
        sel_g = (src == 2 * HEADS + d * HEADS + head_of_lane).astype(F32)
        cum_bd = jnp.where(jnp.logical_and(bd, incl[d]), 1.0, 0.0)
        beta.append(_dot(bg, sel_b, HI))
        gcum.append(_dot(_dot(cum_bd, bg, HI), sel_g, HI))
        grow = _dot(ones_bd, gcum[d] * eye_t, HI)
        decay.append(jnp.where(incl[d], jnp.exp(jnp.where(incl[d], gcum[d] - grow, 0.0)), 0.0))
        kb.append(k * beta[d])
        egc.append(jnp.exp(gcum[d]))
        qi_out[0, d] = (q * egc[d]).astype(BF16)
    pairs = [(c, d) for c in range(CPT) for d in range(2)]
    rows = [slice(c * CH, (c + 1) * CH) for c in range(CPT)]
    eye = eye_t[:CH]
    a = {}
    for c in range(CPT):
        r = rows[c]
        lhs = jnp.concatenate([kb[0][r], kb[1][r], q[r]], axis=0).astype(BF16)
        aq = _dot_nt(lhs, _bd(k[r], bd))
        for d in range(2):
            dec = decay[d][r]
            a[c, d] = jnp.where(strict[d][:CH], aq[d * CH:(d + 1) * CH] * dec, 0.0)
            a_out[0, d, r, :] = jnp.where(incl[d][:CH], aq[2 * CH:] * dec, 0.0).astype(BF16)
    t_inv = {cd: eye - a[cd] for cd in pairs}
    p = {cd: _dot(a[cd].astype(BF16), _bd(a[cd], bd)) for cd in pairs}
    for it in range(5):
        for cd in pairs:
            pbd = _bd(p[cd], bd)
            if it < 4:
                res = _dot(jnp.concatenate([t_inv[cd], p[cd]], axis=0).astype(BF16), pbd)
                t_inv[cd] = t_inv[cd] + res[:CH]
                p[cd] = res[CH:]
            else:
                t_inv[cd] = t_inv[cd] + _dot(t_inv[cd].astype(BF16), pbd)
    for c, d in pairs:
        r = rows[c]
        tb = t_inv[c, d].astype(BF16)
        u_out[0, d, r, :] = _dot(tb, _bd(v[r] * beta[d][r], bd))
        w_out[0, d, r, :] = _dot(tb, _bd(kb[d][r] * egc[d][r], bd)).astype(BF16)
        last = (c + 1) * CH - 1 if d == 0 else c * CH
        gtot = gcum[d][last:last + 1]
        k_end = (k[r] * jnp.exp(gtot - gcum[d][r])).astype(BF16)
        ket_out[0, d, c] = _dot_nt(eye_bf, k_end).astype(BF16)
        ge_out[0, d, c] = jnp.exp(gtot)


def _dn_local(q, k, v, bg):
    b, t, _ = q.shape
    nt = t // TM
    row = lambda bb, i: (bb, i, 0)
    drow = pl.BlockSpec((1, 2, TM, GW), lambda bb, i: (bb, 0, i, 0))
    return pl.pallas_call(
        _dn_local_kernel,
        grid=(b, nt),
        in_specs=[pl.BlockSpec((1, TM, GW), row)] * 3 + [pl.BlockSpec((1, TM, LANES), row)],
        out_specs=[drow, drow, drow, drow,
                   pl.BlockSpec((1, 2, CPT, GW, CH), lambda bb, i: (bb, 0, i, 0, 0)),
                   pl.BlockSpec((1, 2, CPT, 1, GW), lambda bb, i: (bb, 0, i, 0, 0))],
        out_shape=[jax.ShapeDtypeStruct((b, 2, t, GW), F32)] + [jax.ShapeDtypeStruct((b, 2, t, GW), BF16)] * 3
        + [jax.ShapeDtypeStruct((b, 2, t // CH, GW, CH), BF16), jax.ShapeDtypeStruct((b, 2, t // CH, 1, GW), F32)],
        compiler_params=_cp(("parallel", "arbitrary")),
    )(q, k, v, bg)


def _dn_scan_kernel(uf, wf, qf, af, kf, gf, ur, wr, qr, ar, kr, gr, of_ref, or_ref, s_scr):
    @pl.when(pl.program_id(1) == 0)
    def _():
        s_scr[...] = jnp.zeros(s_scr.shape, F32)

    bd = (_iota((GW, GW), 0) >> 6) == (_iota((GW, GW), 1) >> 6)
    dirs = ((uf, wf, qf, af, kf, gf, of_ref), (ur, wr, qr, ar, kr, gr, or_ref))
    for c in range(CPT):
        for d, (u, w, qi, a, ket, ge, o_ref) in enumerate(dirs):
            cc = c if d == 0 else CPT - 1 - c
            rows = slice(cc * CH, (cc + 1) * CH)
            s = s_scr[d]
            wq = _dot(jnp.concatenate([w[0, 0, rows, :], qi[0, 0, rows, :]], axis=0), s.astype(BF16))
            v_new = u[0, 0, rows, :] - wq[:CH]
            o_ref[0, rows, :] = wq[CH:] + _dot(a[0, 0, rows, :], _bd(v_new, bd))
            s_scr[d] = s * ge[0, 0, cc] + jnp.where(bd, _dot(ket[0, 0, cc], v_new.astype(BF16)), 0.0)


def _dir_specs(shape_tail, n_tiles, chunked):
    blk = (1, 1, CPT if chunked else TM) + shape_tail
    zeros = (0,) * len(shape_tail)
    fwd = pl.BlockSpec(blk, lambda b, s: (b, 0, s) + zeros)
    rev = pl.BlockSpec(blk, lambda b, s: (b, 1, _rev_tile(s, n_tiles)) + zeros)
    return fwd, rev


def _dn_scan(u, w, qi, a, ket, ge):
    b, _, t, _ = u.shape
    nt = t // TM
    rowf, rowr = _dir_specs((GW,), nt, False)
    ketf, ketr = _dir_specs((GW, CH), nt, True)
    gef, ger = _dir_specs((1, GW), nt, True)
    return pl.pallas_call(
        _dn_scan_kernel,
        grid=(b, nt),
        in_specs=[rowf, rowf, rowf, rowf, ketf, gef, rowr, rowr, rowr, rowr, ketr, ger],
        out_specs=[pl.BlockSpec((1, TM, GW), lambda bb, s: (bb, s, 0)),
                   pl.BlockSpec((1, TM, GW), lambda bb, s: (bb, _rev_tile(s, nt), 0))],
        out_shape=[jax.ShapeDtypeStruct((b, t, GW), F32)] * 2,
        scratch_shapes=[pltpu.VMEM((2, GW, GW), F32)],
        compiler_params=_cp(("parallel", "arbitrary")),
    )(u, w, qi, a, ket, ge, u, w, qi, a, ket, ge)


GLA_QK = HEADS * GLA_K


def _gla_local_kernel(pd_ref, pdl_ref, w2_ref, b2_ref, qi_out, a_out, ke_out, vt_out, vb_out, de_out):
    _, incl, _, cum, _ = _chunk_masks()
    eye_bf = (_iota((GW, GW), 0) == _iota((GW, GW), 1)).astype(BF16)
    bdk = (_iota((GW, GLA_QK), 0) >> 6) == (_iota((GW, GLA_QK), 1) >> 5)
    z = _dot(pdl_ref[0], w2_ref[...], HI) + b2_ref[...]
    gk_all = -_softplus(-z) * (1.0 / GLA_TAU)
    for c in range(CPT):
        rows = slice(c * CH, (c + 1) * CH)
        p = pd_ref[0, rows, :]
        q = p[:, :GLA_QK] * (GLA_K ** -0.5)
        k = p[:, GLA_QK:2 * GLA_QK]
        vb = p[:, 2 * GLA_QK:].astype(BF16)
        vb_out[0, rows, :] = vb
        vt_out[0, c] = _dot_nt(eye_bf, vb).astype(BF16)
        for d in range(2):
            bcs = _dot(cum[d], gk_all[rows, d * GLA_QK:(d + 1) * GLA_QK], HI)
            bend = bcs[CH - 1:CH] if d == 0 else bcs[0:1]
            q_in = (q * jnp.exp(bcs)).astype(BF16)
            kdec = jnp.where(bdk, _tile4(k * jnp.exp(-bcs)), 0.0).astype(BF16)
            a_out[0, d, rows, :] = jnp.where(incl[d], _dot_nt(q_in, kdec), 0.0).astype(BF16)
            qi_out[0, d, rows, :] = q_in
            ke_out[0, d, rows, :] = (k * jnp.exp(bend - bcs)).astype(BF16)
            de_out[0, d, c] = jnp.exp(bend)


def _gla_local(pd, pdl, w2bd, b2):
    b, t, w = pd.shape
    nt = t // TM
    row = lambda bb, i: (bb, i, 0)
    return pl.pallas_call(
        _gla_local_kernel,
        grid=(b, nt),
        in_specs=[pl.BlockSpec((1, TM, w), row), pl.BlockSpec((1, TM, LANES), row),
                  pl.BlockSpec((LANES, GW), lambda bb, i: (0, 0)), pl.BlockSpec((1, GW), lambda bb, i: (0, 0))],
        out_specs=[pl.BlockSpec((1, 2, TM, GLA_QK), lambda bb, i: (bb, 0, i, 0)),
                   pl.BlockSpec((1, 2, TM, GW), lambda bb, i: (bb, 0, i, 0)),
                   pl.BlockSpec((1, 2, TM, GLA_QK), lambda bb, i: (bb, 0, i, 0)),
                   pl.BlockSpec((1, CPT, GW, CH), lambda bb, i: (bb, i, 0, 0)),
                   pl.BlockSpec((1, TM, GW), row),
                   pl.BlockSpec((1, 2, CPT, 1, GLA_QK), lambda bb, i: (bb, 0, i, 0, 0))],
        out_shape=[jax.ShapeDtypeStruct((b, 2, t, GLA_QK), BF16), jax.ShapeDtypeStruct((b, 2, t, GW), BF16),
                   jax.ShapeDtypeStruct((b, 2, t, GLA_QK), BF16), jax.ShapeDtypeStruct((b, t // CH, GW, CH), BF16),
                   jax.ShapeDtypeStruct((b, t, GW), BF16), jax.ShapeDtypeStruct((b, 2, t // CH, 1, GLA_QK), F32)],
        compiler_params=_cp(("parallel", "arbitrary")),
    )(pd, pdl, w2bd, b2)


def _gla_scan_kernel(qf, af, kf, df, vtf, vf, qr, ar, kr, dr, vtr, vr, of_ref, or_ref, s_scr):
    @pl.when(pl.program_id(1) == 0)
    def _():
        s_scr[...] = jnp.zeros(s_scr.shape, F32)

    bd = (_iota((GW, GW), 0) >> 6) == (_iota((GW, GW), 1) >> 6)
    bdt = (_iota((GW, GLA_QK), 0) >> 6) == (_iota((GW, GLA_QK), 1) >> 5)
    dirs = ((qf, af, kf, df, vtf, vf, of_ref), (qr, ar, kr, dr, vtr, vr, or_ref))
    for c in range(CPT):
        for d, (qi, a, ke, de, vt, v, o_ref) in enumerate(dirs):
            cc = c if d == 0 else CPT - 1 - c
            rows = slice(cc * CH, (cc + 1) * CH)
            st = s_scr[d]
            vbd = jnp.where(bd, _tile4(v[0, rows, :]), jnp.zeros((), BF16))
            o_ref[0, rows, :] = _dot_nt(qi[0, 0, rows, :], st.astype(BF16)) + _dot(a[0, 0, rows, :], vbd)
            s_scr[d] = st * de[0, 0, cc] + jnp.where(bdt, _dot(vt[0, cc], ke[0, 0, rows, :]), 0.0)


def _gla_scan(qi, a, ke, de, vt, vb):
    b, _, t, _ = a.shape
    nt = t // TM
    qf, qr = _dir_specs((GLA_QK,), nt, False)
    af, ar = _dir_specs((GW,), nt, False)
    df, dr = _dir_specs((1, GLA_QK), nt, True)
    vtf = pl.BlockSpec((1, CPT, GW, CH), lambda bb, s: (bb, s, 0, 0))
    vtr = pl.BlockSpec((1, CPT, GW, CH), lambda bb, s: (bb, _rev_tile(s, nt), 0, 0))
    vf = pl.BlockSpec((1, TM, GW), lambda bb, s: (bb, s, 0))
    vr = pl.BlockSpec((1, TM, GW), lambda bb, s: (bb, _rev_tile(s, nt), 0))
    return pl.pallas_call(
        _gla_scan_kernel,
        grid=(b, nt),
        in_specs=[qf, af, qf, df, vtf, vf, qr, ar, qr, dr, vtr, vr],
        out_specs=[vf, vr],
        out_shape=[jax.ShapeDtypeStruct((b, t, GW), F32)] * 2,
        scratch_shapes=[pltpu.VMEM((2, GW, GLA_QK), F32)],
        compiler_params=_cp(("parallel", "arbitrary")),
    )(qi, a, ke, de, vt, vb, qi, a, ke, de, vt, vb)


def _outproj_kernel(x_ref, ya_ref, yb_ref, cf_ref, cr_ref, df_ref, dr_ref, pg_ref, g1_ref, gc_ref, gd_ref, w_ref, o_ref):
    ones64 = _group_ones(GW, 6)

    def fin(o, g, gate):
        ms = _dot(o * o, ones64, HI) * (1.0 / HEAD_DIM)
        return (o * lax.rsqrt(ms + EPS) * g * _silu(gate)).astype(BF16)

    pg = pg_ref[0]
    yc = fin(cf_ref[0] + cr_ref[0], gc_ref[...], pg[:, :GW])
    yd = fin(df_ref[0] + dr_ref[0], gd_ref[...], pg[:, GW:])
    res = (_dot(ya_ref[0].astype(BF16), w_ref[0:GW, :]) + _dot(yb_ref[0].astype(BF16), w_ref[GW:2 * GW, :])
           + _dot(yc, w_ref[2 * GW:3 * GW, :]) + _dot(yd, w_ref[3 * GW:, :]))
    o_ref[0] = x_ref[0] + g1_ref[0] * res


def _outproj(xs, ya, yb, ocf, ocr, odf, odr, pg, mod3, dn_g, gla_g, w_out_bf, n_batch, first_tile, yb_first):
    b, t, d = xs.shape
    n = t // TM - first_tile
    row = lambda bb, i: (bb, i + first_tile, 0)
    g256 = pl.BlockSpec((1, TM, GW), row)
    vec = pl.BlockSpec((1, GW), lambda bb, i: (0, 0))
    reps = GW // HEAD_DIM
    return pl.pallas_call(
        _outproj_kernel,
        grid=(b, n),
        in_specs=[pl.BlockSpec((1, TM, d), row), g256,
                  pl.BlockSpec((1, TM, GW), lambda bb, i: (bb, i + first_tile - yb_first, 0)),
                  g256, g256, g256, g256, pl.BlockSpec((1, TM, 2 * GW), row),
                  _mod_spec(2, n_batch, first_tile), vec, vec, pl.BlockSpec((d, d), lambda bb, i: (0, 0))],
        out_specs=pl.BlockSpec((1, TM, d), lambda bb, i: (bb, i, 0)),
        out_shape=jax.ShapeDtypeStruct((b, n * TM, d), F32),
        compiler_params=_cp(("parallel", "arbitrary")),
    )(xs, ya, yb, ocf, ocr, odf, odr, pg, mod3, jnp.tile(dn_g, reps).reshape(1, GW),
      jnp.tile(gla_g, reps).reshape(1, GW), w_out_bf)


FF_HALO = 8


def _ffn_kernel(xm_ref, xl_ref, xr_ref, sh_ref, sc_ref, g2_ref, ng_ref, wa_ref, wg_ref, cwa_ref, cwg_ref, wd_ref,
                o_ref, ext_ref, *, first_tile, n_tiles):
    left_ok, right_ok = _halo_ok(pl.program_id(1) + first_tile, n_tiles)
    rows_ext = _iota((TM + 2 * FF_HALO, 1), 0)
    keep = jnp.logical_and(jnp.logical_or(rows_ext >= FF_HALO, left_ok),
                           jnp.logical_or(rows_ext < FF_HALO + TM, right_ok)).astype(F32)
    x = jnp.concatenate([xl_ref[0], xm_ref[0], xr_ref[0]], axis=0)
    y = x * lax.rsqrt(jnp.mean(x * x, axis=-1, keepdims=True) + EPS) * ng_ref[...]
    h = ((y * (1.0 + sc_ref[0]) + sh_ref[0]) * keep).astype(BF16)
    acc = jnp.zeros((TM, D_MODEL), F32)
    for j in range(N_FF_BLK):
        def conv(w_ref, cw_ref, half):
            ext_ref[half] = _dot(h, w_ref[j])
            cw = cw_ref[j]
            return (cw[0:1] * ext_ref[half, pl.ds(FF_HALO - 1, TM), :] + cw[1:2] * ext_ref[half, pl.ds(FF_HALO, TM), :]
                    + cw[2:3] * ext_ref[half, pl.ds(FF_HALO + 1, TM), :])
        a = conv(wa_ref, cwa_ref, 0)
        g = conv(wg_ref, cwg_ref, 1)
        acc = acc + _dot((_silu(g) * a).astype(BF16), wd_ref[j])
    o_ref[0] = xm_ref[0] + g2_ref[0] * acc


def _ffn(x1, mod3, norm_g, wa, wg, cwa, cwg, wd, n_batch, first_tile, n_tiles_total):
    b, t, d = x1.shape
    n = t // TM
    per = TM // FF_HALO
    left = pl.BlockSpec((1, FF_HALO, d), lambda bb, i: (bb, jnp.maximum(i * per - 1, 0), 0))
    right = pl.BlockSpec((1, FF_HALO, d), lambda bb, i: (bb, jnp.minimum((i + 1) * per, n * per - 1), 0))
    const3 = lambda bb, i: (0, 0, 0)
    kern = functools.partial(_ffn_kernel, first_tile=first_tile, n_tiles=n_tiles_total)
    return pl.pallas_call(
        kern,
        grid=(b, n),
        in_specs=[pl.BlockSpec((1, TM, d), lambda bb, i: (bb, i, 0)), left, right,
                  _mod_spec(3, n_batch, first_tile), _mod_spec(4, n_batch, first_tile), _mod_spec(5, n_batch, first_tile),
                  pl.BlockSpec((1, d), lambda bb, i: (0, 0)),
                  pl.BlockSpec((N_FF_BLK, d, FF_BLK), const3), pl.BlockSpec((N_FF_BLK, d, FF_BLK), const3),
                  pl.BlockSpec((N_FF_BLK, 3, FF_BLK), const3), pl.BlockSpec((N_FF_BLK, 3, FF_BLK), const3),
                  pl.BlockSpec((N_FF_BLK, FF_BLK, d), const3)],
        out_specs=pl.BlockSpec((1, TM, d), lambda bb, i: (bb, i, 0)),
        out_shape=jax.ShapeDtypeStruct((b, t, d), F32),
        scratch_shapes=[pltpu.VMEM((2, TM + 2 * FF_HALO, FF_BLK), F32)],
        compiler_params=_cp(("parallel", "arbitrary")),
    )(x1, x1, x1, mod3, mod3, mod3, norm_g.reshape(1, d), wa, wg, cwa, cwg, wd)


def _rope_tables(seq, ctx_len):
    rows = seq // GRID_W
    row = jnp.repeat(jnp.arange(rows, dtype=F32), GRID_W)
    col = jnp.tile(jnp.arange(GRID_W, dtype=F32), rows)
    nf = QK_DIM // 4
    inv = ROPE_THETA ** (-jnp.arange(nf, dtype=F32) / nf)
    ang = jnp.concatenate([row[:, None] * inv, col[:, None] * inv], axis=-1)
    cos = jnp.concatenate([jnp.ones((ctx_len, QK_DIM // 2), F32), jnp.cos(ang)], axis=0)
    sin = jnp.concatenate([jnp.zeros((ctx_len, QK_DIM // 2), F32), jnp.sin(ang)], axis=0)
    reps = GW // QK_DIM
    return (jnp.tile(jnp.concatenate([cos, cos], axis=-1), (1, reps)),
            jnp.tile(jnp.concatenate([-sin, sin], axis=-1), (1, reps)))


def _regroup_w_in(w):
    d = w.shape[0]
    z = lambda n: jnp.zeros((d, n), w.dtype)
    return jnp.concatenate([w[:, :2048], w[:, 2048:2064], z(LANES - 16), w[:, 2320:2832], w[:, 2832:2864],
                            z(LANES - 32), w[:, 2064:2320], w[:, 2864:3120]], axis=1).astype(BF16)


def _gla_w2_blockdiag(w2):
    out = jnp.zeros((LANES, GW), F32)
    out = out.at[0:GLA_RANK, 0:GLA_QK].set(w2[0])
    return out.at[GLA_RANK:2 * GLA_RANK, GLA_QK:].set(w2[1])


def _layer(xs, mod3, lp, cos_t, sin_t, layer_idx, last, n_batch):
    b, t, d = xs.shape
    nt = t // TM
    pa, pb, pc, pcs, pd, pdl, pg = _inproj(xs, mod3, lp["norm1_g"], _regroup_w_in(lp["w_in"]), n_batch)

    ya = _conv_module(pa, lp["cm_conv_w"], lp["cm_conv_b"], lp["cm_ln_g"], lp["cm_ln_b"])

    lam_init = 0.8 - 0.6 * math.exp(-0.3 * layer_idx)
    q, k, vt, qn, kn = _attn_prep(pb, lp["da_qnorm_g"], lp["da_knorm_g"], cos_t, sin_t)
    yb = _attention(q, k, vt, qn, kn, lp["da_lambda"], lp["da_subln_g"], lam_init, 1, nt - 1, nt, t - TM)
    yb_first = 1
    if not last:
        yb_ctx = _attention(q, k, vt, qn, kn, lp["da_lambda"], lp["da_subln_g"], lam_init, 0, 1, 1, TM)
        yb = jnp.concatenate([yb_ctx, yb], axis=1)
        yb_first = 0

    dq, dk, dv, bg = _dn_prep(pc, pcs, lp["dn_conv_w"], lp["dn_a_log"], lp["dn_dt_bias"])
    ocf, ocr = _dn_scan(*_dn_local(dq, dk, dv, bg))

    qi, a, ke, vt, vb, de = _gla_local(pd, pdl, _gla_w2_blockdiag(lp["gla_w2"]), lp["gla_b2"].reshape(1, GW))
    odf, odr = _gla_scan(qi, a, ke, de, vt, vb)

    first_tile = 1 if last else 0
    x1 = _outproj(xs, ya, yb, ocf, ocr, odf, odr, pg, mod3, lp["dn_onorm_g"], lp["gla_onorm_g"],
                  lp["w_out"].astype(BF16), n_batch, first_tile, yb_first)

    w_up = lp["ffn_w_up"].astype(BF16)
    blocks = lambda m: jnp.transpose(m.reshape(m.shape[0], N_FF_BLK, FF_BLK), (1, 0, 2))
    cw = lp["ffn_conv_w"]
    return _ffn(x1, mod3, lp["norm2_g"], blocks(w_up[:, :D_FF]), blocks(w_up[:, D_FF:]),
                blocks(cw[:, :D_FF]), blocks(cw[:, D_FF:]),
                lp["ffn_w_down"].astype(BF16).reshape(N_FF_BLK, FF_BLK, d), n_batch, first_tile, nt)


def kernel(x, c, ctx, c_ctx, w_mod, b_mod, norm1_g, norm2_g, w_in, w_out, cm_conv_w, cm_conv_b, cm_ln_g, cm_ln_b, da_qnorm_g, da_knorm_g, da_lambda, da_subln_g, dn_conv_w, dn_a_log, dn_dt_bias, dn_onorm_g, gla_w2, gla_b2, gla_onorm_g, ffn_w_up, ffn_conv_w, ffn_w_down):
    n_batch, seq, d = x.shape
    ctx_len = ctx.shape[1]
    assert ctx_len == TM and seq % TM == 0 and d == D_MODEL
    depth = w_mod.shape[0]
    cos_t, sin_t = _rope_tables(seq, ctx_len)
    xs = jnp.concatenate([ctx, x], axis=1)
    mod_rows = 16
    c_rows = jnp.zeros((mod_rows, d), F32).at[:n_batch].set(c).at[n_batch].set(c_ctx)
    params = dict(w_mod=w_mod, b_mod=b_mod, norm1_g=norm1_g, norm2_g=norm2_g, w_in=w_in, w_out=w_out,
                  cm_conv_w=cm_conv_w, cm_conv_b=cm_conv_b, cm_ln_g=cm_ln_g, cm_ln_b=cm_ln_b,
                  da_qnorm_g=da_qnorm_g, da_knorm_g=da_knorm_g, da_lambda=da_lambda, da_subln_g=da_subln_g,
                  dn_conv_w=dn_conv_w, dn_a_log=dn_a_log, dn_dt_bias=dn_dt_bias, dn_onorm_g=dn_onorm_g,
                  gla_w2=gla_w2, gla_b2=gla_b2, gla_onorm_g=gla_onorm_g,
                  ffn_w_up=ffn_w_up, ffn_conv_w=ffn_conv_w, ffn_w_down=ffn_w_down)
    for l in range(depth):
        lp = {k: v[l] for k, v in params.items()}
        mod3 = _modulation(c_rows, lp["w_mod"], lp["b_mod"]).reshape(mod_rows * 6, 1, d)
        xs = _layer(xs, mod3, lp, cos_t, sin_t, l, l == depth - 1, n_batch)
    return xs
```

```python
import functools
import math

import jax
import jax.numpy as jnp
from jax import lax
from jax.experimental import pallas as pl
from jax.experimental.pallas import tpu as pltpu

F32 = jnp.float32
BF16 = jnp.bfloat16
HI = lax.Precision.HIGHEST
EPS = 1e-6

D_MODEL = 1024
GRID_W = 64
HEADS = 4
HEAD_DIM = 64
GW = 256
QK_DIM = 32
GLA_K = 32
GLA_RANK = 16
GLA_TAU = 16.0
CM_KERNEL = 31
DN_CONV = 5
ROPE_THETA = 10000.0
CH = 64
TM = 256
CPT = TM // CH
D_FF = 2816
FF_BLK = 256
N_FF_BLK = D_FF // FF_BLK
LANES = 128

IN_GROUPS = (("pa", 512), ("pb", 768), ("pc", 768), ("pcs", LANES), ("pd", 512), ("pdl", LANES), ("pg", 512))
IN_COLS_PAD = sum(w for _, w in IN_GROUPS)

VMEM_LIMIT = 56 * 1024 * 1024


def _cp(sem):
    return pltpu.CompilerParams(dimension_semantics=sem, vmem_limit_bytes=VMEM_LIMIT)


def _dot(a, b, prec=None):
    return jnp.dot(a, b, preferred_element_type=F32, precision=prec)


def _dot_nt(a, b, prec=None):
    return lax.dot_general(a, b, (((1,), (1,)), ((), ())), preferred_element_type=F32, precision=prec)


def _sigmoid(x):
    return 1.0 / (1.0 + jnp.exp(-x))


def _silu(x):
    return x * _sigmoid(x)


def _softplus(x):
    return jnp.maximum(x, 0.0) + jnp.log(1.0 + jnp.exp(-jnp.abs(x)))


def _iota(shape, dim):
    return lax.broadcasted_iota(jnp.int32, shape, dim)


def _group_ones(n, shift):
    return ((_iota((n, n), 0) >> shift) == (_iota((n, n), 1) >> shift)).astype(F32)


def _tile4(y):
    return jnp.concatenate([y, y, y, y], axis=0)


def _mod_kernel(c_ref, w_ref, b_ref, o_ref):
    o_ref[...] = _dot(_silu(c_ref[...]), w_ref[...], HI) + b_ref[...]


def _modulation(c_rows, w_mod, b_mod):
    r, d = c_rows.shape
    n = w_mod.shape[1] // d
    return pl.pallas_call(
        _mod_kernel,
        grid=(n,),
        in_specs=[pl.BlockSpec((r, d), lambda j: (0, 0)),
                  pl.BlockSpec((d, d), lambda j: (0, j)),
                  pl.BlockSpec((1, d), lambda j: (0, j))],
        out_specs=pl.BlockSpec((r, d), lambda j: (0, j)),
        out_shape=jax.ShapeDtypeStruct((r, n * d), F32),
        compiler_params=_cp(("arbitrary",)),
    )(c_rows, w_mod, b_mod.reshape(1, -1))


def _mod_spec(k, n_batch, first_tile):
    def imap(b, i):
        sel = jnp.where(i + first_tile == 0, n_batch, b)
        return (sel * 6 + k, 0, 0)
    return pl.BlockSpec((1, 1, D_MODEL), imap)


def _inproj_kernel(x_ref, sh_ref, sc_ref, g_ref, w_ref, *outs):
    x = x_ref[0]
    y = x * lax.rsqrt(jnp.mean(x * x, axis=-1, keepdims=True) + EPS) * g_ref[...]
    h = y * (1.0 + sc_ref[0]) + sh_ref[0]
    p = _dot(h.astype(BF16), w_ref[...])
    off = 0
    for o_ref, (_, width) in zip(outs, IN_GROUPS):
        o_ref[0] = p[:, off:off + width]
        off += width


def _inproj(xs, mod3, norm_g, w_in_r, n_batch):
    b, t, d = xs.shape
    row = lambda bb, i: (bb, i, 0)
    return pl.pallas_call(
        _inproj_kernel,
        grid=(b, t // TM),
        in_specs=[pl.BlockSpec((1, TM, d), row), _mod_spec(0, n_batch, 0), _mod_spec(1, n_batch, 0),
                  pl.BlockSpec((1, d), lambda bb, i: (0, 0)),
                  pl.BlockSpec((d, IN_COLS_PAD), lambda bb, i: (0, 0))],
        out_specs=[pl.BlockSpec((1, TM, w), row) for _, w in IN_GROUPS],
        out_shape=[jax.ShapeDtypeStruct((b, t, w), F32) for _, w in IN_GROUPS],
        compiler_params=_cp(("parallel", "arbitrary")),
    )(xs, mod3, mod3, norm_g.reshape(1, d), w_in_r)


def _halo_specs(width, halo, n_tiles):
    per = TM // halo
    left = pl.BlockSpec((1, halo, width), lambda b, i: (b, jnp.maximum(i * per - 1, 0), 0))
    right = pl.BlockSpec((1, halo, width), lambda b, i: (b, jnp.minimum((i + 1) * per, n_tiles * per - 1), 0))
    return left, right


def _halo_ok(i, n_tiles):
    return i >= 2, jnp.logical_and(i >= 1, i < n_tiles - 1)


CM_HALO = 16


def _convmod_kernel(pm_ref, pl_ref, pr_ref, cw_ref, cb_ref, lg_ref, lb_ref, o_ref, ext_ref):
    left_ok, right_ok = _halo_ok(pl.program_id(1), pl.num_programs(1))

    def glu(p):
        return p[:, :GW] * _sigmoid(p[:, GW:])

    ext_ref[0:CM_HALO] = jnp.where(left_ok, glu(pl_ref[0]), 0.0)
    ext_ref[CM_HALO:CM_HALO + TM] = glu(pm_ref[0])
    ext_ref[CM_HALO + TM:] = jnp.where(right_ok, glu(pr_ref[0]), 0.0)
    pad = CM_KERNEL // 2
    acc = jnp.zeros((TM, GW), F32)
    for j in range(CM_KERNEL):
        acc = acc + cw_ref[j:j + 1, :] * ext_ref[pl.ds(CM_HALO - pad + j, TM), :]
    y = acc + cb_ref[...]
    mu = jnp.mean(y, axis=-1, keepdims=True)
    yc = y - mu
    var = jnp.mean(yc * yc, axis=-1, keepdims=True)
    o_ref[0] = _silu(yc * lax.rsqrt(var + EPS) * lg_ref[...] + lb_ref[...])


def _conv_module(pa, conv_w, conv_b, ln_g, ln_b):
    b, t, w = pa.shape
    nt = t // TM
    left, right = _halo_specs(w, CM_HALO, nt)
    vec = pl.BlockSpec((1, GW), lambda bb, i: (0, 0))
    return pl.pallas_call(
        _convmod_kernel,
        grid=(b, nt),
        in_specs=[pl.BlockSpec((1, TM, w), lambda bb, i: (bb, i, 0)), left, right,
                  pl.BlockSpec((CM_KERNEL, GW), lambda bb, i: (0, 0)), vec, vec, vec],
        out_specs=pl.BlockSpec((1, TM, GW), lambda bb, i: (bb, i, 0)),
        out_shape=jax.ShapeDtypeStruct((b, t, GW), F32),
        scratch_shapes=[pltpu.VMEM((TM + 2 * CM_HALO, GW), F32)],
        compiler_params=_cp(("parallel", "arbitrary")),
    )(pa, pa, pa, conv_w, conv_b.reshape(1, GW), ln_g.reshape(1, GW), ln_b.reshape(1, GW))


N_MAPS = 2 * HEADS
V_EXT = 2 * HEAD_DIM


def _attn_prep_kernel(pb_ref, qg_ref, kg_ref, cos_ref, sin_ref, q_out, k_out, vt_out, qn_out, kn_out):
    p = pb_ref[0]
    ones32 = _group_ones(GW, 5)
    first = (_iota((TM, GW), 1) & (QK_DIM - 1)) < QK_DIM // 2
    cos = cos_ref[...]
    sin = sin_ref[...]

    def norm_rope(t, g):
        ms = _dot(t * t, ones32, HI) * (1.0 / QK_DIM)
        tn = t * lax.rsqrt(ms + EPS) * g
        partner = jnp.where(first, pltpu.roll(tn, GW - QK_DIM // 2, 1), pltpu.roll(tn, QK_DIM // 2, 1))
        return tn * cos + partner * sin

    qf = norm_rope(p[:, :GW], qg_ref[...]) * (QK_DIM ** -0.5)
    kf = norm_rope(p[:, GW:2 * GW], kg_ref[...])
    map_sel = ((_iota((N_MAPS, GW), 1) >> 5) == _iota((N_MAPS, GW), 0)).astype(F32)
    qn_out[0, 0] = _dot_nt(map_sel, qf * qf, HI)
    kn_out[0, 0] = _dot_nt(map_sel, kf * kf, HI)
    q = qf.astype(BF16)
    k = kf.astype(BF16)
    for g in range(N_MAPS):
        q_out[0, 0, g] = q[:, g * QK_DIM:(g + 1) * QK_DIM]
        k_out[0, 0, g] = k[:, g * QK_DIM:(g + 1) * QK_DIM]
    vt = p[:, 2 * GW:].T.astype(BF16)
    ones = jnp.ones((HEAD_DIM, TM), BF16)
    for h in range(HEADS):
        vt_out[0, 0, h, 0:HEAD_DIM, :] = vt[h * HEAD_DIM:(h + 1) * HEAD_DIM, :]
        vt_out[0, 0, h, HEAD_DIM:, :] = ones


def _attn_prep(pb, qn_g, kn_g, cos_t, sin_t):
    b, t, w = pb.shape
    nt = t // TM
    vec = pl.BlockSpec((1, GW), lambda bb, i: (0, 0))
    tab = pl.BlockSpec((TM, GW), lambda bb, i: (i, 0))
    reps = GW // QK_DIM
    qk_spec = pl.BlockSpec((1, 1, N_MAPS, TM, QK_DIM), lambda bb, i: (bb, i, 0, 0, 0))
    qk_shape = jax.ShapeDtypeStruct((b, nt, N_MAPS, TM, QK_DIM), BF16)
    n_spec = pl.BlockSpec((1, 1, N_MAPS, TM), lambda bb, i: (bb, i, 0, 0))
    n_shape = jax.ShapeDtypeStruct((b, nt, N_MAPS, TM), F32)
    return pl.pallas_call(
        _attn_prep_kernel,
        grid=(b, nt),
        in_specs=[pl.BlockSpec((1, TM, w), lambda bb, i: (bb, i, 0)), vec, vec, tab, tab],
        out_specs=[qk_spec, qk_spec,
                   pl.BlockSpec((1, 1, HEADS, V_EXT, TM), lambda bb, i: (bb, i, 0, 0, 0)), n_spec, n_spec],
        out_shape=[qk_shape, qk_shape, jax.ShapeDtypeStruct((b, nt, HEADS, V_EXT, TM), BF16), n_shape, n_shape],
        compiler_params=_cp(("parallel", "arbitrary")),
    )(pb, jnp.tile(qn_g, reps).reshape(1, GW), jnp.tile(kn_g, reps).reshape(1, GW), cos_t, sin_t)


SHIFT_MAX = 40.0


def _attn_kernel(q_ref, k_ref, vt_ref, qn_ref, kn_ref, lam_ref, sg_ref, o_ref, m_scr, acc_scr, s_scr, *, n_chunks, lam_init):
    acc_scr[...] = jnp.zeros(acc_scr.shape, F32)
    k2 = jnp.max(jnp.max(kn_ref[0], axis=0), axis=-1, keepdims=True)
    bound = jnp.sqrt(qn_ref[0, 0] * k2)
    safe = jnp.max(bound) <= SHIFT_MAX

    @pl.when(safe)
    def _():
        for g in range(N_MAPS):
            s_scr[0, g] = _dot_nt(k_ref[0, 0, g], q_ref[0, 0, g])

        def body(c, carry):
            cur = c & 1
            nxt = jnp.minimum(c + 1, n_chunks - 1)
            for g in range(N_MAPS):
                s_next = _dot_nt(k_ref[0, nxt, g], q_ref[0, 0, g])
                p = jnp.exp(s_scr[cur, g] - bound[g:g + 1]).astype(BF16)
                acc_scr[g] += _dot(vt_ref[0, c, g // 2], p)
                s_scr[1 - cur, g] = s_next
            return carry

        lax.fori_loop(0, n_chunks, body, 0)

    @pl.when(jnp.logical_not(safe))
    def _():
        m_scr[...] = jnp.full(m_scr.shape, -jnp.inf, F32)

        def body(c, carry):
            for g in range(N_MAPS):
                s = _dot_nt(k_ref[0, c, g], q_ref[0, 0, g])
                m_old = m_scr[g]
                m_new = jnp.maximum(m_old, jnp.max(s, axis=0, keepdims=True))
                p = jnp.exp(s - m_new).astype(BF16)
                acc_scr[g] = jnp.exp(m_old - m_new) * acc_scr[g] + _dot(vt_ref[0, c, g // 2], p)
                m_scr[g] = m_new
            return carry

        lax.fori_loop(0, n_chunks, body, 0)

    lp = lam_ref[...]
    lam = (jnp.exp(jnp.sum(lp[0:1] * lp[1:2], axis=-1, keepdims=True))
           - jnp.exp(jnp.sum(lp[2:3] * lp[3:4], axis=-1, keepdims=True)) + lam_init)
    outs = []
    for h in range(HEADS):
        a0 = acc_scr[2 * h]
        a1 = acc_scr[2 * h + 1]
        o = a0[:HEAD_DIM] / a0[HEAD_DIM:HEAD_DIM + 1] - lam * (a1[:HEAD_DIM] / a1[HEAD_DIM:HEAD_DIM + 1])
        outs.append(o * lax.rsqrt(jnp.mean(o * o, axis=0, keepdims=True) + EPS) * sg_ref[...] * (1.0 - lam_init))
    o_ref[0] = jnp.concatenate(outs, axis=0).T


def _attention(q, k, vt, qn, kn, lam_p, subln_g, lam_init, first_tile, n_q, n_kv, t_out):
    b = q.shape[0]
    kern = functools.partial(_attn_kernel, n_chunks=n_kv, lam_init=lam_init)
    return pl.pallas_call(
        kern,
        grid=(b, n_q),
        in_specs=[pl.BlockSpec((1, 1, N_MAPS, TM, QK_DIM), lambda bb, i: (bb, i + first_tile, 0, 0, 0)),
                  pl.BlockSpec((1, n_kv, N_MAPS, TM, QK_DIM), lambda bb, i: (bb, 0, 0, 0, 0)),
                  pl.BlockSpec((1, n_kv, HEADS, V_EXT, TM), lambda bb, i: (bb, 0, 0, 0, 0)),
                  pl.BlockSpec((1, 1, N_MAPS, TM), lambda bb, i: (bb, i + first_tile, 0, 0)),
                  pl.BlockSpec((1, n_kv, N_MAPS, TM), lambda bb, i: (bb, 0, 0, 0)),
                  pl.BlockSpec((4, QK_DIM), lambda bb, i: (0, 0)),
                  pl.BlockSpec((HEAD_DIM, 1), lambda bb, i: (0, 0))],
        out_specs=pl.BlockSpec((1, TM, GW), lambda bb, i: (bb, i, 0)),
        out_shape=jax.ShapeDtypeStruct((b, t_out, GW), F32),
        scratch_shapes=[pltpu.VMEM((N_MAPS, 1, TM), F32), pltpu.VMEM((N_MAPS, V_EXT, TM), F32),
                        pltpu.VMEM((2, N_MAPS, TM, TM), F32)],
        compiler_params=_cp(("parallel", "arbitrary")),
    )(q, k, vt, qn, kn, lam_p, subln_g.reshape(HEAD_DIM, 1))


def _chunk_masks():
    i = _iota((CH, GW), 0)
    j = _iota((CH, GW), 1) & (CH - 1)
    eye = (i == j).astype(F32)
    incl = (j <= i, j >= i)
    strict = (j < i, j > i)
    r = _iota((CH, CH), 0)
    c = _iota((CH, CH), 1)
    cum = ((c <= r).astype(F32), (c >= r).astype(F32))
    bd = (_iota((GW, GW), 0) >> 6) == (_iota((GW, GW), 1) >> 6)
    return eye, incl, strict, cum, bd


def _bd(y, bd):
    return jnp.where(bd, _tile4(y), 0.0).astype(BF16)


def _rev_tile(s, n_tiles):
    return jnp.where(s == 0, 0, n_tiles - s)


DN_HALO = 8


def _dn_prep_kernel(pm_ref, pl_ref, pr_ref, pcs_ref, cw_ref, alog_ref, dtb_ref, q_out, k_out, v_out, bg_out, ext_ref):
    left_ok, right_ok = _halo_ok(pl.program_id(1), pl.num_programs(1))
    ext_ref[0:DN_HALO] = jnp.where(left_ok, pl_ref[0], 0.0)
    ext_ref[DN_HALO:DN_HALO + TM] = pm_ref[0]
    ext_ref[DN_HALO + TM:] = jnp.where(right_ok, pr_ref[0], 0.0)
    pad = DN_CONV // 2
    acc = jnp.zeros((TM, 3 * GW), F32)
    for j in range(DN_CONV):
        acc = acc + cw_ref[j:j + 1, :] * ext_ref[pl.ds(DN_HALO - pad + j, TM), :]
    qkv = _silu(acc)
    ones64 = _group_ones(GW, 6)

    def l2n(t):
        return t * lax.rsqrt(_dot(t * t, ones64, HI) + EPS)

    q_out[0] = l2n(qkv[:, :GW]) * (HEAD_DIM ** -0.5)
    k_out[0] = l2n(qkv[:, GW:2 * GW])
    v_out[0] = qkv[:, 2 * GW:]
    s = pcs_ref[0]
    col = _iota(s.shape, 1)
    gate = -jnp.exp(alog_ref[...]) * _softplus(s + dtb_ref[...])
    bg_out[0] = jnp.where(col < 2 * HEADS, _sigmoid(s), jnp.where(col < 4 * HEADS, gate, 0.0))


def _dn_prep(pc, pcs, conv_w, a_log, dt_bias):
    b, t, w = pc.shape
    nt = t // TM
    left, right = _halo_specs(w, DN_HALO, nt)
    row = lambda bb, i: (bb, i, 0)
    pad_vec = lambda a: jnp.zeros((1, LANES), F32).at[0, 2 * HEADS:4 * HEADS].set(a.reshape(-1))
    vec = pl.BlockSpec((1, LANES), lambda bb, i: (0, 0))
    return pl.pallas_call(
        _dn_prep_kernel,
        grid=(b, nt),
        in_specs=[pl.BlockSpec((1, TM, w), row), left, right, pl.BlockSpec((1, TM, LANES), row),
                  pl.BlockSpec((DN_CONV, w), lambda bb, i: (0, 0)), vec, vec],
        out_specs=[pl.BlockSpec((1, TM, GW), row)] * 3 + [pl.BlockSpec((1, TM, LANES), row)],
        out_shape=[jax.ShapeDtypeStruct((b, t, GW), F32)] * 3 + [jax.ShapeDtypeStruct((b, t, LANES), F32)],
        scratch_shapes=[pltpu.VMEM((TM + 2 * DN_HALO, w), F32)],
        compiler_params=_cp(("parallel", "arbitrary")),
    )(pc, pc, pc, pcs, conv_w, pad_vec(a_log), pad_vec(dt_bias))


def _dn_local_kernel(q_ref, k_ref, v_ref, bg_ref, u_out, w_out, qi_out, a_out, ket_out, ge_out):
    rr = _iota((TM, GW), 0)
    cc = _iota((TM, GW), 1)
    i_in = rr & (CH - 1)
    j_in = cc & (CH - 1)
    bd = (rr >> 6) == (cc >> 6)
    eye_t = (i_in == j_in).astype(F32)
    incl = (j_in <= i_in, j_in >= i_in)
    strict = (j_in < i_in, j_in > i_in)
    ones_bd = bd.astype(F32)
    eye_bf = (rr == cc).astype(BF16)
    head_of_lane = _iota((LANES, GW), 1) >> 6
    src = _iota((LANES, GW), 0)
    q = q_ref[0]
    k = k_ref[0]
    v = v_ref[0]
    bg = bg_ref[0]
    beta, gcum, decay, kb, egc = [], [], [], [], []
    for d in range(2):
        sel_b = (src == d * HEADS + head_of_lane).astype(F32)
        sel_g = (src == 2 * HEADS + d * HEADS + head_of_lane).astype(F32)
        cum_bd = jnp.where(jnp.logical_and(bd, incl[d]), 1.0, 0.0)
        beta.append(_dot(bg, sel_b, HI))
        gcum.append(_dot(_dot(cum_bd, bg, HI), sel_g, HI))
        grow = _dot(ones_bd, gcum[d] * eye_t, HI)
        decay.append(jnp.where(incl[d], jnp.exp(jnp.where(incl[d], gcum[d] - grow, 0.0)), 0.0))
        kb.append(k * beta[d])
        egc.append(jnp.exp(gcum[d]))
        qi_out[0, d] = (q * egc[d]).astype(BF16)
    pairs = [(c, d) for c in range(CPT) for d in range(2)]
    rows = [slice(c * CH, (c + 1) * CH) for c in range(CPT)]
    eye = eye_t[:CH]
    a = {}
    for c in range(CPT):
        r = rows[c]
        lhs = jnp.concatenate([kb[0][r], kb[1][r], q[r]], axis=0).astype(BF16)
        aq = _dot_nt(lhs, _bd(k[r], bd))
        for d in range(2):
            dec = decay[d][r]
            a[c, d] = jnp.where(strict[d][:CH], aq[d * CH:(d + 1) * CH] * dec, 0.0)
            a_out[0, d, r, :] = jnp.where(incl[d][:CH], aq[2 * CH:] * dec, 0.0).astype(BF16)
    t_inv = {cd: eye - a[cd] for cd in pairs}
    p = {cd: _dot(a[cd].astype(BF16), _bd(a[cd], bd)) for cd in pairs}
    for it in range(5):
        for cd in pairs:
            pbd = _bd(p[cd], bd)
            if it < 4:
                res = _dot(jnp.concatenate([t_inv[cd], p[cd]], axis=0).astype(BF16), pbd)
                t_inv[cd] = t_inv[cd] + res[:CH]
                p[cd] = res[CH:]
            else:
                t_inv[cd] = t_inv[cd] + _dot(t_inv[cd].astype(BF16), pbd)
    for cd in pairs:
        x0 = t_inv[cd]
        resid = eye - x0 - _dot(a[cd], jnp.where(bd, _tile4(x0), 0.0), HI)
        t_inv[cd] = x0 + _dot(x0.astype(BF16), _bd(resid, bd))
    for c, d in pairs:
        r = rows[c]
        tb = t_inv[c, d].astype(BF16)
        u_out[0, d, r, :] = _dot(tb, _bd(v[r] * beta[d][r], bd))
        w_out[0, d, r, :] = _dot(tb, _bd(kb[d][r] * egc[d][r], bd)).astype(BF16)
        last = (c + 1) * CH - 1 if d == 0 else c * CH
        gtot = gcum[d][last:last + 1]
        k_end = (k[r] * jnp.exp(gtot - gcum[d][r])).astype(BF16)
        ket_out[0, d, c] = _dot_nt(eye_bf, k_end).astype(BF16)
        ge_out[0, d, c] = jnp.exp(gtot)


def _dn_local(q, k, v, bg):
    b, t, _ = q.shape
    nt = t // TM
    row = lambda bb, i: (bb, i, 0)
    drow = pl.BlockSpec((1, 2, TM, GW), lambda bb, i: (bb, 0, i, 0))
    return pl.pallas_call(
        _dn_local_kernel,
        grid=(b, nt),
        in_specs=[pl.BlockSpec((1, TM, GW), row)] * 3 + [pl.BlockSpec((1, TM, LANES), row)],
        out_specs=[drow, drow, drow, drow,
                   pl.BlockSpec((1, 2, CPT, GW, CH), lambda bb, i: (bb, 0, i, 0, 0)),
                   pl.BlockSpec((1, 2, CPT, 1, GW), lambda bb, i: (bb, 0, i, 0, 0))],
        out_shape=[jax.ShapeDtypeStruct((b, 2, t, GW), F32)] + [jax.ShapeDtypeStruct((b, 2, t, GW), BF16)] * 3
        + [jax.ShapeDtypeStruct((b, 2, t // CH, GW, CH), BF16), jax.ShapeDtypeStruct((b, 2, t // CH, 1, GW), F32)],
        compiler_params=_cp(("parallel", "arbitrary")),
    )(q, k, v, bg)


def _dn_scan_kernel(uf, wf, qf, af, kf, gf, ur, wr, qr, ar, kr, gr, of_ref, or_ref, s_scr):
    @pl.when(pl.program_id(1) == 0)
    def _():
        s_scr[...] = jnp.zeros(s_scr.shape, F32)

    bd = (_iota((GW, GW), 0) >> 6) == (_iota((GW, GW), 1) >> 6)
    dirs = ((uf, wf, qf, af, kf, gf, of_ref), (ur, wr, qr, ar, kr, gr, or_ref))
    for c in range(CPT):
        for d, (u, w, qi, a, ket, ge, o_ref) in enumerate(dirs):
            cc = c if d == 0 else CPT - 1 - c
            rows = slice(cc * CH, (cc + 1) * CH)
            s = s_scr[d]
            wq = _dot(jnp.concatenate([w[0, 0, rows, :], qi[0, 0, rows, :]], axis=0), s.astype(BF16))
            v_new = u[0, 0, rows, :] - wq[:CH]
            o_ref[0, rows, :] = wq[CH:] + _dot(a[0, 0, rows, :], _bd(v_new, bd))
            s_scr[d] = s * ge[0, 0, cc] + jnp.where(bd, _dot(ket[0, 0, cc], v_new.astype(BF16)), 0.0)


def _dir_specs(shape_tail, n_tiles, chunked):
    blk = (1, 1, CPT if chunked else TM) + shape_tail
    zeros = (0,) * len(shape_tail)
    fwd = pl.BlockSpec(blk, lambda b, s: (b, 0, s) + zeros)
    rev = pl.BlockSpec(blk, lambda b, s: (b, 1, _rev_tile(s, n_tiles)) + zeros)
    return fwd, rev


def _dn_scan(u, w, qi, a, ket, ge):
    b, _, t, _ = u.shape
    nt = t // TM
    rowf, rowr = _dir_specs((GW,), nt, False)
    ketf, ketr = _dir_specs((GW, CH), nt, True)
    gef, ger = _dir_specs((1, GW), nt, True)
    return pl.pallas_call(
        _dn_scan_kernel,
        grid=(b, nt),
        in_specs=[rowf, rowf, rowf, rowf, ketf, gef, rowr, rowr, rowr, rowr, ketr, ger],
        out_specs=[pl.BlockSpec((1, TM, GW), lambda bb, s: (bb, s, 0)),
                   pl.BlockSpec((1, TM, GW), lambda bb, s: (bb, _rev_tile(s, nt), 0))],
        out_shape=[jax.ShapeDtypeStruct((b, t, GW), F32)] * 2,
        scratch_shapes=[pltpu.VMEM((2, GW, GW), F32)],
        compiler_params=_cp(("parallel", "arbitrary")),
    )(u, w, qi, a, ket, ge, u, w, qi, a, ket, ge)


GLA_QK = HEADS * GLA_K


def _gla_local_kernel(pd_ref, pdl_ref, w2_ref, b2_ref, qi_out, a_out, ke_out, vt_out, vb_out, de_out):
    _, incl, _, cum, _ = _chunk_masks()
    eye_bf = (_iota((GW, GW), 0) == _iota((GW, GW), 1)).astype(BF16)
    bdk = (_iota((GW, GLA_QK), 0) >> 6) == (_iota((GW, GLA_QK), 1) >> 5)
    z = _dot(pdl_ref[0], w2_ref[...], HI) + b2_ref[...]
    gk_all = -_softplus(-z) * (1.0 / GLA_TAU)
    for c in range(CPT):
        rows = slice(c * CH, (c + 1) * CH)
        p = pd_ref[0, rows, :]
        q = p[:, :GLA_QK] * (GLA_K ** -0.5)
        k = p[:, GLA_QK:2 * GLA_QK]
        vb = p[:, 2 * GLA_QK:].astype(BF16)
        vb_out[0, rows, :] = vb
        vt_out[0, c] = _dot_nt(eye_bf, vb).astype(BF16)
        for d in range(2):
            bcs = _dot(cum[d], gk_all[rows, d * GLA_QK:(d + 1) * GLA_QK], HI)
            bend = bcs[CH - 1:CH] if d == 0 else bcs[0:1]
            q_in = (q * jnp.exp(bcs)).astype(BF16)
            kdec = jnp.where(bdk, _tile4(k * jnp.exp(-bcs)), 0.0).astype(BF16)
            a_out[0, d, rows, :] = jnp.where(incl[d], _dot_nt(q_in, kdec), 0.0).astype(BF16)
            qi_out[0, d, rows, :] = q_in
            ke_out[0, d, rows, :] = (k * jnp.exp(bend - bcs)).astype(BF16)
            de_out[0, d, c] = jnp.exp(bend)


def _gla_local(pd, pdl, w2bd, b2):
    b, t, w = pd.shape
    nt = t // TM
    row = lambda bb, i: (bb, i, 0)
    return pl.pallas_call(
        _gla_local_kernel,
        grid=(b, nt),
        in_specs=[pl.BlockSpec((1, TM, w), row), pl.BlockSpec((1, TM, LANES), row),
                  pl.BlockSpec((LANES, GW), lambda bb, i: (0, 0)), pl.BlockSpec((1, GW), lambda bb, i: (0, 0))],
        out_specs=[pl.BlockSpec((1, 2, TM, GLA_QK), lambda bb, i: (bb, 0, i, 0)),
                   pl.BlockSpec((1, 2, TM, GW), lambda bb, i: (bb, 0, i, 0)),
                   pl.BlockSpec((1, 2, TM, GLA_QK), lambda bb, i: (bb, 0, i, 0)),
                   pl.BlockSpec((1, CPT, GW, CH), lambda bb, i: (bb, i, 0, 0)),
                   pl.BlockSpec((1, TM, GW), row),
                   pl.BlockSpec((1, 2, CPT, 1, GLA_QK), lambda bb, i: (bb, 0, i, 0, 0))],
        out_shape=[jax.ShapeDtypeStruct((b, 2, t, GLA_QK), BF16), jax.ShapeDtypeStruct((b, 2, t, GW), BF16),
                   jax.ShapeDtypeStruct((b, 2, t, GLA_QK), BF16), jax.ShapeDtypeStruct((b, t // CH, GW, CH), BF16),
                   jax.ShapeDtypeStruct((b, t, GW), BF16), jax.ShapeDtypeStruct((b, 2, t // CH, 1, GLA_QK), F32)],
        compiler_params=_cp(("parallel", "arbitrary")),
    )(pd, pdl, w2bd, b2)


def _gla_scan_kernel(qf, af, kf, df, vtf, vf, qr, ar, kr, dr, vtr, vr, of_ref, or_ref, s_scr):
    @pl.when(pl.program_id(1) == 0)
    def _():
        s_scr[...] = jnp.zeros(s_scr.shape, F32)

    bd = (_iota((GW, GW), 0) >> 6) == (_iota((GW, GW), 1) >> 6)
    bdt = (_iota((GW, GLA_QK), 0) >> 6) == (_iota((GW, GLA_QK), 1) >> 5)
    dirs = ((qf, af, kf, df, vtf, vf, of_ref), (qr, ar, kr, dr, vtr, vr, or_ref))
    for c in range(CPT):
        for d, (qi, a, ke, de, vt, v, o_ref) in enumerate(dirs):
            cc = c if d == 0 else CPT - 1 - c
            rows = slice(cc * CH, (cc + 1) * CH)
            st = s_scr[d]
            vbd = jnp.where(bd, _tile4(v[0, rows, :]), jnp.zeros((), BF16))
            o_ref[0, rows, :] = _dot_nt(qi[0, 0, rows, :], st.astype(BF16)) + _dot(a[0, 0, rows, :], vbd)
            s_scr[d] = st * de[0, 0, cc] + jnp.where(bdt, _dot(vt[0, cc], ke[0, 0, rows, :]), 0.0)


def _gla_scan(qi, a, ke, de, vt, vb):
    b, _, t, _ = a.shape
    nt = t // TM
    qf, qr = _dir_specs((GLA_QK,), nt, False)
    af, ar = _dir_specs((GW,), nt, False)
    df, dr = _dir_specs((1, GLA_QK), nt, True)
    vtf = pl.BlockSpec((1, CPT, GW, CH), lambda bb, s: (bb, s, 0, 0))
    vtr = pl.BlockSpec((1, CPT, GW, CH), lambda bb, s: (bb, _rev_tile(s, nt), 0, 0))
    vf = pl.BlockSpec((1, TM, GW), lambda bb, s: (bb, s, 0))
    vr = pl.BlockSpec((1, TM, GW), lambda bb, s: (bb, _rev_tile(s, nt), 0))
    return pl.pallas_call(
        _gla_scan_kernel,
        grid=(b, nt),
        in_specs=[qf, af, qf, df, vtf, vf, qr, ar, qr, dr, vtr, vr],
        out_specs=[vf, vr],
        out_shape=[jax.ShapeDtypeStruct((b, t, GW), F32)] * 2,
        scratch_shapes=[pltpu.VMEM((2, GW, GLA_QK), F32)],
        compiler_params=_cp(("parallel", "arbitrary")),
    )(qi, a, ke, de, vt, vb, qi, a, ke, de, vt, vb)


def _outproj_kernel(x_ref, ya_ref, yb_ref, cf_ref, cr_ref, df_ref, dr_ref, pg_ref, g1_ref, gc_ref, gd_ref, w_ref, o_ref):
    ones64 = _group_ones(GW, 6)

    def fin(o, g, gate):
        ms = _dot(o * o, ones64, HI) * (1.0 / HEAD_DIM)
        return (o * lax.rsqrt(ms + EPS) * g * _silu(gate)).astype(BF16)

    pg = pg_ref[0]
    yc = fin(cf_ref[0] + cr_ref[0], gc_ref[...], pg[:, :GW])
    yd = fin(df_ref[0] + dr_ref[0], gd_ref[...], pg[:, GW:])
    res = (_dot(ya_ref[0].astype(BF16), w_ref[0:GW, :]) + _dot(yb_ref[0].astype(BF16), w_ref[GW:2 * GW, :])
           + _dot(yc, w_ref[2 * GW:3 * GW, :]) + _dot(yd, w_ref[3 * GW:, :]))
    o_ref[0] = x_ref[0] + g1_ref[0] * res


def _outproj(xs, ya, yb, ocf, ocr, odf, odr, pg, mod3, dn_g, gla_g, w_out_bf, n_batch, first_tile, yb_first):
    b, t, d = xs.shape
    n = t // TM - first_tile
    row = lambda bb, i: (bb, i + first_tile, 0)
    g256 = pl.BlockSpec((1, TM, GW), row)
    vec = pl.BlockSpec((1, GW), lambda bb, i: (0, 0))
    reps = GW // HEAD_DIM
    return pl.pallas_call(
        _outproj_kernel,
        grid=(b, n),
        in_specs=[pl.BlockSpec((1, TM, d), row), g256,
                  pl.BlockSpec((1, TM, GW), lambda bb, i: (bb, i + first_tile - yb_first, 0)),
                  g256, g256, g256, g256, pl.BlockSpec((1, TM, 2 * GW), row),
                  _mod_spec(2, n_batch, first_tile), vec, vec, pl.BlockSpec((d, d), lambda bb, i: (0, 0))],
        out_specs=pl.BlockSpec((1, TM, d), lambda bb, i: (bb, i, 0)),
        out_shape=jax.ShapeDtypeStruct((b, n * TM, d), F32),
        compiler_params=_cp(("parallel", "arbitrary")),
    )(xs, ya, yb, ocf, ocr, odf, odr, pg, mod3, jnp.tile(dn_g, reps).reshape(1, GW),
      jnp.tile(gla_g, reps).reshape(1, GW), w_out_bf)


FF_HALO = 8


def _ffn_kernel(xm_ref, xl_ref, xr_ref, sh_ref, sc_ref, g2_ref, ng_ref, wa_ref, wg_ref, cwa_ref, cwg_ref, wd_ref,
                o_ref, ext_ref, *, first_tile, n_tiles):
    left_ok, right_ok = _halo_ok(pl.program_id(1) + first_tile, n_tiles)
    rows_ext = _iota((TM + 2 * FF_HALO, 1), 0)
    keep = jnp.logical_and(jnp.logical_or(rows_ext >= FF_HALO, left_ok),
                           jnp.logical_or(rows_ext < FF_HALO + TM, right_ok)).astype(F32)
    x = jnp.concatenate([xl_ref[0], xm_ref[0], xr_ref[0]], axis=0)
    y = x * lax.rsqrt(jnp.mean(x * x, axis=-1, keepdims=True) + EPS) * ng_ref[...]
    h = ((y * (1.0 + sc_ref[0]) + sh_ref[0]) * keep).astype(BF16)
    acc = jnp.zeros((TM, D_MODEL), F32)
    for j in range(N_FF_BLK):
        def conv(w_ref, cw_ref, half):
            ext_ref[half] = _dot(h, w_ref[j])
            cw = cw_ref[j]
            return (cw[0:1] * ext_ref[half, pl.ds(FF_HALO - 1, TM), :] + cw[1:2] * ext_ref[half, pl.ds(FF_HALO, TM), :]
                    + cw[2:3] * ext_ref[half, pl.ds(FF_HALO + 1, TM), :])
        a = conv(wa_ref, cwa_ref, 0)
        g = conv(wg_ref, cwg_ref, 1)
        acc = acc + _dot((_silu(g) * a).astype(BF16), wd_ref[j])
    o_ref[0] = xm_ref[0] + g2_ref[0] * acc


def _ffn(x1, mod3, norm_g, wa, wg, cwa, cwg, wd, n_batch, first_tile, n_tiles_total):
    b, t, d = x1.shape
    n = t // TM
    per = TM // FF_HALO
    left = pl.BlockSpec((1, FF_HALO, d), lambda bb, i: (bb, jnp.maximum(i * per - 1, 0), 0))
    right = pl.BlockSpec((1, FF_HALO, d), lambda bb, i: (bb, jnp.minimum((i + 1) * per, n * per - 1), 0))
    const3 = lambda bb, i: (0, 0, 0)
    kern = functools.partial(_ffn_kernel, first_tile=first_tile, n_tiles=n_tiles_total)
    return pl.pallas_call(
        kern,
        grid=(b, n),
        in_specs=[pl.BlockSpec((1, TM, d), lambda bb, i: (bb, i, 0)), left, right,
                  _mod_spec(3, n_batch, first_tile), _mod_spec(4, n_batch, first_tile), _mod_spec(5, n_batch, first_tile),
                  pl.BlockSpec((1, d), lambda bb, i: (0, 0)),
                  pl.BlockSpec((N_FF_BLK, d, FF_BLK), const3), pl.BlockSpec((N_FF_BLK, d, FF_BLK), const3),
                  pl.BlockSpec((N_FF_BLK, 3, FF_BLK), const3), pl.BlockSpec((N_FF_BLK, 3, FF_BLK), const3),
                  pl.BlockSpec((N_FF_BLK, FF_BLK, d), const3)],
        out_specs=pl.BlockSpec((1, TM, d), lambda bb, i: (bb, i, 0)),
        out_shape=jax.ShapeDtypeStruct((b, t, d), F32),
        scratch_shapes=[pltpu.VMEM((2, TM + 2 * FF_HALO, FF_BLK), F32)],
        compiler_params=_cp(("parallel", "arbitrary")),
    )(x1, x1, x1, mod3, mod3, mod3, norm_g.reshape(1, d), wa, wg, cwa, cwg, wd)


def _rope_tables(seq, ctx_len):
    rows = seq // GRID_W
    row = jnp.repeat(jnp.arange(rows, dtype=F32), GRID_W)
    col = jnp.tile(jnp.arange(GRID_W, dtype=F32), rows)
    nf = QK_DIM // 4
    inv = ROPE_THETA ** (-jnp.arange(nf, dtype=F32) / nf)
    ang = jnp.concatenate([row[:, None] * inv, col[:, None] * inv], axis=-1)
    cos = jnp.concatenate([jnp.ones((ctx_len, QK_DIM // 2), F32), jnp.cos(ang)], axis=0)
    sin = jnp.concatenate([jnp.zeros((ctx_len, QK_DIM // 2), F32), jnp.sin(ang)], axis=0)
    reps = GW // QK_DIM
    return (jnp.tile(jnp.concatenate([cos, cos], axis=-1), (1, reps)),
            jnp.tile(jnp.concatenate([-sin, sin], axis=-1), (1, reps)))


def _regroup_w_in(w):
    d = w.shape[0]
    z = lambda n: jnp.zeros((d, n), w.dtype)
    return jnp.concatenate([w[:, :2048], w[:, 2048:2064], z(LANES - 16), w[:, 2320:2832], w[:, 2832:2864],
                            z(LANES - 32), w[:, 2064:2320], w[:, 2864:3120]], axis=1).astype(BF16)


def _gla_w2_blockdiag(w2):
    out = jnp.zeros((LANES, GW), F32)
    out = out.at[0:GLA_RANK, 0:GLA_QK].set(w2[0])
    return out.at[GLA_RANK:2 * GLA_RANK, GLA_QK:].set(w2[1])


def _layer(xs, mod3, lp, cos_t, sin_t, layer_idx, last, n_batch):
    b, t, d = xs.shape
    nt = t // TM
    pa, pb, pc, pcs, pd, pdl, pg = _inproj(xs, mod3, lp["norm1_g"], _regroup_w_in(lp["w_in"]), n_batch)

    ya = _conv_module(pa, lp["cm_conv_w"], lp["cm_conv_b"], lp["cm_ln_g"], lp["cm_ln_b"])

    lam_init = 0.8 - 0.6 * math.exp(-0.3 * layer_idx)
    q, k, vt, qn, kn = _attn_prep(pb, lp["da_qnorm_g"], lp["da_knorm_g"], cos_t, sin_t)
    yb = _attention(q, k, vt, qn, kn, lp["da_lambda"], lp["da_subln_g"], lam_init, 1, nt - 1, nt, t - TM)
    yb_first = 1
    if not last:
        yb_ctx = _attention(q, k, vt, qn, kn, lp["da_lambda"], lp["da_subln_g"], lam_init, 0, 1, 1, TM)
        yb = jnp.concatenate([yb_ctx, yb], axis=1)
        yb_first = 0

    dq, dk, dv, bg = _dn_prep(pc, pcs, lp["dn_conv_w"], lp["dn_a_log"], lp["dn_dt_bias"])
    ocf, ocr = _dn_scan(*_dn_local(dq, dk, dv, bg))

    qi, a, ke, vt, vb, de = _gla_local(pd, pdl, _gla_w2_blockdiag(lp["gla_w2"]), lp["gla_b2"].reshape(1, GW))
    odf, odr = _gla_scan(qi, a, ke, de, vt, vb)

    first_tile = 1 if last else 0
    x1 = _outproj(xs, ya, yb, ocf, ocr, odf, odr, pg, mod3, lp["dn_onorm_g"], lp["gla_onorm_g"],
                  lp["w_out"].astype(BF16), n_batch, first_tile, yb_first)

    w_up = lp["ffn_w_up"].astype(BF16)
    blocks = lambda m: jnp.transpose(m.reshape(m.shape[0], N_FF_BLK, FF_BLK), (1, 0, 2))
    cw = lp["ffn_conv_w"]
    return _ffn(x1, mod3, lp["norm2_g"], blocks(w_up[:, :D_FF]), blocks(w_up[:, D_FF:]),
                blocks(cw[:, :D_FF]), blocks(cw[:, D_FF:]),
                lp["ffn_w_down"].astype(BF16).reshape(N_FF_BLK, FF_BLK, d), n_batch, first_tile, nt)


def kernel(x, c, ctx, c_ctx, w_mod, b_mod, norm1_g, norm2_g, w_in, w_out, cm_conv_w, cm_conv_b, cm_ln_g, cm_ln_b, da_qnorm_g, da_knorm_g, da_lambda, da_subln_g, dn_conv_w, dn_a_log, dn_dt_bias, dn_onorm_g, gla_w2, gla_b2, gla_onorm_g, ffn_w_up, ffn_conv_w, ffn_w_down):
    n_batch, seq, d = x.shape
    ctx_len = ctx.shape[1]
    assert ctx_len == TM and seq % TM == 0 and d == D_MODEL
    depth = w_mod.shape[0]
    cos_t, sin_t = _rope_tables(seq, ctx_len)
    xs = jnp.concatenate([ctx, x], axis=1)
    mod_rows = 16
    c_rows = jnp.zeros((mod_rows, d), F32).at[:n_batch].set(c).at[n_batch].set(c_ctx)
    params = dict(w_mod=w_mod, b_mod=b_mod, norm1_g=norm1_g, norm2_g=norm2_g, w_in=w_in, w_out=w_out,
                  cm_conv_w=cm_conv_w, cm_conv_b=cm_conv_b, cm_ln_g=cm_ln_g, cm_ln_b=cm_ln_b,
                  da_qnorm_g=da_qnorm_g, da_knorm_g=da_knorm_g, da_lambda=da_lambda, da_subln_g=da_subln_g,
                  dn_conv_w=dn_conv_w, dn_a_log=dn_a_log, dn_dt_bias=dn_dt_bias, dn_onorm_g=dn_onorm_g,
                  gla_w2=gla_w2, gla_b2=gla_b2, gla_onorm_g=gla_onorm_g,
                  ffn_w_up=ffn_w_up, ffn_conv_w=ffn_conv_w, ffn_w_down=ffn_w_down)
    for l in range(depth):
        lp = {k: v[l] for k, v in params.items()}
        mod3 = _modulation(c_rows, lp["w_mod"], lp["b_mod"]).reshape(mod_rows * 6, 1, d)
        xs = _layer(xs, mod3, lp, cos_t, sin_t, l, l == depth - 1, n_batch)
    return xs
```

```python
import functools
import math

import jax
import jax.numpy as jnp
from jax import lax
from jax.experimental import pallas as pl
from jax.experimental.pallas import tpu as pltpu

F32 = jnp.float32
BF16 = jnp.bfloat16
HI = lax.Precision.HIGHEST
EPS = 1e-6

D_MODEL = 1024
GRID_W = 64
HEADS = 4
HEAD_DIM = 64
GW = 256
QK_DIM = 32
GLA_K = 32
GLA_RANK = 16
GLA_TAU = 16.0
CM_KERNEL = 31
DN_CONV = 5
ROPE_THETA = 10000.0
CH = 64
TM = 256
CPT = TM // CH
D_FF = 2816
FF_BLK = 256
N_FF_BLK = D_FF // FF_BLK
LANES = 128

IN_GROUPS = (("pa", 512), ("pb", 768), ("pc", 768), ("pcs", LANES), ("pd", 512), ("pdl", LANES), ("pg", 512))
IN_COLS_PAD = sum(w for _, w in IN_GROUPS)

VMEM_LIMIT = 56 * 1024 * 1024


def _cp(sem):
    return pltpu.CompilerParams(dimension_semantics=sem, vmem_limit_bytes=VMEM_LIMIT)


def _dot(a, b, prec=None):
    return jnp.dot(a, b, preferred_element_type=F32, precision=prec)


def _dot_nt(a, b, prec=None):
    return lax.dot_general(a, b, (((1,), (1,)), ((), ())), preferred_element_type=F32, precision=prec)


def _sigmoid(x):
    return 1.0 / (1.0 + jnp.exp(-x))


def _silu(x):
    return x * _sigmoid(x)


def _softplus(x):
    return jnp.maximum(x, 0.0) + jnp.log(1.0 + jnp.exp(-jnp.abs(x)))


def _iota(shape, dim):
    return lax.broadcasted_iota(jnp.int32, shape, dim)


def _group_ones(n, shift):
    return ((_iota((n, n), 0) >> shift) == (_iota((n, n), 1) >> shift)).astype(F32)


def _tile4(y):
    return jnp.concatenate([y, y, y, y], axis=0)


def _mod_kernel(c_ref, w_ref, b_ref, o_ref):
    o_ref[...] = _dot(_silu(c_ref[...]), w_ref[...], HI) + b_ref[...]


def _modulation(c_rows, w_mod, b_mod):
    r, d = c_rows.shape
    n = w_mod.shape[1] // d
    return pl.pallas_call(
        _mod_kernel,
        grid=(n,),
        in_specs=[pl.BlockSpec((r, d), lambda j: (0, 0)),
                  pl.BlockSpec((d, d), lambda j: (0, j)),
                  pl.BlockSpec((1, d), lambda j: (0, j))],
        out_specs=pl.BlockSpec((r, d), lambda j: (0, j)),
        out_shape=jax.ShapeDtypeStruct((r, n * d), F32),
        compiler_params=_cp(("arbitrary",)),
    )(c_rows, w_mod, b_mod.reshape(1, -1))


def _mod_spec(k, n_batch, ctx_tile):
    def imap(b, i):
        sel = jnp.where(i == ctx_tile, n_batch, b)
        return (sel * 6 + k, 0, 0)
    return pl.BlockSpec((1, 1, D_MODEL), imap)


def _inproj_kernel(x_ref, sh_ref, sc_ref, g_ref, w_ref, *outs):
    x = x_ref[0]
    y = x * lax.rsqrt(jnp.mean(x * x, axis=-1, keepdims=True) + EPS) * g_ref[...]
    h = y * (1.0 + sc_ref[0]) + sh_ref[0]
    p = _dot(h.astype(BF16), w_ref[...])
    off = 0
    for o_ref, (_, width) in zip(outs, IN_GROUPS):
        o_ref[0] = p[:, off:off + width]
        off += width


def _inproj(xs, mod3, norm_g, w_in_r, n_batch):
    b, t, d = xs.shape
    row = lambda bb, i: (bb, i, 0)
    return pl.pallas_call(
        _inproj_kernel,
        grid=(b, t // TM),
        in_specs=[pl.BlockSpec((1, TM, d), row), _mod_spec(0, n_batch, t // TM - 1), _mod_spec(1, n_batch, t // TM - 1),
                  pl.BlockSpec((1, d), lambda bb, i: (0, 0)),
                  pl.BlockSpec((d, IN_COLS_PAD), lambda bb, i: (0, 0))],
        out_specs=[pl.BlockSpec((1, TM, w), row) for _, w in IN_GROUPS],
        out_shape=[jax.ShapeDtypeStruct((b, t, w), F32) for _, w in IN_GROUPS],
        compiler_params=_cp(("parallel", "arbitrary")),
    )(xs, mod3, mod3, norm_g.reshape(1, d), w_in_r)


def _halo_specs(width, halo, n_tiles):
    per = TM // halo
    left = pl.BlockSpec((1, halo, width), lambda b, i: (b, jnp.maximum(i * per - 1, 0), 0))
    right = pl.BlockSpec((1, halo, width), lambda b, i: (b, jnp.minimum((i + 1) * per, n_tiles * per - 1), 0))
    return left, right


def _halo_ok(i, n_tiles):
    return jnp.logical_and(i >= 1, i < n_tiles - 1), i < n_tiles - 2


CM_HALO = 16


def _convmod_kernel(pm_ref, pl_ref, pr_ref, cw_ref, cb_ref, lg_ref, lb_ref, o_ref, ext_ref):
    left_ok, right_ok = _halo_ok(pl.program_id(1), pl.num_programs(1))

    def glu(p):
        return p[:, :GW] * _sigmoid(p[:, GW:])

    ext_ref[0:CM_HALO] = jnp.where(left_ok, glu(pl_ref[0]), 0.0)
    ext_ref[CM_HALO:CM_HALO + TM] = glu(pm_ref[0])
    ext_ref[CM_HALO + TM:] = jnp.where(right_ok, glu(pr_ref[0]), 0.0)
    pad = CM_KERNEL // 2
    acc = jnp.zeros((TM, GW), F32)
    for j in range(CM_KERNEL):
        acc = acc + cw_ref[j:j + 1, :] * ext_ref[pl.ds(CM_HALO - pad + j, TM), :]
    y = acc + cb_ref[...]
    mu = jnp.mean(y, axis=-1, keepdims=True)
    yc = y - mu
    var = jnp.mean(yc * yc, axis=-1, keepdims=True)
    o_ref[0] = _silu(yc * lax.rsqrt(var + EPS) * lg_ref[...] + lb_ref[...])


def _conv_module(pa, conv_w, conv_b, ln_g, ln_b):
    b, t, w = pa.shape
    nt = t // TM
    left, right = _halo_specs(w, CM_HALO, nt)
    vec = pl.BlockSpec((1, GW), lambda bb, i: (0, 0))
    return pl.pallas_call(
        _convmod_kernel,
        grid=(b, nt),
        in_specs=[pl.BlockSpec((1, TM, w), lambda bb, i: (bb, i, 0)), left, right,
                  pl.BlockSpec((CM_KERNEL, GW), lambda bb, i: (0, 0)), vec, vec, vec],
        out_specs=pl.BlockSpec((1, TM, GW), lambda bb, i: (bb, i, 0)),
        out_shape=jax.ShapeDtypeStruct((b, t, GW), F32),
        scratch_shapes=[pltpu.VMEM((TM + 2 * CM_HALO, GW), F32)],
        compiler_params=_cp(("parallel", "arbitrary")),
    )(pa, pa, pa, conv_w, conv_b.reshape(1, GW), ln_g.reshape(1, GW), ln_b.reshape(1, GW))


N_MAPS = 2 * HEADS
QK_PAD = 2 * QK_DIM
V_EXT = 2 * HEAD_DIM
TQ = 1024
SHIFT_MAX = 40.0


def _attn_prep_kernel(pb_ref, qg_ref, kg_ref, cos_ref, sin_ref, q_out, kt_out, v_out, qn_out, kn_out):
    p = pb_ref[0]
    ones32 = _group_ones(GW, 5)
    first = (_iota((TM, GW), 1) & (QK_DIM - 1)) < QK_DIM // 2
    cos = cos_ref[...]
    sin = sin_ref[...]

    def norm_rope(t, g):
        ms = _dot(t * t, ones32, HI) * (1.0 / QK_DIM)
        tn = t * lax.rsqrt(ms + EPS) * g
        partner = jnp.where(first, pltpu.roll(tn, GW - QK_DIM // 2, 1), pltpu.roll(tn, QK_DIM // 2, 1))
        return tn * cos + partner * sin

    qf = norm_rope(p[:, :GW], qg_ref[...]) * (QK_DIM ** -0.5)
    kf = norm_rope(p[:, GW:2 * GW], kg_ref[...])
    map_sel = ((_iota((GW, N_MAPS), 0) >> 5) == _iota((GW, N_MAPS), 1)).astype(F32)
    qn_out[0] = _dot(qf * qf, map_sel, HI)
    kn_out[0] = _dot(kf * kf, map_sel, HI)
    q = qf.astype(BF16)
    kt = kf.T.astype(BF16)
    k_tail = jnp.where(_iota((QK_DIM, TM), 0) == 0, 1.0, 0.0).astype(BF16)
    q_tail = jnp.zeros((TM, QK_DIM), BF16)
    for g in range(N_MAPS):
        q_out[0, 0, g, :, 0:QK_DIM] = q[:, g * QK_DIM:(g + 1) * QK_DIM]
        q_out[0, 0, g, :, QK_DIM:] = q_tail
        kt_out[0, 0, g, 0:QK_DIM, :] = kt[g * QK_DIM:(g + 1) * QK_DIM, :]
        kt_out[0, 0, g, QK_DIM:, :] = k_tail
    v = p[:, 2 * GW:].astype(BF16)
    ones = jnp.ones((TM, HEAD_DIM), BF16)
    for h in range(HEADS):
        v_out[0, 0, h, :, 0:HEAD_DIM] = v[:, h * HEAD_DIM:(h + 1) * HEAD_DIM]
        v_out[0, 0, h, :, HEAD_DIM:] = ones


def _attn_prep(pb, qn_g, kn_g, cos_t, sin_t):
    b, t, w = pb.shape
    nt = t // TM
    vec = pl.BlockSpec((1, GW), lambda bb, i: (0, 0))
    tab = pl.BlockSpec((TM, GW), lambda bb, i: (i, 0))
    reps = GW // QK_DIM
    n_spec = pl.BlockSpec((1, TM, N_MAPS), lambda bb, i: (bb, i, 0))
    n_shape = jax.ShapeDtypeStruct((b, t, N_MAPS), F32)
    return pl.pallas_call(
        _attn_prep_kernel,
        grid=(b, nt),
        in_specs=[pl.BlockSpec((1, TM, w), lambda bb, i: (bb, i, 0)), vec, vec, tab, tab],
        out_specs=[pl.BlockSpec((1, 1, N_MAPS, TM, QK_PAD), lambda bb, i: (bb, i, 0, 0, 0)),
                   pl.BlockSpec((1, 1, N_MAPS, QK_PAD, TM), lambda bb, i: (bb, i, 0, 0, 0)),
                   pl.BlockSpec((1, 1, HEADS, TM, V_EXT), lambda bb, i: (bb, i, 0, 0, 0)), n_spec, n_spec],
        out_shape=[jax.ShapeDtypeStruct((b, nt, N_MAPS, TM, QK_PAD), BF16),
                   jax.ShapeDtypeStruct((b, nt, N_MAPS, QK_PAD, TM), BF16),
                   jax.ShapeDtypeStruct((b, nt, HEADS, TM, V_EXT), BF16), n_shape, n_shape],
        compiler_params=_cp(("parallel", "arbitrary")),
    )(pb, jnp.tile(qn_g, reps).reshape(1, GW), jnp.tile(kn_g, reps).reshape(1, GW), cos_t, sin_t)


def _attn_kernel(q_ref, kt_ref, v_ref, qn_ref, kn_ref, lam_ref, sg_ref, *rest, n_chunks, lam_init):
    o_ref, qa_scr, acc_scr, s_scr, m_scr = rest[-5:]
    tq = o_ref.shape[1]
    acc_scr[...] = jnp.zeros(acc_scr.shape, F32)
    k2 = jnp.max(kn_ref[0], axis=0, keepdims=True)
    bound = jnp.sqrt(qn_ref[0] * k2)
    safe = jnp.max(bound) <= SHIFT_MAX
    shift = jnp.where(safe, bound, 0.0)
    shift_lane = _iota((tq, QK_PAD), 1) == QK_DIM
    for g in range(N_MAPS):
        qg = jnp.concatenate([q_ref[0, t, g] for t in range(tq // TM)], axis=0)
        qa_scr[g] = jnp.where(shift_lane, (-shift[:, g:g + 1]).astype(BF16), qg)

    @pl.when(safe)
    def _():
        s_scr[0] = _dot(qa_scr[0], kt_ref[0, 0, 0])

        def body(c, carry):
            nxt = jnp.minimum(c + 1, n_chunks - 1)
            for g in range(N_MAPS):
                if g + 1 < N_MAPS:
                    s_next = _dot(qa_scr[g + 1], kt_ref[0, c, g + 1])
                else:
                    s_next = _dot(qa_scr[0], kt_ref[0, nxt, 0])
                p = jnp.exp(s_scr[g & 1]).astype(BF16)
                acc_scr[g] += _dot(p, v_ref[0, c, g // 2])
                s_scr[(g + 1) & 1] = s_next
            return carry

        lax.fori_loop(0, n_chunks, body, 0)

    @pl.when(jnp.logical_not(safe))
    def _():
        m_scr[...] = jnp.full(m_scr.shape, -jnp.inf, F32)

        def body(c, carry):
            for g in range(N_MAPS):
                s = _dot(qa_scr[g], kt_ref[0, c, g])
                m_old = m_scr[:, g:g + 1]
                m_new = jnp.maximum(m_old, jnp.max(s, axis=-1, keepdims=True))
                p = jnp.exp(s - m_new).astype(BF16)
                acc_scr[g] = jnp.exp(m_old - m_new) * acc_scr[g] + _dot(p, v_ref[0, c, g // 2])
                m_scr[:, g:g + 1] = m_new
            return carry

        lax.fori_loop(0, n_chunks, body, 0)

    lp = lam_ref[...]
    lam = (jnp.exp(jnp.sum(lp[0:1] * lp[1:2], axis=-1, keepdims=True))
           - jnp.exp(jnp.sum(lp[2:3] * lp[3:4], axis=-1, keepdims=True)) + lam_init)
    for h in range(HEADS):
        a0 = acc_scr[2 * h]
        a1 = acc_scr[2 * h + 1]
        o = (a0[:, :HEAD_DIM] / a0[:, HEAD_DIM:HEAD_DIM + 1]
             - lam * (a1[:, :HEAD_DIM] / a1[:, HEAD_DIM:HEAD_DIM + 1]))
        y = o * lax.rsqrt(jnp.mean(o * o, axis=-1, keepdims=True) + EPS) * sg_ref[...] * (1.0 - lam_init)
        o_ref[0, :, h * HEAD_DIM:(h + 1) * HEAD_DIM] = y


def _attention(q, kt, v, qn, kn, lam_p, subln_g, lam_init, tq, q_blk0, n_q, n_kv, kv_blk, out_rows, prev=None):
    b = q.shape[0]
    qt = tq // TM
    kern = functools.partial(_attn_kernel, n_chunks=n_kv, lam_init=lam_init)
    in_specs = [pl.BlockSpec((1, qt, N_MAPS, TM, QK_PAD), lambda bb, i: (bb, i + q_blk0, 0, 0, 0)),
                pl.BlockSpec((1, n_kv, N_MAPS, QK_PAD, TM), lambda bb, i: (bb, kv_blk, 0, 0, 0)),
                pl.BlockSpec((1, n_kv, HEADS, TM, V_EXT), lambda bb, i: (bb, kv_blk, 0, 0, 0)),
                pl.BlockSpec((1, tq, N_MAPS), lambda bb, i: (bb, i + q_blk0, 0)),
                pl.BlockSpec((1, n_kv * TM, N_MAPS), lambda bb, i: (bb, kv_blk, 0)),
                pl.BlockSpec((4, QK_DIM), lambda bb, i: (0, 0)),
                pl.BlockSpec((1, HEAD_DIM), lambda bb, i: (0, 0))]
    args = [q, kt, v, qn, kn, lam_p, subln_g.reshape(1, HEAD_DIM)]
    aliases = {}
    if prev is not None:
        in_specs.append(pl.BlockSpec(memory_space=pl.ANY))
        aliases = {len(args): 0}
        args.append(prev)
    return pl.pallas_call(
        kern,
        grid=(b, n_q),
        in_specs=in_specs,
        out_specs=pl.BlockSpec((1, tq, GW), lambda bb, i: (bb, i + q_blk0, 0)),
        out_shape=jax.ShapeDtypeStruct((b, out_rows, GW), F32),
        scratch_shapes=[pltpu.VMEM((N_MAPS, tq, QK_PAD), BF16), pltpu.VMEM((N_MAPS, tq, V_EXT), F32),
                        pltpu.VMEM((2, tq, TM), F32), pltpu.VMEM((tq, N_MAPS), F32)],
        input_output_aliases=aliases,
        compiler_params=_cp(("parallel", "arbitrary")),
    )(*args)


def _chunk_masks():
    i = _iota((CH, GW), 0)
    j = _iota((CH, GW), 1) & (CH - 1)
    eye = (i == j).astype(F32)
    incl = (j <= i, j >= i)
    strict = (j < i, j > i)
    r = _iota((CH, CH), 0)
    c = _iota((CH, CH), 1)
    cum = ((c <= r).astype(F32), (c >= r).astype(F32))
    bd = (_iota((GW, GW), 0) >> 6) == (_iota((GW, GW), 1) >> 6)
    return eye, incl, strict, cum, bd


def _bd(y, bd):
    return jnp.where(bd, _tile4(y), 0.0).astype(BF16)


def _fwd_tile(s, n_tiles):
    return jnp.where(s == 0, n_tiles - 1, s - 1)


def _rev_tile(s, n_tiles):
    return jnp.where(s == 0, n_tiles - 1, n_tiles - 1 - s)


DN_HALO = 8


def _dn_prep_kernel(pm_ref, pl_ref, pr_ref, pcs_ref, cw_ref, alog_ref, dtb_ref, q_out, k_out, v_out, bg_out, ext_ref):
    left_ok, right_ok = _halo_ok(pl.program_id(1), pl.num_programs(1))
    ext_ref[0:DN_HALO] = jnp.where(left_ok, pl_ref[0], 0.0)
    ext_ref[DN_HALO:DN_HALO + TM] = pm_ref[0]
    ext_ref[DN_HALO + TM:] = jnp.where(right_ok, pr_ref[0], 0.0)
    pad = DN_CONV // 2
    acc = jnp.zeros((TM, 3 * GW), F32)
    for j in range(DN_CONV):
        acc = acc + cw_ref[j:j + 1, :] * ext_ref[pl.ds(DN_HALO - pad + j, TM), :]
    qkv = _silu(acc)
    ones64 = _group_ones(GW, 6)

    def l2n(t):
        return t * lax.rsqrt(_dot(t * t, ones64, HI) + EPS)

    q_out[0] = l2n(qkv[:, :GW]) * (HEAD_DIM ** -0.5)
    k_out[0] = l2n(qkv[:, GW:2 * GW])
    v_out[0] = qkv[:, 2 * GW:]
    s = pcs_ref[0]
    col = _iota(s.shape, 1)
    gate = -jnp.exp(alog_ref[...]) * _softplus(s + dtb_ref[...])
    bg_out[0] = jnp.where(col < 2 * HEADS, _sigmoid(s), jnp.where(col < 4 * HEADS, gate, 0.0))


def _dn_prep(pc, pcs, conv_w, a_log, dt_bias):
    b, t, w = pc.shape
    nt = t // TM
    left, right = _halo_specs(w, DN_HALO, nt)
    row = lambda bb, i: (bb, i, 0)
    pad_vec = lambda a: jnp.zeros((1, LANES), F32).at[0, 2 * HEADS:4 * HEADS].set(a.reshape(-1))
    vec = pl.BlockSpec((1, LANES), lambda bb, i: (0, 0))
    return pl.pallas_call(
        _dn_prep_kernel,
        grid=(b, nt),
        in_specs=[pl.BlockSpec((1, TM, w), row), left, right, pl.BlockSpec((1, TM, LANES), row),
                  pl.BlockSpec((DN_CONV, w), lambda bb, i: (0, 0)), vec, vec],
        out_specs=[pl.BlockSpec((1, TM, GW), row)] * 3 + [pl.BlockSpec((1, TM, LANES), row)],
        out_shape=[jax.ShapeDtypeStruct((b, t, GW), F32)] * 3 + [jax.ShapeDtypeStruct((b, t, LANES), F32)],
        scratch_shapes=[pltpu.VMEM((TM + 2 * DN_HALO, w), F32)],
        compiler_params=_cp(("parallel", "arbitrary")),
    )(pc, pc, pc, pcs, conv_w, pad_vec(a_log), pad_vec(dt_bias))


def _dn_local_kernel(q_ref, k_ref, v_ref, bg_ref, u_out, w_out, qi_out, a_out, ket_out, ge_out):
    rr = _iota((TM, GW), 0)
    cc = _iota((TM, GW), 1)
    i_in = rr & (CH - 1)
    j_in = cc & (CH - 1)
    bd = (rr >> 6) == (cc >> 6)
    eye_t = (i_in == j_in).astype(F32)
    incl = (j_in <= i_in, j_in >= i_in)
    strict = (j_in < i_in, j_in > i_in)
    ones_bd = bd.astype(F32)
    eye_bf = (rr == cc).astype(BF16)
    head_of_lane = _iota((LANES, GW), 1) >> 6
    src = _iota((LANES, GW), 0)
    q = q_ref[0]
    k = k_ref[0]
    v = v_ref[0]
    bg = bg_ref[0]
    beta, gcum, decay, kb, egc = [], [], [], [], []
    for d in range(2):
        sel_b = (src == d * HEADS + head_of_lane).astype(F32)
        sel_g = (src == 2 * HEADS + d * HEADS + head_of_lane).astype(F32)
        cum_bd = jnp.where(jnp.logical_and(bd, incl[d]), 1.0, 0.0)
        beta.append(_dot(bg, sel_b, HI))
        gcum.append(_dot(_dot(cum_bd, bg, HI), sel_g, HI))
        grow = _dot(ones_bd, gcum[d] * eye_t, HI)
        decay.append(jnp.where(incl[d], jnp.exp(jnp.where(incl[d], gcum[d] - grow, 0.0)), 0.0))
        kb.append(k * beta[d])
        egc.append(jnp.exp(gcum[d]))
        qi_out[0, d] = (q * egc[d]).astype(BF16)
    pairs = [(c, d) for c in range(CPT) for d in range(2)]
    rows = [slice(c * CH, (c + 1) * CH) for c in range(CPT)]
    eye = eye_t[:CH]
    a = {}
    for c in range(CPT):
        r = rows[c]
        lhs = jnp.concatenate([kb[0][r], kb[1][r], q[r]], axis=0).astype(BF16)
        aq = _dot_nt(lhs, _bd(k[r], bd))
        for d in range(2):
            dec = decay[d][r]
            a[c, d] = jnp.where(strict[d][:CH], aq[d * CH:(d + 1) * CH] * dec, 0.0)
            a_out[0, d, r, :] = jnp.where(incl[d][:CH], aq[2 * CH:] * dec, 0.0).astype(BF16)
    t_inv = {cd: eye - a[cd] for cd in pairs}
    p = {cd: _dot(a[cd].astype(BF16), _bd(a[cd], bd)) for cd in pairs}
    for it in range(5):
        for cd in pairs:
            pbd = _bd(p[cd], bd)
            if it < 4:
                res = _dot(jnp.concatenate([t_inv[cd], p[cd]], axis=0).astype(BF16), pbd)
                t_inv[cd] = t_inv[cd] + res[:CH]
                p[cd] = res[CH:]
            else:
                t_inv[cd] = t_inv[cd] + _dot(t_inv[cd].astype(BF16), pbd)

    def split(m):
        hi = m.astype(BF16)
        return hi, (m - hi.astype(F32)).astype(BF16)

    for cd in pairs:
        x0 = t_inv[cd]
        ah, al = split(a[cd])
        xh, xl = split(x0)
        hx = _dot(jnp.concatenate([ah, al], axis=0), _bd(xh, bd))
        resid = eye - x0 - (hx[:CH] + hx[CH:] + _dot(ah, _bd(xl, bd)))
        t_inv[cd] = x0 + _dot(xh, _bd(resid, bd))
    for c, d in pairs:
        r = rows[c]
        tb = t_inv[c, d].astype(BF16)
        u_out[0, d, r, :] = _dot(tb, _bd(v[r] * beta[d][r], bd))
        w_out[0, d, r, :] = _dot(tb, _bd(kb[d][r] * egc[d][r], bd)).astype(BF16)
        last = (c + 1) * CH - 1 if d == 0 else c * CH
        gtot = gcum[d][last:last + 1]
        k_end = (k[r] * jnp.exp(gtot - gcum[d][r])).astype(BF16)
        ket_out[0, d, c] = _dot_nt(eye_bf, k_end).astype(BF16)
        ge_out[0, d, c] = jnp.exp(gtot)


def _dn_local(q, k, v, bg):
    b, t, _ = q.shape
    nt = t // TM
    row = lambda bb, i: (bb, i, 0)
    drow = pl.BlockSpec((1, 2, TM, GW), lambda bb, i: (bb, 0, i, 0))
    return pl.pallas_call(
        _dn_local_kernel,
        grid=(b, nt),
        in_specs=[pl.BlockSpec((1, TM, GW), row)] * 3 + [pl.BlockSpec((1, TM, LANES), row)],
        out_specs=[drow, drow, drow, drow,
                   pl.BlockSpec((1, 2, CPT, GW, CH), lambda bb, i: (bb, 0, i, 0, 0)),
                   pl.BlockSpec((1, 2, CPT, 1, GW), lambda bb, i: (bb, 0, i, 0, 0))],
        out_shape=[jax.ShapeDtypeStruct((b, 2, t, GW), F32)] + [jax.ShapeDtypeStruct((b, 2, t, GW), BF16)] * 3
        + [jax.ShapeDtypeStruct((b, 2, t // CH, GW, CH), BF16), jax.ShapeDtypeStruct((b, 2, t // CH, 1, GW), F32)],
        compiler_params=_cp(("parallel", "arbitrary")),
    )(q, k, v, bg)


def _dn_scan_kernel(uf, wf, qf, af, kf, gf, ur, wr, qr, ar, kr, gr, of_ref, or_ref, s_scr):
    @pl.when(pl.program_id(1) == 0)
    def _():
        s_scr[...] = jnp.zeros(s_scr.shape, F32)

    bd = (_iota((GW, GW), 0) >> 6) == (_iota((GW, GW), 1) >> 6)
    dirs = ((uf, wf, qf, af, kf, gf, of_ref), (ur, wr, qr, ar, kr, gr, or_ref))
    for c in range(CPT):
        for d, (u, w, qi, a, ket, ge, o_ref) in enumerate(dirs):
            cc = c if d == 0 else CPT - 1 - c
            rows = slice(cc * CH, (cc + 1) * CH)
            s = s_scr[d]
            wq = _dot(jnp.concatenate([w[0, 0, rows, :], qi[0, 0, rows, :]], axis=0), s.astype(BF16))
            v_new = u[0, 0, rows, :] - wq[:CH]
            o_ref[0, rows, :] = wq[CH:] + _dot(a[0, 0, rows, :], _bd(v_new, bd))
            s_scr[d] = s * ge[0, 0, cc] + jnp.where(bd, _dot(ket[0, 0, cc], v_new.astype(BF16)), 0.0)


def _dir_specs(shape_tail, n_tiles, chunked):
    blk = (1, 1, CPT if chunked else TM) + shape_tail
    zeros = (0,) * len(shape_tail)
    fwd = pl.BlockSpec(blk, lambda b, s: (b, 0, _fwd_tile(s, n_tiles)) + zeros)
    rev = pl.BlockSpec(blk, lambda b, s: (b, 1, _rev_tile(s, n_tiles)) + zeros)
    return fwd, rev


def _dn_scan(u, w, qi, a, ket, ge):
    b, _, t, _ = u.shape
    nt = t // TM
    rowf, rowr = _dir_specs((GW,), nt, False)
    ketf, ketr = _dir_specs((GW, CH), nt, True)
    gef, ger = _dir_specs((1, GW), nt, True)
    return pl.pallas_call(
        _dn_scan_kernel,
        grid=(b, nt),
        in_specs=[rowf, rowf, rowf, rowf, ketf, gef, rowr, rowr, rowr, rowr, ketr, ger],
        out_specs=[pl.BlockSpec((1, TM, GW), lambda bb, s: (bb, _fwd_tile(s, nt), 0)),
                   pl.BlockSpec((1, TM, GW), lambda bb, s: (bb, _rev_tile(s, nt), 0))],
        out_shape=[jax.ShapeDtypeStruct((b, t, GW), F32)] * 2,
        scratch_shapes=[pltpu.VMEM((2, GW, GW), F32)],
        compiler_params=_cp(("parallel", "arbitrary")),
    )(u, w, qi, a, ket, ge, u, w, qi, a, ket, ge)


GLA_QK = HEADS * GLA_K


def _gla_local_kernel(pd_ref, pdl_ref, w2_ref, b2_ref, qi_out, a_out, ke_out, vt_out, vb_out, de_out):
    _, incl, _, cum, _ = _chunk_masks()
    eye_bf = (_iota((GW, GW), 0) == _iota((GW, GW), 1)).astype(BF16)
    bdk = (_iota((GW, GLA_QK), 0) >> 6) == (_iota((GW, GLA_QK), 1) >> 5)
    z = _dot(pdl_ref[0], w2_ref[...], HI) + b2_ref[...]
    gk_all = -_softplus(-z) * (1.0 / GLA_TAU)
    for c in range(CPT):
        rows = slice(c * CH, (c + 1) * CH)
        p = pd_ref[0, rows, :]
        q = p[:, :GLA_QK] * (GLA_K ** -0.5)
        k = p[:, GLA_QK:2 * GLA_QK]
        vb = p[:, 2 * GLA_QK:].astype(BF16)
        vb_out[0, rows, :] = vb
        vt_out[0, c] = _dot_nt(eye_bf, vb).astype(BF16)
        for d in range(2):
            bcs = _dot(cum[d], gk_all[rows, d * GLA_QK:(d + 1) * GLA_QK], HI)
            bend = bcs[CH - 1:CH] if d == 0 else bcs[0:1]
            q_in = (q * jnp.exp(bcs)).astype(BF16)
            kdec = jnp.where(bdk, _tile4(k * jnp.exp(-bcs)), 0.0).astype(BF16)
            a_out[0, d, rows, :] = jnp.where(incl[d], _dot_nt(q_in, kdec), 0.0).astype(BF16)
            qi_out[0, d, rows, :] = q_in
            ke_out[0, d, rows, :] = (k * jnp.exp(bend - bcs)).astype(BF16)
            de_out[0, d, c] = jnp.exp(bend)


def _gla_local(pd, pdl, w2bd, b2):
    b, t, w = pd.shape
    nt = t // TM
    row = lambda bb, i: (bb, i, 0)
    return pl.pallas_call(
        _gla_local_kernel,
        grid=(b, nt),
        in_specs=[pl.BlockSpec((1, TM, w), row), pl.BlockSpec((1, TM, LANES), row),
                  pl.BlockSpec((LANES, GW), lambda bb, i: (0, 0)), pl.BlockSpec((1, GW), lambda bb, i: (0, 0))],
        out_specs=[pl.BlockSpec((1, 2, TM, GLA_QK), lambda bb, i: (bb, 0, i, 0)),
                   pl.BlockSpec((1, 2, TM, GW), lambda bb, i: (bb, 0, i, 0)),
                   pl.BlockSpec((1, 2, TM, GLA_QK), lambda bb, i: (bb, 0, i, 0)),
                   pl.BlockSpec((1, CPT, GW, CH), lambda bb, i: (bb, i, 0, 0)),
                   pl.BlockSpec((1, TM, GW), row),
                   pl.BlockSpec((1, 2, CPT, 1, GLA_QK), lambda bb, i: (bb, 0, i, 0, 0))],
        out_shape=[jax.ShapeDtypeStruct((b, 2, t, GLA_QK), BF16), jax.ShapeDtypeStruct((b, 2, t, GW), BF16),
                   jax.ShapeDtypeStruct((b, 2, t, GLA_QK), BF16), jax.ShapeDtypeStruct((b, t // CH, GW, CH), BF16),
                   jax.ShapeDtypeStruct((b, t, GW), BF16), jax.ShapeDtypeStruct((b, 2, t // CH, 1, GLA_QK), F32)],
        compiler_params=_cp(("parallel", "arbitrary")),
    )(pd, pdl, w2bd, b2)


def _gla_scan_kernel(qf, af, kf, df, vtf, vf, qr, ar, kr, dr, vtr, vr, of_ref, or_ref, s_scr):
    @pl.when(pl.program_id(1) == 0)
    def _():
        s_scr[...] = jnp.zeros(s_scr.shape, F32)

    bd = (_iota((GW, GW), 0) >> 6) == (_iota((GW, GW), 1) >> 6)
    bdt = (_iota((GW, GLA_QK), 0) >> 6) == (_iota((GW, GLA_QK), 1) >> 5)
    dirs = ((qf, af, kf, df, vtf, vf, of_ref), (qr, ar, kr, dr, vtr, vr, or_ref))
    for c in range(CPT):
        for d, (qi, a, ke, de, vt, v, o_ref) in enumerate(dirs):
            cc = c if d == 0 else CPT - 1 - c
            rows = slice(cc * CH, (cc + 1) * CH)
            st = s_scr[d]
            vbd = jnp.where(bd, _tile4(v[0, rows, :]), jnp.zeros((), BF16))
            o_ref[0, rows, :] = _dot_nt(qi[0, 0, rows, :], st.astype(BF16)) + _dot(a[0, 0, rows, :], vbd)
            s_scr[d] = st * de[0, 0, cc] + jnp.where(bdt, _dot(vt[0, cc], ke[0, 0, rows, :]), 0.0)


def _gla_scan(qi, a, ke, de, vt, vb):
    b, _, t, _ = a.shape
    nt = t // TM
    qf, qr = _dir_specs((GLA_QK,), nt, False)
    af, ar = _dir_specs((GW,), nt, False)
    df, dr = _dir_specs((1, GLA_QK), nt, True)
    vtf = pl.BlockSpec((1, CPT, GW, CH), lambda bb, s: (bb, _fwd_tile(s, nt), 0, 0))
    vtr = pl.BlockSpec((1, CPT, GW, CH), lambda bb, s: (bb, _rev_tile(s, nt), 0, 0))
    vf = pl.BlockSpec((1, TM, GW), lambda bb, s: (bb, _fwd_tile(s, nt), 0))
    vr = pl.BlockSpec((1, TM, GW), lambda bb, s: (bb, _rev_tile(s, nt), 0))
    return pl.pallas_call(
        _gla_scan_kernel,
        grid=(b, nt),
        in_specs=[qf, af, qf, df, vtf, vf, qr, ar, qr, dr, vtr, vr],
        out_specs=[vf, vr],
        out_shape=[jax.ShapeDtypeStruct((b, t, GW), F32)] * 2,
        scratch_shapes=[pltpu.VMEM((2, GW, GLA_QK), F32)],
        compiler_params=_cp(("parallel", "arbitrary")),
    )(qi, a, ke, de, vt, vb, qi, a, ke, de, vt, vb)


def _outproj_kernel(x_ref, ya_ref, yb_ref, cf_ref, cr_ref, df_ref, dr_ref, pg_ref, g1_ref, gc_ref, gd_ref, w_ref, o_ref):
    ones64 = _group_ones(GW, 6)

    def fin(o, g, gate):
        ms = _dot(o * o, ones64, HI) * (1.0 / HEAD_DIM)
        return (o * lax.rsqrt(ms + EPS) * g * _silu(gate)).astype(BF16)

    pg = pg_ref[0]
    yc = fin(cf_ref[0] + cr_ref[0], gc_ref[...], pg[:, :GW])
    yd = fin(df_ref[0] + dr_ref[0], gd_ref[...], pg[:, GW:])
    res = (_dot(ya_ref[0].astype(BF16), w_ref[0:GW, :]) + _dot(yb_ref[0].astype(BF16), w_ref[GW:2 * GW, :])
           + _dot(yc, w_ref[2 * GW:3 * GW, :]) + _dot(yd, w_ref[3 * GW:, :]))
    o_ref[0] = x_ref[0] + g1_ref[0] * res


def _outproj(xs, ya, yb, ocf, ocr, odf, odr, pg, mod3, dn_g, gla_g, w_out_bf, n_batch, n_tiles):
    b, t, d = xs.shape
    row = lambda bb, i: (bb, i, 0)
    g256 = pl.BlockSpec((1, TM, GW), row)
    vec = pl.BlockSpec((1, GW), lambda bb, i: (0, 0))
    reps = GW // HEAD_DIM
    return pl.pallas_call(
        _outproj_kernel,
        grid=(b, n_tiles),
        in_specs=[pl.BlockSpec((1, TM, d), row), g256, g256, g256, g256, g256, g256,
                  pl.BlockSpec((1, TM, 2 * GW), row),
                  _mod_spec(2, n_batch, t // TM - 1), vec, vec, pl.BlockSpec((d, d), lambda bb, i: (0, 0))],
        out_specs=pl.BlockSpec((1, TM, d), row),
        out_shape=jax.ShapeDtypeStruct((b, n_tiles * TM, d), F32),
        compiler_params=_cp(("parallel", "arbitrary")),
    )(xs, ya, yb, ocf, ocr, odf, odr, pg, mod3, jnp.tile(dn_g, reps).reshape(1, GW),
      jnp.tile(gla_g, reps).reshape(1, GW), w_out_bf)


FF_HALO = 8


def _ffn_kernel(xm_ref, xl_ref, xr_ref, sh_ref, sc_ref, g2_ref, ng_ref, wa_ref, wg_ref, cwa_ref, cwg_ref, wd_ref,
                o_ref, ext_ref, *, n_tiles):
    left_ok, right_ok = _halo_ok(pl.program_id(1), n_tiles)
    rows_ext = _iota((TM + 2 * FF_HALO, 1), 0)
    keep = jnp.logical_and(jnp.logical_or(rows_ext >= FF_HALO, left_ok),
                           jnp.logical_or(rows_ext < FF_HALO + TM, right_ok)).astype(F32)
    x = jnp.concatenate([xl_ref[0], xm_ref[0], xr_ref[0]], axis=0)
    y = x * lax.rsqrt(jnp.mean(x * x, axis=-1, keepdims=True) + EPS) * ng_ref[...]
    h = ((y * (1.0 + sc_ref[0]) + sh_ref[0]) * keep).astype(BF16)
    acc = jnp.zeros((TM, D_MODEL), F32)
    for j in range(N_FF_BLK):
        def conv(w_ref, cw_ref, half):
            ext_ref[half] = _dot(h, w_ref[j])
            cw = cw_ref[j]
            return (cw[0:1] * ext_ref[half, pl.ds(FF_HALO - 1, TM), :] + cw[1:2] * ext_ref[half, pl.ds(FF_HALO, TM), :]
                    + cw[2:3] * ext_ref[half, pl.ds(FF_HALO + 1, TM), :])
        a = conv(wa_ref, cwa_ref, 0)
        g = conv(wg_ref, cwg_ref, 1)
        acc = acc + _dot((_silu(g) * a).astype(BF16), wd_ref[j])
    o_ref[0] = xm_ref[0] + g2_ref[0] * acc


def _ffn(x1, mod3, norm_g, wa, wg, cwa, cwg, wd, n_batch, n_tiles_total):
    b, t, d = x1.shape
    n = t // TM
    per = TM // FF_HALO
    left = pl.BlockSpec((1, FF_HALO, d), lambda bb, i: (bb, jnp.maximum(i * per - 1, 0), 0))
    right = pl.BlockSpec((1, FF_HALO, d), lambda bb, i: (bb, jnp.minimum((i + 1) * per, n * per - 1), 0))
    const3 = lambda bb, i: (0, 0, 0)
    ctx_tile = n_tiles_total - 1
    kern = functools.partial(_ffn_kernel, n_tiles=n_tiles_total)
    return pl.pallas_call(
        kern,
        grid=(b, n),
        in_specs=[pl.BlockSpec((1, TM, d), lambda bb, i: (bb, i, 0)), left, right,
                  _mod_spec(3, n_batch, ctx_tile), _mod_spec(4, n_batch, ctx_tile), _mod_spec(5, n_batch, ctx_tile),
                  pl.BlockSpec((1, d), lambda bb, i: (0, 0)),
                  pl.BlockSpec((N_FF_BLK, d, FF_BLK), const3), pl.BlockSpec((N_FF_BLK, d, FF_BLK), const3),
                  pl.BlockSpec((N_FF_BLK, 3, FF_BLK), const3), pl.BlockSpec((N_FF_BLK, 3, FF_BLK), const3),
                  pl.BlockSpec((N_FF_BLK, FF_BLK, d), const3)],
        out_specs=pl.BlockSpec((1, TM, d), lambda bb, i: (bb, i, 0)),
        out_shape=jax.ShapeDtypeStruct((b, t, d), F32),
        scratch_shapes=[pltpu.VMEM((2, TM + 2 * FF_HALO, FF_BLK), F32)],
        compiler_params=_cp(("parallel", "arbitrary")),
    )(x1, x1, x1, mod3, mod3, mod3, norm_g.reshape(1, d), wa, wg, cwa, cwg, wd)


def _rope_tables(seq, ctx_len):
    rows = seq // GRID_W
    row = jnp.repeat(jnp.arange(rows, dtype=F32), GRID_W)
    col = jnp.tile(jnp.arange(GRID_W, dtype=F32), rows)
    nf = QK_DIM // 4
    inv = ROPE_THETA ** (-jnp.arange(nf, dtype=F32) / nf)
    ang = jnp.concatenate([row[:, None] * inv, col[:, None] * inv], axis=-1)
    cos = jnp.concatenate([jnp.cos(ang), jnp.ones((ctx_len, QK_DIM // 2), F32)], axis=0)
    sin = jnp.concatenate([jnp.sin(ang), jnp.zeros((ctx_len, QK_DIM // 2), F32)], axis=0)
    reps = GW // QK_DIM
    return (jnp.tile(jnp.concatenate([cos, cos], axis=-1), (1, reps)),
            jnp.tile(jnp.concatenate([-sin, sin], axis=-1), (1, reps)))


def _regroup_w_in(w):
    d = w.shape[0]
    z = lambda n: jnp.zeros((d, n), w.dtype)
    return jnp.concatenate([w[:, :2048], w[:, 2048:2064], z(LANES - 16), w[:, 2320:2832], w[:, 2832:2864],
                            z(LANES - 32), w[:, 2064:2320], w[:, 2864:3120]], axis=1).astype(BF16)


def _gla_w2_blockdiag(w2):
    out = jnp.zeros((LANES, GW), F32)
    out = out.at[0:GLA_RANK, 0:GLA_QK].set(w2[0])
    return out.at[GLA_RANK:2 * GLA_RANK, GLA_QK:].set(w2[1])


def _layer(xs, mod3, lp, cos_t, sin_t, layer_idx, last, n_batch):
    b, t, d = xs.shape
    nt = t // TM
    pa, pb, pc, pcs, pd, pdl, pg = _inproj(xs, mod3, lp["norm1_g"], _regroup_w_in(lp["w_in"]), n_batch)

    ya = _conv_module(pa, lp["cm_conv_w"], lp["cm_conv_b"], lp["cm_ln_g"], lp["cm_ln_b"])

    lam_init = 0.8 - 0.6 * math.exp(-0.3 * layer_idx)
    q, kt, v, qn, kn = _attn_prep(pb, lp["da_qnorm_g"], lp["da_knorm_g"], cos_t, sin_t)
    attn_args = (lp["da_lambda"], lp["da_subln_g"], lam_init)
    lat_rows = t - TM
    yb = _attention(q, kt, v, qn, kn, *attn_args, TQ, 0, lat_rows // TQ, nt, 0, lat_rows if last else t)
    if not last:
        yb = _attention(q, kt, v, qn, kn, *attn_args, TM, nt - 1, 1, 1, nt - 1, t, prev=yb)

    dq, dk, dv, bg = _dn_prep(pc, pcs, lp["dn_conv_w"], lp["dn_a_log"], lp["dn_dt_bias"])
    ocf, ocr = _dn_scan(*_dn_local(dq, dk, dv, bg))

    qi, a, ke, vt, vb, de = _gla_local(pd, pdl, _gla_w2_blockdiag(lp["gla_w2"]), lp["gla_b2"].reshape(1, GW))
    odf, odr = _gla_scan(qi, a, ke, de, vt, vb)

    n_out = nt - 1 if last else nt
    x1 = _outproj(xs, ya, yb, ocf, ocr, odf, odr, pg, mod3, lp["dn_onorm_g"], lp["gla_onorm_g"],
                  lp["w_out"].astype(BF16), n_batch, n_out)

    w_up = lp["ffn_w_up"].astype(BF16)
    blocks = lambda m: jnp.transpose(m.reshape(m.shape[0], N_FF_BLK, FF_BLK), (1, 0, 2))
    cw = lp["ffn_conv_w"]
    return _ffn(x1, mod3, lp["norm2_g"], blocks(w_up[:, :D_FF]), blocks(w_up[:, D_FF:]),
                blocks(cw[:, :D_FF]), blocks(cw[:, D_FF:]),
                lp["ffn_w_down"].astype(BF16).reshape(N_FF_BLK, FF_BLK, d), n_batch, nt)


def kernel(x, c, ctx, c_ctx, w_mod, b_mod, norm1_g, norm2_g, w_in, w_out, cm_conv_w, cm_conv_b, cm_ln_g, cm_ln_b, da_qnorm_g, da_knorm_g, da_lambda, da_subln_g, dn_conv_w, dn_a_log, dn_dt_bias, dn_onorm_g, gla_w2, gla_b2, gla_onorm_g, ffn_w_up, ffn_conv_w, ffn_w_down):
    n_batch, seq, d = x.shape
    ctx_len = ctx.shape[1]
    assert ctx_len == TM and seq % TQ == 0 and d == D_MODEL
    depth = w_mod.shape[0]
    cos_t, sin_t = _rope_tables(seq, ctx_len)
    xs = jnp.concatenate([x, ctx], axis=1)
    mod_rows = 16
    c_rows = jnp.zeros((mod_rows, d), F32).at[:n_batch].set(c).at[n_batch].set(c_ctx)
    params = dict(w_mod=w_mod, b_mod=b_mod, norm1_g=norm1_g, norm2_g=norm2_g, w_in=w_in, w_out=w_out,
                  cm_conv_w=cm_conv_w, cm_conv_b=cm_conv_b, cm_ln_g=cm_ln_g, cm_ln_b=cm_ln_b,
                  da_qnorm_g=da_qnorm_g, da_knorm_g=da_knorm_g, da_lambda=da_lambda, da_subln_g=da_subln_g,
                  dn_conv_w=dn_conv_w, dn_a_log=dn_a_log, dn_dt_bias=dn_dt_bias, dn_onorm_g=dn_onorm_g,
                  gla_w2=gla_w2, gla_b2=gla_b2, gla_onorm_g=gla_onorm_g,
                  ffn_w_up=ffn_w_up, ffn_conv_w=ffn_conv_w, ffn_w_down=ffn_w_down)
    for l in range(depth):
        lp = {k: v[l] for k, v in params.items()}
        mod3 = _modulation(c_rows, lp["w_mod"], lp["b_mod"]).reshape(mod_rows * 6, 1, d)
        xs = _layer(xs, mod3, lp, cos_t, sin_t, l, l == depth - 1, n_batch)
    return xs
```

```python
import functools
import math

import jax
import jax.numpy as jnp
from jax import lax
from jax.experimental import pallas as pl
from jax.experimental.pallas import tpu as pltpu

F32 = jnp.float32
BF16 = jnp.bfloat16
HI = lax.Precision.HIGHEST
EPS = 1e-6

D_MODEL = 1024
GRID_W = 64
HEADS = 4
HEAD_DIM = 64
GW = 256
QK_DIM = 32
GLA_K = 32
GLA_RANK = 16
GLA_TAU = 16.0
CM_KERNEL = 31
DN_CONV = 5
ROPE_THETA = 10000.0
CH = 64
TM = 256
CPT = TM // CH
D_FF = 2816
FF_BLK = 256
N_FF_BLK = D_FF // FF_BLK
LANES = 128

IN_GROUPS = (("pa", 512), ("pb", 768), ("pc", 768), ("pcs", LANES), ("pd", 512), ("pdl", LANES), ("pg", 512))
IN_COLS_PAD = sum(w for _, w in IN_GROUPS)

VMEM_LIMIT = 56 * 1024 * 1024


def _cp(sem):
    return pltpu.CompilerParams(dimension_semantics=sem, vmem_limit_bytes=VMEM_LIMIT)


def _dot(a, b, prec=None):
    return jnp.dot(a, b, preferred_element_type=F32, precision=prec)


def _dot_nt(a, b, prec=None):
    return lax.dot_general(a, b, (((1,), (1,)), ((), ())), preferred_element_type=F32, precision=prec)


def _sigmoid(x):
    return 1.0 / (1.0 + jnp.exp(-x))


def _silu(x):
    return x * _sigmoid(x)


def _softplus(x):
    return jnp.maximum(x, 0.0) + jnp.log(1.0 + jnp.exp(-jnp.abs(x)))


def _iota(shape, dim):
    return lax.broadcasted_iota(jnp.int32, shape, dim)


def _group_ones(n, shift):
    return ((_iota((n, n), 0) >> shift) == (_iota((n, n), 1) >> shift)).astype(F32)


def _tile4(y):
    return jnp.concatenate([y, y, y, y], axis=0)


def _mod_kernel(c_ref, w_ref, b_ref, o_ref):
    o_ref[...] = _dot(_silu(c_ref[...]), w_ref[...], HI) + b_ref[...]


def _modulation(c_rows, w_mod, b_mod):
    r, d = c_rows.shape
    n = w_mod.shape[1] // d
    return pl.pallas_call(
        _mod_kernel,
        grid=(n,),
        in_specs=[pl.BlockSpec((r, d), lambda j: (0, 0)),
                  pl.BlockSpec((d, d), lambda j: (0, j)),
                  pl.BlockSpec((1, d), lambda j: (0, j))],
        out_specs=pl.BlockSpec((r, d), lambda j: (0, j)),
        out_shape=jax.ShapeDtypeStruct((r, n * d), F32),
        compiler_params=_cp(("arbitrary",)),
    )(c_rows, w_mod, b_mod.reshape(1, -1))


TR = 512


def _mod_spec(k, n_batch, ctx):
    if ctx:
        return pl.BlockSpec((1, 1, D_MODEL), lambda b, i: (n_batch * 6 + k, 0, 0))
    return pl.BlockSpec((1, 1, D_MODEL), lambda b, i: (b * 6 + k, 0, 0))


def _resident(shape):
    zeros = (0,) * len(shape)
    return pl.BlockSpec(shape, lambda b, i: zeros, pipeline_mode=pl.Buffered(1))


def _row_call(kern, stream_rows, ctx, n_batch, args, specs, out_widths, out_rows, prev, scratch=()):
    tm = TM if ctx else TR
    blk0 = (stream_rows - TM) // TM if ctx else 0
    n_blk = 1 if ctx else (stream_rows - TM) // TR
    in_specs = list(specs(tm, blk0))
    args = list(args)
    aliases = {}
    if prev is not None:
        aliases = {len(args) + k: k for k in range(len(prev))}
        in_specs += [pl.BlockSpec(memory_space=pl.ANY)] * len(prev)
        args += list(prev)
    return pl.pallas_call(
        kern,
        grid=(n_batch, n_blk),
        in_specs=in_specs,
        out_specs=[pl.BlockSpec((1, tm, w), lambda bb, i: (bb, i + blk0, 0)) for w in out_widths],
        out_shape=[jax.ShapeDtypeStruct((n_batch, out_rows, w), F32) for w in out_widths],
        scratch_shapes=list(scratch),
        input_output_aliases=aliases,
        compiler_params=_cp(("parallel", "arbitrary")),
    )(*args)


def _inproj_kernel(x_ref, sh_ref, sc_ref, g_ref, w_ref, *rest):
    outs = rest[-len(IN_GROUPS):]
    x = x_ref[0]
    y = x * lax.rsqrt(jnp.mean(x * x, axis=-1, keepdims=True) + EPS) * g_ref[...]
    h = y * (1.0 + sc_ref[0]) + sh_ref[0]
    p = _dot(h.astype(BF16), w_ref[...])
    off = 0
    for o_ref, (_, width) in zip(outs, IN_GROUPS):
        o_ref[0] = p[:, off:off + width]
        off += width


def _inproj(xs, mod3, norm_g, w_in_r, n_batch, ctx, prev=None):
    b, t, d = xs.shape

    def specs(tm, blk0):
        return [pl.BlockSpec((1, tm, d), lambda bb, i: (bb, i + blk0, 0)), _mod_spec(0, n_batch, ctx),
                _mod_spec(1, n_batch, ctx), _resident((1, d)), _resident((d, IN_COLS_PAD))]

    return _row_call(_inproj_kernel, t, ctx, n_batch, (xs, mod3, mod3, norm_g.reshape(1, d), w_in_r), specs,
                     [w for _, w in IN_GROUPS], t, prev)


def _halo_specs(width, halo, n_tiles):
    per = TM // halo
    left = pl.BlockSpec((1, halo, width), lambda b, i: (b, jnp.maximum(i * per - 1, 0), 0))
    right = pl.BlockSpec((1, halo, width), lambda b, i: (b, jnp.minimum((i + 1) * per, n_tiles * per - 1), 0))
    return left, right


def _halo_ok(i, n_tiles):
    return jnp.logical_and(i >= 1, i < n_tiles - 1), i < n_tiles - 2


CM_HALO = 16


def _convmod_kernel(pm_ref, pl_ref, pr_ref, cw_ref, cb_ref, lg_ref, lb_ref, o_ref, ext_ref):
    left_ok, right_ok = _halo_ok(pl.program_id(1), pl.num_programs(1))

    def glu(p):
        return p[:, :GW] * _sigmoid(p[:, GW:])

    ext_ref[0:CM_HALO] = jnp.where(left_ok, glu(pl_ref[0]), 0.0)
    ext_ref[CM_HALO:CM_HALO + TM] = glu(pm_ref[0])
    ext_ref[CM_HALO + TM:] = jnp.where(right_ok, glu(pr_ref[0]), 0.0)
    pad = CM_KERNEL // 2
    acc = jnp.zeros((TM, GW), F32)
    for j in range(CM_KERNEL):
        acc = acc + cw_ref[j:j + 1, :] * ext_ref[pl.ds(CM_HALO - pad + j, TM), :]
    y = acc + cb_ref[...]
    mu = jnp.mean(y, axis=-1, keepdims=True)
    yc = y - mu
    var = jnp.mean(yc * yc, axis=-1, keepdims=True)
    o_ref[0] = _silu(yc * lax.rsqrt(var + EPS) * lg_ref[...] + lb_ref[...])


def _conv_module(pa, conv_w, conv_b, ln_g, ln_b):
    b, t, w = pa.shape
    nt = t // TM
    left, right = _halo_specs(w, CM_HALO, nt)
    vec = pl.BlockSpec((1, GW), lambda bb, i: (0, 0))
    return pl.pallas_call(
        _convmod_kernel,
        grid=(b, nt),
        in_specs=[pl.BlockSpec((1, TM, w), lambda bb, i: (bb, i, 0)), left, right,
                  pl.BlockSpec((CM_KERNEL, GW), lambda bb, i: (0, 0)), vec, vec, vec],
        out_specs=pl.BlockSpec((1, TM, GW), lambda bb, i: (bb, i, 0)),
        out_shape=jax.ShapeDtypeStruct((b, t, GW), F32),
        scratch_shapes=[pltpu.VMEM((TM + 2 * CM_HALO, GW), F32)],
        compiler_params=_cp(("parallel", "arbitrary")),
    )(pa, pa, pa, conv_w, conv_b.reshape(1, GW), ln_g.reshape(1, GW), ln_b.reshape(1, GW))


N_MAPS = 2 * HEADS
QK_PAD = 2 * QK_DIM
V_EXT = 2 * HEAD_DIM
TQ = 1024
SHIFT_MAX = 40.0


def _attn_prep_kernel(pb_ref, qg_ref, kg_ref, cos_ref, sin_ref, q_out, kt_out, v_out, qn_out, kn_out):
    p = pb_ref[0]
    ones32 = _group_ones(GW, 5)
    first = (_iota((TM, GW), 1) & (QK_DIM - 1)) < QK_DIM // 2
    cos = cos_ref[...]
    sin = sin_ref[...]

    def norm_rope(t, g):
        ms = _dot(t * t, ones32, HI) * (1.0 / QK_DIM)
        tn = t * lax.rsqrt(ms + EPS) * g
        partner = jnp.where(first, pltpu.roll(tn, GW - QK_DIM // 2, 1), pltpu.roll(tn, QK_DIM // 2, 1))
        return tn * cos + partner * sin

    qf = norm_rope(p[:, :GW], qg_ref[...]) * (QK_DIM ** -0.5)
    kf = norm_rope(p[:, GW:2 * GW], kg_ref[...])
    map_sel = ((_iota((GW, N_MAPS), 0) >> 5) == _iota((GW, N_MAPS), 1)).astype(F32)
    qn_out[0] = _dot(qf * qf, map_sel, HI)
    kn_out[0] = _dot(kf * kf, map_sel, HI)
    q = qf.astype(BF16)
    kt = kf.T.astype(BF16)
    k_tail = jnp.where(_iota((QK_DIM, TM), 0) == 0, 1.0, 0.0).astype(BF16)
    q_tail = jnp.zeros((TM, QK_DIM), BF16)
    for g in range(N_MAPS):
        q_out[0, 0, g, :, 0:QK_DIM] = q[:, g * QK_DIM:(g + 1) * QK_DIM]
        q_out[0, 0, g, :, QK_DIM:] = q_tail
        kt_out[0, 0, g, 0:QK_DIM, :] = kt[g * QK_DIM:(g + 1) * QK_DIM, :]
        kt_out[0, 0, g, QK_DIM:, :] = k_tail
    v = p[:, 2 * GW:].astype(BF16)
    ones = jnp.ones((TM, HEAD_DIM), BF16)
    for h in range(HEADS):
        v_out[0, 0, h, :, 0:HEAD_DIM] = v[:, h * HEAD_DIM:(h + 1) * HEAD_DIM]
        v_out[0, 0, h, :, HEAD_DIM:] = ones


def _attn_prep(pb, qn_g, kn_g, cos_t, sin_t):
    b, t, w = pb.shape
    nt = t // TM
    vec = pl.BlockSpec((1, GW), lambda bb, i: (0, 0))
    tab = pl.BlockSpec((TM, GW), lambda bb, i: (i, 0))
    reps = GW // QK_DIM
    n_spec = pl.BlockSpec((1, TM, N_MAPS), lambda bb, i: (bb, i, 0))
    n_shape = jax.ShapeDtypeStruct((b, t, N_MAPS), F32)
    return pl.pallas_call(
        _attn_prep_kernel,
        grid=(b, nt),
        in_specs=[pl.BlockSpec((1, TM, w), lambda bb, i: (bb, i, 0)), vec, vec, tab, tab],
        out_specs=[pl.BlockSpec((1, 1, N_MAPS, TM, QK_PAD), lambda bb, i: (bb, i, 0, 0, 0)),
                   pl.BlockSpec((1, 1, N_MAPS, QK_PAD, TM), lambda bb, i: (bb, i, 0, 0, 0)),
                   pl.BlockSpec((1, 1, HEADS, TM, V_EXT), lambda bb, i: (bb, i, 0, 0, 0)), n_spec, n_spec],
        out_shape=[jax.ShapeDtypeStruct((b, nt, N_MAPS, TM, QK_PAD), BF16),
                   jax.ShapeDtypeStruct((b, nt, N_MAPS, QK_PAD, TM), BF16),
                   jax.ShapeDtypeStruct((b, nt, HEADS, TM, V_EXT), BF16), n_shape, n_shape],
        compiler_params=_cp(("parallel", "arbitrary")),
    )(pb, jnp.tile(qn_g, reps).reshape(1, GW), jnp.tile(kn_g, reps).reshape(1, GW), cos_t, sin_t)


def _attn_kernel(q_ref, kt_ref, v_ref, qn_ref, kn_ref, lam_ref, sg_ref, *rest, n_chunks, lam_init):
    o_ref, qa_scr, acc_scr, s_scr, m_scr = rest[-5:]
    tq = o_ref.shape[1]
    acc_scr[...] = jnp.zeros(acc_scr.shape, F32)
    k2 = jnp.max(kn_ref[0], axis=0, keepdims=True)
    bound = jnp.sqrt(qn_ref[0] * k2)
    safe = jnp.max(bound) <= SHIFT_MAX
    shift = jnp.where(safe, bound, 0.0)
    shift_lane = _iota((tq, QK_PAD), 1) == QK_DIM
    for g in range(N_MAPS):
        qg = jnp.concatenate([q_ref[0, t, g] for t in range(tq // TM)], axis=0)
        qa_scr[g] = jnp.where(shift_lane, (-shift[:, g:g + 1]).astype(BF16), qg)

    @pl.when(safe)
    def _():
        s_scr[0] = _dot(qa_scr[0], kt_ref[0, 0, 0])

        def body(c, carry):
            nxt = jnp.minimum(c + 1, n_chunks - 1)
            for g in range(N_MAPS):
                if g + 1 < N_MAPS:
                    s_next = _dot(qa_scr[g + 1], kt_ref[0, c, g + 1])
                else:
                    s_next = _dot(qa_scr[0], kt_ref[0, nxt, 0])
                p = jnp.exp(s_scr[g & 1]).astype(BF16)
                acc_scr[g] += _dot(p, v_ref[0, c, g // 2])
                s_scr[(g + 1) & 1] = s_next
            return carry

        lax.fori_loop(0, n_chunks, body, 0)

    @pl.when(jnp.logical_not(safe))
    def _():
        m_scr[...] = jnp.full(m_scr.shape, -jnp.inf, F32)

        def body(c, carry):
            for g in range(N_MAPS):
                s = _dot(qa_scr[g], kt_ref[0, c, g])
                m_old = m_scr[:, g:g + 1]
                m_new = jnp.maximum(m_old, jnp.max(s, axis=-1, keepdims=True))
                p = jnp.exp(s - m_new).astype(BF16)
                acc_scr[g] = jnp.exp(m_old - m_new) * acc_scr[g] + _dot(p, v_ref[0, c, g // 2])
                m_scr[:, g:g + 1] = m_new
            return carry

        lax.fori_loop(0, n_chunks, body, 0)

    lp = lam_ref[...]
    lam = (jnp.exp(jnp.sum(lp[0:1] * lp[1:2], axis=-1, keepdims=True))
           - jnp.exp(jnp.sum(lp[2:3] * lp[3:4], axis=-1, keepdims=True)) + lam_init)
    for h in range(HEADS):
        a0 = acc_scr[2 * h]
        a1 = acc_scr[2 * h + 1]
        o = (a0[:, :HEAD_DIM] / a0[:, HEAD_DIM:HEAD_DIM + 1]
             - lam * (a1[:, :HEAD_DIM] / a1[:, HEAD_DIM:HEAD_DIM + 1]))
        y = o * lax.rsqrt(jnp.mean(o * o, axis=-1, keepdims=True) + EPS) * sg_ref[...] * (1.0 - lam_init)
        o_ref[0, :, h * HEAD_DIM:(h + 1) * HEAD_DIM] = y


def _attention(q, kt, v, qn, kn, lam_p, subln_g, lam_init, tq, q_blk0, n_q, n_kv, kv_blk, out_rows, prev=None):
    b = q.shape[0]
    qt = tq // TM
    kern = functools.partial(_attn_kernel, n_chunks=n_kv, lam_init=lam_init)
    in_specs = [pl.BlockSpec((1, qt, N_MAPS, TM, QK_PAD), lambda bb, i: (bb, i + q_blk0, 0, 0, 0)),
                pl.BlockSpec((1, n_kv, N_MAPS, QK_PAD, TM), lambda bb, i: (bb, kv_blk, 0, 0, 0)),
                pl.BlockSpec((1, n_kv, HEADS, TM, V_EXT), lambda bb, i: (bb, kv_blk, 0, 0, 0)),
                pl.BlockSpec((1, tq, N_MAPS), lambda bb, i: (bb, i + q_blk0, 0)),
                pl.BlockSpec((1, n_kv * TM, N_MAPS), lambda bb, i: (bb, kv_blk, 0)),
                pl.BlockSpec((4, QK_DIM), lambda bb, i: (0, 0)),
                pl.BlockSpec((1, HEAD_DIM), lambda bb, i: (0, 0))]
    args = [q, kt, v, qn, kn, lam_p, subln_g.reshape(1, HEAD_DIM)]
    aliases = {}
    if prev is not None:
        in_specs.append(pl.BlockSpec(memory_space=pl.ANY))
        aliases = {len(args): 0}
        args.append(prev)
    return pl.pallas_call(
        kern,
        grid=(b, n_q),
        in_specs=in_specs,
        out_specs=pl.BlockSpec((1, tq, GW), lambda bb, i: (bb, i + q_blk0, 0)),
        out_shape=jax.ShapeDtypeStruct((b, out_rows, GW), F32),
        scratch_shapes=[pltpu.VMEM((N_MAPS, tq, QK_PAD), BF16), pltpu.VMEM((N_MAPS, tq, V_EXT), F32),
                        pltpu.VMEM((2, tq, TM), F32), pltpu.VMEM((tq, N_MAPS), F32)],
        input_output_aliases=aliases,
        compiler_params=_cp(("parallel", "arbitrary")),
    )(*args)


def _chunk_masks():
    i = _iota((CH, GW), 0)
    j = _iota((CH, GW), 1) & (CH - 1)
    eye = (i == j).astype(F32)
    incl = (j <= i, j >= i)
    strict = (j < i, j > i)
    r = _iota((CH, CH), 0)
    c = _iota((CH, CH), 1)
    cum = ((c <= r).astype(F32), (c >= r).astype(F32))
    bd = (_iota((GW, GW), 0) >> 6) == (_iota((GW, GW), 1) >> 6)
    return eye, incl, strict, cum, bd


def _bd(y, bd):
    return jnp.where(bd, _tile4(y), 0.0).astype(BF16)


def _fwd_tile(s, n_tiles):
    return jnp.where(s == 0, n_tiles - 1, s - 1)


def _rev_tile(s, n_tiles):
    return jnp.where(s == 0, n_tiles - 1, n_tiles - 1 - s)


DN_HALO = 8


def _dn_prep_kernel(pm_ref, pl_ref, pr_ref, pcs_ref, cw_ref, alog_ref, dtb_ref, q_out, k_out, v_out, bg_out, ext_ref):
    left_ok, right_ok = _halo_ok(pl.program_id(1), pl.num_programs(1))
    ext_ref[0:DN_HALO] = jnp.where(left_ok, pl_ref[0], 0.0)
    ext_ref[DN_HALO:DN_HALO + TM] = pm_ref[0]
    ext_ref[DN_HALO + TM:] = jnp.where(right_ok, pr_ref[0], 0.0)
    pad = DN_CONV // 2
    acc = jnp.zeros((TM, 3 * GW), F32)
    for j in range(DN_CONV):
        acc = acc + cw_ref[j:j + 1, :] * ext_ref[pl.ds(DN_HALO - pad + j, TM), :]
    qkv = _silu(acc)
    ones64 = _group_ones(GW, 6)

    def l2n(t):
        return t * lax.rsqrt(_dot(t * t, ones64, HI) + EPS)

    q_out[0] = l2n(qkv[:, :GW]) * (HEAD_DIM ** -0.5)
    k_out[0] = l2n(qkv[:, GW:2 * GW])
    v_out[0] = qkv[:, 2 * GW:]
    s = pcs_ref[0]
    col = _iota(s.shape, 1)
    gate = -jnp.exp(alog_ref[...]) * _softplus(s + dtb_ref[...])
    bg_out[0] = jnp.where(col < 2 * HEADS, _sigmoid(s), jnp.where(col < 4 * HEADS, gate, 0.0))


def _dn_prep(pc, pcs, conv_w, a_log, dt_bias):
    b, t, w = pc.shape
    nt = t // TM
    left, right = _halo_specs(w, DN_HALO, nt)
    row = lambda bb, i: (bb, i, 0)
    pad_vec = lambda a: jnp.zeros((1, LANES), F32).at[0, 2 * HEADS:4 * HEADS].set(a.reshape(-1))
    vec = pl.BlockSpec((1, LANES), lambda bb, i: (0, 0))
    return pl.pallas_call(
        _dn_prep_kernel,
        grid=(b, nt),
        in_specs=[pl.BlockSpec((1, TM, w), row), left, right, pl.BlockSpec((1, TM, LANES), row),
                  pl.BlockSpec((DN_CONV, w), lambda bb, i: (0, 0)), vec, vec],
        out_specs=[pl.BlockSpec((1, TM, GW), row)] * 3 + [pl.BlockSpec((1, TM, LANES), row)],
        out_shape=[jax.ShapeDtypeStruct((b, t, GW), F32)] * 3 + [jax.ShapeDtypeStruct((b, t, LANES), F32)],
        scratch_shapes=[pltpu.VMEM((TM + 2 * DN_HALO, w), F32)],
        compiler_params=_cp(("parallel", "arbitrary")),
    )(pc, pc, pc, pcs, conv_w, pad_vec(a_log), pad_vec(dt_bias))


def _dn_local_kernel(q_ref, k_ref, v_ref, bg_ref, u_out, w_out, qi_out, a_out, ket_out, ge_out):
    rr = _iota((TM, GW), 0)
    cc = _iota((TM, GW), 1)
    i_in = rr & (CH - 1)
    j_in = cc & (CH - 1)
    bd = (rr >> 6) == (cc >> 6)
    eye_t = (i_in == j_in).astype(F32)
    incl = (j_in <= i_in, j_in >= i_in)
    strict = (j_in < i_in, j_in > i_in)
    ones_bd = bd.astype(F32)
    eye_bf = (rr == cc).astype(BF16)
    head_of_lane = _iota((LANES, GW), 1) >> 6
    src = _iota((LANES, GW), 0)
    q = q_ref[0]
    k = k_ref[0]
    v = v_ref[0]
    bg = bg_ref[0]
    def parts(m, n):
        out = []
        for _ in range(n):
            hi = m.astype(BF16)
            out.append(hi)
            m = m - hi.astype(F32)
        return out

    ones_bf = ones_bd.astype(BF16)
    eye_f = eye_t
    beta, gcum, decay, kb, egc = [], [], [], [], []
    for d in range(2):
        sel_b = (src == d * HEADS + head_of_lane).astype(BF16)
        sel_g = (src == 2 * HEADS + d * HEADS + head_of_lane).astype(BF16)
        cum_bd = jnp.where(jnp.logical_and(bd, incl[d]), 1.0, 0.0).astype(BF16)
        g_exp = _dot(jnp.concatenate(parts(bg, 2), axis=0), sel_g)
        beta.append(_dot(bg.astype(BF16), sel_b))
        cs = _dot(cum_bd, jnp.concatenate([g_exp[:TM].astype(BF16), g_exp[TM:].astype(BF16)], axis=1))
        gcum.append(cs[:, :GW] + cs[:, GW:])
        gparts = parts(gcum[d] * eye_f, 3)
        tr = _dot(ones_bf, jnp.concatenate(gparts, axis=1))
        grow = tr[:, :GW] + tr[:, GW:2 * GW] + tr[:, 2 * GW:]
        decay.append(jnp.where(incl[d], jnp.exp(jnp.where(incl[d], gcum[d] - grow, 0.0)), 0.0))
        kb.append(k * beta[d])
        egc.append(jnp.exp(gcum[d]))
        qi_out[0, d] = (q * egc[d]).astype(BF16)
    pairs = [(c, d) for c in range(CPT) for d in range(2)]
    rows = [slice(c * CH, (c + 1) * CH) for c in range(CPT)]
    eye = eye_t[:CH]
    a = {}
    for c in range(CPT):
        r = rows[c]
        lhs = jnp.concatenate([kb[0][r], kb[1][r], q[r]], axis=0).astype(BF16)
        aq = _dot_nt(lhs, _bd(k[r], bd))
        for d in range(2):
            dec = decay[d][r]
            a[c, d] = jnp.where(strict[d][:CH], aq[d * CH:(d + 1) * CH] * dec, 0.0)
            a_out[0, d, r, :] = jnp.where(incl[d][:CH], aq[2 * CH:] * dec, 0.0).astype(BF16)
    t_inv = {cd: eye - a[cd] for cd in pairs}
    p = {cd: _dot(a[cd].astype(BF16), _bd(a[cd], bd)) for cd in pairs}
    for it in range(5):
        for cd in pairs:
            pbd = _bd(p[cd], bd)
            if it < 4:
                res = _dot(jnp.concatenate([t_inv[cd], p[cd]], axis=0).astype(BF16), pbd)
                t_inv[cd] = t_inv[cd] + res[:CH]
                p[cd] = res[CH:]
            else:
                t_inv[cd] = t_inv[cd] + _dot(t_inv[cd].astype(BF16), pbd)

    def split(m):
        hi = m.astype(BF16)
        return hi, (m - hi.astype(F32)).astype(BF16)

    for cd in pairs:
        x0 = t_inv[cd]
        ah, al = split(a[cd])
        xh, xl = split(x0)
        hx = _dot(jnp.concatenate([ah, al], axis=0), _bd(xh, bd))
        resid = eye - x0 - (hx[:CH] + hx[CH:] + _dot(ah, _bd(xl, bd)))
        t_inv[cd] = x0 + _dot(xh, _bd(resid, bd))
    for c, d in pairs:
        r = rows[c]
        tb = t_inv[c, d].astype(BF16)
        u_out[0, d, r, :] = _dot(tb, _bd(v[r] * beta[d][r], bd))
        w_out[0, d, r, :] = _dot(tb, _bd(kb[d][r] * egc[d][r], bd)).astype(BF16)
        last = (c + 1) * CH - 1 if d == 0 else c * CH
        gtot = gcum[d][last:last + 1]
        k_end = (k[r] * jnp.exp(gtot - gcum[d][r])).astype(BF16)
        ket_out[0, d, c] = _dot_nt(eye_bf, k_end).astype(BF16)
        ge_out[0, d, c] = jnp.exp(gtot)


def _dn_local(q, k, v, bg):
    b, t, _ = q.shape
    nt = t // TM
    row = lambda bb, i: (bb, i, 0)
    drow = pl.BlockSpec((1, 2, TM, GW), lambda bb, i: (bb, 0, i, 0))
    return pl.pallas_call(
        _dn_local_kernel,
        grid=(b, nt),
        in_specs=[pl.BlockSpec((1, TM, GW), row)] * 3 + [pl.BlockSpec((1, TM, LANES), row)],
        out_specs=[drow, drow, drow, drow,
                   pl.BlockSpec((1, 2, CPT, GW, CH), lambda bb, i: (bb, 0, i, 0, 0)),
                   pl.BlockSpec((1, 2, CPT, 1, GW), lambda bb, i: (bb, 0, i, 0, 0))],
        out_shape=[jax.ShapeDtypeStruct((b, 2, t, GW), F32)] + [jax.ShapeDtypeStruct((b, 2, t, GW), BF16)] * 3
        + [jax.ShapeDtypeStruct((b, 2, t // CH, GW, CH), BF16), jax.ShapeDtypeStruct((b, 2, t // CH, 1, GW), F32)],
        compiler_params=_cp(("parallel", "arbitrary")),
    )(q, k, v, bg)


def _dn_scan_kernel(uf, wf, qf, af, kf, gf, ur, wr, qr, ar, kr, gr, of_ref, or_ref, s_scr):
    @pl.when(pl.program_id(1) == 0)
    def _():
        s_scr[...] = jnp.zeros(s_scr.shape, F32)

    bd = (_iota((GW, GW), 0) >> 6) == (_iota((GW, GW), 1) >> 6)
    dirs = ((uf, wf, qf, af, kf, gf, of_ref), (ur, wr, qr, ar, kr, gr, or_ref))
    for c in range(CPT):
        for d, (u, w, qi, a, ket, ge, o_ref) in enumerate(dirs):
            cc = c if d == 0 else CPT - 1 - c
            rows = slice(cc * CH, (cc + 1) * CH)
            s = s_scr[d]
            wq = _dot(jnp.concatenate([w[0, 0, rows, :], qi[0, 0, rows, :]], axis=0), s.astype(BF16))
            v_new = u[0, 0, rows, :] - wq[:CH]
            o_ref[0, rows, :] = wq[CH:] + _dot(a[0, 0, rows, :], _bd(v_new, bd))
            s_scr[d] = s * ge[0, 0, cc] + jnp.where(bd, _dot(ket[0, 0, cc], v_new.astype(BF16)), 0.0)


def _dir_specs(shape_tail, n_tiles, chunked):
    blk = (1, 1, CPT if chunked else TM) + shape_tail
    zeros = (0,) * len(shape_tail)
    fwd = pl.BlockSpec(blk, lambda b, s: (b, 0, _fwd_tile(s, n_tiles)) + zeros)
    rev = pl.BlockSpec(blk, lambda b, s: (b, 1, _rev_tile(s, n_tiles)) + zeros)
    return fwd, rev


def _dn_scan(u, w, qi, a, ket, ge):
    b, _, t, _ = u.shape
    nt = t // TM
    rowf, rowr = _dir_specs((GW,), nt, False)
    ketf, ketr = _dir_specs((GW, CH), nt, True)
    gef, ger = _dir_specs((1, GW), nt, True)
    return pl.pallas_call(
        _dn_scan_kernel,
        grid=(b, nt),
        in_specs=[rowf, rowf, rowf, rowf, ketf, gef, rowr, rowr, rowr, rowr, ketr, ger],
        out_specs=[pl.BlockSpec((1, TM, GW), lambda bb, s: (bb, _fwd_tile(s, nt), 0)),
                   pl.BlockSpec((1, TM, GW), lambda bb, s: (bb, _rev_tile(s, nt), 0))],
        out_shape=[jax.ShapeDtypeStruct((b, t, GW), F32)] * 2,
        scratch_shapes=[pltpu.VMEM((2, GW, GW), F32)],
        compiler_params=_cp(("parallel", "arbitrary")),
    )(u, w, qi, a, ket, ge, u, w, qi, a, ket, ge)


GLA_QK = HEADS * GLA_K


def _gla_local_kernel(pd_ref, pdl_ref, w2_ref, b2_ref, qi_out, a_out, ke_out, vt_out, vb_out, de_out):
    _, incl, _, cum, _ = _chunk_masks()
    eye_bf = (_iota((GW, GW), 0) == _iota((GW, GW), 1)).astype(BF16)
    bdk = (_iota((GW, GLA_QK), 0) >> 6) == (_iota((GW, GLA_QK), 1) >> 5)
    z = _dot(pdl_ref[0], w2_ref[...], HI) + b2_ref[...]
    gk_all = -_softplus(-z) * (1.0 / GLA_TAU)
    for c in range(CPT):
        rows = slice(c * CH, (c + 1) * CH)
        p = pd_ref[0, rows, :]
        q = p[:, :GLA_QK] * (GLA_K ** -0.5)
        k = p[:, GLA_QK:2 * GLA_QK]
        vb = p[:, 2 * GLA_QK:].astype(BF16)
        vb_out[0, rows, :] = vb
        vt_out[0, c] = _dot_nt(eye_bf, vb).astype(BF16)
        for d in range(2):
            bcs = _dot(cum[d], gk_all[rows, d * GLA_QK:(d + 1) * GLA_QK], HI)
            bend = bcs[CH - 1:CH] if d == 0 else bcs[0:1]
            q_in = (q * jnp.exp(bcs)).astype(BF16)
            kdec = jnp.where(bdk, _tile4(k * jnp.exp(-bcs)), 0.0).astype(BF16)
            a_out[0, d, rows, :] = jnp.where(incl[d], _dot_nt(q_in, kdec), 0.0).astype(BF16)
            qi_out[0, d, rows, :] = q_in
            ke_out[0, d, rows, :] = (k * jnp.exp(bend - bcs)).astype(BF16)
            de_out[0, d, c] = jnp.exp(bend)


def _gla_local(pd, pdl, w2bd, b2):
    b, t, w = pd.shape
    nt = t // TM
    row = lambda bb, i: (bb, i, 0)
    return pl.pallas_call(
        _gla_local_kernel,
        grid=(b, nt),
        in_specs=[pl.BlockSpec((1, TM, w), row), pl.BlockSpec((1, TM, LANES), row),
                  pl.BlockSpec((LANES, GW), lambda bb, i: (0, 0)), pl.BlockSpec((1, GW), lambda bb, i: (0, 0))],
        out_specs=[pl.BlockSpec((1, 2, TM, GLA_QK), lambda bb, i: (bb, 0, i, 0)),
                   pl.BlockSpec((1, 2, TM, GW), lambda bb, i: (bb, 0, i, 0)),
                   pl.BlockSpec((1, 2, TM, GLA_QK), lambda bb, i: (bb, 0, i, 0)),
                   pl.BlockSpec((1, CPT, GW, CH), lambda bb, i: (bb, i, 0, 0)),
                   pl.BlockSpec((1, TM, GW), row),
                   pl.BlockSpec((1, 2, CPT, 1, GLA_QK), lambda bb, i: (bb, 0, i, 0, 0))],
        out_shape=[jax.ShapeDtypeStruct((b, 2, t, GLA_QK), BF16), jax.ShapeDtypeStruct((b, 2, t, GW), BF16),
                   jax.ShapeDtypeStruct((b, 2, t, GLA_QK), BF16), jax.ShapeDtypeStruct((b, t // CH, GW, CH), BF16),
                   jax.ShapeDtypeStruct((b, t, GW), BF16), jax.ShapeDtypeStruct((b, 2, t // CH, 1, GLA_QK), F32)],
        compiler_params=_cp(("parallel", "arbitrary")),
    )(pd, pdl, w2bd, b2)


def _gla_scan_kernel(qf, af, kf, df, vtf, vf, qr, ar, kr, dr, vtr, vr, of_ref, or_ref, s_scr):
    @pl.when(pl.program_id(1) == 0)
    def _():
        s_scr[...] = jnp.zeros(s_scr.shape, F32)

    bd = (_iota((GW, GW), 0) >> 6) == (_iota((GW, GW), 1) >> 6)
    bdt = (_iota((GW, GLA_QK), 0) >> 6) == (_iota((GW, GLA_QK), 1) >> 5)
    dirs = ((qf, af, kf, df, vtf, vf, of_ref), (qr, ar, kr, dr, vtr, vr, or_ref))
    for c in range(CPT):
        for d, (qi, a, ke, de, vt, v, o_ref) in enumerate(dirs):
            cc = c if d == 0 else CPT - 1 - c
            rows = slice(cc * CH, (cc + 1) * CH)
            st = s_scr[d]
            vbd = jnp.where(bd, _tile4(v[0, rows, :]), jnp.zeros((), BF16))
            o_ref[0, rows, :] = _dot_nt(qi[0, 0, rows, :], st.astype(BF16)) + _dot(a[0, 0, rows, :], vbd)
            s_scr[d] = st * de[0, 0, cc] + jnp.where(bdt, _dot(vt[0, cc], ke[0, 0, rows, :]), 0.0)


def _gla_scan(qi, a, ke, de, vt, vb):
    b, _, t, _ = a.shape
    nt = t // TM
    qf, qr = _dir_specs((GLA_QK,), nt, False)
    af, ar = _dir_specs((GW,), nt, False)
    df, dr = _dir_specs((1, GLA_QK), nt, True)
    vtf = pl.BlockSpec((1, CPT, GW, CH), lambda bb, s: (bb, _fwd_tile(s, nt), 0, 0))
    vtr = pl.BlockSpec((1, CPT, GW, CH), lambda bb, s: (bb, _rev_tile(s, nt), 0, 0))
    vf = pl.BlockSpec((1, TM, GW), lambda bb, s: (bb, _fwd_tile(s, nt), 0))
    vr = pl.BlockSpec((1, TM, GW), lambda bb, s: (bb, _rev_tile(s, nt), 0))
    return pl.pallas_call(
        _gla_scan_kernel,
        grid=(b, nt),
        in_specs=[qf, af, qf, df, vtf, vf, qr, ar, qr, dr, vtr, vr],
        out_specs=[vf, vr],
        out_shape=[jax.ShapeDtypeStruct((b, t, GW), F32)] * 2,
        scratch_shapes=[pltpu.VMEM((2, GW, GLA_QK), F32)],
        compiler_params=_cp(("parallel", "arbitrary")),
    )(qi, a, ke, de, vt, vb, qi, a, ke, de, vt, vb)


def _outproj_kernel(x_ref, ya_ref, yb_ref, cf_ref, cr_ref, df_ref, dr_ref, pg_ref, g1_ref, gc_ref, gd_ref, w_ref, *rest):
    o_ref = rest[-1]
    ones64 = _group_ones(GW, 6)

    def fin(o, g, gate):
        ms = _dot(o * o, ones64, HI) * (1.0 / HEAD_DIM)
        return (o * lax.rsqrt(ms + EPS) * g * _silu(gate)).astype(BF16)

    pg = pg_ref[0]
    yc = fin(cf_ref[0] + cr_ref[0], gc_ref[...], pg[:, :GW])
    yd = fin(df_ref[0] + dr_ref[0], gd_ref[...], pg[:, GW:])
    res = (_dot(ya_ref[0].astype(BF16), w_ref[0:GW, :]) + _dot(yb_ref[0].astype(BF16), w_ref[GW:2 * GW, :])
           + _dot(yc, w_ref[2 * GW:3 * GW, :]) + _dot(yd, w_ref[3 * GW:, :]))
    o_ref[0] = x_ref[0] + g1_ref[0] * res


def _outproj(xs, ya, yb, ocf, ocr, odf, odr, pg, mod3, dn_g, gla_g, w_out_bf, n_batch, ctx, out_rows, prev=None):
    b, t, d = xs.shape
    reps = GW // HEAD_DIM

    def specs(tm, blk0):
        row = lambda bb, i: (bb, i + blk0, 0)
        g256 = pl.BlockSpec((1, tm, GW), row)
        return [pl.BlockSpec((1, tm, d), row), g256, g256, g256, g256, g256, g256, pl.BlockSpec((1, tm, 2 * GW), row),
                _mod_spec(2, n_batch, ctx), _resident((1, GW)), _resident((1, GW)), _resident((d, d))]

    args = (xs, ya, yb, ocf, ocr, odf, odr, pg, mod3, jnp.tile(dn_g, reps).reshape(1, GW),
            jnp.tile(gla_g, reps).reshape(1, GW), w_out_bf)
    return _row_call(_outproj_kernel, t, ctx, n_batch, args, specs, [d], out_rows, prev)[0]


FF_HALO = 8


def _ffn_kernel(xm_ref, xl_ref, xr_ref, sh_ref, sc_ref, g2_ref, ng_ref, wa_ref, wg_ref, cwa_ref, cwg_ref, wd_ref,
                *rest, ctx):
    o_ref, ext_ref = rest[-2:]
    tm = xm_ref.shape[1]
    i = pl.program_id(1)
    left_ok = jnp.logical_and(i > 0, not ctx)
    right_ok = jnp.logical_and(i < pl.num_programs(1) - 1, not ctx)
    rows_ext = _iota((tm + 2 * FF_HALO, 1), 0)
    keep = jnp.logical_and(jnp.logical_or(rows_ext >= FF_HALO, left_ok),
                           jnp.logical_or(rows_ext < FF_HALO + tm, right_ok))
    x = jnp.concatenate([xl_ref[0], xm_ref[0], xr_ref[0]], axis=0)
    y = x * lax.rsqrt(jnp.mean(x * x, axis=-1, keepdims=True) + EPS) * ng_ref[...]
    h = jnp.where(keep, y * (1.0 + sc_ref[0]) + sh_ref[0], 0.0).astype(BF16)
    acc = jnp.zeros((tm, D_MODEL), F32)
    for j in range(N_FF_BLK):
        def conv(w_ref, cw_ref, half):
            ext_ref[half] = _dot(h, w_ref[j])
            cw = cw_ref[j]
            return (cw[0:1] * ext_ref[half, pl.ds(FF_HALO - 1, tm), :] + cw[1:2] * ext_ref[half, pl.ds(FF_HALO, tm), :]
                    + cw[2:3] * ext_ref[half, pl.ds(FF_HALO + 1, tm), :])
        a = conv(wa_ref, cwa_ref, 0)
        g = conv(wg_ref, cwg_ref, 1)
        acc = acc + _dot((_silu(g) * a).astype(BF16), wd_ref[j])
    o_ref[0] = xm_ref[0] + g2_ref[0] * acc


def _ffn(x1, stream_rows, mod3, norm_g, wa, wg, cwa, cwg, wd, n_batch, ctx, prev=None):
    b, t, d = x1.shape
    tm = TM if ctx else TR

    def specs(tm, blk0):
        per = tm // FF_HALO
        last = t // FF_HALO - 1
        return [pl.BlockSpec((1, tm, d), lambda bb, i: (bb, i + blk0, 0)),
                pl.BlockSpec((1, FF_HALO, d), lambda bb, i: (bb, jnp.maximum((i + blk0) * per - 1, 0), 0)),
                pl.BlockSpec((1, FF_HALO, d), lambda bb, i: (bb, jnp.minimum((i + blk0 + 1) * per, last), 0)),
                _mod_spec(3, n_batch, ctx), _mod_spec(4, n_batch, ctx), _mod_spec(5, n_batch, ctx), _resident((1, d)),
                _resident((N_FF_BLK, d, FF_BLK)), _resident((N_FF_BLK, d, FF_BLK)),
                _resident((N_FF_BLK, 3, FF_BLK)), _resident((N_FF_BLK, 3, FF_BLK)), _resident((N_FF_BLK, FF_BLK, d))]

    args = (x1, x1, x1, mod3, mod3, mod3, norm_g.reshape(1, d), wa, wg, cwa, cwg, wd)
    kern = functools.partial(_ffn_kernel, ctx=ctx)
    return _row_call(kern, stream_rows, ctx, n_batch, args, specs, [d], t, prev,
                     scratch=[pltpu.VMEM((2, tm + 2 * FF_HALO, FF_BLK), F32)])[0]


def _rope_tables(seq, ctx_len):
    rows = seq // GRID_W
    row = jnp.repeat(jnp.arange(rows, dtype=F32), GRID_W)
    col = jnp.tile(jnp.arange(GRID_W, dtype=F32), rows)
    nf = QK_DIM // 4
    inv = ROPE_THETA ** (-jnp.arange(nf, dtype=F32) / nf)
    ang = jnp.concatenate([row[:, None] * inv, col[:, None] * inv], axis=-1)
    cos = jnp.concatenate([jnp.cos(ang), jnp.ones((ctx_len, QK_DIM // 2), F32)], axis=0)
    sin = jnp.concatenate([jnp.sin(ang), jnp.zeros((ctx_len, QK_DIM // 2), F32)], axis=0)
    reps = GW // QK_DIM
    return (jnp.tile(jnp.concatenate([cos, cos], axis=-1), (1, reps)),
            jnp.tile(jnp.concatenate([-sin, sin], axis=-1), (1, reps)))


def _regroup_w_in(w):
    d = w.shape[0]
    z = lambda n: jnp.zeros((d, n), w.dtype)
    return jnp.concatenate([w[:, :2048], w[:, 2048:2064], z(LANES - 16), w[:, 2320:2832], w[:, 2832:2864],
                            z(LANES - 32), w[:, 2064:2320], w[:, 2864:3120]], axis=1).astype(BF16)


def _gla_w2_blockdiag(w2):
    out = jnp.zeros((LANES, GW), F32)
    out = out.at[0:GLA_RANK, 0:GLA_QK].set(w2[0])
    return out.at[GLA_RANK:2 * GLA_RANK, GLA_QK:].set(w2[1])


def _layer(xs, mod3, lp, cos_t, sin_t, layer_idx, last, n_batch):
    b, t, d = xs.shape
    nt = t // TM
    w_in_r = _regroup_w_in(lp["w_in"])
    p_lat = _inproj(xs, mod3, lp["norm1_g"], w_in_r, n_batch, False)
    pa, pb, pc, pcs, pd, pdl, pg = _inproj(xs, mod3, lp["norm1_g"], w_in_r, n_batch, True, prev=p_lat)

    ya = _conv_module(pa, lp["cm_conv_w"], lp["cm_conv_b"], lp["cm_ln_g"], lp["cm_ln_b"])

    lam_init = 0.8 - 0.6 * math.exp(-0.3 * layer_idx)
    q, kt, v, qn, kn = _attn_prep(pb, lp["da_qnorm_g"], lp["da_knorm_g"], cos_t, sin_t)
    attn_args = (lp["da_lambda"], lp["da_subln_g"], lam_init)
    lat_rows = t - TM
    yb = _attention(q, kt, v, qn, kn, *attn_args, TQ, 0, lat_rows // TQ, nt, 0, lat_rows if last else t)
    if not last:
        yb = _attention(q, kt, v, qn, kn, *attn_args, TM, nt - 1, 1, 1, nt - 1, t, prev=yb)

    dq, dk, dv, bg = _dn_prep(pc, pcs, lp["dn_conv_w"], lp["dn_a_log"], lp["dn_dt_bias"])
    ocf, ocr = _dn_scan(*_dn_local(dq, dk, dv, bg))

    qi, a, ke, vt, vb, de = _gla_local(pd, pdl, _gla_w2_blockdiag(lp["gla_w2"]), lp["gla_b2"].reshape(1, GW))
    odf, odr = _gla_scan(qi, a, ke, de, vt, vb)

    out_rows = lat_rows if last else t
    op_args = (xs, ya, yb, ocf, ocr, odf, odr, pg, mod3, lp["dn_onorm_g"], lp["gla_onorm_g"],
               lp["w_out"].astype(BF16), n_batch)
    x1 = _outproj(*op_args, False, out_rows)
    if not last:
        x1 = _outproj(*op_args, True, out_rows, prev=[x1])

    w_up = lp["ffn_w_up"].astype(BF16)
    blocks = lambda m: jnp.transpose(m.reshape(m.shape[0], N_FF_BLK, FF_BLK), (1, 0, 2))
    cw = lp["ffn_conv_w"]
    ffn_args = (t, mod3, lp["norm2_g"], blocks(w_up[:, :D_FF]), blocks(w_up[:, D_FF:]),
                blocks(cw[:, :D_FF]), blocks(cw[:, D_FF:]),
                lp["ffn_w_down"].astype(BF16).reshape(N_FF_BLK, FF_BLK, d), n_batch)
    x2 = _ffn(x1, *ffn_args, False)
    if not last:
        x2 = _ffn(x1, *ffn_args, True, prev=[x2])
    return x2


def kernel(x, c, ctx, c_ctx, w_mod, b_mod, norm1_g, norm2_g, w_in, w_out, cm_conv_w, cm_conv_b, cm_ln_g, cm_ln_b, da_qnorm_g, da_knorm_g, da_lambda, da_subln_g, dn_conv_w, dn_a_log, dn_dt_bias, dn_onorm_g, gla_w2, gla_b2, gla_onorm_g, ffn_w_up, ffn_conv_w, ffn_w_down):
    n_batch, seq, d = x.shape
    ctx_len = ctx.shape[1]
    assert ctx_len == TM and seq % TQ == 0 and d == D_MODEL
    depth = w_mod.shape[0]
    cos_t, sin_t = _rope_tables(seq, ctx_len)
    xs = jnp.concatenate([x, ctx], axis=1)
    mod_rows = 16
    c_rows = jnp.zeros((mod_rows, d), F32).at[:n_batch].set(c).at[n_batch].set(c_ctx)
    params = dict(w_mod=w_mod, b_mod=b_mod, norm1_g=norm1_g, norm2_g=norm2_g, w_in=w_in, w_out=w_out,
                  cm_conv_w=cm_conv_w, cm_conv_b=cm_conv_b, cm_ln_g=cm_ln_g, cm_ln_b=cm_ln_b,
                  da_qnorm_g=da_qnorm_g, da_knorm_g=da_knorm_g, da_lambda=da_lambda, da_subln_g=da_subln_g,
                  dn_conv_w=dn_conv_w, dn_a_log=dn_a_log, dn_dt_bias=dn_dt_bias, dn_onorm_g=dn_onorm_g,
                  gla_w2=gla_w2, gla_b2=gla_b2, gla_onorm_g=gla_onorm_g,
                  ffn_w_up=ffn_w_up, ffn_conv_w=ffn_conv_w, ffn_w_down=ffn_w_down)
    for l in range(depth):
        lp = {k: v[l] for k, v in params.items()}
        mod3 = _modulation(c_rows, lp["w_mod"], lp["b_mod"]).reshape(mod_rows * 6, 1, d)
        xs = _layer(xs, mod3, lp, cos_t, sin_t, l, l == depth - 1, n_batch)
    return xs
```

```python
import functools
import math

import jax
import jax.numpy as jnp
from jax import lax
from jax.experimental import pallas as pl
from jax.experimental.pallas import tpu as pltpu

F32 = jnp.float32
BF16 = jnp.bfloat16
HI = lax.Precision.HIGHEST
EPS = 1e-6

D_MODEL = 1024
GRID_W = 64
HEADS = 4
HEAD_DIM = 64
GW = 256
QK_DIM = 32
GLA_K = 32
GLA_RANK = 16
GLA_TAU = 16.0
CM_KERNEL = 31
DN_CONV = 5
ROPE_THETA = 10000.0
CH = 64
TM = 256
CPT = TM // CH
D_FF = 2816
FF_BLK = 256
N_FF_BLK = D_FF // FF_BLK
LANES = 128

IN_GROUPS = (("pa", 512), ("pb", 768), ("pc", 768), ("pcs", LANES), ("pd", 512), ("pdl", LANES), ("pg", 512))
IN_COLS_PAD = sum(w for _, w in IN_GROUPS)

VMEM_LIMIT = 56 * 1024 * 1024


def _cp(sem):
    return pltpu.CompilerParams(dimension_semantics=sem, vmem_limit_bytes=VMEM_LIMIT)


def _dot(a, b, prec=None):
    return jnp.dot(a, b, preferred_element_type=F32, precision=prec)


def _dot_nt(a, b, prec=None):
    return lax.dot_general(a, b, (((1,), (1,)), ((), ())), preferred_element_type=F32, precision=prec)


def _sigmoid(x):
    return 1.0 / (1.0 + jnp.exp(-x))


def _silu(x):
    return x * _sigmoid(x)


def _softplus(x):
    return jnp.maximum(x, 0.0) + jnp.log(1.0 + jnp.exp(-jnp.abs(x)))


def _iota(shape, dim):
    return lax.broadcasted_iota(jnp.int32, shape, dim)


def _group_ones(n, shift):
    return ((_iota((n, n), 0) >> shift) == (_iota((n, n), 1) >> shift)).astype(F32)


def _tile4(y):
    return jnp.concatenate([y, y, y, y], axis=0)


def _mod_kernel(c_ref, w_ref, b_ref, o_ref):
    o_ref[...] = _dot(_silu(c_ref[...]), w_ref[...], HI) + b_ref[...]


def _modulation(c_rows, w_mod, b_mod):
    r, d = c_rows.shape
    n = w_mod.shape[1] // d
    return pl.pallas_call(
        _mod_kernel,
        grid=(n,),
        in_specs=[pl.BlockSpec((r, d), lambda j: (0, 0)),
                  pl.BlockSpec((d, d), lambda j: (0, j)),
                  pl.BlockSpec((1, d), lambda j: (0, j))],
        out_specs=pl.BlockSpec((r, d), lambda j: (0, j)),
        out_shape=jax.ShapeDtypeStruct((r, n * d), F32),
        compiler_params=_cp(("arbitrary",)),
    )(c_rows, w_mod, b_mod.reshape(1, -1))


N_BLK = 4


def _mod_specs(k, n_batch):
    return [pl.BlockSpec((1, 1, D_MODEL), lambda b, i: (b * 6 + k, 0, 0)),
            pl.BlockSpec((1, 1, D_MODEL), lambda b, i: (n_batch * 6 + k, 0, 0))]


def _ctx_rows(tb, lat_rows):
    return pl.program_id(1) * tb + _iota((tb, 1), 0) >= lat_rows


def _resident(shape):
    zeros = (0,) * len(shape)
    return pl.BlockSpec(shape, lambda b, i: zeros, pipeline_mode=pl.Buffered(1))


def _row_call(kern, n_batch, rows, args, in_specs, out_widths, scratch=()):
    tb = rows // N_BLK
    return pl.pallas_call(
        kern,
        grid=(n_batch, N_BLK),
        in_specs=in_specs,
        out_specs=[pl.BlockSpec((1, tb, w), lambda bb, i: (bb, i, 0)) for w in out_widths],
        out_shape=[jax.ShapeDtypeStruct((n_batch, rows, w), F32) for w in out_widths],
        scratch_shapes=list(scratch),
        compiler_params=_cp(("parallel", "arbitrary")),
    )(*args)


def _inproj_kernel(x_ref, shb_ref, shc_ref, scb_ref, scc_ref, g_ref, w_ref, *outs, lat_rows):
    x = x_ref[0]
    ctx = _ctx_rows(x.shape[0], lat_rows)
    y = x * lax.rsqrt(jnp.mean(x * x, axis=-1, keepdims=True) + EPS) * g_ref[...]
    h = y * (1.0 + jnp.where(ctx, scc_ref[0], scb_ref[0])) + jnp.where(ctx, shc_ref[0], shb_ref[0])
    p = _dot(h.astype(BF16), w_ref[...])
    off = 0
    for o_ref, (_, width) in zip(outs, IN_GROUPS):
        o_ref[0] = p[:, off:off + width]
        off += width


def _inproj(xs, mod3, norm_g, w_in_r, n_batch):
    b, t, d = xs.shape
    in_specs = ([pl.BlockSpec((1, t // N_BLK, d), lambda bb, i: (bb, i, 0))] + _mod_specs(0, n_batch)
                + _mod_specs(1, n_batch) + [_resident((1, d)), _resident((d, IN_COLS_PAD))])
    kern = functools.partial(_inproj_kernel, lat_rows=t - TM)
    return _row_call(kern, n_batch, t, (xs, mod3, mod3, mod3, mod3, norm_g.reshape(1, d), w_in_r), in_specs,
                     [w for _, w in IN_GROUPS])


def _halo_specs(width, halo, n_tiles):
    per = TM // halo
    left = pl.BlockSpec((1, halo, width), lambda b, i: (b, jnp.maximum(i * per - 1, 0), 0))
    right = pl.BlockSpec((1, halo, width), lambda b, i: (b, jnp.minimum((i + 1) * per, n_tiles * per - 1), 0))
    return left, right


def _halo_ok(i, n_tiles):
    return jnp.logical_and(i >= 1, i < n_tiles - 1), i < n_tiles - 2


CM_HALO = 16


def _convmod_kernel(pm_ref, pl_ref, pr_ref, cw_ref, cb_ref, lg_ref, lb_ref, o_ref, ext_ref):
    left_ok, right_ok = _halo_ok(pl.program_id(1), pl.num_programs(1))

    def glu(p):
        return p[:, :GW] * _sigmoid(p[:, GW:])

    ext_ref[0:CM_HALO] = jnp.where(left_ok, glu(pl_ref[0]), 0.0)
    ext_ref[CM_HALO:CM_HALO + TM] = glu(pm_ref[0])
    ext_ref[CM_HALO + TM:] = jnp.where(right_ok, glu(pr_ref[0]), 0.0)
    pad = CM_KERNEL // 2
    acc = jnp.zeros((TM, GW), F32)
    for j in range(CM_KERNEL):
        acc = acc + cw_ref[j:j + 1, :] * ext_ref[pl.ds(CM_HALO - pad + j, TM), :]
    y = acc + cb_ref[...]
    mu = jnp.mean(y, axis=-1, keepdims=True)
    yc = y - mu
    var = jnp.mean(yc * yc, axis=-1, keepdims=True)
    o_ref[0] = _silu(yc * lax.rsqrt(var + EPS) * lg_ref[...] + lb_ref[...])


def _conv_module(pa, conv_w, conv_b, ln_g, ln_b):
    b, t, w = pa.shape
    nt = t // TM
    left, right = _halo_specs(w, CM_HALO, nt)
    vec = pl.BlockSpec((1, GW), lambda bb, i: (0, 0))
    return pl.pallas_call(
        _convmod_kernel,
        grid=(b, nt),
        in_specs=[pl.BlockSpec((1, TM, w), lambda bb, i: (bb, i, 0)), left, right,
                  pl.BlockSpec((CM_KERNEL, GW), lambda bb, i: (0, 0)), vec, vec, vec],
        out_specs=pl.BlockSpec((1, TM, GW), lambda bb, i: (bb, i, 0)),
        out_shape=jax.ShapeDtypeStruct((b, t, GW), F32),
        scratch_shapes=[pltpu.VMEM((TM + 2 * CM_HALO, GW), F32)],
        compiler_params=_cp(("parallel", "arbitrary")),
    )(pa, pa, pa, conv_w, conv_b.reshape(1, GW), ln_g.reshape(1, GW), ln_b.reshape(1, GW))


N_MAPS = 2 * HEADS
QK_PAD = 2 * QK_DIM
V_EXT = 2 * HEAD_DIM
MASK_BIG = 8192.0
SHIFT_MAX = 40.0


def _attn_prep_kernel(pb_ref, qg_ref, kg_ref, cos_ref, sin_ref, q_out, kt_out, v_out, qn_out, kn_out):
    p = pb_ref[0]
    ones32 = _group_ones(GW, 5)
    first = (_iota((TM, GW), 1) & (QK_DIM - 1)) < QK_DIM // 2
    cos = cos_ref[...]
    sin = sin_ref[...]

    def norm_rope(t, g):
        ms = _dot(t * t, ones32, HI) * (1.0 / QK_DIM)
        tn = t * lax.rsqrt(ms + EPS) * g
        partner = jnp.where(first, pltpu.roll(tn, GW - QK_DIM // 2, 1), pltpu.roll(tn, QK_DIM // 2, 1))
        return tn * cos + partner * sin

    qf = norm_rope(p[:, :GW], qg_ref[...]) * (QK_DIM ** -0.5)
    kf = norm_rope(p[:, GW:2 * GW], kg_ref[...])
    map_sel = ((_iota((GW, N_MAPS), 0) >> 5) == _iota((GW, N_MAPS), 1)).astype(F32)
    qn_out[0] = _dot(qf * qf, map_sel, HI)
    kn_out[0] = _dot(kf * kf, map_sel, HI)
    q = qf.astype(BF16)
    kt = kf.T.astype(BF16)
    is_ctx = pl.program_id(1) == pl.num_programs(1) - 1
    k_row = _iota((QK_DIM, TM), 0)
    k_tail = jnp.where(k_row == 0, 1.0, jnp.where(jnp.logical_and(k_row == 1, jnp.logical_not(is_ctx)), 1.0, 0.0))
    k_tail = k_tail.astype(BF16)
    q_tail = jnp.where(jnp.logical_and(_iota((TM, QK_DIM), 1) == 1, is_ctx), -MASK_BIG, 0.0).astype(BF16)
    for g in range(N_MAPS):
        q_out[0, g, :, 0:QK_DIM] = q[:, g * QK_DIM:(g + 1) * QK_DIM]
        q_out[0, g, :, QK_DIM:] = q_tail
        kt_out[0, 0, g, 0:QK_DIM, :] = kt[g * QK_DIM:(g + 1) * QK_DIM, :]
        kt_out[0, 0, g, QK_DIM:, :] = k_tail
    v = p[:, 2 * GW:].astype(BF16)
    ones = jnp.ones((TM, HEAD_DIM), BF16)
    for h in range(HEADS):
        v_out[0, 0, h, :, 0:HEAD_DIM] = v[:, h * HEAD_DIM:(h + 1) * HEAD_DIM]
        v_out[0, 0, h, :, HEAD_DIM:] = ones


def _attn_prep(pb, qn_g, kn_g, cos_t, sin_t):
    b, t, w = pb.shape
    nt = t // TM
    vec = pl.BlockSpec((1, GW), lambda bb, i: (0, 0))
    tab = pl.BlockSpec((TM, GW), lambda bb, i: (i, 0))
    reps = GW // QK_DIM
    n_spec = pl.BlockSpec((1, TM, N_MAPS), lambda bb, i: (bb, i, 0))
    n_shape = jax.ShapeDtypeStruct((b, t, N_MAPS), F32)
    return pl.pallas_call(
        _attn_prep_kernel,
        grid=(b, nt),
        in_specs=[pl.BlockSpec((1, TM, w), lambda bb, i: (bb, i, 0)), vec, vec, tab, tab],
        out_specs=[pl.BlockSpec((1, N_MAPS, TM, QK_PAD), lambda bb, i: (bb, 0, i, 0)),
                   pl.BlockSpec((1, 1, N_MAPS, QK_PAD, TM), lambda bb, i: (bb, i, 0, 0, 0)),
                   pl.BlockSpec((1, 1, HEADS, TM, V_EXT), lambda bb, i: (bb, i, 0, 0, 0)), n_spec, n_spec],
        out_shape=[jax.ShapeDtypeStruct((b, N_MAPS, t, QK_PAD), BF16),
                   jax.ShapeDtypeStruct((b, nt, N_MAPS, QK_PAD, TM), BF16),
                   jax.ShapeDtypeStruct((b, nt, HEADS, TM, V_EXT), BF16), n_shape, n_shape],
        compiler_params=_cp(("parallel", "arbitrary")),
    )(pb, jnp.tile(qn_g, reps).reshape(1, GW), jnp.tile(kn_g, reps).reshape(1, GW), cos_t, sin_t)


def _attn_kernel(q_ref, kt_ref, v_ref, qn_ref, kn_ref, lam_ref, sg_ref, o_ref, qa_scr, acc_scr, s_scr, m_scr,
                 *, n_chunks, lam_init):
    tq = o_ref.shape[1]
    acc_scr[...] = jnp.zeros(acc_scr.shape, F32)
    k2 = jnp.max(kn_ref[0], axis=0, keepdims=True)
    bound = jnp.sqrt(qn_ref[0] * k2)
    safe = jnp.max(bound) <= SHIFT_MAX
    shift = jnp.where(safe, bound, 0.0)
    shift_lane = _iota((tq, QK_PAD), 1) == QK_DIM
    for g in range(N_MAPS):
        qa_scr[g] = jnp.where(shift_lane, (-shift[:, g:g + 1]).astype(BF16), q_ref[0, g])

    @pl.when(safe)
    def _():
        s_scr[0] = _dot(qa_scr[0], kt_ref[0, 0, 0])

        def body(c, carry):
            nxt = jnp.minimum(c + 1, n_chunks - 1)
            for g in range(N_MAPS):
                if g + 1 < N_MAPS:
                    s_next = _dot(qa_scr[g + 1], kt_ref[0, c, g + 1])
                else:
                    s_next = _dot(qa_scr[0], kt_ref[0, nxt, 0])
                p = jnp.exp(s_scr[g & 1]).astype(BF16)
                acc_scr[g] += _dot(p, v_ref[0, c, g // 2])
                s_scr[(g + 1) & 1] = s_next
            return carry

        lax.fori_loop(0, n_chunks, body, 0)

    @pl.when(jnp.logical_not(safe))
    def _():
        m_scr[...] = jnp.full(m_scr.shape, -jnp.inf, F32)

        def body(c, carry):
            for g in range(N_MAPS):
                s = _dot(qa_scr[g], kt_ref[0, c, g])
                m_old = m_scr[:, g:g + 1]
                m_new = jnp.maximum(m_old, jnp.max(s, axis=-1, keepdims=True))
                p = jnp.exp(s - m_new).astype(BF16)
                acc_scr[g] = jnp.exp(m_old - m_new) * acc_scr[g] + _dot(p, v_ref[0, c, g // 2])
                m_scr[:, g:g + 1] = m_new
            return carry

        lax.fori_loop(0, n_chunks, body, 0)

    lp = lam_ref[...]
    lam = (jnp.exp(jnp.sum(lp[0:1] * lp[1:2], axis=-1, keepdims=True))
           - jnp.exp(jnp.sum(lp[2:3] * lp[3:4], axis=-1, keepdims=True)) + lam_init)
    for h in range(HEADS):
        a0 = acc_scr[2 * h]
        a1 = acc_scr[2 * h + 1]
        o = (a0[:, :HEAD_DIM] / a0[:, HEAD_DIM:HEAD_DIM + 1]
             - lam * (a1[:, :HEAD_DIM] / a1[:, HEAD_DIM:HEAD_DIM + 1]))
        y = o * lax.rsqrt(jnp.mean(o * o, axis=-1, keepdims=True) + EPS) * sg_ref[...] * (1.0 - lam_init)
        o_ref[0, :, h * HEAD_DIM:(h + 1) * HEAD_DIM] = y


def _attention(q, kt, v, qn, kn, lam_p, subln_g, lam_init):
    b, _, t, _ = q.shape
    tq = t // N_BLK
    nt = t // TM
    kern = functools.partial(_attn_kernel, n_chunks=nt, lam_init=lam_init)
    return pl.pallas_call(
        kern,
        grid=(b, N_BLK),
        in_specs=[pl.BlockSpec((1, N_MAPS, tq, QK_PAD), lambda bb, i: (bb, 0, i, 0)),
                  pl.BlockSpec((1, nt, N_MAPS, QK_PAD, TM), lambda bb, i: (bb, 0, 0, 0, 0)),
                  pl.BlockSpec((1, nt, HEADS, TM, V_EXT), lambda bb, i: (bb, 0, 0, 0, 0)),
                  pl.BlockSpec((1, tq, N_MAPS), lambda bb, i: (bb, i, 0)),
                  pl.BlockSpec((1, t, N_MAPS), lambda bb, i: (bb, 0, 0)),
                  _resident((4, QK_DIM)), _resident((1, HEAD_DIM))],
        out_specs=pl.BlockSpec((1, tq, GW), lambda bb, i: (bb, i, 0)),
        out_shape=jax.ShapeDtypeStruct((b, t, GW), F32),
        scratch_shapes=[pltpu.VMEM((N_MAPS, tq, QK_PAD), BF16), pltpu.VMEM((N_MAPS, tq, V_EXT), F32),
                        pltpu.VMEM((2, tq, TM), F32), pltpu.VMEM((tq, N_MAPS), F32)],
        compiler_params=_cp(("parallel", "arbitrary")),
    )(q, kt, v, qn, kn, lam_p, subln_g.reshape(1, HEAD_DIM))


def _chunk_masks():
    i = _iota((CH, GW), 0)
    j = _iota((CH, GW), 1) & (CH - 1)
    eye = (i == j).astype(F32)
    incl = (j <= i, j >= i)
    strict = (j < i, j > i)
    r = _iota((CH, CH), 0)
    c = _iota((CH, CH), 1)
    cum = ((c <= r).astype(F32), (c >= r).astype(F32))
    bd = (_iota((GW, GW), 0) >> 6) == (_iota((GW, GW), 1) >> 6)
    return eye, incl, strict, cum, bd


def _bd(y, bd):
    return jnp.where(bd, _tile4(y), 0.0).astype(BF16)


def _fwd_tile(s, n_tiles):
    return jnp.where(s == 0, n_tiles - 1, s - 1)


def _rev_tile(s, n_tiles):
    return jnp.where(s == 0, n_tiles - 1, n_tiles - 1 - s)


DN_HALO = 8


def _dn_prep_kernel(pm_ref, pl_ref, pr_ref, pcs_ref, cw_ref, alog_ref, dtb_ref, q_out, k_out, v_out, bg_out, ext_ref):
    left_ok, right_ok = _halo_ok(pl.program_id(1), pl.num_programs(1))
    ext_ref[0:DN_HALO] = jnp.where(left_ok, pl_ref[0], 0.0)
    ext_ref[DN_HALO:DN_HALO + TM] = pm_ref[0]
    ext_ref[DN_HALO + TM:] = jnp.where(right_ok, pr_ref[0], 0.0)
    pad = DN_CONV // 2
    acc = jnp.zeros((TM, 3 * GW), F32)
    for j in range(DN_CONV):
        acc = acc + cw_ref[j:j + 1, :] * ext_ref[pl.ds(DN_HALO - pad + j, TM), :]
    qkv = _silu(acc)
    ones64 = _group_ones(GW, 6)

    def l2n(t):
        return t * lax.rsqrt(_dot(t * t, ones64, HI) + EPS)

    q_out[0] = l2n(qkv[:, :GW]) * (HEAD_DIM ** -0.5)
    k_out[0] = l2n(qkv[:, GW:2 * GW])
    v_out[0] = qkv[:, 2 * GW:]
    s = pcs_ref[0]
    col = _iota(s.shape, 1)
    gate = -jnp.exp(alog_ref[...]) * _softplus(s + dtb_ref[...])
    bg_out[0] = jnp.where(col < 2 * HEADS, _sigmoid(s), jnp.where(col < 4 * HEADS, gate, 0.0))


def _dn_prep(pc, pcs, conv_w, a_log, dt_bias):
    b, t, w = pc.shape
    nt = t // TM
    left, right = _halo_specs(w, DN_HALO, nt)
    row = lambda bb, i: (bb, i, 0)
    pad_vec = lambda a: jnp.zeros((1, LANES), F32).at[0, 2 * HEADS:4 * HEADS].set(a.reshape(-1))
    vec = pl.BlockSpec((1, LANES), lambda bb, i: (0, 0))
    return pl.pallas_call(
        _dn_prep_kernel,
        grid=(b, nt),
        in_specs=[pl.BlockSpec((1, TM, w), row), left, right, pl.BlockSpec((1, TM, LANES), row),
                  pl.BlockSpec((DN_CONV, w), lambda bb, i: (0, 0)), vec, vec],
        out_specs=[pl.BlockSpec((1, TM, GW), row)] * 3 + [pl.BlockSpec((1, TM, LANES), row)],
        out_shape=[jax.ShapeDtypeStruct((b, t, GW), F32)] * 3 + [jax.ShapeDtypeStruct((b, t, LANES), F32)],
        scratch_shapes=[pltpu.VMEM((TM + 2 * DN_HALO, w), F32)],
        compiler_params=_cp(("parallel", "arbitrary")),
    )(pc, pc, pc, pcs, conv_w, pad_vec(a_log), pad_vec(dt_bias))


def _dn_local_kernel(q_ref, k_ref, v_ref, bg_ref, u_out, w_out, qi_out, a_out, ket_out, ge_out):
    rr = _iota((TM, GW), 0)
    cc = _iota((TM, GW), 1)
    i_in = rr & (CH - 1)
    j_in = cc & (CH - 1)
    bd = (rr >> 6) == (cc >> 6)
    eye_t = (i_in == j_in).astype(F32)
    incl = (j_in <= i_in, j_in >= i_in)
    strict = (j_in < i_in, j_in > i_in)
    ones_bd = bd.astype(F32)
    eye_bf = (rr == cc).astype(BF16)
    head_of_lane = _iota((LANES, GW), 1) >> 6
    src = _iota((LANES, GW), 0)
    q = q_ref[0]
    k = k_ref[0]
    v = v_ref[0]
    bg = bg_ref[0]
    def parts(m, n):
        out = []
        for _ in range(n):
            hi = m.astype(BF16)
            out.append(hi)
            m = m - hi.astype(F32)
        return out

    ones_bf = ones_bd.astype(BF16)
    eye_f = eye_t
    beta, gcum, decay, kb, egc = [], [], [], [], []
    for d in range(2):
        sel_b = (src == d * HEADS + head_of_lane).astype(BF16)
        sel_g = (src == 2 * HEADS + d * HEADS + head_of_lane).astype(BF16)
        cum_bd = jnp.where(jnp.logical_and(bd, incl[d]), 1.0, 0.0).astype(BF16)
        g_exp = _dot(jnp.concatenate(parts(bg, 2), axis=0), sel_g)
        beta.append(_dot(bg.astype(BF16), sel_b))
        cs = _dot(cum_bd, jnp.concatenate([g_exp[:TM].astype(BF16), g_exp[TM:].astype(BF16)], axis=1))
        gcum.append(cs[:, :GW] + cs[:, GW:])
        gparts = parts(gcum[d] * eye_f, 3)
        tr = _dot(ones_bf, jnp.concatenate(gparts, axis=1))
        grow = tr[:, :GW] + tr[:, GW:2 * GW] + tr[:, 2 * GW:]
        decay.append(jnp.where(incl[d], jnp.exp(jnp.where(incl[d], gcum[d] - grow, 0.0)), 0.0))
        kb.append(k * beta[d])
        egc.append(jnp.exp(gcum[d]))
        qi_out[0, d] = (q * egc[d]).astype(BF16)
    pairs = [(c, d) for c in range(CPT) for d in range(2)]
    rows = [slice(c * CH, (c + 1) * CH) for c in range(CPT)]
    eye = eye_t[:CH]
    a = {}
    for c in range(CPT):
        r = rows[c]
        lhs = jnp.concatenate([kb[0][r], kb[1][r], q[r]], axis=0).astype(BF16)
        aq = _dot_nt(lhs, _bd(k[r], bd))
        for d in range(2):
            dec = decay[d][r]
            a[c, d] = jnp.where(strict[d][:CH], aq[d * CH:(d + 1) * CH] * dec, 0.0)
            a_out[0, d, r, :] = jnp.where(incl[d][:CH], aq[2 * CH:] * dec, 0.0).astype(BF16)
    t_inv = {cd: eye - a[cd] for cd in pairs}
    p = {cd: _dot(a[cd].astype(BF16), _bd(a[cd], bd)) for cd in pairs}
    for it in range(5):
        for cd in pairs:
            pbd = _bd(p[cd], bd)
            if it < 4:
                res = _dot(jnp.concatenate([t_inv[cd], p[cd]], axis=0).astype(BF16), pbd)
                t_inv[cd] = t_inv[cd] + res[:CH]
                p[cd] = res[CH:]
            else:
                t_inv[cd] = t_inv[cd] + _dot(t_inv[cd].astype(BF16), pbd)

    def split(m):
        hi = m.astype(BF16)
        return hi, (m - hi.astype(F32)).astype(BF16)

    for cd in pairs:
        x0 = t_inv[cd]
        ah, al = split(a[cd])
        xh, xl = split(x0)
        hx = _dot(jnp.concatenate([ah, al], axis=0), _bd(xh, bd))
        resid = eye - x0 - (hx[:CH] + hx[CH:] + _dot(ah, _bd(xl, bd)))
        t_inv[cd] = x0 + _dot(xh, _bd(resid, bd))
    for c, d in pairs:
        r = rows[c]
        tb = t_inv[c, d].astype(BF16)
        u_out[0, d, r, :] = _dot(tb, _bd(v[r] * beta[d][r], bd))
        w_out[0, d, r, :] = _dot(tb, _bd(kb[d][r] * egc[d][r], bd)).astype(BF16)
        last = (c + 1) * CH - 1 if d == 0 else c * CH
        gtot = gcum[d][last:last + 1]
        k_end = (k[r] * jnp.exp(gtot - gcum[d][r])).astype(BF16)
        ket_out[0, d, c] = _dot_nt(eye_bf, k_end).astype(BF16)
        ge_out[0, d, c] = jnp.exp(gtot)


def _dn_local(q, k, v, bg):
    b, t, _ = q.shape
    nt = t // TM
    row = lambda bb, i: (bb, i, 0)
    drow = pl.BlockSpec((1, 2, TM, GW), lambda bb, i: (bb, 0, i, 0))
    return pl.pallas_call(
        _dn_local_kernel,
        grid=(b, nt),
        in_specs=[pl.BlockSpec((1, TM, GW), row)] * 3 + [pl.BlockSpec((1, TM, LANES), row)],
        out_specs=[drow, drow, drow, drow,
                   pl.BlockSpec((1, 2, CPT, GW, CH), lambda bb, i: (bb, 0, i, 0, 0)),
                   pl.BlockSpec((1, 2, CPT, 1, GW), lambda bb, i: (bb, 0, i, 0, 0))],
        out_shape=[jax.ShapeDtypeStruct((b, 2, t, GW), F32)] + [jax.ShapeDtypeStruct((b, 2, t, GW), BF16)] * 3
        + [jax.ShapeDtypeStruct((b, 2, t // CH, GW, CH), BF16), jax.ShapeDtypeStruct((b, 2, t // CH, 1, GW), F32)],
        compiler_params=_cp(("parallel", "arbitrary")),
    )(q, k, v, bg)


def _dn_scan_kernel(uf, wf, qf, af, kf, gf, ur, wr, qr, ar, kr, gr, of_ref, or_ref, s_scr):
    @pl.when(pl.program_id(1) == 0)
    def _():
        s_scr[...] = jnp.zeros(s_scr.shape, F32)

    bd = (_iota((GW, GW), 0) >> 6) == (_iota((GW, GW), 1) >> 6)
    dirs = ((uf, wf, qf, af, kf, gf, of_ref), (ur, wr, qr, ar, kr, gr, or_ref))
    for c in range(CPT):
        for d, (u, w, qi, a, ket, ge, o_ref) in enumerate(dirs):
            cc = c if d == 0 else CPT - 1 - c
            rows = slice(cc * CH, (cc + 1) * CH)
            s = s_scr[d]
            wq = _dot(jnp.concatenate([w[0, 0, rows, :], qi[0, 0, rows, :]], axis=0), s.astype(BF16))
            v_new = u[0, 0, rows, :] - wq[:CH]
            o_ref[0, rows, :] = wq[CH:] + _dot(a[0, 0, rows, :], _bd(v_new, bd))
            s_scr[d] = s * ge[0, 0, cc] + jnp.where(bd, _dot(ket[0, 0, cc], v_new.astype(BF16)), 0.0)


def _dir_specs(shape_tail, n_tiles, chunked):
    blk = (1, 1, CPT if chunked else TM) + shape_tail
    zeros = (0,) * len(shape_tail)
    fwd = pl.BlockSpec(blk, lambda b, s: (b, 0, _fwd_tile(s, n_tiles)) + zeros)
    rev = pl.BlockSpec(blk, lambda b, s: (b, 1, _rev_tile(s, n_tiles)) + zeros)
    return fwd, rev


def _dn_scan(u, w, qi, a, ket, ge):
    b, _, t, _ = u.shape
    nt = t // TM
    rowf, rowr = _dir_specs((GW,), nt, False)
    ketf, ketr = _dir_specs((GW, CH), nt, True)
    gef, ger = _dir_specs((1, GW), nt, True)
    return pl.pallas_call(
        _dn_scan_kernel,
        grid=(b, nt),
        in_specs=[rowf, rowf, rowf, rowf, ketf, gef, rowr, rowr, rowr, rowr, ketr, ger],
        out_specs=[pl.BlockSpec((1, TM, GW), lambda bb, s: (bb, _fwd_tile(s, nt), 0)),
                   pl.BlockSpec((1, TM, GW), lambda bb, s: (bb, _rev_tile(s, nt), 0))],
        out_shape=[jax.ShapeDtypeStruct((b, t, GW), F32)] * 2,
        scratch_shapes=[pltpu.VMEM((2, GW, GW), F32)],
        compiler_params=_cp(("parallel", "arbitrary")),
    )(u, w, qi, a, ket, ge, u, w, qi, a, ket, ge)


GLA_QK = HEADS * GLA_K


def _gla_local_kernel(pd_ref, pdl_ref, w2_ref, b2_ref, qi_out, a_out, ke_out, vt_out, vb_out, de_out):
    _, incl, _, cum, _ = _chunk_masks()
    eye_bf = (_iota((GW, GW), 0) == _iota((GW, GW), 1)).astype(BF16)
    bdk = (_iota((GW, GLA_QK), 0) >> 6) == (_iota((GW, GLA_QK), 1) >> 5)
    z = _dot(pdl_ref[0], w2_ref[...], HI) + b2_ref[...]
    gk_all = -_softplus(-z) * (1.0 / GLA_TAU)
    for c in range(CPT):
        rows = slice(c * CH, (c + 1) * CH)
        p = pd_ref[0, rows, :]
        q = p[:, :GLA_QK] * (GLA_K ** -0.5)
        k = p[:, GLA_QK:2 * GLA_QK]
        vb = p[:, 2 * GLA_QK:].astype(BF16)
        vb_out[0, rows, :] = vb
        vt_out[0, c] = _dot_nt(eye_bf, vb).astype(BF16)
        for d in range(2):
            bcs = _dot(cum[d], gk_all[rows, d * GLA_QK:(d + 1) * GLA_QK], HI)
            bend = bcs[CH - 1:CH] if d == 0 else bcs[0:1]
            q_in = (q * jnp.exp(bcs)).astype(BF16)
            kdec = jnp.where(bdk, _tile4(k * jnp.exp(-bcs)), 0.0).astype(BF16)
            a_out[0, d, rows, :] = jnp.where(incl[d], _dot_nt(q_in, kdec), 0.0).astype(BF16)
            qi_out[0, d, rows, :] = q_in
            ke_out[0, d, rows, :] = (k * jnp.exp(bend - bcs)).astype(BF16)
            de_out[0, d, c] = jnp.exp(bend)


def _gla_local(pd, pdl, w2bd, b2):
    b, t, w = pd.shape
    nt = t // TM
    row = lambda bb, i: (bb, i, 0)
    return pl.pallas_call(
        _gla_local_kernel,
        grid=(b, nt),
        in_specs=[pl.BlockSpec((1, TM, w), row), pl.BlockSpec((1, TM, LANES), row),
                  pl.BlockSpec((LANES, GW), lambda bb, i: (0, 0)), pl.BlockSpec((1, GW), lambda bb, i: (0, 0))],
        out_specs=[pl.BlockSpec((1, 2, TM, GLA_QK), lambda bb, i: (bb, 0, i, 0)),
                   pl.BlockSpec((1, 2, TM, GW), lambda bb, i: (bb, 0, i, 0)),
                   pl.BlockSpec((1, 2, TM, GLA_QK), lambda bb, i: (bb, 0, i, 0)),
                   pl.BlockSpec((1, CPT, GW, CH), lambda bb, i: (bb, i, 0, 0)),
                   pl.BlockSpec((1, TM, GW), row),
                   pl.BlockSpec((1, 2, CPT, 1, GLA_QK), lambda bb, i: (bb, 0, i, 0, 0))],
        out_shape=[jax.ShapeDtypeStruct((b, 2, t, GLA_QK), BF16), jax.ShapeDtypeStruct((b, 2, t, GW), BF16),
                   jax.ShapeDtypeStruct((b, 2, t, GLA_QK), BF16), jax.ShapeDtypeStruct((b, t // CH, GW, CH), BF16),
                   jax.ShapeDtypeStruct((b, t, GW), BF16), jax.ShapeDtypeStruct((b, 2, t // CH, 1, GLA_QK), F32)],
        compiler_params=_cp(("parallel", "arbitrary")),
    )(pd, pdl, w2bd, b2)


def _gla_scan_kernel(qf, af, kf, df, vtf, vf, qr, ar, kr, dr, vtr, vr, of_ref, or_ref, s_scr):
    @pl.when(pl.program_id(1) == 0)
    def _():
        s_scr[...] = jnp.zeros(s_scr.shape, F32)

    bd = (_iota((GW, GW), 0) >> 6) == (_iota((GW, GW), 1) >> 6)
    bdt = (_iota((GW, GLA_QK), 0) >> 6) == (_iota((GW, GLA_QK), 1) >> 5)
    dirs = ((qf, af, kf, df, vtf, vf, of_ref), (qr, ar, kr, dr, vtr, vr, or_ref))
    for c in range(CPT):
        for d, (qi, a, ke, de, vt, v, o_ref) in enumerate(dirs):
            cc = c if d == 0 else CPT - 1 - c
            rows = slice(cc * CH, (cc + 1) * CH)
            st = s_scr[d]
            vbd = jnp.where(bd, _tile4(v[0, rows, :]), jnp.zeros((), BF16))
            o_ref[0, rows, :] = _dot_nt(qi[0, 0, rows, :], st.astype(BF16)) + _dot(a[0, 0, rows, :], vbd)
            s_scr[d] = st * de[0, 0, cc] + jnp.where(bdt, _dot(vt[0, cc], ke[0, 0, rows, :]), 0.0)


def _gla_scan(qi, a, ke, de, vt, vb):
    b, _, t, _ = a.shape
    nt = t // TM
    qf, qr = _dir_specs((GLA_QK,), nt, False)
    af, ar = _dir_specs((GW,), nt, False)
    df, dr = _dir_specs((1, GLA_QK), nt, True)
    vtf = pl.BlockSpec((1, CPT, GW, CH), lambda bb, s: (bb, _fwd_tile(s, nt), 0, 0))
    vtr = pl.BlockSpec((1, CPT, GW, CH), lambda bb, s: (bb, _rev_tile(s, nt), 0, 0))
    vf = pl.BlockSpec((1, TM, GW), lambda bb, s: (bb, _fwd_tile(s, nt), 0))
    vr = pl.BlockSpec((1, TM, GW), lambda bb, s: (bb, _rev_tile(s, nt), 0))
    return pl.pallas_call(
        _gla_scan_kernel,
        grid=(b, nt),
        in_specs=[qf, af, qf, df, vtf, vf, qr, ar, qr, dr, vtr, vr],
        out_specs=[vf, vr],
        out_shape=[jax.ShapeDtypeStruct((b, t, GW), F32)] * 2,
        scratch_shapes=[pltpu.VMEM((2, GW, GLA_QK), F32)],
        compiler_params=_cp(("parallel", "arbitrary")),
    )(qi, a, ke, de, vt, vb, qi, a, ke, de, vt, vb)


def _outproj_kernel(x_ref, ya_ref, yb_ref, cf_ref, cr_ref, df_ref, dr_ref, pg_ref, g1b_ref, g1c_ref, gc_ref, gd_ref,
                    w_ref, o_ref, *, lat_rows):
    ones64 = _group_ones(GW, 6)

    def fin(o, g, gate):
        ms = _dot(o * o, ones64, HI) * (1.0 / HEAD_DIM)
        return (o * lax.rsqrt(ms + EPS) * g * _silu(gate)).astype(BF16)

    pg = pg_ref[0]
    yc = fin(cf_ref[0] + cr_ref[0], gc_ref[...], pg[:, :GW])
    yd = fin(df_ref[0] + dr_ref[0], gd_ref[...], pg[:, GW:])
    res = (_dot(ya_ref[0].astype(BF16), w_ref[0:GW, :]) + _dot(yb_ref[0].astype(BF16), w_ref[GW:2 * GW, :])
           + _dot(yc, w_ref[2 * GW:3 * GW, :]) + _dot(yd, w_ref[3 * GW:, :]))
    g1 = jnp.where(_ctx_rows(res.shape[0], lat_rows), g1c_ref[0], g1b_ref[0])
    o_ref[0] = x_ref[0] + g1 * res


def _outproj(xs, ya, yb, ocf, ocr, odf, odr, pg, mod3, dn_g, gla_g, w_out_bf, n_batch):
    b, t, d = xs.shape
    reps = GW // HEAD_DIM
    tb = t // N_BLK
    row = lambda bb, i: (bb, i, 0)
    g256 = pl.BlockSpec((1, tb, GW), row)
    in_specs = ([pl.BlockSpec((1, tb, d), row), g256, g256, g256, g256, g256, g256, pl.BlockSpec((1, tb, 2 * GW), row)]
                + _mod_specs(2, n_batch) + [_resident((1, GW)), _resident((1, GW)), _resident((d, d))])
    args = (xs, ya, yb, ocf, ocr, odf, odr, pg, mod3, mod3, jnp.tile(dn_g, reps).reshape(1, GW),
            jnp.tile(gla_g, reps).reshape(1, GW), w_out_bf)
    kern = functools.partial(_outproj_kernel, lat_rows=t - TM)
    return _row_call(kern, n_batch, t, args, in_specs, [d])[0]


FF_HALO = 8


def _ffn_kernel(xm_ref, xl_ref, xr_ref, shb_ref, shc_ref, scb_ref, scc_ref, g2b_ref, g2c_ref, ng_ref,
                wa_ref, wg_ref, cwa_ref, cwg_ref, wd_ref, o_ref, ext_ref, *, lat_rows):
    tb = xm_ref.shape[1]
    i = pl.program_id(1)
    rows_ext = _iota((tb + 2 * FF_HALO, 1), 0)
    keep = jnp.logical_and(jnp.logical_or(rows_ext >= FF_HALO, i > 0),
                           jnp.logical_or(rows_ext < FF_HALO + tb, i < pl.num_programs(1) - 1))
    grow = i * tb - FF_HALO + rows_ext
    ctx_ext = grow >= lat_rows
    x = jnp.concatenate([xl_ref[0], xm_ref[0], xr_ref[0]], axis=0)
    y = x * lax.rsqrt(jnp.mean(x * x, axis=-1, keepdims=True) + EPS) * ng_ref[...]
    h = y * (1.0 + jnp.where(ctx_ext, scc_ref[0], scb_ref[0])) + jnp.where(ctx_ext, shc_ref[0], shb_ref[0])
    h = jnp.where(keep, h, 0.0).astype(BF16)
    row = grow[FF_HALO:FF_HALO + tb]
    m_prev = jnp.broadcast_to(jnp.where(row == lat_rows, 0.0, 1.0), (tb, FF_BLK))
    m_next = jnp.broadcast_to(jnp.where(row == lat_rows - 1, 0.0, 1.0), (tb, FF_BLK))
    acc = jnp.zeros((tb, D_MODEL), F32)
    for j in range(N_FF_BLK):
        def conv(w_ref, cw_ref, half):
            ext_ref[half] = _dot(h, w_ref[j])
            cw = cw_ref[j]
            return (cw[0:1] * (m_prev * ext_ref[half, pl.ds(FF_HALO - 1, tb), :])
                    + cw[1:2] * ext_ref[half, pl.ds(FF_HALO, tb), :]
                    + cw[2:3] * (m_next * ext_ref[half, pl.ds(FF_HALO + 1, tb), :]))
        a = conv(wa_ref, cwa_ref, 0)
        g = conv(wg_ref, cwg_ref, 1)
        acc = acc + _dot((_silu(g) * a).astype(BF16), wd_ref[j])
    g2 = jnp.where(ctx_ext[FF_HALO:FF_HALO + tb], g2c_ref[0], g2b_ref[0])
    o_ref[0] = xm_ref[0] + g2 * acc


def _ffn(x1, mod3, norm_g, wa, wg, cwa, cwg, wd, n_batch):
    b, t, d = x1.shape
    tb = t // N_BLK
    per = tb // FF_HALO
    last = t // FF_HALO - 1
    in_specs = ([pl.BlockSpec((1, tb, d), lambda bb, i: (bb, i, 0)),
                 pl.BlockSpec((1, FF_HALO, d), lambda bb, i: (bb, jnp.maximum(i * per - 1, 0), 0)),
                 pl.BlockSpec((1, FF_HALO, d), lambda bb, i: (bb, jnp.minimum((i + 1) * per, last), 0))]
                + _mod_specs(3, n_batch) + _mod_specs(4, n_batch) + _mod_specs(5, n_batch)
                + [_resident((1, d)), _resident((N_FF_BLK, d, FF_BLK)), _resident((N_FF_BLK, d, FF_BLK)),
                   _resident((N_FF_BLK, 3, FF_BLK)), _resident((N_FF_BLK, 3, FF_BLK)), _resident((N_FF_BLK, FF_BLK, d))])
    args = (x1, x1, x1, mod3, mod3, mod3, mod3, mod3, mod3, norm_g.reshape(1, d), wa, wg, cwa, cwg, wd)
    kern = functools.partial(_ffn_kernel, lat_rows=t - TM)
    return _row_call(kern, n_batch, t, args, in_specs, [d],
                     scratch=[pltpu.VMEM((2, tb + 2 * FF_HALO, FF_BLK), F32)])[0]


def _rope_tables(seq, ctx_len):
    rows = seq // GRID_W
    row = jnp.repeat(jnp.arange(rows, dtype=F32), GRID_W)
    col = jnp.tile(jnp.arange(GRID_W, dtype=F32), rows)
    nf = QK_DIM // 4
    inv = ROPE_THETA ** (-jnp.arange(nf, dtype=F32) / nf)
    ang = jnp.concatenate([row[:, None] * inv, col[:, None] * inv], axis=-1)
    cos = jnp.concatenate([jnp.cos(ang), jnp.ones((ctx_len, QK_DIM // 2), F32)], axis=0)
    sin = jnp.concatenate([jnp.sin(ang), jnp.zeros((ctx_len, QK_DIM // 2), F32)], axis=0)
    reps = GW // QK_DIM
    return (jnp.tile(jnp.concatenate([cos, cos], axis=-1), (1, reps)),
            jnp.tile(jnp.concatenate([-sin, sin], axis=-1), (1, reps)))


def _regroup_w_in(w):
    d = w.shape[0]
    z = lambda n: jnp.zeros((d, n), w.dtype)
    return jnp.concatenate([w[:, :2048], w[:, 2048:2064], z(LANES - 16), w[:, 2320:2832], w[:, 2832:2864],
                            z(LANES - 32), w[:, 2064:2320], w[:, 2864:3120]], axis=1).astype(BF16)


def _gla_w2_blockdiag(w2):
    out = jnp.zeros((LANES, GW), F32)
    out = out.at[0:GLA_RANK, 0:GLA_QK].set(w2[0])
    return out.at[GLA_RANK:2 * GLA_RANK, GLA_QK:].set(w2[1])


def _layer(xs, mod3, lp, cos_t, sin_t, layer_idx, n_batch):
    b, t, d = xs.shape
    nt = t // TM
    pa, pb, pc, pcs, pd, pdl, pg = _inproj(xs, mod3, lp["norm1_g"], _regroup_w_in(lp["w_in"]), n_batch)

    ya = _conv_module(pa, lp["cm_conv_w"], lp["cm_conv_b"], lp["cm_ln_g"], lp["cm_ln_b"])

    lam_init = 0.8 - 0.6 * math.exp(-0.3 * layer_idx)
    q, kt, v, qn, kn = _attn_prep(pb, lp["da_qnorm_g"], lp["da_knorm_g"], cos_t, sin_t)
    yb = _attention(q, kt, v, qn, kn, lp["da_lambda"], lp["da_subln_g"], lam_init)

    dq, dk, dv, bg = _dn_prep(pc, pcs, lp["dn_conv_w"], lp["dn_a_log"], lp["dn_dt_bias"])
    ocf, ocr = _dn_scan(*_dn_local(dq, dk, dv, bg))

    qi, a, ke, vt, vb, de = _gla_local(pd, pdl, _gla_w2_blockdiag(lp["gla_w2"]), lp["gla_b2"].reshape(1, GW))
    odf, odr = _gla_scan(qi, a, ke, de, vt, vb)

    x1 = _outproj(xs, ya, yb, ocf, ocr, odf, odr, pg, mod3, lp["dn_onorm_g"], lp["gla_onorm_g"],
                  lp["w_out"].astype(BF16), n_batch)

    w_up = lp["ffn_w_up"].astype(BF16)
    blocks = lambda m: jnp.transpose(m.reshape(m.shape[0], N_FF_BLK, FF_BLK), (1, 0, 2))
    cw = lp["ffn_conv_w"]
    return _ffn(x1, mod3, lp["norm2_g"], blocks(w_up[:, :D_FF]), blocks(w_up[:, D_FF:]),
                blocks(cw[:, :D_FF]), blocks(cw[:, D_FF:]),
                lp["ffn_w_down"].astype(BF16).reshape(N_FF_BLK, FF_BLK, d), n_batch)


def kernel(x, c, ctx, c_ctx, w_mod, b_mod, norm1_g, norm2_g, w_in, w_out, cm_conv_w, cm_conv_b, cm_ln_g, cm_ln_b, da_qnorm_g, da_knorm_g, da_lambda, da_subln_g, dn_conv_w, dn_a_log, dn_dt_bias, dn_onorm_g, gla_w2, gla_b2, gla_onorm_g, ffn_w_up, ffn_conv_w, ffn_w_down):
    n_batch, seq, d = x.shape
    ctx_len = ctx.shape[1]
    assert ctx_len == TM and seq % TM == 0 and (seq + ctx_len) % (8 * N_BLK) == 0 and d == D_MODEL
    depth = w_mod.shape[0]
    cos_t, sin_t = _rope_tables(seq, ctx_len)
    xs = jnp.concatenate([x, ctx], axis=1)
    mod_rows = 16
    c_rows = jnp.zeros((mod_rows, d), F32).at[:n_batch].set(c).at[n_batch].set(c_ctx)
    params = dict(w_mod=w_mod, b_mod=b_mod, norm1_g=norm1_g, norm2_g=norm2_g, w_in=w_in, w_out=w_out,
                  cm_conv_w=cm_conv_w, cm_conv_b=cm_conv_b, cm_ln_g=cm_ln_g, cm_ln_b=cm_ln_b,
                  da_qnorm_g=da_qnorm_g, da_knorm_g=da_knorm_g, da_lambda=da_lambda, da_subln_g=da_subln_g,
                  dn_conv_w=dn_conv_w, dn_a_log=dn_a_log, dn_dt_bias=dn_dt_bias, dn_onorm_g=dn_onorm_g,
                  gla_w2=gla_w2, gla_b2=gla_b2, gla_onorm_g=gla_onorm_g,
                  ffn_w_up=ffn_w_up, ffn_conv_w=ffn_conv_w, ffn_w_down=ffn_w_down)
    for l in range(depth):
        lp = {k: v[l] for k, v in params.items()}
        mod3 = _modulation(c_rows, lp["w_mod"], lp["b_mod"]).reshape(mod_rows * 6, 1, d)
        xs = _layer(xs, mod3, lp, cos_t, sin_t, l, n_batch)
    return xs[:, :seq]
```

```python
import functools
import math

import jax
import jax.numpy as jnp
from jax import lax
from jax.experimental import pallas as pl
from jax.experimental.pallas import tpu as pltpu

F32 = jnp.float32
BF16 = jnp.bfloat16
HI = lax.Precision.HIGHEST
EPS = 1e-6

D_MODEL = 1024
GRID_W = 64
HEADS = 4
HEAD_DIM = 64
GW = 256
QK_DIM = 32
GLA_K = 32
GLA_RANK = 16
GLA_TAU = 16.0
CM_KERNEL = 31
DN_CONV = 5
ROPE_THETA = 10000.0
CH = 64
TM = 256
CPT = TM // CH
D_FF = 2816
FF_BLK = 256
N_FF_BLK = D_FF // FF_BLK
LANES = 128

IN_GROUPS = (("pa", 512), ("pb", 768), ("pc", 768), ("pcs", LANES), ("pd", 512), ("pdl", LANES), ("pg", 512))
IN_COLS_PAD = sum(w for _, w in IN_GROUPS)

VMEM_LIMIT = 56 * 1024 * 1024


def _cp(sem):
    return pltpu.CompilerParams(dimension_semantics=sem, vmem_limit_bytes=VMEM_LIMIT)


def _dot(a, b, prec=None):
    return jnp.dot(a, b, preferred_element_type=F32, precision=prec)


def _dot_nt(a, b, prec=None):
    return lax.dot_general(a, b, (((1,), (1,)), ((), ())), preferred_element_type=F32, precision=prec)


def _sigmoid(x):
    return 1.0 / (1.0 + jnp.exp(-x))


def _silu(x):
    return x * _sigmoid(x)


def _softplus(x):
    return jnp.maximum(x, 0.0) + jnp.log(1.0 + jnp.exp(-jnp.abs(x)))


def _iota(shape, dim):
    return lax.broadcasted_iota(jnp.int32, shape, dim)


def _group_ones(n, shift):
    return ((_iota((n, n), 0) >> shift) == (_iota((n, n), 1) >> shift)).astype(BF16)


def _bf16_parts(m, n):
    out = []
    for _ in range(n):
        hi = m.astype(BF16)
        out.append(hi)
        m = m - hi.astype(F32)
    return out


def _dot_sel(x, sel, n_parts=2):
    rows = x.shape[0]
    r = _dot(jnp.concatenate(_bf16_parts(x, n_parts), axis=0), sel)
    out = r[:rows]
    for k in range(1, n_parts):
        out = out + r[k * rows:(k + 1) * rows]
    return out


def _tile4(y):
    return jnp.concatenate([y, y, y, y], axis=0)


def _mod_kernel(c_ref, w_ref, b_ref, o_ref):
    o_ref[...] = _dot(_silu(c_ref[...]), w_ref[...], HI) + b_ref[...]


def _modulation(c_rows, w_mod, b_mod):
    r, d = c_rows.shape
    n = w_mod.shape[1] // d
    return pl.pallas_call(
        _mod_kernel,
        grid=(n,),
        in_specs=[pl.BlockSpec((r, d), lambda j: (0, 0)),
                  pl.BlockSpec((d, d), lambda j: (0, j)),
                  pl.BlockSpec((1, d), lambda j: (0, j))],
        out_specs=pl.BlockSpec((r, d), lambda j: (0, j)),
        out_shape=jax.ShapeDtypeStruct((r, n * d), F32),
        compiler_params=_cp(("arbitrary",)),
    )(c_rows, w_mod, b_mod.reshape(1, -1))


N_BLK = 4


def _mod_specs(k, n_batch):
    return [pl.BlockSpec((1, 1, D_MODEL), lambda b, i: (b * 6 + k, 0, 0)),
            pl.BlockSpec((1, 1, D_MODEL), lambda b, i: (n_batch * 6 + k, 0, 0))]


def _ctx_rows(tb, lat_rows):
    return pl.program_id(1) * tb + _iota((tb, 1), 0) >= lat_rows


def _resident(shape):
    zeros = (0,) * len(shape)
    return pl.BlockSpec(shape, lambda b, i: zeros, pipeline_mode=pl.Buffered(1))


def _row_call(kern, n_batch, rows, args, in_specs, out_widths, scratch=()):
    tb = rows // N_BLK
    return pl.pallas_call(
        kern,
        grid=(n_batch, N_BLK),
        in_specs=in_specs,
        out_specs=[pl.BlockSpec((1, tb, w), lambda bb, i: (bb, i, 0)) for w in out_widths],
        out_shape=[jax.ShapeDtypeStruct((n_batch, rows, w), F32) for w in out_widths],
        scratch_shapes=list(scratch),
        compiler_params=_cp(("parallel", "arbitrary")),
    )(*args)


def _inproj_kernel(x_ref, shb_ref, shc_ref, scb_ref, scc_ref, g_ref, w_ref, *outs, lat_rows):
    x = x_ref[0]
    ctx = _ctx_rows(x.shape[0], lat_rows)
    y = x * lax.rsqrt(jnp.mean(x * x, axis=-1, keepdims=True) + EPS) * g_ref[...]
    h = y * (1.0 + jnp.where(ctx, scc_ref[0], scb_ref[0])) + jnp.where(ctx, shc_ref[0], shb_ref[0])
    p = _dot(h.astype(BF16), w_ref[...])
    off = 0
    for o_ref, (_, width) in zip(outs, IN_GROUPS):
        o_ref[0] = p[:, off:off + width]
        off += width


def _inproj(xs, mod3, norm_g, w_in_r, n_batch):
    b, t, d = xs.shape
    in_specs = ([pl.BlockSpec((1, t // N_BLK, d), lambda bb, i: (bb, i, 0))] + _mod_specs(0, n_batch)
                + _mod_specs(1, n_batch) + [_resident((1, d)), _resident((d, IN_COLS_PAD))])
    kern = functools.partial(_inproj_kernel, lat_rows=t - TM)
    return _row_call(kern, n_batch, t, (xs, mod3, mod3, mod3, mod3, norm_g.reshape(1, d), w_in_r), in_specs,
                     [w for _, w in IN_GROUPS])


def _halo_specs(width, halo, n_tiles):
    per = TM // halo
    left = pl.BlockSpec((1, halo, width), lambda b, i: (b, jnp.maximum(i * per - 1, 0), 0))
    right = pl.BlockSpec((1, halo, width), lambda b, i: (b, jnp.minimum((i + 1) * per, n_tiles * per - 1), 0))
    return left, right


def _halo_ok(i, n_tiles):
    return jnp.logical_and(i >= 1, i < n_tiles - 1), i < n_tiles - 2


CM_HALO = 16


def _convmod_kernel(pm_ref, pl_ref, pr_ref, cw_ref, cb_ref, lg_ref, lb_ref, o_ref, ext_ref):
    left_ok, right_ok = _halo_ok(pl.program_id(1), pl.num_programs(1))

    def glu(p):
        return p[:, :GW] * _sigmoid(p[:, GW:])

    ext_ref[0:CM_HALO] = jnp.where(left_ok, glu(pl_ref[0]), 0.0)
    ext_ref[CM_HALO:CM_HALO + TM] = glu(pm_ref[0])
    ext_ref[CM_HALO + TM:] = jnp.where(right_ok, glu(pr_ref[0]), 0.0)
    pad = CM_KERNEL // 2
    acc = jnp.zeros((TM, GW), F32)
    for j in range(CM_KERNEL):
        acc = acc + cw_ref[j:j + 1, :] * ext_ref[pl.ds(CM_HALO - pad + j, TM), :]
    y = acc + cb_ref[...]
    mu = jnp.mean(y, axis=-1, keepdims=True)
    yc = y - mu
    var = jnp.mean(yc * yc, axis=-1, keepdims=True)
    o_ref[0] = _silu(yc * lax.rsqrt(var + EPS) * lg_ref[...] + lb_ref[...])


def _conv_module(pa, conv_w, conv_b, ln_g, ln_b):
    b, t, w = pa.shape
    nt = t // TM
    left, right = _halo_specs(w, CM_HALO, nt)
    vec = pl.BlockSpec((1, GW), lambda bb, i: (0, 0))
    return pl.pallas_call(
        _convmod_kernel,
        grid=(b, nt),
        in_specs=[pl.BlockSpec((1, TM, w), lambda bb, i: (bb, i, 0)), left, right,
                  pl.BlockSpec((CM_KERNEL, GW), lambda bb, i: (0, 0)), vec, vec, vec],
        out_specs=pl.BlockSpec((1, TM, GW), lambda bb, i: (bb, i, 0)),
        out_shape=jax.ShapeDtypeStruct((b, t, GW), F32),
        scratch_shapes=[pltpu.VMEM((TM + 2 * CM_HALO, GW), F32)],
        compiler_params=_cp(("parallel", "arbitrary")),
    )(pa, pa, pa, conv_w, conv_b.reshape(1, GW), ln_g.reshape(1, GW), ln_b.reshape(1, GW))


N_MAPS = 2 * HEADS
QK_PAD = 2 * QK_DIM
V_EXT = 2 * HEAD_DIM
MASK_BIG = 8192.0
SHIFT_MAX = 40.0


def _attn_prep_kernel(pb_ref, qg_ref, kg_ref, cos_ref, sin_ref, q_out, kt_out, v_out, qn_out, kn_out):
    p = pb_ref[0]
    ones32 = _group_ones(GW, 5)
    first = (_iota((TM, GW), 1) & (QK_DIM - 1)) < QK_DIM // 2
    cos = cos_ref[...]
    sin = sin_ref[...]

    def norm_rope(t, g):
        ms = _dot_sel(t * t, ones32) * (1.0 / QK_DIM)
        tn = t * lax.rsqrt(ms + EPS) * g
        partner = jnp.where(first, pltpu.roll(tn, GW - QK_DIM // 2, 1), pltpu.roll(tn, QK_DIM // 2, 1))
        return tn * cos + partner * sin

    qf = norm_rope(p[:, :GW], qg_ref[...]) * (QK_DIM ** -0.5)
    kf = norm_rope(p[:, GW:2 * GW], kg_ref[...])
    map_sel = ((_iota((GW, N_MAPS), 0) >> 5) == _iota((GW, N_MAPS), 1)).astype(BF16)
    qn_out[0] = _dot_sel(qf * qf, map_sel)
    kn_out[0] = _dot_sel(kf * kf, map_sel)
    q = qf.astype(BF16)
    kt = kf.T.astype(BF16)
    is_ctx = pl.program_id(1) == pl.num_programs(1) - 1
    k_row = _iota((QK_DIM, TM), 0)
    k_tail = jnp.where(k_row == 0, 1.0, jnp.where(jnp.logical_and(k_row == 1, jnp.logical_not(is_ctx)), 1.0, 0.0))
    k_tail = k_tail.astype(BF16)
    q_tail = jnp.where(jnp.logical_and(_iota((TM, QK_DIM), 1) == 1, is_ctx), -MASK_BIG, 0.0).astype(BF16)
    for g in range(N_MAPS):
        q_out[0, g, :, 0:QK_DIM] = q[:, g * QK_DIM:(g + 1) * QK_DIM]
        q_out[0, g, :, QK_DIM:] = q_tail
        kt_out[0, 0, g, 0:QK_DIM, :] = kt[g * QK_DIM:(g + 1) * QK_DIM, :]
        kt_out[0, 0, g, QK_DIM:, :] = k_tail
    v = p[:, 2 * GW:].astype(BF16)
    ones = jnp.ones((TM, HEAD_DIM), BF16)
    for h in range(HEADS):
        v_out[0, 0, h, :, 0:HEAD_DIM] = v[:, h * HEAD_DIM:(h + 1) * HEAD_DIM]
        v_out[0, 0, h, :, HEAD_DIM:] = ones


def _attn_prep(pb, qn_g, kn_g, cos_t, sin_t):
    b, t, w = pb.shape
    nt = t // TM
    vec = pl.BlockSpec((1, GW), lambda bb, i: (0, 0))
    tab = pl.BlockSpec((TM, GW), lambda bb, i: (i, 0))
    reps = GW // QK_DIM
    n_spec = pl.BlockSpec((1, TM, N_MAPS), lambda bb, i: (bb, i, 0))
    n_shape = jax.ShapeDtypeStruct((b, t, N_MAPS), F32)
    return pl.pallas_call(
        _attn_prep_kernel,
        grid=(b, nt),
        in_specs=[pl.BlockSpec((1, TM, w), lambda bb, i: (bb, i, 0)), vec, vec, tab, tab],
        out_specs=[pl.BlockSpec((1, N_MAPS, TM, QK_PAD), lambda bb, i: (bb, 0, i, 0)),
                   pl.BlockSpec((1, 1, N_MAPS, QK_PAD, TM), lambda bb, i: (bb, i, 0, 0, 0)),
                   pl.BlockSpec((1, 1, HEADS, TM, V_EXT), lambda bb, i: (bb, i, 0, 0, 0)), n_spec, n_spec],
        out_shape=[jax.ShapeDtypeStruct((b, N_MAPS, t, QK_PAD), BF16),
                   jax.ShapeDtypeStruct((b, nt, N_MAPS, QK_PAD, TM), BF16),
                   jax.ShapeDtypeStruct((b, nt, HEADS, TM, V_EXT), BF16), n_shape, n_shape],
        compiler_params=_cp(("parallel", "arbitrary")),
    )(pb, jnp.tile(qn_g, reps).reshape(1, GW), jnp.tile(kn_g, reps).reshape(1, GW), cos_t, sin_t)


def _attn_kernel(q_ref, kt_ref, v_ref, qn_ref, kn_ref, lam_ref, sg_ref, o_ref, qa_scr, acc_scr, s_scr, m_scr,
                 *, n_chunks, lam_init):
    tq = o_ref.shape[1]
    acc_scr[...] = jnp.zeros(acc_scr.shape, F32)
    k2 = jnp.max(kn_ref[0], axis=0, keepdims=True)
    bound = jnp.sqrt(qn_ref[0] * k2)
    safe = jnp.max(bound) <= SHIFT_MAX
    shift = jnp.where(safe, bound, 0.0)
    shift_lane = _iota((tq, QK_PAD), 1) == QK_DIM
    for g in range(N_MAPS):
        qa_scr[g] = jnp.where(shift_lane, (-shift[:, g:g + 1]).astype(BF16), q_ref[0, g])

    @pl.when(safe)
    def _():
        s_scr[0] = _dot(qa_scr[0], kt_ref[0, 0, 0])

        def body(c, carry):
            nxt = jnp.minimum(c + 1, n_chunks - 1)
            for g in range(N_MAPS):
                if g + 1 < N_MAPS:
                    s_next = _dot(qa_scr[g + 1], kt_ref[0, c, g + 1])
                else:
                    s_next = _dot(qa_scr[0], kt_ref[0, nxt, 0])
                p = jnp.exp(s_scr[g & 1]).astype(BF16)
                acc_scr[g] += _dot(p, v_ref[0, c, g // 2])
                s_scr[(g + 1) & 1] = s_next
            return carry

        lax.fori_loop(0, n_chunks, body, 0)

    @pl.when(jnp.logical_not(safe))
    def _():
        m_scr[...] = jnp.full(m_scr.shape, -jnp.inf, F32)

        def body(c, carry):
            for g in range(N_MAPS):
                s = _dot(qa_scr[g], kt_ref[0, c, g])
                m_old = m_scr[:, g:g + 1]
                m_new = jnp.maximum(m_old, jnp.max(s, axis=-1, keepdims=True))
                p = jnp.exp(s - m_new).astype(BF16)
                acc_scr[g] = jnp.exp(m_old - m_new) * acc_scr[g] + _dot(p, v_ref[0, c, g // 2])
                m_scr[:, g:g + 1] = m_new
            return carry

        lax.fori_loop(0, n_chunks, body, 0)

    lp = lam_ref[...]
    lam = (jnp.exp(jnp.sum(lp[0:1] * lp[1:2], axis=-1, keepdims=True))
           - jnp.exp(jnp.sum(lp[2:3] * lp[3:4], axis=-1, keepdims=True)) + lam_init)
    for h in range(HEADS):
        a0 = acc_scr[2 * h]
        a1 = acc_scr[2 * h + 1]
        o = (a0[:, :HEAD_DIM] / a0[:, HEAD_DIM:HEAD_DIM + 1]
             - lam * (a1[:, :HEAD_DIM] / a1[:, HEAD_DIM:HEAD_DIM + 1]))
        y = o * lax.rsqrt(jnp.mean(o * o, axis=-1, keepdims=True) + EPS) * sg_ref[...] * (1.0 - lam_init)
        o_ref[0, :, h * HEAD_DIM:(h + 1) * HEAD_DIM] = y


def _attention(q, kt, v, qn, kn, lam_p, subln_g, lam_init):
    b, _, t, _ = q.shape
    tq = t // N_BLK
    nt = t // TM
    kern = functools.partial(_attn_kernel, n_chunks=nt, lam_init=lam_init)
    return pl.pallas_call(
        kern,
        grid=(b, N_BLK),
        in_specs=[pl.BlockSpec((1, N_MAPS, tq, QK_PAD), lambda bb, i: (bb, 0, i, 0)),
                  pl.BlockSpec((1, nt, N_MAPS, QK_PAD, TM), lambda bb, i: (bb, 0, 0, 0, 0)),
                  pl.BlockSpec((1, nt, HEADS, TM, V_EXT), lambda bb, i: (bb, 0, 0, 0, 0)),
                  pl.BlockSpec((1, tq, N_MAPS), lambda bb, i: (bb, i, 0)),
                  pl.BlockSpec((1, t, N_MAPS), lambda bb, i: (bb, 0, 0)),
                  _resident((4, QK_DIM)), _resident((1, HEAD_DIM))],
        out_specs=pl.BlockSpec((1, tq, GW), lambda bb, i: (bb, i, 0)),
        out_shape=jax.ShapeDtypeStruct((b, t, GW), F32),
        scratch_shapes=[pltpu.VMEM((N_MAPS, tq, QK_PAD), BF16), pltpu.VMEM((N_MAPS, tq, V_EXT), F32),
                        pltpu.VMEM((2, tq, TM), F32), pltpu.VMEM((tq, N_MAPS), F32)],
        compiler_params=_cp(("parallel", "arbitrary")),
    )(q, kt, v, qn, kn, lam_p, subln_g.reshape(1, HEAD_DIM))


def _chunk_masks():
    i = _iota((CH, GW), 0)
    j = _iota((CH, GW), 1) & (CH - 1)
    eye = (i == j).astype(F32)
    incl = (j <= i, j >= i)
    strict = (j < i, j > i)
    r = _iota((CH, CH), 0)
    c = _iota((CH, CH), 1)
    cum = ((c <= r).astype(F32), (c >= r).astype(F32))
    bd = (_iota((GW, GW), 0) >> 6) == (_iota((GW, GW), 1) >> 6)
    return eye, incl, strict, cum, bd


def _bd(y, bd):
    return jnp.where(bd, _tile4(y), 0.0).astype(BF16)


def _fwd_tile(s, n_tiles):
    return jnp.where(s == 0, n_tiles - 1, s - 1)


def _rev_tile(s, n_tiles):
    return jnp.where(s == 0, n_tiles - 1, n_tiles - 1 - s)


DN_HALO = 8


def _dn_prep_kernel(pm_ref, pl_ref, pr_ref, pcs_ref, cw_ref, alog_ref, dtb_ref, q_out, k_out, v_out, bg_out, ext_ref):
    left_ok, right_ok = _halo_ok(pl.program_id(1), pl.num_programs(1))
    ext_ref[0:DN_HALO] = jnp.where(left_ok, pl_ref[0], 0.0)
    ext_ref[DN_HALO:DN_HALO + TM] = pm_ref[0]
    ext_ref[DN_HALO + TM:] = jnp.where(right_ok, pr_ref[0], 0.0)
    pad = DN_CONV // 2
    acc = jnp.zeros((TM, 3 * GW), F32)
    for j in range(DN_CONV):
        acc = acc + cw_ref[j:j + 1, :] * ext_ref[pl.ds(DN_HALO - pad + j, TM), :]
    qkv = _silu(acc)
    ones64 = _group_ones(GW, 6)

    def l2n(t):
        return t * lax.rsqrt(_dot_sel(t * t, ones64) + EPS)

    q_out[0] = l2n(qkv[:, :GW]) * (HEAD_DIM ** -0.5)
    k_out[0] = l2n(qkv[:, GW:2 * GW])
    v_out[0] = qkv[:, 2 * GW:]
    s = pcs_ref[0]
    col = _iota(s.shape, 1)
    gate = -jnp.exp(alog_ref[...]) * _softplus(s + dtb_ref[...])
    bg_out[0] = jnp.where(col < 2 * HEADS, _sigmoid(s), jnp.where(col < 4 * HEADS, gate, 0.0))


def _dn_prep(pc, pcs, conv_w, a_log, dt_bias):
    b, t, w = pc.shape
    nt = t // TM
    left, right = _halo_specs(w, DN_HALO, nt)
    row = lambda bb, i: (bb, i, 0)
    pad_vec = lambda a: jnp.zeros((1, LANES), F32).at[0, 2 * HEADS:4 * HEADS].set(a.reshape(-1))
    vec = pl.BlockSpec((1, LANES), lambda bb, i: (0, 0))
    return pl.pallas_call(
        _dn_prep_kernel,
        grid=(b, nt),
        in_specs=[pl.BlockSpec((1, TM, w), row), left, right, pl.BlockSpec((1, TM, LANES), row),
                  pl.BlockSpec((DN_CONV, w), lambda bb, i: (0, 0)), vec, vec],
        out_specs=[pl.BlockSpec((1, TM, GW), row)] * 3 + [pl.BlockSpec((1, TM, LANES), row)],
        out_shape=[jax.ShapeDtypeStruct((b, t, GW), F32)] * 3 + [jax.ShapeDtypeStruct((b, t, LANES), F32)],
        scratch_shapes=[pltpu.VMEM((TM + 2 * DN_HALO, w), F32)],
        compiler_params=_cp(("parallel", "arbitrary")),
    )(pc, pc, pc, pcs, conv_w, pad_vec(a_log), pad_vec(dt_bias))


def _dn_local_kernel(q_ref, k_ref, v_ref, bg_ref, u_out, w_out, qi_out, a_out, ket_out, ge_out):
    rr = _iota((TM, GW), 0)
    cc = _iota((TM, GW), 1)
    i_in = rr & (CH - 1)
    j_in = cc & (CH - 1)
    bd = (rr >> 6) == (cc >> 6)
    eye_t = (i_in == j_in).astype(F32)
    incl = (j_in <= i_in, j_in >= i_in)
    strict = (j_in < i_in, j_in > i_in)
    ones_bd = bd.astype(F32)
    eye_bf = (rr == cc).astype(BF16)
    head_of_lane = _iota((LANES, GW), 1) >> 6
    src = _iota((LANES, GW), 0)
    q = q_ref[0]
    k = k_ref[0]
    v = v_ref[0]
    bg = bg_ref[0]
    def parts(m, n):
        out = []
        for _ in range(n):
            hi = m.astype(BF16)
            out.append(hi)
            m = m - hi.astype(F32)
        return out

    ones_bf = ones_bd.astype(BF16)
    eye_f = eye_t
    beta, gcum, decay, kb, egc = [], [], [], [], []
    for d in range(2):
        sel_b = (src == d * HEADS + head_of_lane).astype(BF16)
        sel_g = (src == 2 * HEADS + d * HEADS + head_of_lane).astype(BF16)
        cum_bd = jnp.where(jnp.logical_and(bd, incl[d]), 1.0, 0.0).astype(BF16)
        g_exp = _dot(jnp.concatenate(parts(bg, 2), axis=0), sel_g)
        beta.append(_dot(bg.astype(BF16), sel_b))
        cs = _dot(cum_bd, jnp.concatenate([g_exp[:TM].astype(BF16), g_exp[TM:].astype(BF16)], axis=1))
        gcum.append(cs[:, :GW] + cs[:, GW:])
        gparts = parts(gcum[d] * eye_f, 3)
        tr = _dot(ones_bf, jnp.concatenate(gparts, axis=1))
        grow = tr[:, :GW] + tr[:, GW:2 * GW] + tr[:, 2 * GW:]
        decay.append(jnp.where(incl[d], jnp.exp(jnp.where(incl[d], gcum[d] - grow, 0.0)), 0.0))
        kb.append(k * beta[d])
        egc.append(jnp.exp(gcum[d]))
        qi_out[0, d] = (q * egc[d]).astype(BF16)
    pairs = [(c, d) for c in range(CPT) for d in range(2)]
    rows = [slice(c * CH, (c + 1) * CH) for c in range(CPT)]
    eye = eye_t[:CH]
    a = {}
    for c in range(CPT):
        r = rows[c]
        lhs = jnp.concatenate([kb[0][r], kb[1][r], q[r]], axis=0).astype(BF16)
        aq = _dot_nt(lhs, _bd(k[r], bd))
        for d in range(2):
            dec = decay[d][r]
            a[c, d] = jnp.where(strict[d][:CH], aq[d * CH:(d + 1) * CH] * dec, 0.0)
            a_out[0, d, r, :] = jnp.where(incl[d][:CH], aq[2 * CH:] * dec, 0.0).astype(BF16)
    t_inv = {cd: eye - a[cd] for cd in pairs}
    p = {cd: _dot(a[cd].astype(BF16), _bd(a[cd], bd)) for cd in pairs}
    for it in range(5):
        for cd in pairs:
            pbd = _bd(p[cd], bd)
            if it < 4:
                res = _dot(jnp.concatenate([t_inv[cd], p[cd]], axis=0).astype(BF16), pbd)
                t_inv[cd] = t_inv[cd] + res[:CH]
                p[cd] = res[CH:]
            else:
                t_inv[cd] = t_inv[cd] + _dot(t_inv[cd].astype(BF16), pbd)

    def split(m):
        hi = m.astype(BF16)
        return hi, (m - hi.astype(F32)).astype(BF16)

    for cd in pairs:
        x0 = t_inv[cd]
        ah, al = split(a[cd])
        xh, xl = split(x0)
        hx = _dot(jnp.concatenate([ah, al], axis=0), _bd(xh, bd))
        resid = eye - x0 - (hx[:CH] + hx[CH:] + _dot(ah, _bd(xl, bd)))
        t_inv[cd] = x0 + _dot(xh, _bd(resid, bd))
    for c, d in pairs:
        r = rows[c]
        tb = t_inv[c, d].astype(BF16)
        u_out[0, d, r, :] = _dot(tb, _bd(v[r] * beta[d][r], bd))
        w_out[0, d, r, :] = _dot(tb, _bd(kb[d][r] * egc[d][r], bd)).astype(BF16)
        last = (c + 1) * CH - 1 if d == 0 else c * CH
        gtot = gcum[d][last:last + 1]
        k_end = (k[r] * jnp.exp(gtot - gcum[d][r])).astype(BF16)
        ket_out[0, d, c] = _dot_nt(eye_bf, k_end).astype(BF16)
        ge_out[0, d, c] = jnp.exp(gtot)


def _dn_local(q, k, v, bg):
    b, t, _ = q.shape
    nt = t // TM
    row = lambda bb, i: (bb, i, 0)
    drow = pl.BlockSpec((1, 2, TM, GW), lambda bb, i: (bb, 0, i, 0))
    return pl.pallas_call(
        _dn_local_kernel,
        grid=(b, nt),
        in_specs=[pl.BlockSpec((1, TM, GW), row)] * 3 + [pl.BlockSpec((1, TM, LANES), row)],
        out_specs=[drow, drow, drow, drow,
                   pl.BlockSpec((1, 2, CPT, GW, CH), lambda bb, i: (bb, 0, i, 0, 0)),
                   pl.BlockSpec((1, 2, CPT, 1, GW), lambda bb, i: (bb, 0, i, 0, 0))],
        out_shape=[jax.ShapeDtypeStruct((b, 2, t, GW), F32)] + [jax.ShapeDtypeStruct((b, 2, t, GW), BF16)] * 3
        + [jax.ShapeDtypeStruct((b, 2, t // CH, GW, CH), BF16), jax.ShapeDtypeStruct((b, 2, t // CH, 1, GW), F32)],
        compiler_params=_cp(("parallel", "arbitrary")),
    )(q, k, v, bg)


def _dn_scan_kernel(uf, wf, qf, af, kf, gf, ur, wr, qr, ar, kr, gr, of_ref, or_ref, s_scr):
    @pl.when(pl.program_id(1) == 0)
    def _():
        s_scr[...] = jnp.zeros(s_scr.shape, F32)

    bd = (_iota((GW, GW), 0) >> 6) == (_iota((GW, GW), 1) >> 6)
    dirs = ((uf, wf, qf, af, kf, gf, of_ref), (ur, wr, qr, ar, kr, gr, or_ref))
    for c in range(CPT):
        for d, (u, w, qi, a, ket, ge, o_ref) in enumerate(dirs):
            cc = c if d == 0 else CPT - 1 - c
            rows = slice(cc * CH, (cc + 1) * CH)
            s = s_scr[d]
            wq = _dot(jnp.concatenate([w[0, 0, rows, :], qi[0, 0, rows, :]], axis=0), s.astype(BF16))
            v_new = u[0, 0, rows, :] - wq[:CH]
            o_ref[0, rows, :] = wq[CH:] + _dot(a[0, 0, rows, :], _bd(v_new, bd))
            s_scr[d] = s * ge[0, 0, cc] + jnp.where(bd, _dot(ket[0, 0, cc], v_new.astype(BF16)), 0.0)


def _dir_specs(shape_tail, n_tiles, chunked):
    blk = (1, 1, CPT if chunked else TM) + shape_tail
    zeros = (0,) * len(shape_tail)
    fwd = pl.BlockSpec(blk, lambda b, s: (b, 0, _fwd_tile(s, n_tiles)) + zeros)
    rev = pl.BlockSpec(blk, lambda b, s: (b, 1, _rev_tile(s, n_tiles)) + zeros)
    return fwd, rev


def _dn_scan(u, w, qi, a, ket, ge):
    b, _, t, _ = u.shape
    nt = t // TM
    rowf, rowr = _dir_specs((GW,), nt, False)
    ketf, ketr = _dir_specs((GW, CH), nt, True)
    gef, ger = _dir_specs((1, GW), nt, True)
    return pl.pallas_call(
        _dn_scan_kernel,
        grid=(b, nt),
        in_specs=[rowf, rowf, rowf, rowf, ketf, gef, rowr, rowr, rowr, rowr, ketr, ger],
        out_specs=[pl.BlockSpec((1, TM, GW), lambda bb, s: (bb, _fwd_tile(s, nt), 0)),
                   pl.BlockSpec((1, TM, GW), lambda bb, s: (bb, _rev_tile(s, nt), 0))],
        out_shape=[jax.ShapeDtypeStruct((b, t, GW), F32)] * 2,
        scratch_shapes=[pltpu.VMEM((2, GW, GW), F32)],
        compiler_params=_cp(("parallel", "arbitrary")),
    )(u, w, qi, a, ket, ge, u, w, qi, a, ket, ge)


GLA_QK = HEADS * GLA_K


def _gla_local_kernel(pd_ref, pdl_ref, w2_ref, b2_ref, qi_out, a_out, ke_out, vt_out, vb_out, de_out):
    _, incl, _, _, _ = _chunk_masks()
    rr = _iota((TM, TM), 0)
    cc = _iota((TM, TM), 1)
    same_chunk = (rr >> 6) == (cc >> 6)
    eye_bf = (rr == cc).astype(BF16)
    bdk = (_iota((GW, GLA_QK), 0) >> 6) == (_iota((GW, GLA_QK), 1) >> 5)
    z = _dot(pdl_ref[0], w2_ref[...], HI) + b2_ref[...]
    gk_all = -_softplus(-z) * (1.0 / GLA_TAU)
    bcs_all = []
    for d in range(2):
        cum_bd = jnp.where(jnp.logical_and(same_chunk, cc <= rr if d == 0 else cc >= rr), 1.0, 0.0).astype(BF16)
        gparts = _bf16_parts(gk_all[:, d * GLA_QK:(d + 1) * GLA_QK], 2)
        cs = _dot(cum_bd, jnp.concatenate(gparts, axis=1))
        bcs_all.append(cs[:, :GLA_QK] + cs[:, GLA_QK:])
    for c in range(CPT):
        rows = slice(c * CH, (c + 1) * CH)
        p = pd_ref[0, rows, :]
        q = p[:, :GLA_QK] * (GLA_K ** -0.5)
        k = p[:, GLA_QK:2 * GLA_QK]
        vb = p[:, 2 * GLA_QK:].astype(BF16)
        vb_out[0, rows, :] = vb
        vt_out[0, c] = _dot_nt(eye_bf, vb).astype(BF16)
        for d in range(2):
            bcs = bcs_all[d][rows]
            bend = bcs[CH - 1:CH] if d == 0 else bcs[0:1]
            q_in = (q * jnp.exp(bcs)).astype(BF16)
            kdec = jnp.where(bdk, _tile4(k * jnp.exp(-bcs)), 0.0).astype(BF16)
            a_out[0, d, rows, :] = jnp.where(incl[d], _dot_nt(q_in, kdec), 0.0).astype(BF16)
            qi_out[0, d, rows, :] = q_in
            ke_out[0, d, rows, :] = (k * jnp.exp(bend - bcs)).astype(BF16)
            de_out[0, d, c] = jnp.exp(bend)


def _gla_local(pd, pdl, w2bd, b2):
    b, t, w = pd.shape
    nt = t // TM
    row = lambda bb, i: (bb, i, 0)
    return pl.pallas_call(
        _gla_local_kernel,
        grid=(b, nt),
        in_specs=[pl.BlockSpec((1, TM, w), row), pl.BlockSpec((1, TM, LANES), row),
                  pl.BlockSpec((LANES, GW), lambda bb, i: (0, 0)), pl.BlockSpec((1, GW), lambda bb, i: (0, 0))],
        out_specs=[pl.BlockSpec((1, 2, TM, GLA_QK), lambda bb, i: (bb, 0, i, 0)),
                   pl.BlockSpec((1, 2, TM, GW), lambda bb, i: (bb, 0, i, 0)),
                   pl.BlockSpec((1, 2, TM, GLA_QK), lambda bb, i: (bb, 0, i, 0)),
                   pl.BlockSpec((1, CPT, GW, CH), lambda bb, i: (bb, i, 0, 0)),
                   pl.BlockSpec((1, TM, GW), row),
                   pl.BlockSpec((1, 2, CPT, 1, GLA_QK), lambda bb, i: (bb, 0, i, 0, 0))],
        out_shape=[jax.ShapeDtypeStruct((b, 2, t, GLA_QK), BF16), jax.ShapeDtypeStruct((b, 2, t, GW), BF16),
                   jax.ShapeDtypeStruct((b, 2, t, GLA_QK), BF16), jax.ShapeDtypeStruct((b, t // CH, GW, CH), BF16),
                   jax.ShapeDtypeStruct((b, t, GW), BF16), jax.ShapeDtypeStruct((b, 2, t // CH, 1, GLA_QK), F32)],
        compiler_params=_cp(("parallel", "arbitrary")),
    )(pd, pdl, w2bd, b2)


def _gla_scan_kernel(qf, af, kf, df, vtf, vf, qr, ar, kr, dr, vtr, vr, of_ref, or_ref, s_scr):
    @pl.when(pl.program_id(1) == 0)
    def _():
        s_scr[...] = jnp.zeros(s_scr.shape, F32)

    bd = (_iota((GW, GW), 0) >> 6) == (_iota((GW, GW), 1) >> 6)
    bdt = (_iota((GW, GLA_QK), 0) >> 6) == (_iota((GW, GLA_QK), 1) >> 5)
    dirs = ((qf, af, kf, df, vtf, vf, of_ref), (qr, ar, kr, dr, vtr, vr, or_ref))
    for c in range(CPT):
        for d, (qi, a, ke, de, vt, v, o_ref) in enumerate(dirs):
            cc = c if d == 0 else CPT - 1 - c
            rows = slice(cc * CH, (cc + 1) * CH)
            st = s_scr[d]
            vbd = jnp.where(bd, _tile4(v[0, rows, :]), jnp.zeros((), BF16))
            o_ref[0, rows, :] = _dot_nt(qi[0, 0, rows, :], st.astype(BF16)) + _dot(a[0, 0, rows, :], vbd)
            s_scr[d] = st * de[0, 0, cc] + jnp.where(bdt, _dot(vt[0, cc], ke[0, 0, rows, :]), 0.0)


def _gla_scan(qi, a, ke, de, vt, vb):
    b, _, t, _ = a.shape
    nt = t // TM
    qf, qr = _dir_specs((GLA_QK,), nt, False)
    af, ar = _dir_specs((GW,), nt, False)
    df, dr = _dir_specs((1, GLA_QK), nt, True)
    vtf = pl.BlockSpec((1, CPT, GW, CH), lambda bb, s: (bb, _fwd_tile(s, nt), 0, 0))
    vtr = pl.BlockSpec((1, CPT, GW, CH), lambda bb, s: (bb, _rev_tile(s, nt), 0, 0))
    vf = pl.BlockSpec((1, TM, GW), lambda bb, s: (bb, _fwd_tile(s, nt), 0))
    vr = pl.BlockSpec((1, TM, GW), lambda bb, s: (bb, _rev_tile(s, nt), 0))
    return pl.pallas_call(
        _gla_scan_kernel,
        grid=(b, nt),
        in_specs=[qf, af, qf, df, vtf, vf, qr, ar, qr, dr, vtr, vr],
        out_specs=[vf, vr],
        out_shape=[jax.ShapeDtypeStruct((b, t, GW), F32)] * 2,
        scratch_shapes=[pltpu.VMEM((2, GW, GLA_QK), F32)],
        compiler_params=_cp(("parallel", "arbitrary")),
    )(qi, a, ke, de, vt, vb, qi, a, ke, de, vt, vb)


def _outproj_kernel(x_ref, ya_ref, yb_ref, cf_ref, cr_ref, df_ref, dr_ref, pg_ref, g1b_ref, g1c_ref, gc_ref, gd_ref,
                    w_ref, o_ref, *, lat_rows):
    ones64 = _group_ones(GW, 6)

    def fin(o, g, gate):
        ms = _dot_sel(o * o, ones64) * (1.0 / HEAD_DIM)
        return (o * lax.rsqrt(ms + EPS) * g * _silu(gate)).astype(BF16)

    pg = pg_ref[0]
    yc = fin(cf_ref[0] + cr_ref[0], gc_ref[...], pg[:, :GW])
    yd = fin(df_ref[0] + dr_ref[0], gd_ref[...], pg[:, GW:])
    res = (_dot(ya_ref[0].astype(BF16), w_ref[0:GW, :]) + _dot(yb_ref[0].astype(BF16), w_ref[GW:2 * GW, :])
           + _dot(yc, w_ref[2 * GW:3 * GW, :]) + _dot(yd, w_ref[3 * GW:, :]))
    g1 = jnp.where(_ctx_rows(res.shape[0], lat_rows), g1c_ref[0], g1b_ref[0])
    o_ref[0] = x_ref[0] + g1 * res


def _outproj(xs, ya, yb, ocf, ocr, odf, odr, pg, mod3, dn_g, gla_g, w_out_bf, n_batch):
    b, t, d = xs.shape
    reps = GW // HEAD_DIM
    tb = t // N_BLK
    row = lambda bb, i: (bb, i, 0)
    g256 = pl.BlockSpec((1, tb, GW), row)
    in_specs = ([pl.BlockSpec((1, tb, d), row), g256, g256, g256, g256, g256, g256, pl.BlockSpec((1, tb, 2 * GW), row)]
                + _mod_specs(2, n_batch) + [_resident((1, GW)), _resident((1, GW)), _resident((d, d))])
    args = (xs, ya, yb, ocf, ocr, odf, odr, pg, mod3, mod3, jnp.tile(dn_g, reps).reshape(1, GW),
            jnp.tile(gla_g, reps).reshape(1, GW), w_out_bf)
    kern = functools.partial(_outproj_kernel, lat_rows=t - TM)
    return _row_call(kern, n_batch, t, args, in_specs, [d])[0]


FF_HALO = 8


def _ffn_kernel(xm_ref, xl_ref, xr_ref, shb_ref, shc_ref, scb_ref, scc_ref, g2b_ref, g2c_ref, ng_ref,
                wa_ref, wg_ref, cwa_ref, cwg_ref, wd_ref, o_ref, ext_ref, *, lat_rows):
    tb = xm_ref.shape[1]
    i = pl.program_id(1)
    rows_ext = _iota((tb + 2 * FF_HALO, 1), 0)
    keep = jnp.logical_and(jnp.logical_or(rows_ext >= FF_HALO, i > 0),
                           jnp.logical_or(rows_ext < FF_HALO + tb, i < pl.num_programs(1) - 1))
    grow = i * tb - FF_HALO + rows_ext
    ctx_ext = grow >= lat_rows
    x = jnp.concatenate([xl_ref[0], xm_ref[0], xr_ref[0]], axis=0)
    y = x * lax.rsqrt(jnp.mean(x * x, axis=-1, keepdims=True) + EPS) * ng_ref[...]
    h = y * (1.0 + jnp.where(ctx_ext, scc_ref[0], scb_ref[0])) + jnp.where(ctx_ext, shc_ref[0], shb_ref[0])
    h = jnp.where(keep, h, 0.0).astype(BF16)
    row = grow[FF_HALO:FF_HALO + tb]
    m_prev = jnp.broadcast_to(jnp.where(row == lat_rows, 0.0, 1.0), (tb, FF_BLK))
    m_next = jnp.broadcast_to(jnp.where(row == lat_rows - 1, 0.0, 1.0), (tb, FF_BLK))
    acc = jnp.zeros((tb, D_MODEL), F32)
    for j in range(N_FF_BLK):
        def conv(w_ref, cw_ref, half):
            ext_ref[half] = _dot(h, w_ref[j])
            cw = cw_ref[j]
            return (cw[0:1] * (m_prev * ext_ref[half, pl.ds(FF_HALO - 1, tb), :])
                    + cw[1:2] * ext_ref[half, pl.ds(FF_HALO, tb), :]
                    + cw[2:3] * (m_next * ext_ref[half, pl.ds(FF_HALO + 1, tb), :]))
        a = conv(wa_ref, cwa_ref, 0)
        g = conv(wg_ref, cwg_ref, 1)
        acc = acc + _dot((_silu(g) * a).astype(BF16), wd_ref[j])
    g2 = jnp.where(ctx_ext[FF_HALO:FF_HALO + tb], g2c_ref[0], g2b_ref[0])
    o_ref[0] = xm_ref[0] + g2 * acc


def _ffn(x1, mod3, norm_g, wa, wg, cwa, cwg, wd, n_batch):
    b, t, d = x1.shape
    tb = t // N_BLK
    per = tb // FF_HALO
    last = t // FF_HALO - 1
    in_specs = ([pl.BlockSpec((1, tb, d), lambda bb, i: (bb, i, 0)),
                 pl.BlockSpec((1, FF_HALO, d), lambda bb, i: (bb, jnp.maximum(i * per - 1, 0), 0)),
                 pl.BlockSpec((1, FF_HALO, d), lambda bb, i: (bb, jnp.minimum((i + 1) * per, last), 0))]
                + _mod_specs(3, n_batch) + _mod_specs(4, n_batch) + _mod_specs(5, n_batch)
                + [_resident((1, d)), _resident((N_FF_BLK, d, FF_BLK)), _resident((N_FF_BLK, d, FF_BLK)),
                   _resident((N_FF_BLK, 3, FF_BLK)), _resident((N_FF_BLK, 3, FF_BLK)), _resident((N_FF_BLK, FF_BLK, d))])
    args = (x1, x1, x1, mod3, mod3, mod3, mod3, mod3, mod3, norm_g.reshape(1, d), wa, wg, cwa, cwg, wd)
    kern = functools.partial(_ffn_kernel, lat_rows=t - TM)
    return _row_call(kern, n_batch, t, args, in_specs, [d],
                     scratch=[pltpu.VMEM((2, tb + 2 * FF_HALO, FF_BLK), F32)])[0]


def _rope_tables(seq, ctx_len):
    rows = seq // GRID_W
    row = jnp.repeat(jnp.arange(rows, dtype=F32), GRID_W)
    col = jnp.tile(jnp.arange(GRID_W, dtype=F32), rows)
    nf = QK_DIM // 4
    inv = ROPE_THETA ** (-jnp.arange(nf, dtype=F32) / nf)
    ang = jnp.concatenate([row[:, None] * inv, col[:, None] * inv], axis=-1)
    cos = jnp.concatenate([jnp.cos(ang), jnp.ones((ctx_len, QK_DIM // 2), F32)], axis=0)
    sin = jnp.concatenate([jnp.sin(ang), jnp.zeros((ctx_len, QK_DIM // 2), F32)], axis=0)
    reps = GW // QK_DIM
    return (jnp.tile(jnp.concatenate([cos, cos], axis=-1), (1, reps)),
            jnp.tile(jnp.concatenate([-sin, sin], axis=-1), (1, reps)))


def _regroup_w_in(w):
    d = w.shape[0]
    z = lambda n: jnp.zeros((d, n), w.dtype)
    return jnp.concatenate([w[:, :2048], w[:, 2048:2064], z(LANES - 16), w[:, 2320:2832], w[:, 2832:2864],
                            z(LANES - 32), w[:, 2064:2320], w[:, 2864:3120]], axis=1).astype(BF16)


def _gla_w2_blockdiag(w2):
    out = jnp.zeros((LANES, GW), F32)
    out = out.at[0:GLA_RANK, 0:GLA_QK].set(w2[0])
    return out.at[GLA_RANK:2 * GLA_RANK, GLA_QK:].set(w2[1])


def _layer(xs, mod3, lp, cos_t, sin_t, layer_idx, n_batch):
    b, t, d = xs.shape
    nt = t // TM
    pa, pb, pc, pcs, pd, pdl, pg = _inproj(xs, mod3, lp["norm1_g"], _regroup_w_in(lp["w_in"]), n_batch)

    ya = _conv_module(pa, lp["cm_conv_w"], lp["cm_conv_b"], lp["cm_ln_g"], lp["cm_ln_b"])

    lam_init = 0.8 - 0.6 * math.exp(-0.3 * layer_idx)
    q, kt, v, qn, kn = _attn_prep(pb, lp["da_qnorm_g"], lp["da_knorm_g"], cos_t, sin_t)
    yb = _attention(q, kt, v, qn, kn, lp["da_lambda"], lp["da_subln_g"], lam_init)

    dq, dk, dv, bg = _dn_prep(pc, pcs, lp["dn_conv_w"], lp["dn_a_log"], lp["dn_dt_bias"])
    ocf, ocr = _dn_scan(*_dn_local(dq, dk, dv, bg))

    qi, a, ke, vt, vb, de = _gla_local(pd, pdl, _gla_w2_blockdiag(lp["gla_w2"]), lp["gla_b2"].reshape(1, GW))
    odf, odr = _gla_scan(qi, a, ke, de, vt, vb)

    x1 = _outproj(xs, ya, yb, ocf, ocr, odf, odr, pg, mod3, lp["dn_onorm_g"], lp["gla_onorm_g"],
                  lp["w_out"].astype(BF16), n_batch)

    w_up = lp["ffn_w_up"].astype(BF16)
    blocks = lambda m: jnp.transpose(m.reshape(m.shape[0], N_FF_BLK, FF_BLK), (1, 0, 2))
    cw = lp["ffn_conv_w"]
    return _ffn(x1, mod3, lp["norm2_g"], blocks(w_up[:, :D_FF]), blocks(w_up[:, D_FF:]),
                blocks(cw[:, :D_FF]), blocks(cw[:, D_FF:]),
                lp["ffn_w_down"].astype(BF16).reshape(N_FF_BLK, FF_BLK, d), n_batch)


def kernel(x, c, ctx, c_ctx, w_mod, b_mod, norm1_g, norm2_g, w_in, w_out, cm_conv_w, cm_conv_b, cm_ln_g, cm_ln_b, da_qnorm_g, da_knorm_g, da_lambda, da_subln_g, dn_conv_w, dn_a_log, dn_dt_bias, dn_onorm_g, gla_w2, gla_b2, gla_onorm_g, ffn_w_up, ffn_conv_w, ffn_w_down):
    n_batch, seq, d = x.shape
    ctx_len = ctx.shape[1]
    assert ctx_len == TM and seq % TM == 0 and (seq + ctx_len) % (8 * N_BLK) == 0 and d == D_MODEL
    depth = w_mod.shape[0]
    cos_t, sin_t = _rope_tables(seq, ctx_len)
    xs = jnp.concatenate([x, ctx], axis=1)
    mod_rows = 16
    c_rows = jnp.zeros((mod_rows, d), F32).at[:n_batch].set(c).at[n_batch].set(c_ctx)
    params = dict(w_mod=w_mod, b_mod=b_mod, norm1_g=norm1_g, norm2_g=norm2_g, w_in=w_in, w_out=w_out,
                  cm_conv_w=cm_conv_w, cm_conv_b=cm_conv_b, cm_ln_g=cm_ln_g, cm_ln_b=cm_ln_b,
                  da_qnorm_g=da_qnorm_g, da_knorm_g=da_knorm_g, da_lambda=da_lambda, da_subln_g=da_subln_g,
                  dn_conv_w=dn_conv_w, dn_a_log=dn_a_log, dn_dt_bias=dn_dt_bias, dn_onorm_g=dn_onorm_g,
                  gla_w2=gla_w2, gla_b2=gla_b2, gla_onorm_g=gla_onorm_g,
                  ffn_w_up=ffn_w_up, ffn_conv_w=ffn_conv_w, ffn_w_down=ffn_w_down)
    for l in range(depth):
        lp = {k: v[l] for k, v in params.items()}
        mod3 = _modulation(c_rows, lp["w_mod"], lp["b_mod"]).reshape(mod_rows * 6, 1, d)
        xs = _layer(xs, mod3, lp, cos_t, sin_t, l, n_batch)
    return xs[:, :seq]
```

```python
import functools
import math

import jax
import jax.numpy as jnp
from jax import lax
from jax.experimental import pallas as pl
from jax.experimental.pallas import tpu as pltpu

F32 = jnp.float32
BF16 = jnp.bfloat16
HI = lax.Precision.HIGHEST
EPS = 1e-6

D_MODEL = 1024
GRID_W = 64
HEADS = 4
HEAD_DIM = 64
GW = 256
QK_DIM = 32
GLA_K = 32
GLA_RANK = 16
GLA_TAU = 16.0
CM_KERNEL = 31
DN_CONV = 5
ROPE_THETA = 10000.0
CH = 64
TM = 256
CPT = TM // CH
D_FF = 2816
FF_BLK = 256
N_FF_BLK = D_FF // FF_BLK
LANES = 128

IN_GROUPS = (("pa", 512), ("pb", 768), ("pc", 768), ("pcs", LANES), ("pd", 512), ("pdl", LANES), ("pg", 512))
IN_COLS_PAD = sum(w for _, w in IN_GROUPS)

VMEM_LIMIT = 56 * 1024 * 1024


def _cp(sem):
    return pltpu.CompilerParams(dimension_semantics=sem, vmem_limit_bytes=VMEM_LIMIT)


def _dot(a, b, prec=None):
    return jnp.dot(a, b, preferred_element_type=F32, precision=prec)


def _dot_nt(a, b, prec=None):
    return lax.dot_general(a, b, (((1,), (1,)), ((), ())), preferred_element_type=F32, precision=prec)


def _sigmoid(x):
    return 1.0 / (1.0 + jnp.exp(-x))


def _silu(x):
    return x * _sigmoid(x)


def _softplus(x):
    return jnp.maximum(x, 0.0) + jnp.log(1.0 + jnp.exp(-jnp.abs(x)))


def _iota(shape, dim):
    return lax.broadcasted_iota(jnp.int32, shape, dim)


def _group_ones(n, shift):
    return ((_iota((n, n), 0) >> shift) == (_iota((n, n), 1) >> shift)).astype(BF16)


def _bf16_parts(m, n):
    out = []
    for _ in range(n):
        hi = m.astype(BF16)
        out.append(hi)
        m = m - hi.astype(F32)
    return out


def _dot_sel(x, sel, n_parts=2):
    rows = x.shape[0]
    r = _dot(jnp.concatenate(_bf16_parts(x, n_parts), axis=0), sel)
    out = r[:rows]
    for k in range(1, n_parts):
        out = out + r[k * rows:(k + 1) * rows]
    return out


def _tile4(y):
    return jnp.concatenate([y, y, y, y], axis=0)


def _mod_kernel(c_ref, w_ref, b_ref, o_ref):
    o_ref[...] = _dot(_silu(c_ref[...]), w_ref[...], HI) + b_ref[...]


def _modulation(c_rows, w_mod, b_mod):
    r, d = c_rows.shape
    n = w_mod.shape[1] // d
    return pl.pallas_call(
        _mod_kernel,
        grid=(n,),
        in_specs=[pl.BlockSpec((r, d), lambda j: (0, 0)),
                  pl.BlockSpec((d, d), lambda j: (0, j)),
                  pl.BlockSpec((1, d), lambda j: (0, j))],
        out_specs=pl.BlockSpec((r, d), lambda j: (0, j)),
        out_shape=jax.ShapeDtypeStruct((r, n * d), F32),
        compiler_params=_cp(("arbitrary",)),
    )(c_rows, w_mod, b_mod.reshape(1, -1))


N_BLK = 4
N_BLK_FFN = 8


def _mod_specs(k, n_batch):
    return [pl.BlockSpec((1, 1, D_MODEL), lambda b, i: (b * 6 + k, 0, 0)),
            pl.BlockSpec((1, 1, D_MODEL), lambda b, i: (n_batch * 6 + k, 0, 0))]


def _ctx_rows(tb, lat_rows):
    return pl.program_id(1) * tb + _iota((tb, 1), 0) >= lat_rows


def _resident(shape):
    zeros = (0,) * len(shape)
    return pl.BlockSpec(shape, lambda b, i: zeros, pipeline_mode=pl.Buffered(1))


def _row_call(kern, n_batch, rows, args, in_specs, out_widths, scratch=(), n_blk=N_BLK):
    tb = rows // n_blk
    return pl.pallas_call(
        kern,
        grid=(n_batch, n_blk),
        in_specs=in_specs,
        out_specs=[pl.BlockSpec((1, tb, w), lambda bb, i: (bb, i, 0)) for w in out_widths],
        out_shape=[jax.ShapeDtypeStruct((n_batch, rows, w), F32) for w in out_widths],
        scratch_shapes=list(scratch),
        compiler_params=_cp(("parallel", "arbitrary")),
    )(*args)


def _inproj_kernel(x_ref, shb_ref, shc_ref, scb_ref, scc_ref, g_ref, w_ref, *outs, lat_rows):
    x = x_ref[0]
    ctx = _ctx_rows(x.shape[0], lat_rows)
    y = x * lax.rsqrt(jnp.mean(x * x, axis=-1, keepdims=True) + EPS) * g_ref[...]
    h = y * (1.0 + jnp.where(ctx, scc_ref[0], scb_ref[0])) + jnp.where(ctx, shc_ref[0], shb_ref[0])
    p = _dot(h.astype(BF16), w_ref[...])
    off = 0
    for o_ref, (_, width) in zip(outs, IN_GROUPS):
        o_ref[0] = p[:, off:off + width]
        off += width


def _inproj(xs, mod3, norm_g, w_in_r, n_batch):
    b, t, d = xs.shape
    in_specs = ([pl.BlockSpec((1, t // N_BLK, d), lambda bb, i: (bb, i, 0))] + _mod_specs(0, n_batch)
                + _mod_specs(1, n_batch) + [_resident((1, d)), _resident((d, IN_COLS_PAD))])
    kern = functools.partial(_inproj_kernel, lat_rows=t - TM)
    return _row_call(kern, n_batch, t, (xs, mod3, mod3, mod3, mod3, norm_g.reshape(1, d), w_in_r), in_specs,
                     [w for _, w in IN_GROUPS])


def _halo_specs(width, halo, n_tiles):
    per = TM // halo
    left = pl.BlockSpec((1, halo, width), lambda b, i: (b, jnp.maximum(i * per - 1, 0), 0))
    right = pl.BlockSpec((1, halo, width), lambda b, i: (b, jnp.minimum((i + 1) * per, n_tiles * per - 1), 0))
    return left, right


def _halo_ok(i, n_tiles):
    return jnp.logical_and(i >= 1, i < n_tiles - 1), i < n_tiles - 2


CM_HALO = 16


def _convmod_kernel(pm_ref, pl_ref, pr_ref, cw_ref, cb_ref, lg_ref, lb_ref, o_ref, ext_ref):
    left_ok, right_ok = _halo_ok(pl.program_id(1), pl.num_programs(1))

    def glu(p):
        return p[:, :GW] * _sigmoid(p[:, GW:])

    ext_ref[0:CM_HALO] = jnp.where(left_ok, glu(pl_ref[0]), 0.0)
    ext_ref[CM_HALO:CM_HALO + TM] = glu(pm_ref[0])
    ext_ref[CM_HALO + TM:] = jnp.where(right_ok, glu(pr_ref[0]), 0.0)
    pad = CM_KERNEL // 2
    acc = jnp.zeros((TM, GW), F32)
    for j in range(CM_KERNEL):
        acc = acc + cw_ref[j:j + 1, :] * ext_ref[pl.ds(CM_HALO - pad + j, TM), :]
    y = acc + cb_ref[...]
    mu = jnp.mean(y, axis=-1, keepdims=True)
    yc = y - mu
    var = jnp.mean(yc * yc, axis=-1, keepdims=True)
    o_ref[0] = _silu(yc * lax.rsqrt(var + EPS) * lg_ref[...] + lb_ref[...])


def _conv_module(pa, conv_w, conv_b, ln_g, ln_b):
    b, t, w = pa.shape
    nt = t // TM
    left, right = _halo_specs(w, CM_HALO, nt)
    vec = pl.BlockSpec((1, GW), lambda bb, i: (0, 0))
    return pl.pallas_call(
        _convmod_kernel,
        grid=(b, nt),
        in_specs=[pl.BlockSpec((1, TM, w), lambda bb, i: (bb, i, 0)), left, right,
                  pl.BlockSpec((CM_KERNEL, GW), lambda bb, i: (0, 0)), vec, vec, vec],
        out_specs=pl.BlockSpec((1, TM, GW), lambda bb, i: (bb, i, 0)),
        out_shape=jax.ShapeDtypeStruct((b, t, GW), F32),
        scratch_shapes=[pltpu.VMEM((TM + 2 * CM_HALO, GW), F32)],
        compiler_params=_cp(("parallel", "arbitrary")),
    )(pa, pa, pa, conv_w, conv_b.reshape(1, GW), ln_g.reshape(1, GW), ln_b.reshape(1, GW))


N_MAPS = 2 * HEADS
QK_PAD = 2 * QK_DIM
V_EXT = 2 * HEAD_DIM
MASK_BIG = 8192.0
SHIFT_MAX = 40.0


def _attn_prep_kernel(pb_ref, qg_ref, kg_ref, cos_ref, sin_ref, q_out, kt_out, v_out, qn_out, kn_out):
    p = pb_ref[0]
    ones32 = _group_ones(GW, 5)
    first = (_iota((TM, GW), 1) & (QK_DIM - 1)) < QK_DIM // 2
    cos = cos_ref[...]
    sin = sin_ref[...]

    def norm_rope(t, g):
        ms = _dot_sel(t * t, ones32) * (1.0 / QK_DIM)
        tn = t * lax.rsqrt(ms + EPS) * g
        partner = jnp.where(first, pltpu.roll(tn, GW - QK_DIM // 2, 1), pltpu.roll(tn, QK_DIM // 2, 1))
        return tn * cos + partner * sin

    qf = norm_rope(p[:, :GW], qg_ref[...]) * (QK_DIM ** -0.5)
    kf = norm_rope(p[:, GW:2 * GW], kg_ref[...])
    map_sel = ((_iota((GW, N_MAPS), 0) >> 5) == _iota((GW, N_MAPS), 1)).astype(BF16)
    qn_out[0] = _dot_sel(qf * qf, map_sel)
    kn_out[0] = _dot_sel(kf * kf, map_sel)
    q = qf.astype(BF16)
    kt = kf.T.astype(BF16)
    is_ctx = pl.program_id(1) == pl.num_programs(1) - 1
    k_row = _iota((QK_DIM, TM), 0)
    k_tail = jnp.where(k_row == 0, 1.0, jnp.where(jnp.logical_and(k_row == 1, jnp.logical_not(is_ctx)), 1.0, 0.0))
    k_tail = k_tail.astype(BF16)
    q_tail = jnp.where(jnp.logical_and(_iota((TM, QK_DIM), 1) == 1, is_ctx), -MASK_BIG, 0.0).astype(BF16)
    for g in range(N_MAPS):
        q_out[0, g, :, 0:QK_DIM] = q[:, g * QK_DIM:(g + 1) * QK_DIM]
        q_out[0, g, :, QK_DIM:] = q_tail
        kt_out[0, 0, g, 0:QK_DIM, :] = kt[g * QK_DIM:(g + 1) * QK_DIM, :]
        kt_out[0, 0, g, QK_DIM:, :] = k_tail
    v = p[:, 2 * GW:].astype(BF16)
    ones = jnp.ones((TM, HEAD_DIM), BF16)
    for h in range(HEADS):
        v_out[0, 0, h, :, 0:HEAD_DIM] = v[:, h * HEAD_DIM:(h + 1) * HEAD_DIM]
        v_out[0, 0, h, :, HEAD_DIM:] = ones


def _attn_prep(pb, qn_g, kn_g, cos_t, sin_t):
    b, t, w = pb.shape
    nt = t // TM
    vec = pl.BlockSpec((1, GW), lambda bb, i: (0, 0))
    tab = pl.BlockSpec((TM, GW), lambda bb, i: (i, 0))
    reps = GW // QK_DIM
    n_spec = pl.BlockSpec((1, TM, N_MAPS), lambda bb, i: (bb, i, 0))
    n_shape = jax.ShapeDtypeStruct((b, t, N_MAPS), F32)
    return pl.pallas_call(
        _attn_prep_kernel,
        grid=(b, nt),
        in_specs=[pl.BlockSpec((1, TM, w), lambda bb, i: (bb, i, 0)), vec, vec, tab, tab],
        out_specs=[pl.BlockSpec((1, N_MAPS, TM, QK_PAD), lambda bb, i: (bb, 0, i, 0)),
                   pl.BlockSpec((1, 1, N_MAPS, QK_PAD, TM), lambda bb, i: (bb, i, 0, 0, 0)),
                   pl.BlockSpec((1, 1, HEADS, TM, V_EXT), lambda bb, i: (bb, i, 0, 0, 0)), n_spec, n_spec],
        out_shape=[jax.ShapeDtypeStruct((b, N_MAPS, t, QK_PAD), BF16),
                   jax.ShapeDtypeStruct((b, nt, N_MAPS, QK_PAD, TM), BF16),
                   jax.ShapeDtypeStruct((b, nt, HEADS, TM, V_EXT), BF16), n_shape, n_shape],
        compiler_params=_cp(("parallel", "arbitrary")),
    )(pb, jnp.tile(qn_g, reps).reshape(1, GW), jnp.tile(kn_g, reps).reshape(1, GW), cos_t, sin_t)


def _attn_kernel(q_ref, kt_ref, v_ref, qn_ref, kn_ref, lam_ref, sg_ref, o_ref, qa_scr, acc_scr, s_scr, m_scr,
                 *, n_chunks, lam_init):
    tq = o_ref.shape[1]
    acc_scr[...] = jnp.zeros(acc_scr.shape, F32)
    k2 = jnp.max(kn_ref[0], axis=0, keepdims=True)
    bound = jnp.sqrt(qn_ref[0] * k2)
    safe = jnp.max(bound) <= SHIFT_MAX
    shift = jnp.where(safe, bound, 0.0)
    shift_lane = _iota((tq, QK_PAD), 1) == QK_DIM
    for g in range(N_MAPS):
        qa_scr[g] = jnp.where(shift_lane, (-shift[:, g:g + 1]).astype(BF16), q_ref[0, g])

    @pl.when(safe)
    def _():
        s_scr[0] = _dot(qa_scr[0], kt_ref[0, 0, 0])

        def body(c, carry):
            nxt = jnp.minimum(c + 1, n_chunks - 1)
            for g in range(N_MAPS):
                if g + 1 < N_MAPS:
                    s_next = _dot(qa_scr[g + 1], kt_ref[0, c, g + 1])
                else:
                    s_next = _dot(qa_scr[0], kt_ref[0, nxt, 0])
                p = jnp.exp(s_scr[g & 1]).astype(BF16)
                acc_scr[g] += _dot(p, v_ref[0, c, g // 2])
                s_scr[(g + 1) & 1] = s_next
            return carry

        lax.fori_loop(0, n_chunks, body, 0)

    @pl.when(jnp.logical_not(safe))
    def _():
        m_scr[...] = jnp.full(m_scr.shape, -jnp.inf, F32)

        def body(c, carry):
            for g in range(N_MAPS):
                s = _dot(qa_scr[g], kt_ref[0, c, g])
                m_old = m_scr[:, g:g + 1]
                m_new = jnp.maximum(m_old, jnp.max(s, axis=-1, keepdims=True))
                p = jnp.exp(s - m_new).astype(BF16)
                acc_scr[g] = jnp.exp(m_old - m_new) * acc_scr[g] + _dot(p, v_ref[0, c, g // 2])
                m_scr[:, g:g + 1] = m_new
            return carry

        lax.fori_loop(0, n_chunks, body, 0)

    lp = lam_ref[...]
    lam = (jnp.exp(jnp.sum(lp[0:1] * lp[1:2], axis=-1, keepdims=True))
           - jnp.exp(jnp.sum(lp[2:3] * lp[3:4], axis=-1, keepdims=True)) + lam_init)
    for h in range(HEADS):
        a0 = acc_scr[2 * h]
        a1 = acc_scr[2 * h + 1]
        o = (a0[:, :HEAD_DIM] / a0[:, HEAD_DIM:HEAD_DIM + 1]
             - lam * (a1[:, :HEAD_DIM] / a1[:, HEAD_DIM:HEAD_DIM + 1]))
        y = o * lax.rsqrt(jnp.mean(o * o, axis=-1, keepdims=True) + EPS) * sg_ref[...] * (1.0 - lam_init)
        o_ref[0, :, h * HEAD_DIM:(h + 1) * HEAD_DIM] = y


def _attention(q, kt, v, qn, kn, lam_p, subln_g, lam_init):
    b, _, t, _ = q.shape
    tq = t // N_BLK
    nt = t // TM
    kern = functools.partial(_attn_kernel, n_chunks=nt, lam_init=lam_init)
    return pl.pallas_call(
        kern,
        grid=(b, N_BLK),
        in_specs=[pl.BlockSpec((1, N_MAPS, tq, QK_PAD), lambda bb, i: (bb, 0, i, 0)),
                  pl.BlockSpec((1, nt, N_MAPS, QK_PAD, TM), lambda bb, i: (bb, 0, 0, 0, 0)),
                  pl.BlockSpec((1, nt, HEADS, TM, V_EXT), lambda bb, i: (bb, 0, 0, 0, 0)),
                  pl.BlockSpec((1, tq, N_MAPS), lambda bb, i: (bb, i, 0)),
                  pl.BlockSpec((1, t, N_MAPS), lambda bb, i: (bb, 0, 0)),
                  _resident((4, QK_DIM)), _resident((1, HEAD_DIM))],
        out_specs=pl.BlockSpec((1, tq, GW), lambda bb, i: (bb, i, 0)),
        out_shape=jax.ShapeDtypeStruct((b, t, GW), F32),
        scratch_shapes=[pltpu.VMEM((N_MAPS, tq, QK_PAD), BF16), pltpu.VMEM((N_MAPS, tq, V_EXT), F32),
                        pltpu.VMEM((2, tq, TM), F32), pltpu.VMEM((tq, N_MAPS), F32)],
        compiler_params=_cp(("parallel", "arbitrary")),
    )(q, kt, v, qn, kn, lam_p, subln_g.reshape(1, HEAD_DIM))


def _chunk_masks():
    i = _iota((CH, GW), 0)
    j = _iota((CH, GW), 1) & (CH - 1)
    eye = (i == j).astype(F32)
    incl = (j <= i, j >= i)
    strict = (j < i, j > i)
    r = _iota((CH, CH), 0)
    c = _iota((CH, CH), 1)
    cum = ((c <= r).astype(F32), (c >= r).astype(F32))
    bd = (_iota((GW, GW), 0) >> 6) == (_iota((GW, GW), 1) >> 6)
    return eye, incl, strict, cum, bd


def _bd(y, bd):
    return jnp.where(bd, _tile4(y), 0.0).astype(BF16)


def _fwd_tile(s, n_tiles):
    return jnp.where(s == 0, n_tiles - 1, s - 1)


def _rev_tile(s, n_tiles):
    return jnp.where(s == 0, n_tiles - 1, n_tiles - 1 - s)


DN_HALO = 8


def _dn_prep_kernel(pm_ref, pl_ref, pr_ref, pcs_ref, cw_ref, alog_ref, dtb_ref, q_out, k_out, v_out, bg_out, ext_ref):
    left_ok, right_ok = _halo_ok(pl.program_id(1), pl.num_programs(1))
    ext_ref[0:DN_HALO] = jnp.where(left_ok, pl_ref[0], 0.0)
    ext_ref[DN_HALO:DN_HALO + TM] = pm_ref[0]
    ext_ref[DN_HALO + TM:] = jnp.where(right_ok, pr_ref[0], 0.0)
    pad = DN_CONV // 2
    acc = jnp.zeros((TM, 3 * GW), F32)
    for j in range(DN_CONV):
        acc = acc + cw_ref[j:j + 1, :] * ext_ref[pl.ds(DN_HALO - pad + j, TM), :]
    qkv = _silu(acc)
    ones64 = _group_ones(GW, 6)

    def l2n(t):
        return t * lax.rsqrt(_dot_sel(t * t, ones64) + EPS)

    q_out[0] = l2n(qkv[:, :GW]) * (HEAD_DIM ** -0.5)
    k_out[0] = l2n(qkv[:, GW:2 * GW])
    v_out[0] = qkv[:, 2 * GW:]
    s = pcs_ref[0]
    col = _iota(s.shape, 1)
    gate = -jnp.exp(alog_ref[...]) * _softplus(s + dtb_ref[...])
    bg_out[0] = jnp.where(col < 2 * HEADS, _sigmoid(s), jnp.where(col < 4 * HEADS, gate, 0.0))


def _dn_prep(pc, pcs, conv_w, a_log, dt_bias):
    b, t, w = pc.shape
    nt = t // TM
    left, right = _halo_specs(w, DN_HALO, nt)
    row = lambda bb, i: (bb, i, 0)
    pad_vec = lambda a: jnp.zeros((1, LANES), F32).at[0, 2 * HEADS:4 * HEADS].set(a.reshape(-1))
    vec = pl.BlockSpec((1, LANES), lambda bb, i: (0, 0))
    return pl.pallas_call(
        _dn_prep_kernel,
        grid=(b, nt),
        in_specs=[pl.BlockSpec((1, TM, w), row), left, right, pl.BlockSpec((1, TM, LANES), row),
                  pl.BlockSpec((DN_CONV, w), lambda bb, i: (0, 0)), vec, vec],
        out_specs=[pl.BlockSpec((1, TM, GW), row)] * 3 + [pl.BlockSpec((1, TM, LANES), row)],
        out_shape=[jax.ShapeDtypeStruct((b, t, GW), F32)] * 3 + [jax.ShapeDtypeStruct((b, t, LANES), F32)],
        scratch_shapes=[pltpu.VMEM((TM + 2 * DN_HALO, w), F32)],
        compiler_params=_cp(("parallel", "arbitrary")),
    )(pc, pc, pc, pcs, conv_w, pad_vec(a_log), pad_vec(dt_bias))


def _dn_local_kernel(q_ref, k_ref, v_ref, bg_ref, u_out, w_out, qi_out, a_out, ket_out, ge_out):
    rr = _iota((TM, GW), 0)
    cc = _iota((TM, GW), 1)
    i_in = rr & (CH - 1)
    j_in = cc & (CH - 1)
    bd = (rr >> 6) == (cc >> 6)
    eye_t = (i_in == j_in).astype(F32)
    incl = (j_in <= i_in, j_in >= i_in)
    strict = (j_in < i_in, j_in > i_in)
    ones_bd = bd.astype(F32)
    eye_bf = (rr == cc).astype(BF16)
    head_of_lane = _iota((LANES, GW), 1) >> 6
    src = _iota((LANES, GW), 0)
    q = q_ref[0]
    k = k_ref[0]
    v = v_ref[0]
    bg = bg_ref[0]
    def parts(m, n):
        out = []
        for _ in range(n):
            hi = m.astype(BF16)
            out.append(hi)
            m = m - hi.astype(F32)
        return out

    ones_bf = ones_bd.astype(BF16)
    eye_f = eye_t
    beta, gcum, decay, kb, egc = [], [], [], [], []
    for d in range(2):
        sel_b = (src == d * HEADS + head_of_lane).astype(BF16)
        sel_g = (src == 2 * HEADS + d * HEADS + head_of_lane).astype(BF16)
        cum_bd = jnp.where(jnp.logical_and(bd, incl[d]), 1.0, 0.0).astype(BF16)
        g_exp = _dot(jnp.concatenate(parts(bg, 2), axis=0), sel_g)
        beta.append(_dot(bg.astype(BF16), sel_b))
        cs = _dot(cum_bd, jnp.concatenate([g_exp[:TM].astype(BF16), g_exp[TM:].astype(BF16)], axis=1))
        gcum.append(cs[:, :GW] + cs[:, GW:])
        gparts = parts(gcum[d] * eye_f, 3)
        tr = _dot(ones_bf, jnp.concatenate(gparts, axis=1))
        grow = tr[:, :GW] + tr[:, GW:2 * GW] + tr[:, 2 * GW:]
        decay.append(jnp.where(incl[d], jnp.exp(jnp.where(incl[d], gcum[d] - grow, 0.0)), 0.0))
        kb.append(k * beta[d])
        egc.append(jnp.exp(gcum[d]))
        qi_out[0, d] = (q * egc[d]).astype(BF16)
    pairs = [(c, d) for c in range(CPT) for d in range(2)]
    rows = [slice(c * CH, (c + 1) * CH) for c in range(CPT)]
    eye = eye_t[:CH]
    a = {}
    for c in range(CPT):
        r = rows[c]
        lhs = jnp.concatenate([kb[0][r], kb[1][r], q[r]], axis=0).astype(BF16)
        aq = _dot_nt(lhs, _bd(k[r], bd))
        for d in range(2):
            dec = decay[d][r]
            a[c, d] = jnp.where(strict[d][:CH], aq[d * CH:(d + 1) * CH] * dec, 0.0)
            a_out[0, d, r, :] = jnp.where(incl[d][:CH], aq[2 * CH:] * dec, 0.0).astype(BF16)
    t_inv = {cd: eye - a[cd] for cd in pairs}
    p = {cd: _dot(a[cd].astype(BF16), _bd(a[cd], bd)) for cd in pairs}
    for it in range(5):
        for cd in pairs:
            pbd = _bd(p[cd], bd)
            if it < 4:
                res = _dot(jnp.concatenate([t_inv[cd], p[cd]], axis=0).astype(BF16), pbd)
                t_inv[cd] = t_inv[cd] + res[:CH]
                p[cd] = res[CH:]
            else:
                t_inv[cd] = t_inv[cd] + _dot(t_inv[cd].astype(BF16), pbd)

    def split(m):
        hi = m.astype(BF16)
        return hi, (m - hi.astype(F32)).astype(BF16)

    for cd in pairs:
        x0 = t_inv[cd]
        ah, al = split(a[cd])
        xh, xl = split(x0)
        hx = _dot(jnp.concatenate([ah, al], axis=0), _bd(xh, bd))
        resid = eye - x0 - (hx[:CH] + hx[CH:] + _dot(ah, _bd(xl, bd)))
        t_inv[cd] = x0 + _dot(xh, _bd(resid, bd))
    for c, d in pairs:
        r = rows[c]
        tb = t_inv[c, d].astype(BF16)
        u_out[0, d, r, :] = _dot(tb, _bd(v[r] * beta[d][r], bd))
        w_out[0, d, r, :] = _dot(tb, _bd(kb[d][r] * egc[d][r], bd)).astype(BF16)
        last = (c + 1) * CH - 1 if d == 0 else c * CH
        gtot = gcum[d][last:last + 1]
        k_end = (k[r] * jnp.exp(gtot - gcum[d][r])).astype(BF16)
        ket_out[0, d, c] = _dot_nt(eye_bf, k_end).astype(BF16)
        ge_out[0, d, c] = jnp.exp(gtot)


def _dn_local(q, k, v, bg):
    b, t, _ = q.shape
    nt = t // TM
    row = lambda bb, i: (bb, i, 0)
    drow = pl.BlockSpec((1, 2, TM, GW), lambda bb, i: (bb, 0, i, 0))
    return pl.pallas_call(
        _dn_local_kernel,
        grid=(b, nt),
        in_specs=[pl.BlockSpec((1, TM, GW), row)] * 3 + [pl.BlockSpec((1, TM, LANES), row)],
        out_specs=[drow, drow, drow, drow,
                   pl.BlockSpec((1, 2, CPT, GW, CH), lambda bb, i: (bb, 0, i, 0, 0)),
                   pl.BlockSpec((1, 2, CPT, 1, GW), lambda bb, i: (bb, 0, i, 0, 0))],
        out_shape=[jax.ShapeDtypeStruct((b, 2, t, GW), F32)] + [jax.ShapeDtypeStruct((b, 2, t, GW), BF16)] * 3
        + [jax.ShapeDtypeStruct((b, 2, t // CH, GW, CH), BF16), jax.ShapeDtypeStruct((b, 2, t // CH, 1, GW), F32)],
        compiler_params=_cp(("parallel", "arbitrary")),
    )(q, k, v, bg)


def _dn_scan_kernel(uf, wf, qf, af, kf, gf, ur, wr, qr, ar, kr, gr, of_ref, or_ref, s_scr):
    @pl.when(pl.program_id(1) == 0)
    def _():
        s_scr[...] = jnp.zeros(s_scr.shape, F32)

    bd = (_iota((GW, GW), 0) >> 6) == (_iota((GW, GW), 1) >> 6)
    dirs = ((uf, wf, qf, af, kf, gf, of_ref), (ur, wr, qr, ar, kr, gr, or_ref))
    for c in range(CPT):
        for d, (u, w, qi, a, ket, ge, o_ref) in enumerate(dirs):
            cc = c if d == 0 else CPT - 1 - c
            rows = slice(cc * CH, (cc + 1) * CH)
            s = s_scr[d]
            wq = _dot(jnp.concatenate([w[0, 0, rows, :], qi[0, 0, rows, :]], axis=0), s.astype(BF16))
            v_new = u[0, 0, rows, :] - wq[:CH]
            o_ref[0, rows, :] = wq[CH:] + _dot(a[0, 0, rows, :], _bd(v_new, bd))
            s_scr[d] = s * ge[0, 0, cc] + jnp.where(bd, _dot(ket[0, 0, cc], v_new.astype(BF16)), 0.0)


def _dir_specs(shape_tail, n_tiles, chunked):
    blk = (1, 1, CPT if chunked else TM) + shape_tail
    zeros = (0,) * len(shape_tail)
    fwd = pl.BlockSpec(blk, lambda b, s: (b, 0, _fwd_tile(s, n_tiles)) + zeros)
    rev = pl.BlockSpec(blk, lambda b, s: (b, 1, _rev_tile(s, n_tiles)) + zeros)
    return fwd, rev


def _dn_scan(u, w, qi, a, ket, ge):
    b, _, t, _ = u.shape
    nt = t // TM
    rowf, rowr = _dir_specs((GW,), nt, False)
    ketf, ketr = _dir_specs((GW, CH), nt, True)
    gef, ger = _dir_specs((1, GW), nt, True)
    return pl.pallas_call(
        _dn_scan_kernel,
        grid=(b, nt),
        in_specs=[rowf, rowf, rowf, rowf, ketf, gef, rowr, rowr, rowr, rowr, ketr, ger],
        out_specs=[pl.BlockSpec((1, TM, GW), lambda bb, s: (bb, _fwd_tile(s, nt), 0)),
                   pl.BlockSpec((1, TM, GW), lambda bb, s: (bb, _rev_tile(s, nt), 0))],
        out_shape=[jax.ShapeDtypeStruct((b, t, GW), F32)] * 2,
        scratch_shapes=[pltpu.VMEM((2, GW, GW), F32)],
        compiler_params=_cp(("parallel", "arbitrary")),
    )(u, w, qi, a, ket, ge, u, w, qi, a, ket, ge)


GLA_QK = HEADS * GLA_K


def _gla_local_kernel(pd_ref, pdl_ref, w2_ref, b2_ref, qi_out, a_out, ke_out, vt_out, vb_out, de_out):
    _, incl, _, _, _ = _chunk_masks()
    rr = _iota((TM, TM), 0)
    cc = _iota((TM, TM), 1)
    same_chunk = (rr >> 6) == (cc >> 6)
    eye_bf = (rr == cc).astype(BF16)
    bdk = (_iota((GW, GLA_QK), 0) >> 6) == (_iota((GW, GLA_QK), 1) >> 5)
    z = _dot(pdl_ref[0], w2_ref[...], HI) + b2_ref[...]
    gk_all = -_softplus(-z) * (1.0 / GLA_TAU)
    bcs_all = []
    for d in range(2):
        cum_bd = jnp.where(jnp.logical_and(same_chunk, cc <= rr if d == 0 else cc >= rr), 1.0, 0.0).astype(BF16)
        gparts = _bf16_parts(gk_all[:, d * GLA_QK:(d + 1) * GLA_QK], 2)
        cs = _dot(cum_bd, jnp.concatenate(gparts, axis=1))
        bcs_all.append(cs[:, :GLA_QK] + cs[:, GLA_QK:])
    for c in range(CPT):
        rows = slice(c * CH, (c + 1) * CH)
        p = pd_ref[0, rows, :]
        q = p[:, :GLA_QK] * (GLA_K ** -0.5)
        k = p[:, GLA_QK:2 * GLA_QK]
        vb = p[:, 2 * GLA_QK:].astype(BF16)
        vb_out[0, rows, :] = vb
        vt_out[0, c] = _dot_nt(eye_bf, vb).astype(BF16)
        for d in range(2):
            bcs = bcs_all[d][rows]
            bend = bcs[CH - 1:CH] if d == 0 else bcs[0:1]
            q_in = (q * jnp.exp(bcs)).astype(BF16)
            kdec = jnp.where(bdk, _tile4(k * jnp.exp(-bcs)), 0.0).astype(BF16)
            a_out[0, d, rows, :] = jnp.where(incl[d], _dot_nt(q_in, kdec), 0.0).astype(BF16)
            qi_out[0, d, rows, :] = q_in
            ke_out[0, d, rows, :] = (k * jnp.exp(bend - bcs)).astype(BF16)
            de_out[0, d, c] = jnp.exp(bend)


def _gla_local(pd, pdl, w2bd, b2):
    b, t, w = pd.shape
    nt = t // TM
    row = lambda bb, i: (bb, i, 0)
    return pl.pallas_call(
        _gla_local_kernel,
        grid=(b, nt),
        in_specs=[pl.BlockSpec((1, TM, w), row), pl.BlockSpec((1, TM, LANES), row),
                  pl.BlockSpec((LANES, GW), lambda bb, i: (0, 0)), pl.BlockSpec((1, GW), lambda bb, i: (0, 0))],
        out_specs=[pl.BlockSpec((1, 2, TM, GLA_QK), lambda bb, i: (bb, 0, i, 0)),
                   pl.BlockSpec((1, 2, TM, GW), lambda bb, i: (bb, 0, i, 0)),
                   pl.BlockSpec((1, 2, TM, GLA_QK), lambda bb, i: (bb, 0, i, 0)),
                   pl.BlockSpec((1, CPT, GW, CH), lambda bb, i: (bb, i, 0, 0)),
                   pl.BlockSpec((1, TM, GW), row),
                   pl.BlockSpec((1, 2, CPT, 1, GLA_QK), lambda bb, i: (bb, 0, i, 0, 0))],
        out_shape=[jax.ShapeDtypeStruct((b, 2, t, GLA_QK), BF16), jax.ShapeDtypeStruct((b, 2, t, GW), BF16),
                   jax.ShapeDtypeStruct((b, 2, t, GLA_QK), BF16), jax.ShapeDtypeStruct((b, t // CH, GW, CH), BF16),
                   jax.ShapeDtypeStruct((b, t, GW), BF16), jax.ShapeDtypeStruct((b, 2, t // CH, 1, GLA_QK), F32)],
        compiler_params=_cp(("parallel", "arbitrary")),
    )(pd, pdl, w2bd, b2)


def _gla_scan_kernel(qf, af, kf, df, vtf, vf, qr, ar, kr, dr, vtr, vr, of_ref, or_ref, s_scr):
    @pl.when(pl.program_id(1) == 0)
    def _():
        s_scr[...] = jnp.zeros(s_scr.shape, F32)

    bd = (_iota((GW, GW), 0) >> 6) == (_iota((GW, GW), 1) >> 6)
    bdt = (_iota((GW, GLA_QK), 0) >> 6) == (_iota((GW, GLA_QK), 1) >> 5)
    dirs = ((qf, af, kf, df, vtf, vf, of_ref), (qr, ar, kr, dr, vtr, vr, or_ref))
    for c in range(CPT):
        for d, (qi, a, ke, de, vt, v, o_ref) in enumerate(dirs):
            cc = c if d == 0 else CPT - 1 - c
            rows = slice(cc * CH, (cc + 1) * CH)
            st = s_scr[d]
            vbd = jnp.where(bd, _tile4(v[0, rows, :]), jnp.zeros((), BF16))
            o_ref[0, rows, :] = _dot_nt(qi[0, 0, rows, :], st.astype(BF16)) + _dot(a[0, 0, rows, :], vbd)
            s_scr[d] = st * de[0, 0, cc] + jnp.where(bdt, _dot(vt[0, cc], ke[0, 0, rows, :]), 0.0)


def _gla_scan(qi, a, ke, de, vt, vb):
    b, _, t, _ = a.shape
    nt = t // TM
    qf, qr = _dir_specs((GLA_QK,), nt, False)
    af, ar = _dir_specs((GW,), nt, False)
    df, dr = _dir_specs((1, GLA_QK), nt, True)
    vtf = pl.BlockSpec((1, CPT, GW, CH), lambda bb, s: (bb, _fwd_tile(s, nt), 0, 0))
    vtr = pl.BlockSpec((1, CPT, GW, CH), lambda bb, s: (bb, _rev_tile(s, nt), 0, 0))
    vf = pl.BlockSpec((1, TM, GW), lambda bb, s: (bb, _fwd_tile(s, nt), 0))
    vr = pl.BlockSpec((1, TM, GW), lambda bb, s: (bb, _rev_tile(s, nt), 0))
    return pl.pallas_call(
        _gla_scan_kernel,
        grid=(b, nt),
        in_specs=[qf, af, qf, df, vtf, vf, qr, ar, qr, dr, vtr, vr],
        out_specs=[vf, vr],
        out_shape=[jax.ShapeDtypeStruct((b, t, GW), F32)] * 2,
        scratch_shapes=[pltpu.VMEM((2, GW, GLA_QK), F32)],
        compiler_params=_cp(("parallel", "arbitrary")),
    )(qi, a, ke, de, vt, vb, qi, a, ke, de, vt, vb)


def _outproj_kernel(x_ref, ya_ref, yb_ref, cf_ref, cr_ref, df_ref, dr_ref, pg_ref, g1b_ref, g1c_ref, gc_ref, gd_ref,
                    w_ref, o_ref, *, lat_rows):
    ones64 = _group_ones(GW, 6)

    def fin(o, g, gate):
        ms = _dot_sel(o * o, ones64) * (1.0 / HEAD_DIM)
        return (o * lax.rsqrt(ms + EPS) * g * _silu(gate)).astype(BF16)

    pg = pg_ref[0]
    yc = fin(cf_ref[0] + cr_ref[0], gc_ref[...], pg[:, :GW])
    yd = fin(df_ref[0] + dr_ref[0], gd_ref[...], pg[:, GW:])
    res = (_dot(ya_ref[0].astype(BF16), w_ref[0:GW, :]) + _dot(yb_ref[0].astype(BF16), w_ref[GW:2 * GW, :])
           + _dot(yc, w_ref[2 * GW:3 * GW, :]) + _dot(yd, w_ref[3 * GW:, :]))
    g1 = jnp.where(_ctx_rows(res.shape[0], lat_rows), g1c_ref[0], g1b_ref[0])
    o_ref[0] = x_ref[0] + g1 * res


def _outproj(xs, ya, yb, ocf, ocr, odf, odr, pg, mod3, dn_g, gla_g, w_out_bf, n_batch, rows, n_blk):
    b, t, d = xs.shape
    reps = GW // HEAD_DIM
    tb = rows // n_blk
    row = lambda bb, i: (bb, i, 0)
    g256 = pl.BlockSpec((1, tb, GW), row)
    in_specs = ([pl.BlockSpec((1, tb, d), row), g256, g256, g256, g256, g256, g256, pl.BlockSpec((1, tb, 2 * GW), row)]
                + _mod_specs(2, n_batch) + [_resident((1, GW)), _resident((1, GW)), _resident((d, d))])
    args = (xs, ya, yb, ocf, ocr, odf, odr, pg, mod3, mod3, jnp.tile(dn_g, reps).reshape(1, GW),
            jnp.tile(gla_g, reps).reshape(1, GW), w_out_bf)
    kern = functools.partial(_outproj_kernel, lat_rows=t - TM)
    return _row_call(kern, n_batch, rows, args, in_specs, [d], n_blk=n_blk)[0]


FF_HALO = 8


def _ffn_kernel(xm_ref, xl_ref, xr_ref, shb_ref, shc_ref, scb_ref, scc_ref, g2b_ref, g2c_ref, ng_ref,
                wu_ref, cw_ref, wd_ref, o_ref, ext_ref, *, lat_rows):
    tb = xm_ref.shape[1]
    i = pl.program_id(1)
    rows_ext = _iota((tb + 2 * FF_HALO, 1), 0)
    left_ok = jnp.logical_and(i > 0, i * tb != lat_rows)
    right_ok = jnp.logical_and(i < pl.num_programs(1) - 1, (i + 1) * tb != lat_rows)
    keep = jnp.logical_and(jnp.logical_or(rows_ext >= FF_HALO, left_ok),
                           jnp.logical_or(rows_ext < FF_HALO + tb, right_ok))
    grow = i * tb - FF_HALO + rows_ext
    ctx_ext = grow >= lat_rows
    x = jnp.concatenate([xl_ref[0], xm_ref[0], xr_ref[0]], axis=0)
    y = x * lax.rsqrt(jnp.mean(x * x, axis=-1, keepdims=True) + EPS) * ng_ref[...]
    h = y * (1.0 + jnp.where(ctx_ext, scc_ref[0], scb_ref[0])) + jnp.where(ctx_ext, shc_ref[0], shb_ref[0])
    h = jnp.where(keep, h, 0.0).astype(BF16)
    inner_boundary = lat_rows % tb != 0
    if inner_boundary:
        row = grow[FF_HALO:FF_HALO + tb]
        m_prev = jnp.broadcast_to(jnp.where(row == lat_rows, 0.0, 1.0), (tb, FF_BLK))
        m_next = jnp.broadcast_to(jnp.where(row == lat_rows - 1, 0.0, 1.0), (tb, FF_BLK))
    acc = jnp.zeros((tb, D_MODEL), F32)
    for j in range(N_FF_BLK):
        def conv(col0, half):
            cols = slice(col0, col0 + FF_BLK)
            ext_ref[half] = _dot(h, wu_ref[:, cols])
            cw = cw_ref[:, cols]
            prev = ext_ref[half, pl.ds(FF_HALO - 1, tb), :]
            nxt = ext_ref[half, pl.ds(FF_HALO + 1, tb), :]
            if inner_boundary:
                prev = m_prev * prev
                nxt = m_next * nxt
            return cw[0:1] * prev + cw[1:2] * ext_ref[half, pl.ds(FF_HALO, tb), :] + cw[2:3] * nxt
        a = conv(j * FF_BLK, 0)
        g = conv(D_FF + j * FF_BLK, 1)
        acc = acc + _dot((_silu(g) * a).astype(BF16), wd_ref[j * FF_BLK:(j + 1) * FF_BLK, :])
    g2 = jnp.where(ctx_ext[FF_HALO:FF_HALO + tb], g2c_ref[0], g2b_ref[0])
    o_ref[0] = xm_ref[0] + g2 * acc


def _ffn(x1, mod3, norm_g, w_up, cw, w_down, n_batch, lat_rows, rows, n_blk):
    b, t, d = x1.shape
    tb = rows // n_blk
    per = tb // FF_HALO
    last = t // FF_HALO - 1
    in_specs = ([pl.BlockSpec((1, tb, d), lambda bb, i: (bb, i, 0)),
                 pl.BlockSpec((1, FF_HALO, d), lambda bb, i: (bb, jnp.maximum(i * per - 1, 0), 0)),
                 pl.BlockSpec((1, FF_HALO, d), lambda bb, i: (bb, jnp.minimum((i + 1) * per, last), 0))]
                + _mod_specs(3, n_batch) + _mod_specs(4, n_batch) + _mod_specs(5, n_batch)
                + [_resident((1, d)), _resident(w_up.shape), _resident(cw.shape), _resident(w_down.shape)])
    args = (x1, x1, x1, mod3, mod3, mod3, mod3, mod3, mod3, norm_g.reshape(1, d), w_up, cw, w_down)
    kern = functools.partial(_ffn_kernel, lat_rows=lat_rows)
    return _row_call(kern, n_batch, rows, args, in_specs, [d],
                     scratch=[pltpu.VMEM((2, tb + 2 * FF_HALO, FF_BLK), F32)], n_blk=n_blk)[0]


def _rope_tables(seq, ctx_len):
    rows = seq // GRID_W
    row = jnp.repeat(jnp.arange(rows, dtype=F32), GRID_W)
    col = jnp.tile(jnp.arange(GRID_W, dtype=F32), rows)
    nf = QK_DIM // 4
    inv = ROPE_THETA ** (-jnp.arange(nf, dtype=F32) / nf)
    ang = jnp.concatenate([row[:, None] * inv, col[:, None] * inv], axis=-1)
    cos = jnp.concatenate([jnp.cos(ang), jnp.ones((ctx_len, QK_DIM // 2), F32)], axis=0)
    sin = jnp.concatenate([jnp.sin(ang), jnp.zeros((ctx_len, QK_DIM // 2), F32)], axis=0)
    reps = GW // QK_DIM
    return (jnp.tile(jnp.concatenate([cos, cos], axis=-1), (1, reps)),
            jnp.tile(jnp.concatenate([-sin, sin], axis=-1), (1, reps)))


def _regroup_w_in(w):
    d = w.shape[0]
    z = lambda n: jnp.zeros((d, n), w.dtype)
    return jnp.concatenate([w[:, :2048], w[:, 2048:2064], z(LANES - 16), w[:, 2320:2832], w[:, 2832:2864],
                            z(LANES - 32), w[:, 2064:2320], w[:, 2864:3120]], axis=1).astype(BF16)


def _gla_w2_blockdiag(w2):
    out = jnp.zeros((LANES, GW), F32)
    out = out.at[0:GLA_RANK, 0:GLA_QK].set(w2[0])
    return out.at[GLA_RANK:2 * GLA_RANK, GLA_QK:].set(w2[1])


def _layer(xs, mod3, lp, cos_t, sin_t, layer_idx, last, n_batch):
    b, t, d = xs.shape
    nt = t // TM
    pa, pb, pc, pcs, pd, pdl, pg = _inproj(xs, mod3, lp["norm1_g"], _regroup_w_in(lp["w_in"]), n_batch)

    ya = _conv_module(pa, lp["cm_conv_w"], lp["cm_conv_b"], lp["cm_ln_g"], lp["cm_ln_b"])

    lam_init = 0.8 - 0.6 * math.exp(-0.3 * layer_idx)
    q, kt, v, qn, kn = _attn_prep(pb, lp["da_qnorm_g"], lp["da_knorm_g"], cos_t, sin_t)
    yb = _attention(q, kt, v, qn, kn, lp["da_lambda"], lp["da_subln_g"], lam_init)

    dq, dk, dv, bg = _dn_prep(pc, pcs, lp["dn_conv_w"], lp["dn_a_log"], lp["dn_dt_bias"])
    ocf, ocr = _dn_scan(*_dn_local(dq, dk, dv, bg))

    qi, a, ke, vt, vb, de = _gla_local(pd, pdl, _gla_w2_blockdiag(lp["gla_w2"]), lp["gla_b2"].reshape(1, GW))
    odf, odr = _gla_scan(qi, a, ke, de, vt, vb)

    lat_rows = t - TM
    rows = lat_rows if last else t
    x1 = _outproj(xs, ya, yb, ocf, ocr, odf, odr, pg, mod3, lp["dn_onorm_g"], lp["gla_onorm_g"],
                  lp["w_out"].astype(BF16), n_batch, rows, N_BLK_FFN)
    return _ffn(x1, mod3, lp["norm2_g"], lp["ffn_w_up"].astype(BF16), lp["ffn_conv_w"],
                lp["ffn_w_down"].astype(BF16), n_batch, lat_rows, rows, N_BLK_FFN)


def kernel(x, c, ctx, c_ctx, w_mod, b_mod, norm1_g, norm2_g, w_in, w_out, cm_conv_w, cm_conv_b, cm_ln_g, cm_ln_b, da_qnorm_g, da_knorm_g, da_lambda, da_subln_g, dn_conv_w, dn_a_log, dn_dt_bias, dn_onorm_g, gla_w2, gla_b2, gla_onorm_g, ffn_w_up, ffn_conv_w, ffn_w_down):
    n_batch, seq, d = x.shape
    ctx_len = ctx.shape[1]
    assert ctx_len == TM and seq % TM == 0 and (seq + ctx_len) % (8 * N_BLK) == 0 and d == D_MODEL
    depth = w_mod.shape[0]
    cos_t, sin_t = _rope_tables(seq, ctx_len)
    xs = jnp.concatenate([x, ctx], axis=1)
    mod_rows = 16
    c_rows = jnp.zeros((mod_rows, d), F32).at[:n_batch].set(c).at[n_batch].set(c_ctx)
    params = dict(w_mod=w_mod, b_mod=b_mod, norm1_g=norm1_g, norm2_g=norm2_g, w_in=w_in, w_out=w_out,
                  cm_conv_w=cm_conv_w, cm_conv_b=cm_conv_b, cm_ln_g=cm_ln_g, cm_ln_b=cm_ln_b,
                  da_qnorm_g=da_qnorm_g, da_knorm_g=da_knorm_g, da_lambda=da_lambda, da_subln_g=da_subln_g,
                  dn_conv_w=dn_conv_w, dn_a_log=dn_a_log, dn_dt_bias=dn_dt_bias, dn_onorm_g=dn_onorm_g,
                  gla_w2=gla_w2, gla_b2=gla_b2, gla_onorm_g=gla_onorm_g,
                  ffn_w_up=ffn_w_up, ffn_conv_w=ffn_conv_w, ffn_w_down=ffn_w_down)
    for l in range(depth):
        lp = {k: v[l] for k, v in params.items()}
        mod3 = _modulation(c_rows, lp["w_mod"], lp["b_mod"]).reshape(mod_rows * 6, 1, d)
        xs = _layer(xs, mod3, lp, cos_t, sin_t, l, l == depth - 1, n_batch)
    return xs
```

```python
import functools
import math

import jax
import jax.numpy as jnp
from jax import lax
from jax.experimental import pallas as pl
from jax.experimental.pallas import tpu as pltpu

F32 = jnp.float32
BF16 = jnp.bfloat16
HI = lax.Precision.HIGHEST
EPS = 1e-6

D_MODEL = 1024
GRID_W = 64
HEADS = 4
HEAD_DIM = 64
GW = 256
QK_DIM = 32
GLA_K = 32
GLA_RANK = 16
GLA_TAU = 16.0
CM_KERNEL = 31
DN_CONV = 5
ROPE_THETA = 10000.0
CH = 64
TM = 256
CPT = TM // CH
D_FF = 2816
FF_BLK = 256
N_FF_BLK = D_FF // FF_BLK
LANES = 128

IN_GROUPS = (("pa", 512), ("pb", 768), ("pc", 768), ("pcs", LANES), ("pd", 512), ("pdl", LANES), ("pg", 512))
IN_COLS_PAD = sum(w for _, w in IN_GROUPS)

VMEM_LIMIT = 56 * 1024 * 1024


def _cp(sem):
    return pltpu.CompilerParams(dimension_semantics=sem, vmem_limit_bytes=VMEM_LIMIT)


def _dot(a, b, prec=None):
    return jnp.dot(a, b, preferred_element_type=F32, precision=prec)


def _dot_nt(a, b, prec=None):
    return lax.dot_general(a, b, (((1,), (1,)), ((), ())), preferred_element_type=F32, precision=prec)


def _sigmoid(x):
    return 1.0 / (1.0 + jnp.exp(-x))


def _silu(x):
    return x * _sigmoid(x)


def _softplus(x):
    return jnp.maximum(x, 0.0) + jnp.log(1.0 + jnp.exp(-jnp.abs(x)))


def _iota(shape, dim):
    return lax.broadcasted_iota(jnp.int32, shape, dim)


def _group_ones(n, shift):
    return ((_iota((n, n), 0) >> shift) == (_iota((n, n), 1) >> shift)).astype(BF16)


def _bf16_parts(m, n):
    out = []
    for _ in range(n):
        hi = m.astype(BF16)
        out.append(hi)
        m = m - hi.astype(F32)
    return out


def _dot_sel(x, sel, n_parts=2):
    rows = x.shape[0]
    r = _dot(jnp.concatenate(_bf16_parts(x, n_parts), axis=0), sel)
    out = r[:rows]
    for k in range(1, n_parts):
        out = out + r[k * rows:(k + 1) * rows]
    return out


def _tile4(y):
    return jnp.concatenate([y, y, y, y], axis=0)


def _mod_kernel(c_ref, w_ref, b_ref, o_ref):
    o_ref[...] = _dot(_silu(c_ref[...]), w_ref[...], HI) + b_ref[...]


def _modulation(c_rows, w_mod, b_mod):
    r, d = c_rows.shape
    n = w_mod.shape[1] // d
    return pl.pallas_call(
        _mod_kernel,
        grid=(n,),
        in_specs=[pl.BlockSpec((r, d), lambda j: (0, 0)),
                  pl.BlockSpec((d, d), lambda j: (0, j)),
                  pl.BlockSpec((1, d), lambda j: (0, j))],
        out_specs=pl.BlockSpec((r, d), lambda j: (0, j)),
        out_shape=jax.ShapeDtypeStruct((r, n * d), F32),
        compiler_params=_cp(("arbitrary",)),
    )(c_rows, w_mod, b_mod.reshape(1, -1))


N_BLK = 4
N_BLK_FFN = 8


def _mod_specs(k, n_batch):
    return [pl.BlockSpec((1, 1, D_MODEL), lambda b, i: (b * 6 + k, 0, 0)),
            pl.BlockSpec((1, 1, D_MODEL), lambda b, i: (n_batch * 6 + k, 0, 0))]


def _ctx_rows(tb, lat_rows):
    return pl.program_id(1) * tb + _iota((tb, 1), 0) >= lat_rows


def _resident(shape):
    zeros = (0,) * len(shape)
    return pl.BlockSpec(shape, lambda b, i: zeros, pipeline_mode=pl.Buffered(1))


def _row_call(kern, n_batch, rows, args, in_specs, out_widths, scratch=(), n_blk=N_BLK):
    tb = rows // n_blk
    return pl.pallas_call(
        kern,
        grid=(n_batch, n_blk),
        in_specs=in_specs,
        out_specs=[pl.BlockSpec((1, tb, w), lambda bb, i: (bb, i, 0)) for w in out_widths],
        out_shape=[jax.ShapeDtypeStruct((n_batch, rows, w), F32) for w in out_widths],
        scratch_shapes=list(scratch),
        compiler_params=_cp(("parallel", "arbitrary")),
    )(*args)


def _inproj_kernel(x_ref, shb_ref, shc_ref, scb_ref, scc_ref, g_ref, w_ref, *outs, lat_rows):
    x = x_ref[0]
    ctx = _ctx_rows(x.shape[0], lat_rows)
    y = x * lax.rsqrt(jnp.mean(x * x, axis=-1, keepdims=True) + EPS) * g_ref[...]
    h = y * (1.0 + jnp.where(ctx, scc_ref[0], scb_ref[0])) + jnp.where(ctx, shc_ref[0], shb_ref[0])
    p = _dot(h.astype(BF16), w_ref[...])
    off = 0
    for o_ref, (_, width) in zip(outs, IN_GROUPS):
        o_ref[0] = p[:, off:off + width]
        off += width


def _inproj(xs, mod3, norm_g, w_in_r, n_batch):
    b, t, d = xs.shape
    in_specs = ([pl.BlockSpec((1, t // N_BLK, d), lambda bb, i: (bb, i, 0))] + _mod_specs(0, n_batch)
                + _mod_specs(1, n_batch) + [_resident((1, d)), _resident((d, IN_COLS_PAD))])
    kern = functools.partial(_inproj_kernel, lat_rows=t - TM)
    return _row_call(kern, n_batch, t, (xs, mod3, mod3, mod3, mod3, norm_g.reshape(1, d), w_in_r), in_specs,
                     [w for _, w in IN_GROUPS])


def _halo_specs(width, halo, n_tiles):
    per = TM // halo
    left = pl.BlockSpec((1, halo, width), lambda b, i: (b, jnp.maximum(i * per - 1, 0), 0))
    right = pl.BlockSpec((1, halo, width), lambda b, i: (b, jnp.minimum((i + 1) * per, n_tiles * per - 1), 0))
    return left, right


def _halo_ok(i, n_tiles):
    return jnp.logical_and(i >= 1, i < n_tiles - 1), i < n_tiles - 2


CM_HALO = 16


def _convmod_kernel(pm_ref, pl_ref, pr_ref, cw_ref, cb_ref, lg_ref, lb_ref, o_ref, ext_ref):
    left_ok, right_ok = _halo_ok(pl.program_id(1), pl.num_programs(1))

    def glu(p):
        return p[:, :GW] * _sigmoid(p[:, GW:])

    ext_ref[0:CM_HALO] = jnp.where(left_ok, glu(pl_ref[0]), 0.0)
    ext_ref[CM_HALO:CM_HALO + TM] = glu(pm_ref[0])
    ext_ref[CM_HALO + TM:] = jnp.where(right_ok, glu(pr_ref[0]), 0.0)
    pad = CM_KERNEL // 2
    sub = 8
    y = cb_ref[...]
    for r in range(sub):
        acc = None
        for j in range(CM_KERNEL):
            off = CM_HALO - pad + j
            if off % sub == r:
                term = cw_ref[j:j + 1, :] * ext_ref[pl.ds(off - r, TM + sub), :]
                acc = term if acc is None else acc + term
        if acc is not None:
            y = y + acc[r:r + TM]
    mu = jnp.mean(y, axis=-1, keepdims=True)
    yc = y - mu
    var = jnp.mean(yc * yc, axis=-1, keepdims=True)
    o_ref[0] = _silu(yc * lax.rsqrt(var + EPS) * lg_ref[...] + lb_ref[...])


def _conv_module(pa, conv_w, conv_b, ln_g, ln_b):
    b, t, w = pa.shape
    nt = t // TM
    left, right = _halo_specs(w, CM_HALO, nt)
    vec = pl.BlockSpec((1, GW), lambda bb, i: (0, 0))
    return pl.pallas_call(
        _convmod_kernel,
        grid=(b, nt),
        in_specs=[pl.BlockSpec((1, TM, w), lambda bb, i: (bb, i, 0)), left, right,
                  pl.BlockSpec((CM_KERNEL, GW), lambda bb, i: (0, 0)), vec, vec, vec],
        out_specs=pl.BlockSpec((1, TM, GW), lambda bb, i: (bb, i, 0)),
        out_shape=jax.ShapeDtypeStruct((b, t, GW), F32),
        scratch_shapes=[pltpu.VMEM((TM + 2 * CM_HALO, GW), F32)],
        compiler_params=_cp(("parallel", "arbitrary")),
    )(pa, pa, pa, conv_w, conv_b.reshape(1, GW), ln_g.reshape(1, GW), ln_b.reshape(1, GW))


N_MAPS = 2 * HEADS
QK_PAD = 2 * QK_DIM
V_EXT = 2 * HEAD_DIM
MASK_BIG = 8192.0
SHIFT_MAX = 40.0


def _attn_prep_kernel(pb_ref, qg_ref, kg_ref, cos_ref, sin_ref, q_out, kt_out, v_out, qn_out, kn_out):
    p = pb_ref[0]
    ones32 = _group_ones(GW, 5)
    first = (_iota((TM, GW), 1) & (QK_DIM - 1)) < QK_DIM // 2
    cos = cos_ref[...]
    sin = sin_ref[...]

    def norm_rope(t, g):
        ms = _dot_sel(t * t, ones32) * (1.0 / QK_DIM)
        tn = t * lax.rsqrt(ms + EPS) * g
        partner = jnp.where(first, pltpu.roll(tn, GW - QK_DIM // 2, 1), pltpu.roll(tn, QK_DIM // 2, 1))
        return tn * cos + partner * sin

    qf = norm_rope(p[:, :GW], qg_ref[...]) * (QK_DIM ** -0.5)
    kf = norm_rope(p[:, GW:2 * GW], kg_ref[...])
    map_sel = ((_iota((GW, N_MAPS), 0) >> 5) == _iota((GW, N_MAPS), 1)).astype(BF16)
    qn_out[0] = _dot_sel(qf * qf, map_sel)
    kn_out[0] = _dot_sel(kf * kf, map_sel)
    q = qf.astype(BF16)
    kt = kf.T.astype(BF16)
    is_ctx = pl.program_id(1) == pl.num_programs(1) - 1
    k_row = _iota((QK_DIM, TM), 0)
    k_tail = jnp.where(k_row == 0, 1.0, jnp.where(jnp.logical_and(k_row == 1, jnp.logical_not(is_ctx)), 1.0, 0.0))
    k_tail = k_tail.astype(BF16)
    q_tail = jnp.where(jnp.logical_and(_iota((TM, QK_DIM), 1) == 1, is_ctx), -MASK_BIG, 0.0).astype(BF16)
    for g in range(N_MAPS):
        q_out[0, g, :, 0:QK_DIM] = q[:, g * QK_DIM:(g + 1) * QK_DIM]
        q_out[0, g, :, QK_DIM:] = q_tail
        kt_out[0, 0, g, 0:QK_DIM, :] = kt[g * QK_DIM:(g + 1) * QK_DIM, :]
        kt_out[0, 0, g, QK_DIM:, :] = k_tail
    v = p[:, 2 * GW:].astype(BF16)
    ones = jnp.ones((TM, HEAD_DIM), BF16)
    for h in range(HEADS):
        v_out[0, 0, h, :, 0:HEAD_DIM] = v[:, h * HEAD_DIM:(h + 1) * HEAD_DIM]
        v_out[0, 0, h, :, HEAD_DIM:] = ones


def _attn_prep(pb, qn_g, kn_g, cos_t, sin_t):
    b, t, w = pb.shape
    nt = t // TM
    vec = pl.BlockSpec((1, GW), lambda bb, i: (0, 0))
    tab = pl.BlockSpec((TM, GW), lambda bb, i: (i, 0))
    reps = GW // QK_DIM
    n_spec = pl.BlockSpec((1, TM, N_MAPS), lambda bb, i: (bb, i, 0))
    n_shape = jax.ShapeDtypeStruct((b, t, N_MAPS), F32)
    return pl.pallas_call(
        _attn_prep_kernel,
        grid=(b, nt),
        in_specs=[pl.BlockSpec((1, TM, w), lambda bb, i: (bb, i, 0)), vec, vec, tab, tab],
        out_specs=[pl.BlockSpec((1, N_MAPS, TM, QK_PAD), lambda bb, i: (bb, 0, i, 0)),
                   pl.BlockSpec((1, 1, N_MAPS, QK_PAD, TM), lambda bb, i: (bb, i, 0, 0, 0)),
                   pl.BlockSpec((1, 1, HEADS, TM, V_EXT), lambda bb, i: (bb, i, 0, 0, 0)), n_spec, n_spec],
        out_shape=[jax.ShapeDtypeStruct((b, N_MAPS, t, QK_PAD), BF16),
                   jax.ShapeDtypeStruct((b, nt, N_MAPS, QK_PAD, TM), BF16),
                   jax.ShapeDtypeStruct((b, nt, HEADS, TM, V_EXT), BF16), n_shape, n_shape],
        compiler_params=_cp(("parallel", "arbitrary")),
    )(pb, jnp.tile(qn_g, reps).reshape(1, GW), jnp.tile(kn_g, reps).reshape(1, GW), cos_t, sin_t)


def _attn_kernel(q_ref, kt_ref, v_ref, qn_ref, kn_ref, lam_ref, sg_ref, o_ref, qa_scr, acc_scr, s_scr, m_scr,
                 *, n_chunks, lam_init):
    tq = o_ref.shape[1]
    acc_scr[...] = jnp.zeros(acc_scr.shape, F32)
    k2 = jnp.max(kn_ref[0], axis=0, keepdims=True)
    bound = jnp.sqrt(qn_ref[0] * k2)
    safe = jnp.max(bound) <= SHIFT_MAX
    shift = jnp.where(safe, bound, 0.0)
    shift_lane = _iota((tq, QK_PAD), 1) == QK_DIM
    for g in range(N_MAPS):
        qa_scr[g] = jnp.where(shift_lane, (-shift[:, g:g + 1]).astype(BF16), q_ref[0, g])

    @pl.when(safe)
    def _():
        s_scr[0] = _dot(qa_scr[0], kt_ref[0, 0, 0])

        def body(c, carry):
            nxt = jnp.minimum(c + 1, n_chunks - 1)
            for g in range(N_MAPS):
                if g + 1 < N_MAPS:
                    s_next = _dot(qa_scr[g + 1], kt_ref[0, c, g + 1])
                else:
                    s_next = _dot(qa_scr[0], kt_ref[0, nxt, 0])
                p = jnp.exp(s_scr[g & 1]).astype(BF16)
                acc_scr[g] += _dot(p, v_ref[0, c, g // 2])
                s_scr[(g + 1) & 1] = s_next
            return carry

        lax.fori_loop(0, n_chunks, body, 0)

    @pl.when(jnp.logical_not(safe))
    def _():
        m_scr[...] = jnp.full(m_scr.shape, -jnp.inf, F32)

        def body(c, carry):
            for g in range(N_MAPS):
                s = _dot(qa_scr[g], kt_ref[0, c, g])
                m_old = m_scr[:, g:g + 1]
                m_new = jnp.maximum(m_old, jnp.max(s, axis=-1, keepdims=True))
                p = jnp.exp(s - m_new).astype(BF16)
                acc_scr[g] = jnp.exp(m_old - m_new) * acc_scr[g] + _dot(p, v_ref[0, c, g // 2])
                m_scr[:, g:g + 1] = m_new
            return carry

        lax.fori_loop(0, n_chunks, body, 0)

    lp = lam_ref[...]
    lam = (jnp.exp(jnp.sum(lp[0:1] * lp[1:2], axis=-1, keepdims=True))
           - jnp.exp(jnp.sum(lp[2:3] * lp[3:4], axis=-1, keepdims=True)) + lam_init)
    for h in range(HEADS):
        a0 = acc_scr[2 * h]
        a1 = acc_scr[2 * h + 1]
        o = (a0[:, :HEAD_DIM] / a0[:, HEAD_DIM:HEAD_DIM + 1]
             - lam * (a1[:, :HEAD_DIM] / a1[:, HEAD_DIM:HEAD_DIM + 1]))
        y = o * lax.rsqrt(jnp.mean(o * o, axis=-1, keepdims=True) + EPS) * sg_ref[...] * (1.0 - lam_init)
        o_ref[0, :, h * HEAD_DIM:(h + 1) * HEAD_DIM] = y


def _attention(q, kt, v, qn, kn, lam_p, subln_g, lam_init):
    b, _, t, _ = q.shape
    tq = t // N_BLK
    nt = t // TM
    kern = functools.partial(_attn_kernel, n_chunks=nt, lam_init=lam_init)
    return pl.pallas_call(
        kern,
        grid=(b, N_BLK),
        in_specs=[pl.BlockSpec((1, N_MAPS, tq, QK_PAD), lambda bb, i: (bb, 0, i, 0)),
                  pl.BlockSpec((1, nt, N_MAPS, QK_PAD, TM), lambda bb, i: (bb, 0, 0, 0, 0)),
                  pl.BlockSpec((1, nt, HEADS, TM, V_EXT), lambda bb, i: (bb, 0, 0, 0, 0)),
                  pl.BlockSpec((1, tq, N_MAPS), lambda bb, i: (bb, i, 0)),
                  pl.BlockSpec((1, t, N_MAPS), lambda bb, i: (bb, 0, 0)),
                  _resident((4, QK_DIM)), _resident((1, HEAD_DIM))],
        out_specs=pl.BlockSpec((1, tq, GW), lambda bb, i: (bb, i, 0)),
        out_shape=jax.ShapeDtypeStruct((b, t, GW), F32),
        scratch_shapes=[pltpu.VMEM((N_MAPS, tq, QK_PAD), BF16), pltpu.VMEM((N_MAPS, tq, V_EXT), F32),
                        pltpu.VMEM((2, tq, TM), F32), pltpu.VMEM((tq, N_MAPS), F32)],
        compiler_params=_cp(("parallel", "arbitrary")),
    )(q, kt, v, qn, kn, lam_p, subln_g.reshape(1, HEAD_DIM))


def _chunk_masks():
    i = _iota((CH, GW), 0)
    j = _iota((CH, GW), 1) & (CH - 1)
    eye = (i == j).astype(F32)
    incl = (j <= i, j >= i)
    strict = (j < i, j > i)
    r = _iota((CH, CH), 0)
    c = _iota((CH, CH), 1)
    cum = ((c <= r).astype(F32), (c >= r).astype(F32))
    bd = (_iota((GW, GW), 0) >> 6) == (_iota((GW, GW), 1) >> 6)
    return eye, incl, strict, cum, bd


def _bd(y, bd):
    return jnp.where(bd, _tile4(y), 0.0).astype(BF16)


def _fwd_tile(s, n_tiles):
    return jnp.where(s == 0, n_tiles - 1, s - 1)


def _rev_tile(s, n_tiles):
    return jnp.where(s == 0, n_tiles - 1, n_tiles - 1 - s)


DN_HALO = 8


def _dn_prep_kernel(pm_ref, pl_ref, pr_ref, pcs_ref, cw_ref, alog_ref, dtb_ref, q_out, k_out, v_out, bg_out, ext_ref):
    left_ok, right_ok = _halo_ok(pl.program_id(1), pl.num_programs(1))
    ext_ref[0:DN_HALO] = jnp.where(left_ok, pl_ref[0], 0.0)
    ext_ref[DN_HALO:DN_HALO + TM] = pm_ref[0]
    ext_ref[DN_HALO + TM:] = jnp.where(right_ok, pr_ref[0], 0.0)
    pad = DN_CONV // 2
    acc = jnp.zeros((TM, 3 * GW), F32)
    for j in range(DN_CONV):
        acc = acc + cw_ref[j:j + 1, :] * ext_ref[pl.ds(DN_HALO - pad + j, TM), :]
    qkv = _silu(acc)
    ones64 = _group_ones(GW, 6)

    def l2n(t):
        return t * lax.rsqrt(_dot_sel(t * t, ones64) + EPS)

    q_out[0] = l2n(qkv[:, :GW]) * (HEAD_DIM ** -0.5)
    k_out[0] = l2n(qkv[:, GW:2 * GW])
    v_out[0] = qkv[:, 2 * GW:]
    s = pcs_ref[0]
    col = _iota(s.shape, 1)
    gate = -jnp.exp(alog_ref[...]) * _softplus(s + dtb_ref[...])
    bg_out[0] = jnp.where(col < 2 * HEADS, _sigmoid(s), jnp.where(col < 4 * HEADS, gate, 0.0))


def _dn_prep(pc, pcs, conv_w, a_log, dt_bias):
    b, t, w = pc.shape
    nt = t // TM
    left, right = _halo_specs(w, DN_HALO, nt)
    row = lambda bb, i: (bb, i, 0)
    pad_vec = lambda a: jnp.zeros((1, LANES), F32).at[0, 2 * HEADS:4 * HEADS].set(a.reshape(-1))
    vec = pl.BlockSpec((1, LANES), lambda bb, i: (0, 0))
    return pl.pallas_call(
        _dn_prep_kernel,
        grid=(b, nt),
        in_specs=[pl.BlockSpec((1, TM, w), row), left, right, pl.BlockSpec((1, TM, LANES), row),
                  pl.BlockSpec((DN_CONV, w), lambda bb, i: (0, 0)), vec, vec],
        out_specs=[pl.BlockSpec((1, TM, GW), row)] * 3 + [pl.BlockSpec((1, TM, LANES), row)],
        out_shape=[jax.ShapeDtypeStruct((b, t, GW), F32)] * 3 + [jax.ShapeDtypeStruct((b, t, LANES), F32)],
        scratch_shapes=[pltpu.VMEM((TM + 2 * DN_HALO, w), F32)],
        compiler_params=_cp(("parallel", "arbitrary")),
    )(pc, pc, pc, pcs, conv_w, pad_vec(a_log), pad_vec(dt_bias))


def _dn_local_kernel(q_ref, k_ref, v_ref, bg_ref, u_out, w_out, qi_out, a_out, ket_out, ge_out):
    rr = _iota((TM, GW), 0)
    cc = _iota((TM, GW), 1)
    i_in = rr & (CH - 1)
    j_in = cc & (CH - 1)
    bd = (rr >> 6) == (cc >> 6)
    eye_t = (i_in == j_in).astype(F32)
    incl = (j_in <= i_in, j_in >= i_in)
    strict = (j_in < i_in, j_in > i_in)
    ones_bd = bd.astype(F32)
    eye_bf = (rr == cc).astype(BF16)
    head_of_lane = _iota((LANES, GW), 1) >> 6
    src = _iota((LANES, GW), 0)
    q = q_ref[0]
    k = k_ref[0]
    v = v_ref[0]
    bg = bg_ref[0]
    def parts(m, n):
        out = []
        for _ in range(n):
            hi = m.astype(BF16)
            out.append(hi)
            m = m - hi.astype(F32)
        return out

    ones_bf = ones_bd.astype(BF16)
    eye_f = eye_t
    beta, gcum, decay, kb, egc = [], [], [], [], []
    for d in range(2):
        sel_b = (src == d * HEADS + head_of_lane).astype(BF16)
        sel_g = (src == 2 * HEADS + d * HEADS + head_of_lane).astype(BF16)
        cum_bd = jnp.where(jnp.logical_and(bd, incl[d]), 1.0, 0.0).astype(BF16)
        g_exp = _dot(jnp.concatenate(parts(bg, 2), axis=0), sel_g)
        beta.append(_dot(bg.astype(BF16), sel_b))
        cs = _dot(cum_bd, jnp.concatenate([g_exp[:TM].astype(BF16), g_exp[TM:].astype(BF16)], axis=1))
        gcum.append(cs[:, :GW] + cs[:, GW:])
        gparts = parts(gcum[d] * eye_f, 3)
        tr = _dot(ones_bf, jnp.concatenate(gparts, axis=1))
        grow = tr[:, :GW] + tr[:, GW:2 * GW] + tr[:, 2 * GW:]
        decay.append(jnp.where(incl[d], jnp.exp(jnp.where(incl[d], gcum[d] - grow, 0.0)), 0.0))
        kb.append(k * beta[d])
        egc.append(jnp.exp(gcum[d]))
        qi_out[0, d] = (q * egc[d]).astype(BF16)
    pairs = [(c, d) for c in range(CPT) for d in range(2)]
    rows = [slice(c * CH, (c + 1) * CH) for c in range(CPT)]
    eye = eye_t[:CH]
    a = {}
    for c in range(CPT):
        r = rows[c]
        lhs = jnp.concatenate([kb[0][r], kb[1][r], q[r]], axis=0).astype(BF16)
        aq = _dot_nt(lhs, _bd(k[r], bd))
        for d in range(2):
            dec = decay[d][r]
            a[c, d] = jnp.where(strict[d][:CH], aq[d * CH:(d + 1) * CH] * dec, 0.0)
            a_out[0, d, r, :] = jnp.where(incl[d][:CH], aq[2 * CH:] * dec, 0.0).astype(BF16)
    t_inv = {cd: eye - a[cd] for cd in pairs}
    p = {cd: _dot(a[cd].astype(BF16), _bd(a[cd], bd)) for cd in pairs}
    for it in range(5):
        for cd in pairs:
            pbd = _bd(p[cd], bd)
            if it < 4:
                res = _dot(jnp.concatenate([t_inv[cd], p[cd]], axis=0).astype(BF16), pbd)
                t_inv[cd] = t_inv[cd] + res[:CH]
                p[cd] = res[CH:]
            else:
                t_inv[cd] = t_inv[cd] + _dot(t_inv[cd].astype(BF16), pbd)

    def split(m):
        hi = m.astype(BF16)
        return hi, (m - hi.astype(F32)).astype(BF16)

    for cd in pairs:
        x0 = t_inv[cd]
        ah, al = split(a[cd])
        xh, xl = split(x0)
        hx = _dot(jnp.concatenate([ah, al], axis=0), _bd(xh, bd))
        resid = eye - x0 - (hx[:CH] + hx[CH:] + _dot(ah, _bd(xl, bd)))
        t_inv[cd] = x0 + _dot(xh, _bd(resid, bd))
    for c, d in pairs:
        r = rows[c]
        tb = t_inv[c, d].astype(BF16)
        u_out[0, d, r, :] = _dot(tb, _bd(v[r] * beta[d][r], bd))
        w_out[0, d, r, :] = _dot(tb, _bd(kb[d][r] * egc[d][r], bd)).astype(BF16)
        last = (c + 1) * CH - 1 if d == 0 else c * CH
        gtot = gcum[d][last:last + 1]
        k_end = (k[r] * jnp.exp(gtot - gcum[d][r])).astype(BF16)
        ket_out[0, d, c] = _dot_nt(eye_bf, k_end).astype(BF16)
        ge_out[0, d, c] = jnp.exp(gtot)


def _dn_local(q, k, v, bg):
    b, t, _ = q.shape
    nt = t // TM
    row = lambda bb, i: (bb, i, 0)
    drow = pl.BlockSpec((1, 2, TM, GW), lambda bb, i: (bb, 0, i, 0))
    return pl.pallas_call(
        _dn_local_kernel,
        grid=(b, nt),
        in_specs=[pl.BlockSpec((1, TM, GW), row)] * 3 + [pl.BlockSpec((1, TM, LANES), row)],
        out_specs=[drow, drow, drow, drow,
                   pl.BlockSpec((1, 2, CPT, GW, CH), lambda bb, i: (bb, 0, i, 0, 0)),
                   pl.BlockSpec((1, 2, CPT, 1, GW), lambda bb, i: (bb, 0, i, 0, 0))],
        out_shape=[jax.ShapeDtypeStruct((b, 2, t, GW), F32)] + [jax.ShapeDtypeStruct((b, 2, t, GW), BF16)] * 3
        + [jax.ShapeDtypeStruct((b, 2, t // CH, GW, CH), BF16), jax.ShapeDtypeStruct((b, 2, t // CH, 1, GW), F32)],
        compiler_params=_cp(("parallel", "arbitrary")),
    )(q, k, v, bg)


def _dn_chunk(refs, o_ref, s_scr, d, cc, bd):
    u, w, qi, a, ket, ge = refs
    rows = slice(cc * CH, (cc + 1) * CH)
    s = s_scr[d]
    wq = _dot(jnp.concatenate([w[0, 0, rows, :], qi[0, 0, rows, :]], axis=0), s.astype(BF16))
    v_new = u[0, 0, rows, :] - wq[:CH]
    o_ref[0, rows, :] = wq[CH:] + _dot(a[0, 0, rows, :], _bd(v_new, bd))
    s_scr[d] = s * ge[0, 0, cc] + jnp.where(bd, _dot(ket[0, 0, cc], v_new.astype(BF16)), 0.0)


def _dir_specs(shape_tail, n_tiles, chunked):
    blk = (1, 1, CPT if chunked else TM) + shape_tail
    zeros = (0,) * len(shape_tail)
    fwd = pl.BlockSpec(blk, lambda b, s: (b, 0, _fwd_tile(s, n_tiles)) + zeros)
    rev = pl.BlockSpec(blk, lambda b, s: (b, 1, _rev_tile(s, n_tiles)) + zeros)
    return fwd, rev


GLA_QK = HEADS * GLA_K


def _gla_local_kernel(pd_ref, pdl_ref, w2_ref, b2_ref, qi_out, a_out, ke_out, vt_out, vb_out, de_out):
    _, incl, _, _, _ = _chunk_masks()
    rr = _iota((TM, TM), 0)
    cc = _iota((TM, TM), 1)
    same_chunk = (rr >> 6) == (cc >> 6)
    eye_bf = (rr == cc).astype(BF16)
    bdk = (_iota((GW, GLA_QK), 0) >> 6) == (_iota((GW, GLA_QK), 1) >> 5)
    z = _dot(pdl_ref[0], w2_ref[...], HI) + b2_ref[...]
    gk_all = -_softplus(-z) * (1.0 / GLA_TAU)
    bcs_all = []
    for d in range(2):
        cum_bd = jnp.where(jnp.logical_and(same_chunk, cc <= rr if d == 0 else cc >= rr), 1.0, 0.0).astype(BF16)
        gparts = _bf16_parts(gk_all[:, d * GLA_QK:(d + 1) * GLA_QK], 2)
        cs = _dot(cum_bd, jnp.concatenate(gparts, axis=1))
        bcs_all.append(cs[:, :GLA_QK] + cs[:, GLA_QK:])
    for c in range(CPT):
        rows = slice(c * CH, (c + 1) * CH)
        p = pd_ref[0, rows, :]
        q = p[:, :GLA_QK] * (GLA_K ** -0.5)
        k = p[:, GLA_QK:2 * GLA_QK]
        vb = p[:, 2 * GLA_QK:].astype(BF16)
        vb_out[0, rows, :] = vb
        vt_out[0, c] = _dot_nt(eye_bf, vb).astype(BF16)
        for d in range(2):
            bcs = bcs_all[d][rows]
            bend = bcs[CH - 1:CH] if d == 0 else bcs[0:1]
            q_in = (q * jnp.exp(bcs)).astype(BF16)
            kdec = jnp.where(bdk, _tile4(k * jnp.exp(-bcs)), 0.0).astype(BF16)
            a_out[0, d, rows, :] = jnp.where(incl[d], _dot_nt(q_in, kdec), 0.0).astype(BF16)
            qi_out[0, d, rows, :] = q_in
            ke_out[0, d, rows, :] = (k * jnp.exp(bend - bcs)).astype(BF16)
            de_out[0, d, c] = jnp.exp(bend)


def _gla_local(pd, pdl, w2bd, b2):
    b, t, w = pd.shape
    nt = t // TM
    row = lambda bb, i: (bb, i, 0)
    return pl.pallas_call(
        _gla_local_kernel,
        grid=(b, nt),
        in_specs=[pl.BlockSpec((1, TM, w), row), pl.BlockSpec((1, TM, LANES), row),
                  pl.BlockSpec((LANES, GW), lambda bb, i: (0, 0)), pl.BlockSpec((1, GW), lambda bb, i: (0, 0))],
        out_specs=[pl.BlockSpec((1, 2, TM, GLA_QK), lambda bb, i: (bb, 0, i, 0)),
                   pl.BlockSpec((1, 2, TM, GW), lambda bb, i: (bb, 0, i, 0)),
                   pl.BlockSpec((1, 2, TM, GLA_QK), lambda bb, i: (bb, 0, i, 0)),
                   pl.BlockSpec((1, CPT, GW, CH), lambda bb, i: (bb, i, 0, 0)),
                   pl.BlockSpec((1, TM, GW), row),
                   pl.BlockSpec((1, 2, CPT, 1, GLA_QK), lambda bb, i: (bb, 0, i, 0, 0))],
        out_shape=[jax.ShapeDtypeStruct((b, 2, t, GLA_QK), BF16), jax.ShapeDtypeStruct((b, 2, t, GW), BF16),
                   jax.ShapeDtypeStruct((b, 2, t, GLA_QK), BF16), jax.ShapeDtypeStruct((b, t // CH, GW, CH), BF16),
                   jax.ShapeDtypeStruct((b, t, GW), BF16), jax.ShapeDtypeStruct((b, 2, t // CH, 1, GLA_QK), F32)],
        compiler_params=_cp(("parallel", "arbitrary")),
    )(pd, pdl, w2bd, b2)


def _gla_chunk(refs, o_ref, s_scr, d, cc, bd, bdt):
    qi, a, ke, de, vt, v = refs
    rows = slice(cc * CH, (cc + 1) * CH)
    st = s_scr[d]
    vbd = jnp.where(bd, _tile4(v[0, rows, :]), jnp.zeros((), BF16))
    o_ref[0, rows, :] = _dot_nt(qi[0, 0, rows, :], st.astype(BF16)) + _dot(a[0, 0, rows, :], vbd)
    s_scr[d] = st * de[0, 0, cc] + jnp.where(bdt, _dot(vt[0, cc], ke[0, 0, rows, :]), 0.0)


def _scan_kernel(*refs):
    dn_in = (refs[0:6], refs[6:12])
    gla_in = (refs[12:18], refs[18:24])
    dn_out = refs[24:26]
    gla_out = refs[26:28]
    dn_s, gla_s = refs[28:30]

    @pl.when(pl.program_id(1) == 0)
    def _():
        dn_s[...] = jnp.zeros(dn_s.shape, F32)
        gla_s[...] = jnp.zeros(gla_s.shape, F32)

    bd = (_iota((GW, GW), 0) >> 6) == (_iota((GW, GW), 1) >> 6)
    bdt = (_iota((GW, GLA_QK), 0) >> 6) == (_iota((GW, GLA_QK), 1) >> 5)
    for c in range(CPT):
        for d in range(2):
            cc = c if d == 0 else CPT - 1 - c
            _dn_chunk(dn_in[d], dn_out[d], dn_s, d, cc, bd)
            _gla_chunk(gla_in[d], gla_out[d], gla_s, d, cc, bd, bdt)


def _scans(u, w, dqi, da, ket, ge, qi, a, ke, de, vt, vb):
    b, _, t, _ = u.shape
    nt = t // TM
    rowf, rowr = _dir_specs((GW,), nt, False)
    ketf, ketr = _dir_specs((GW, CH), nt, True)
    gef, ger = _dir_specs((1, GW), nt, True)
    qf, qr = _dir_specs((GLA_QK,), nt, False)
    df, dr = _dir_specs((1, GLA_QK), nt, True)
    vtf = pl.BlockSpec((1, CPT, GW, CH), lambda bb, s: (bb, _fwd_tile(s, nt), 0, 0))
    vtr = pl.BlockSpec((1, CPT, GW, CH), lambda bb, s: (bb, _rev_tile(s, nt), 0, 0))
    vf = pl.BlockSpec((1, TM, GW), lambda bb, s: (bb, _fwd_tile(s, nt), 0))
    vr = pl.BlockSpec((1, TM, GW), lambda bb, s: (bb, _rev_tile(s, nt), 0))
    dn_args = (u, w, dqi, da, ket, ge)
    gla_args = (qi, a, ke, de, vt, vb)
    return pl.pallas_call(
        _scan_kernel,
        grid=(b, nt),
        in_specs=[rowf, rowf, rowf, rowf, ketf, gef, rowr, rowr, rowr, rowr, ketr, ger,
                  qf, rowf, qf, df, vtf, vf, qr, rowr, qr, dr, vtr, vr],
        out_specs=[vf, vr, vf, vr],
        out_shape=[jax.ShapeDtypeStruct((b, t, GW), F32)] * 4,
        scratch_shapes=[pltpu.VMEM((2, GW, GW), F32), pltpu.VMEM((2, GW, GLA_QK), F32)],
        compiler_params=_cp(("parallel", "arbitrary")),
    )(*dn_args, *dn_args, *gla_args, *gla_args)


def _outproj_kernel(x_ref, ya_ref, yb_ref, cf_ref, cr_ref, df_ref, dr_ref, pg_ref, g1b_ref, g1c_ref, gc_ref, gd_ref,
                    w_ref, o_ref, *, lat_rows):
    ones64 = _group_ones(GW, 6)

    def fin(o, g, gate):
        ms = _dot_sel(o * o, ones64) * (1.0 / HEAD_DIM)
        return (o * lax.rsqrt(ms + EPS) * g * _silu(gate)).astype(BF16)

    pg = pg_ref[0]
    yc = fin(cf_ref[0] + cr_ref[0], gc_ref[...], pg[:, :GW])
    yd = fin(df_ref[0] + dr_ref[0], gd_ref[...], pg[:, GW:])
    res = (_dot(ya_ref[0].astype(BF16), w_ref[0:GW, :]) + _dot(yb_ref[0].astype(BF16), w_ref[GW:2 * GW, :])
           + _dot(yc, w_ref[2 * GW:3 * GW, :]) + _dot(yd, w_ref[3 * GW:, :]))
    g1 = jnp.where(_ctx_rows(res.shape[0], lat_rows), g1c_ref[0], g1b_ref[0])
    o_ref[0] = x_ref[0] + g1 * res


def _outproj(xs, ya, yb, ocf, ocr, odf, odr, pg, mod3, dn_g, gla_g, w_out_bf, n_batch, rows, n_blk):
    b, t, d = xs.shape
    reps = GW // HEAD_DIM
    tb = rows // n_blk
    row = lambda bb, i: (bb, i, 0)
    g256 = pl.BlockSpec((1, tb, GW), row)
    in_specs = ([pl.BlockSpec((1, tb, d), row), g256, g256, g256, g256, g256, g256, pl.BlockSpec((1, tb, 2 * GW), row)]
                + _mod_specs(2, n_batch) + [_resident((1, GW)), _resident((1, GW)), _resident((d, d))])
    args = (xs, ya, yb, ocf, ocr, odf, odr, pg, mod3, mod3, jnp.tile(dn_g, reps).reshape(1, GW),
            jnp.tile(gla_g, reps).reshape(1, GW), w_out_bf)
    kern = functools.partial(_outproj_kernel, lat_rows=t - TM)
    return _row_call(kern, n_batch, rows, args, in_specs, [d], n_blk=n_blk)[0]


FF_HALO = 8


def _ffn_kernel(xm_ref, xl_ref, xr_ref, shb_ref, shc_ref, scb_ref, scc_ref, g2b_ref, g2c_ref, ng_ref,
                wu_ref, cw_ref, wd_ref, o_ref, ext_ref, *, lat_rows):
    tb = xm_ref.shape[1]
    i = pl.program_id(1)
    rows_ext = _iota((tb + 2 * FF_HALO, 1), 0)
    left_ok = jnp.logical_and(i > 0, i * tb != lat_rows)
    right_ok = jnp.logical_and(i < pl.num_programs(1) - 1, (i + 1) * tb != lat_rows)
    keep = jnp.logical_and(jnp.logical_or(rows_ext >= FF_HALO, left_ok),
                           jnp.logical_or(rows_ext < FF_HALO + tb, right_ok))
    grow = i * tb - FF_HALO + rows_ext
    ctx_ext = grow >= lat_rows
    x = jnp.concatenate([xl_ref[0], xm_ref[0], xr_ref[0]], axis=0)
    y = x * lax.rsqrt(jnp.mean(x * x, axis=-1, keepdims=True) + EPS) * ng_ref[...]
    h = y * (1.0 + jnp.where(ctx_ext, scc_ref[0], scb_ref[0])) + jnp.where(ctx_ext, shc_ref[0], shb_ref[0])
    h = jnp.where(keep, h, 0.0).astype(BF16)
    inner_boundary = lat_rows % tb != 0
    if inner_boundary:
        row = grow[FF_HALO:FF_HALO + tb]
        m_prev = jnp.broadcast_to(jnp.where(row == lat_rows, 0.0, 1.0), (tb, FF_BLK))
        m_next = jnp.broadcast_to(jnp.where(row == lat_rows - 1, 0.0, 1.0), (tb, FF_BLK))
    acc = jnp.zeros((tb, D_MODEL), F32)
    for j in range(N_FF_BLK):
        def conv(col0, half):
            cols = slice(col0, col0 + FF_BLK)
            ext_ref[half] = _dot(h, wu_ref[:, cols])
            cw = cw_ref[:, cols]
            prev = ext_ref[half, pl.ds(FF_HALO - 1, tb), :]
            nxt = ext_ref[half, pl.ds(FF_HALO + 1, tb), :]
            if inner_boundary:
                prev = m_prev * prev
                nxt = m_next * nxt
            return cw[0:1] * prev + cw[1:2] * ext_ref[half, pl.ds(FF_HALO, tb), :] + cw[2:3] * nxt
        a = conv(j * FF_BLK, 0)
        g = conv(D_FF + j * FF_BLK, 1)
        acc = acc + _dot((_silu(g) * a).astype(BF16), wd_ref[j * FF_BLK:(j + 1) * FF_BLK, :])
    g2 = jnp.where(ctx_ext[FF_HALO:FF_HALO + tb], g2c_ref[0], g2b_ref[0])
    o_ref[0] = xm_ref[0] + g2 * acc


def _ffn(x1, mod3, norm_g, w_up, cw, w_down, n_batch, lat_rows, rows, n_blk):
    b, t, d = x1.shape
    tb = rows // n_blk
    per = tb // FF_HALO
    last = t // FF_HALO - 1
    in_specs = ([pl.BlockSpec((1, tb, d), lambda bb, i: (bb, i, 0)),
                 pl.BlockSpec((1, FF_HALO, d), lambda bb, i: (bb, jnp.maximum(i * per - 1, 0), 0)),
                 pl.BlockSpec((1, FF_HALO, d), lambda bb, i: (bb, jnp.minimum((i + 1) * per, last), 0))]
                + _mod_specs(3, n_batch) + _mod_specs(4, n_batch) + _mod_specs(5, n_batch)
                + [_resident((1, d)), _resident(w_up.shape), _resident(cw.shape), _resident(w_down.shape)])
    args = (x1, x1, x1, mod3, mod3, mod3, mod3, mod3, mod3, norm_g.reshape(1, d), w_up, cw, w_down)
    kern = functools.partial(_ffn_kernel, lat_rows=lat_rows)
    return _row_call(kern, n_batch, rows, args, in_specs, [d],
                     scratch=[pltpu.VMEM((2, tb + 2 * FF_HALO, FF_BLK), F32)], n_blk=n_blk)[0]


def _rope_tables(seq, ctx_len):
    rows = seq // GRID_W
    row = jnp.repeat(jnp.arange(rows, dtype=F32), GRID_W)
    col = jnp.tile(jnp.arange(GRID_W, dtype=F32), rows)
    nf = QK_DIM // 4
    inv = ROPE_THETA ** (-jnp.arange(nf, dtype=F32) / nf)
    ang = jnp.concatenate([row[:, None] * inv, col[:, None] * inv], axis=-1)
    cos = jnp.concatenate([jnp.cos(ang), jnp.ones((ctx_len, QK_DIM // 2), F32)], axis=0)
    sin = jnp.concatenate([jnp.sin(ang), jnp.zeros((ctx_len, QK_DIM // 2), F32)], axis=0)
    reps = GW // QK_DIM
    return (jnp.tile(jnp.concatenate([cos, cos], axis=-1), (1, reps)),
            jnp.tile(jnp.concatenate([-sin, sin], axis=-1), (1, reps)))


def _regroup_w_in(w):
    d = w.shape[0]
    z = lambda n: jnp.zeros((d, n), w.dtype)
    return jnp.concatenate([w[:, :2048], w[:, 2048:2064], z(LANES - 16), w[:, 2320:2832], w[:, 2832:2864],
                            z(LANES - 32), w[:, 2064:2320], w[:, 2864:3120]], axis=1).astype(BF16)


def _gla_w2_blockdiag(w2):
    out = jnp.zeros((LANES, GW), F32)
    out = out.at[0:GLA_RANK, 0:GLA_QK].set(w2[0])
    return out.at[GLA_RANK:2 * GLA_RANK, GLA_QK:].set(w2[1])


def _layer(xs, mod3, lp, cos_t, sin_t, layer_idx, last, n_batch):
    b, t, d = xs.shape
    nt = t // TM
    pa, pb, pc, pcs, pd, pdl, pg = _inproj(xs, mod3, lp["norm1_g"], _regroup_w_in(lp["w_in"]), n_batch)

    ya = _conv_module(pa, lp["cm_conv_w"], lp["cm_conv_b"], lp["cm_ln_g"], lp["cm_ln_b"])

    lam_init = 0.8 - 0.6 * math.exp(-0.3 * layer_idx)
    q, kt, v, qn, kn = _attn_prep(pb, lp["da_qnorm_g"], lp["da_knorm_g"], cos_t, sin_t)
    yb = _attention(q, kt, v, qn, kn, lp["da_lambda"], lp["da_subln_g"], lam_init)

    dq, dk, dv, bg = _dn_prep(pc, pcs, lp["dn_conv_w"], lp["dn_a_log"], lp["dn_dt_bias"])
    qi, a, ke, vt, vb, de = _gla_local(pd, pdl, _gla_w2_blockdiag(lp["gla_w2"]), lp["gla_b2"].reshape(1, GW))
    ocf, ocr, odf, odr = _scans(*_dn_local(dq, dk, dv, bg), qi, a, ke, de, vt, vb)

    lat_rows = t - TM
    rows = lat_rows if last else t
    x1 = _outproj(xs, ya, yb, ocf, ocr, odf, odr, pg, mod3, lp["dn_onorm_g"], lp["gla_onorm_g"],
                  lp["w_out"].astype(BF16), n_batch, rows, N_BLK_FFN)
    return _ffn(x1, mod3, lp["norm2_g"], lp["ffn_w_up"].astype(BF16), lp["ffn_conv_w"],
                lp["ffn_w_down"].astype(BF16), n_batch, lat_rows, rows, N_BLK_FFN)


def kernel(x, c, ctx, c_ctx, w_mod, b_mod, norm1_g, norm2_g, w_in, w_out, cm_conv_w, cm_conv_b, cm_ln_g, cm_ln_b, da_qnorm_g, da_knorm_g, da_lambda, da_subln_g, dn_conv_w, dn_a_log, dn_dt_bias, dn_onorm_g, gla_w2, gla_b2, gla_onorm_g, ffn_w_up, ffn_conv_w, ffn_w_down):
    n_batch, seq, d = x.shape
    ctx_len = ctx.shape[1]
    assert ctx_len == TM and seq % TM == 0 and (seq + ctx_len) % (8 * N_BLK) == 0 and d == D_MODEL
    depth = w_mod.shape[0]
    cos_t, sin_t = _rope_tables(seq, ctx_len)
    xs = jnp.concatenate([x, ctx], axis=1)
    mod_rows = 16
    c_rows = jnp.zeros((mod_rows, d), F32).at[:n_batch].set(c).at[n_batch].set(c_ctx)
    params = dict(w_mod=w_mod, b_mod=b_mod, norm1_g=norm1_g, norm2_g=norm2_g, w_in=w_in, w_out=w_out,
                  cm_conv_w=cm_conv_w, cm_conv_b=cm_conv_b, cm_ln_g=cm_ln_g, cm_ln_b=cm_ln_b,
                  da_qnorm_g=da_qnorm_g, da_knorm_g=da_knorm_g, da_lambda=da_lambda, da_subln_g=da_subln_g,
                  dn_conv_w=dn_conv_w, dn_a_log=dn_a_log, dn_dt_bias=dn_dt_bias, dn_onorm_g=dn_onorm_g,
                  gla_w2=gla_w2, gla_b2=gla_b2, gla_onorm_g=gla_onorm_g,
                  ffn_w_up=ffn_w_up, ffn_conv_w=ffn_conv_w, ffn_w_down=ffn_w_down)
    for l in range(depth):
        lp = {k: v[l] for k, v in params.items()}
        mod3 = _modulation(c_rows, lp["w_mod"], lp["b_mod"]).reshape(mod_rows * 6, 1, d)
        xs = _layer(xs, mod3, lp, cos_t, sin_t, l, l == depth - 1, n_batch)
    return xs
```

```python
import functools
import math
from typing import Any, NamedTuple

import jax
import jax.numpy as jnp
from jax import lax
from jax.experimental import pallas as pl
from jax.experimental.pallas import tpu as pltpu

F32 = jnp.float32
BF16 = jnp.bfloat16
HI = lax.Precision.HIGHEST
EPS = 1e-6

D_MODEL = 1024
GRID_W = 64
HEADS = 4
HEAD_DIM = 64
GW = 256
QK_DIM = 32
GLA_K = 32
GLA_RANK = 16
GLA_TAU = 16.0
CM_KERNEL = 31
DN_CONV = 5
ROPE_THETA = 10000.0
CH = 64
TM = 256
CPT = TM // CH
D_FF = 2816
FF_BLK = 256
N_FF_BLK = D_FF // FF_BLK
LANES = 128

IN_GROUPS = (("pa", 512), ("pb", 768), ("pc", 768), ("pcs", LANES), ("pd", 512), ("pdl", LANES), ("pg", 512))
IN_COLS_PAD = sum(w for _, w in IN_GROUPS)

VMEM_LIMIT = 56 * 1024 * 1024


def _cp(sem):
    return pltpu.CompilerParams(dimension_semantics=sem, vmem_limit_bytes=VMEM_LIMIT)


def _dot(a, b, prec=None):
    return jnp.dot(a, b, preferred_element_type=F32, precision=prec)


def _dot_nt(a, b, prec=None):
    return lax.dot_general(a, b, (((1,), (1,)), ((), ())), preferred_element_type=F32, precision=prec)


def _sigmoid(x):
    return 1.0 / (1.0 + jnp.exp(-x))


def _silu(x):
    return x * _sigmoid(x)


def _softplus(x):
    return jnp.maximum(x, 0.0) + jnp.log(1.0 + jnp.exp(-jnp.abs(x)))


def _iota(shape, dim):
    return lax.broadcasted_iota(jnp.int32, shape, dim)


def _group_ones(n, shift):
    return ((_iota((n, n), 0) >> shift) == (_iota((n, n), 1) >> shift)).astype(BF16)


def _bf16_parts(m, n):
    out = []
    for _ in range(n):
        hi = m.astype(BF16)
        out.append(hi)
        m = m - hi.astype(F32)
    return out


def _dot_sel(x, sel, n_parts=2):
    rows = x.shape[0]
    r = _dot(jnp.concatenate(_bf16_parts(x, n_parts), axis=0), sel)
    out = r[:rows]
    for k in range(1, n_parts):
        out = out + r[k * rows:(k + 1) * rows]
    return out


def _tile4(y):
    return jnp.concatenate([y, y, y, y], axis=0)


def _mod_kernel(c_ref, w_ref, b_ref, o_ref):
    o_ref[...] = _dot(_silu(c_ref[...]), w_ref[...], HI) + b_ref[...]


def _modulation(c_rows, w_mod, b_mod):
    r, d = c_rows.shape
    n = w_mod.shape[1] // d
    return pl.pallas_call(
        _mod_kernel,
        grid=(n,),
        in_specs=[pl.BlockSpec((r, d), lambda j: (0, 0)),
                  pl.BlockSpec((d, d), lambda j: (0, j)),
                  pl.BlockSpec((1, d), lambda j: (0, j))],
        out_specs=pl.BlockSpec((r, d), lambda j: (0, j)),
        out_shape=jax.ShapeDtypeStruct((r, n * d), F32),
        compiler_params=_cp(("arbitrary",)),
    )(c_rows, w_mod, b_mod.reshape(1, -1))


N_BLK = 4
N_BLK_FFN = 8


def _mod_specs(k, n_batch):
    return [pl.BlockSpec((1, 1, D_MODEL), lambda b, i: (b * 6 + k, 0, 0)),
            pl.BlockSpec((1, 1, D_MODEL), lambda b, i: (n_batch * 6 + k, 0, 0))]


def _ctx_rows(tb, lat_rows):
    return pl.program_id(1) * tb + _iota((tb, 1), 0) >= lat_rows


def _resident(shape):
    zeros = (0,) * len(shape)
    return pl.BlockSpec(shape, lambda b, i: zeros, pipeline_mode=pl.Buffered(1))


def _row_call(kern, n_batch, rows, args, in_specs, out_widths, scratch=(), n_blk=N_BLK):
    tb = rows // n_blk
    return pl.pallas_call(
        kern,
        grid=(n_batch, n_blk),
        in_specs=in_specs,
        out_specs=[pl.BlockSpec((1, tb, w), lambda bb, i: (bb, i, 0)) for w in out_widths],
        out_shape=[jax.ShapeDtypeStruct((n_batch, rows, w), F32) for w in out_widths],
        scratch_shapes=list(scratch),
        compiler_params=_cp(("parallel", "arbitrary")),
    )(*args)


def _inproj_kernel(x_ref, shb_ref, shc_ref, scb_ref, scc_ref, g_ref, w_ref, *outs, lat_rows):
    x = x_ref[0]
    ctx = _ctx_rows(x.shape[0], lat_rows)
    y = x * lax.rsqrt(jnp.mean(x * x, axis=-1, keepdims=True) + EPS) * g_ref[...]
    h = y * (1.0 + jnp.where(ctx, scc_ref[0], scb_ref[0])) + jnp.where(ctx, shc_ref[0], shb_ref[0])
    p = _dot(h.astype(BF16), w_ref[...])
    off = 0
    for o_ref, (_, width) in zip(outs, IN_GROUPS):
        o_ref[0] = p[:, off:off + width]
        off += width


def _inproj(xs, mod3, norm_g, w_in_r, n_batch):
    b, t, d = xs.shape
    in_specs = ([pl.BlockSpec((1, t // N_BLK, d), lambda bb, i: (bb, i, 0))] + _mod_specs(0, n_batch)
                + _mod_specs(1, n_batch) + [_resident((1, d)), _resident((d, IN_COLS_PAD))])
    kern = functools.partial(_inproj_kernel, lat_rows=t - TM)
    return _row_call(kern, n_batch, t, (xs, mod3, mod3, mod3, mod3, norm_g.reshape(1, d), w_in_r), in_specs,
                     [w for _, w in IN_GROUPS])


class _Part(NamedTuple):
    body: Any
    args: Any
    in_specs: Any
    out_specs: Any
    out_shapes: Any
    scratch: Any


def _fused_call(parts, grid):
    n_in = [len(p.args) for p in parts]
    n_out = [len(p.out_specs) for p in parts]
    n_scr = [len(p.scratch) for p in parts]

    def kern(*refs):
        ins, outs, scr = refs[:sum(n_in)], refs[sum(n_in):sum(n_in) + sum(n_out)], refs[sum(n_in) + sum(n_out):]
        i = o = s = 0
        for p, ni, no, ns in zip(parts, n_in, n_out, n_scr):
            p.body(*ins[i:i + ni], *outs[o:o + no], *scr[s:s + ns])
            i, o, s = i + ni, o + no, s + ns

    res = pl.pallas_call(
        kern,
        grid=grid,
        in_specs=[sp for p in parts for sp in p.in_specs],
        out_specs=[sp for p in parts for sp in p.out_specs],
        out_shape=[sh for p in parts for sh in p.out_shapes],
        scratch_shapes=[sc for p in parts for sc in p.scratch],
        compiler_params=_cp(("parallel", "arbitrary")),
    )(*[a for p in parts for a in p.args])
    out, o = [], 0
    for no in n_out:
        out.append(res[o:o + no])
        o += no
    return out


def _halo_specs(width, halo, n_tiles):
    per = TM // halo
    left = pl.BlockSpec((1, halo, width), lambda b, i: (b, jnp.maximum(i * per - 1, 0), 0))
    right = pl.BlockSpec((1, halo, width), lambda b, i: (b, jnp.minimum((i + 1) * per, n_tiles * per - 1), 0))
    return left, right


def _halo_ok(i, n_tiles):
    return jnp.logical_and(i >= 1, i < n_tiles - 1), i < n_tiles - 2


CM_HALO = 16


def _convmod_kernel(pm_ref, pl_ref, pr_ref, cw_ref, cb_ref, lg_ref, lb_ref, o_ref, ext_ref):
    left_ok, right_ok = _halo_ok(pl.program_id(1), pl.num_programs(1))

    def glu(p):
        return p[:, :GW] * _sigmoid(p[:, GW:])

    ext_ref[0:CM_HALO] = jnp.where(left_ok, glu(pl_ref[0]), 0.0)
    ext_ref[CM_HALO:CM_HALO + TM] = glu(pm_ref[0])
    ext_ref[CM_HALO + TM:] = jnp.where(right_ok, glu(pr_ref[0]), 0.0)
    pad = CM_KERNEL // 2
    sub = 8
    y = cb_ref[...]
    for r in range(sub):
        acc = None
        for j in range(CM_KERNEL):
            off = CM_HALO - pad + j
            if off % sub == r:
                term = cw_ref[j:j + 1, :] * ext_ref[pl.ds(off - r, TM + sub), :]
                acc = term if acc is None else acc + term
        if acc is not None:
            y = y + acc[r:r + TM]
    mu = jnp.mean(y, axis=-1, keepdims=True)
    yc = y - mu
    var = jnp.mean(yc * yc, axis=-1, keepdims=True)
    o_ref[0] = _silu(yc * lax.rsqrt(var + EPS) * lg_ref[...] + lb_ref[...])


def _conv_module(pa, conv_w, conv_b, ln_g, ln_b):
    b, t, w = pa.shape
    nt = t // TM
    left, right = _halo_specs(w, CM_HALO, nt)
    vec = pl.BlockSpec((1, GW), lambda bb, i: (0, 0))
    return _Part(
        _convmod_kernel,
        (pa, pa, pa, conv_w, conv_b.reshape(1, GW), ln_g.reshape(1, GW), ln_b.reshape(1, GW)),
        [pl.BlockSpec((1, TM, w), lambda bb, i: (bb, i, 0)), left, right,
         pl.BlockSpec((CM_KERNEL, GW), lambda bb, i: (0, 0)), vec, vec, vec],
        [pl.BlockSpec((1, TM, GW), lambda bb, i: (bb, i, 0))],
        [jax.ShapeDtypeStruct((b, t, GW), F32)],
        [pltpu.VMEM((TM + 2 * CM_HALO, GW), F32)])


N_MAPS = 2 * HEADS
QK_PAD = 2 * QK_DIM
V_EXT = 2 * HEAD_DIM
MASK_BIG = 8192.0
SHIFT_MAX = 40.0


def _attn_prep_kernel(pb_ref, qg_ref, kg_ref, cos_ref, sin_ref, q_out, kt_out, v_out, qn_out, kn_out):
    p = pb_ref[0]
    ones32 = _group_ones(GW, 5)
    first = (_iota((TM, GW), 1) & (QK_DIM - 1)) < QK_DIM // 2
    cos = cos_ref[...]
    sin = sin_ref[...]

    def norm_rope(t, g):
        ms = _dot_sel(t * t, ones32) * (1.0 / QK_DIM)
        tn = t * lax.rsqrt(ms + EPS) * g
        partner = jnp.where(first, pltpu.roll(tn, GW - QK_DIM // 2, 1), pltpu.roll(tn, QK_DIM // 2, 1))
        return tn * cos + partner * sin

    qf = norm_rope(p[:, :GW], qg_ref[...]) * (QK_DIM ** -0.5)
    kf = norm_rope(p[:, GW:2 * GW], kg_ref[...])
    map_sel = ((_iota((GW, N_MAPS), 0) >> 5) == _iota((GW, N_MAPS), 1)).astype(BF16)
    qn_out[0] = _dot_sel(qf * qf, map_sel)
    kn_out[0] = _dot_sel(kf * kf, map_sel)
    q = qf.astype(BF16)
    kt = kf.T.astype(BF16)
    is_ctx = pl.program_id(1) == pl.num_programs(1) - 1
    k_row = _iota((QK_DIM, TM), 0)
    k_tail = jnp.where(k_row == 0, 1.0, jnp.where(jnp.logical_and(k_row == 1, jnp.logical_not(is_ctx)), 1.0, 0.0))
    k_tail = k_tail.astype(BF16)
    q_tail = jnp.where(jnp.logical_and(_iota((TM, QK_DIM), 1) == 1, is_ctx), -MASK_BIG, 0.0).astype(BF16)
    for g in range(N_MAPS):
        q_out[0, g, :, 0:QK_DIM] = q[:, g * QK_DIM:(g + 1) * QK_DIM]
        q_out[0, g, :, QK_DIM:] = q_tail
        kt_out[0, 0, g, 0:QK_DIM, :] = kt[g * QK_DIM:(g + 1) * QK_DIM, :]
        kt_out[0, 0, g, QK_DIM:, :] = k_tail
    v = p[:, 2 * GW:].astype(BF16)
    ones = jnp.ones((TM, HEAD_DIM), BF16)
    for h in range(HEADS):
        v_out[0, 0, h, :, 0:HEAD_DIM] = v[:, h * HEAD_DIM:(h + 1) * HEAD_DIM]
        v_out[0, 0, h, :, HEAD_DIM:] = ones


def _attn_prep(pb, qn_g, kn_g, cos_t, sin_t):
    b, t, w = pb.shape
    nt = t // TM
    vec = pl.BlockSpec((1, GW), lambda bb, i: (0, 0))
    tab = pl.BlockSpec((TM, GW), lambda bb, i: (i, 0))
    reps = GW // QK_DIM
    n_spec = pl.BlockSpec((1, TM, N_MAPS), lambda bb, i: (bb, i, 0))
    n_shape = jax.ShapeDtypeStruct((b, t, N_MAPS), F32)
    return pl.pallas_call(
        _attn_prep_kernel,
        grid=(b, nt),
        in_specs=[pl.BlockSpec((1, TM, w), lambda bb, i: (bb, i, 0)), vec, vec, tab, tab],
        out_specs=[pl.BlockSpec((1, N_MAPS, TM, QK_PAD), lambda bb, i: (bb, 0, i, 0)),
                   pl.BlockSpec((1, 1, N_MAPS, QK_PAD, TM), lambda bb, i: (bb, i, 0, 0, 0)),
                   pl.BlockSpec((1, 1, HEADS, TM, V_EXT), lambda bb, i: (bb, i, 0, 0, 0)), n_spec, n_spec],
        out_shape=[jax.ShapeDtypeStruct((b, N_MAPS, t, QK_PAD), BF16),
                   jax.ShapeDtypeStruct((b, nt, N_MAPS, QK_PAD, TM), BF16),
                   jax.ShapeDtypeStruct((b, nt, HEADS, TM, V_EXT), BF16), n_shape, n_shape],
        compiler_params=_cp(("parallel", "arbitrary")),
    )(pb, jnp.tile(qn_g, reps).reshape(1, GW), jnp.tile(kn_g, reps).reshape(1, GW), cos_t, sin_t)


def _attn_kernel(q_ref, kt_ref, v_ref, qn_ref, kn_ref, lam_ref, sg_ref, o_ref, qa_scr, acc_scr, s_scr, m_scr,
                 *, n_chunks, lam_init):
    tq = o_ref.shape[1]
    acc_scr[...] = jnp.zeros(acc_scr.shape, F32)
    k2 = jnp.max(kn_ref[0], axis=0, keepdims=True)
    bound = jnp.sqrt(qn_ref[0] * k2)
    safe = jnp.max(bound) <= SHIFT_MAX
    shift = jnp.where(safe, bound, 0.0)
    shift_lane = _iota((tq, QK_PAD), 1) == QK_DIM
    for g in range(N_MAPS):
        qa_scr[g] = jnp.where(shift_lane, (-shift[:, g:g + 1]).astype(BF16), q_ref[0, g])

    @pl.when(safe)
    def _():
        s_scr[0] = _dot(qa_scr[0], kt_ref[0, 0, 0])

        def body(c, carry):
            nxt = jnp.minimum(c + 1, n_chunks - 1)
            for g in range(N_MAPS):
                if g + 1 < N_MAPS:
                    s_next = _dot(qa_scr[g + 1], kt_ref[0, c, g + 1])
                else:
                    s_next = _dot(qa_scr[0], kt_ref[0, nxt, 0])
                p = jnp.exp(s_scr[g & 1]).astype(BF16)
                acc_scr[g] += _dot(p, v_ref[0, c, g // 2])
                s_scr[(g + 1) & 1] = s_next
            return carry

        lax.fori_loop(0, n_chunks, body, 0)

    @pl.when(jnp.logical_not(safe))
    def _():
        m_scr[...] = jnp.full(m_scr.shape, -jnp.inf, F32)

        def body(c, carry):
            for g in range(N_MAPS):
                s = _dot(qa_scr[g], kt_ref[0, c, g])
                m_old = m_scr[:, g:g + 1]
                m_new = jnp.maximum(m_old, jnp.max(s, axis=-1, keepdims=True))
                p = jnp.exp(s - m_new).astype(BF16)
                acc_scr[g] = jnp.exp(m_old - m_new) * acc_scr[g] + _dot(p, v_ref[0, c, g // 2])
                m_scr[:, g:g + 1] = m_new
            return carry

        lax.fori_loop(0, n_chunks, body, 0)

    lp = lam_ref[...]
    lam = (jnp.exp(jnp.sum(lp[0:1] * lp[1:2], axis=-1, keepdims=True))
           - jnp.exp(jnp.sum(lp[2:3] * lp[3:4], axis=-1, keepdims=True)) + lam_init)
    for h in range(HEADS):
        a0 = acc_scr[2 * h]
        a1 = acc_scr[2 * h + 1]
        o = (a0[:, :HEAD_DIM] / a0[:, HEAD_DIM:HEAD_DIM + 1]
             - lam * (a1[:, :HEAD_DIM] / a1[:, HEAD_DIM:HEAD_DIM + 1]))
        y = o * lax.rsqrt(jnp.mean(o * o, axis=-1, keepdims=True) + EPS) * sg_ref[...] * (1.0 - lam_init)
        o_ref[0, :, h * HEAD_DIM:(h + 1) * HEAD_DIM] = y


def _attention(q, kt, v, qn, kn, lam_p, subln_g, lam_init):
    b, _, t, _ = q.shape
    tq = t // N_BLK
    nt = t // TM
    kern = functools.partial(_attn_kernel, n_chunks=nt, lam_init=lam_init)
    return pl.pallas_call(
        kern,
        grid=(b, N_BLK),
        in_specs=[pl.BlockSpec((1, N_MAPS, tq, QK_PAD), lambda bb, i: (bb, 0, i, 0)),
                  pl.BlockSpec((1, nt, N_MAPS, QK_PAD, TM), lambda bb, i: (bb, 0, 0, 0, 0)),
                  pl.BlockSpec((1, nt, HEADS, TM, V_EXT), lambda bb, i: (bb, 0, 0, 0, 0)),
                  pl.BlockSpec((1, tq, N_MAPS), lambda bb, i: (bb, i, 0)),
                  pl.BlockSpec((1, t, N_MAPS), lambda bb, i: (bb, 0, 0)),
                  _resident((4, QK_DIM)), _resident((1, HEAD_DIM))],
        out_specs=pl.BlockSpec((1, tq, GW), lambda bb, i: (bb, i, 0)),
        out_shape=jax.ShapeDtypeStruct((b, t, GW), F32),
        scratch_shapes=[pltpu.VMEM((N_MAPS, tq, QK_PAD), BF16), pltpu.VMEM((N_MAPS, tq, V_EXT), F32),
                        pltpu.VMEM((2, tq, TM), F32), pltpu.VMEM((tq, N_MAPS), F32)],
        compiler_params=_cp(("parallel", "arbitrary")),
    )(q, kt, v, qn, kn, lam_p, subln_g.reshape(1, HEAD_DIM))


def _chunk_masks():
    i = _iota((CH, GW), 0)
    j = _iota((CH, GW), 1) & (CH - 1)
    eye = (i == j).astype(F32)
    incl = (j <= i, j >= i)
    strict = (j < i, j > i)
    r = _iota((CH, CH), 0)
    c = _iota((CH, CH), 1)
    cum = ((c <= r).astype(F32), (c >= r).astype(F32))
    bd = (_iota((GW, GW), 0) >> 6) == (_iota((GW, GW), 1) >> 6)
    return eye, incl, strict, cum, bd


def _bd(y, bd):
    return jnp.where(bd, _tile4(y), 0.0).astype(BF16)


def _fwd_tile(s, n_tiles):
    return jnp.where(s == 0, n_tiles - 1, s - 1)


def _rev_tile(s, n_tiles):
    return jnp.where(s == 0, n_tiles - 1, n_tiles - 1 - s)


DN_HALO = 8


def _dn_prep(pm_ref, pl_ref, pr_ref, pcs_ref, cw_ref, alog_ref, dtb_ref, ext_ref):
    left_ok, right_ok = _halo_ok(pl.program_id(1), pl.num_programs(1))
    ext_ref[0:DN_HALO] = jnp.where(left_ok, pl_ref[0], 0.0)
    ext_ref[DN_HALO:DN_HALO + TM] = pm_ref[0]
    ext_ref[DN_HALO + TM:] = jnp.where(right_ok, pr_ref[0], 0.0)
    pad = DN_CONV // 2
    acc = jnp.zeros((TM, 3 * GW), F32)
    for j in range(DN_CONV):
        acc = acc + cw_ref[j:j + 1, :] * ext_ref[pl.ds(DN_HALO - pad + j, TM), :]
    qkv = _silu(acc)
    ones64 = _group_ones(GW, 6)

    def l2n(t):
        return t * lax.rsqrt(_dot_sel(t * t, ones64) + EPS)

    s = pcs_ref[0]
    col = _iota(s.shape, 1)
    gate = -jnp.exp(alog_ref[...]) * _softplus(s + dtb_ref[...])
    bg = jnp.where(col < 2 * HEADS, _sigmoid(s), jnp.where(col < 4 * HEADS, gate, 0.0))
    return l2n(qkv[:, :GW]) * (HEAD_DIM ** -0.5), l2n(qkv[:, GW:2 * GW]), qkv[:, 2 * GW:], bg


def _dn_local_kernel(pm_ref, pl_ref, pr_ref, pcs_ref, cw_ref, alog_ref, dtb_ref,
                     u_out, w_out, qi_out, a_out, ket_out, ge_out, ext_ref):
    rr = _iota((TM, GW), 0)
    cc = _iota((TM, GW), 1)
    i_in = rr & (CH - 1)
    j_in = cc & (CH - 1)
    bd = (rr >> 6) == (cc >> 6)
    eye_t = (i_in == j_in).astype(F32)
    incl = (j_in <= i_in, j_in >= i_in)
    strict = (j_in < i_in, j_in > i_in)
    ones_bd = bd.astype(F32)
    eye_bf = (rr == cc).astype(BF16)
    head_of_lane = _iota((LANES, GW), 1) >> 6
    src = _iota((LANES, GW), 0)
    q, k, v, bg = _dn_prep(pm_ref, pl_ref, pr_ref, pcs_ref, cw_ref, alog_ref, dtb_ref, ext_ref)
    def parts(m, n):
        out = []
        for _ in range(n):
            hi = m.astype(BF16)
            out.append(hi)
            m = m - hi.astype(F32)
        return out

    ones_bf = ones_bd.astype(BF16)
    eye_f = eye_t
    beta, gcum, decay, kb, egc = [], [], [], [], []
    for d in range(2):
        sel_b = (src == d * HEADS + head_of_lane).astype(BF16)
        sel_g = (src == 2 * HEADS + d * HEADS + head_of_lane).astype(BF16)
        cum_bd = jnp.where(jnp.logical_and(bd, incl[d]), 1.0, 0.0).astype(BF16)
        g_exp = _dot(jnp.concatenate(parts(bg, 2), axis=0), sel_g)
        beta.append(_dot(bg.astype(BF16), sel_b))
        cs = _dot(cum_bd, jnp.concatenate([g_exp[:TM].astype(BF16), g_exp[TM:].astype(BF16)], axis=1))
        gcum.append(cs[:, :GW] + cs[:, GW:])
        gparts = parts(gcum[d] * eye_f, 3)
        tr = _dot(ones_bf, jnp.concatenate(gparts, axis=1))
        grow = tr[:, :GW] + tr[:, GW:2 * GW] + tr[:, 2 * GW:]
        decay.append(jnp.where(incl[d], jnp.exp(jnp.where(incl[d], gcum[d] - grow, 0.0)), 0.0))
        kb.append(k * beta[d])
        egc.append(jnp.exp(gcum[d]))
        qi_out[0, d] = (q * egc[d]).astype(BF16)
    pairs = [(c, d) for c in range(CPT) for d in range(2)]
    rows = [slice(c * CH, (c + 1) * CH) for c in range(CPT)]
    eye = eye_t[:CH]
    a = {}
    for c in range(CPT):
        r = rows[c]
        lhs = jnp.concatenate([kb[0][r], kb[1][r], q[r]], axis=0).astype(BF16)
        aq = _dot_nt(lhs, _bd(k[r], bd))
        for d in range(2):
            dec = decay[d][r]
            a[c, d] = jnp.where(strict[d][:CH], aq[d * CH:(d + 1) * CH] * dec, 0.0)
            a_out[0, d, r, :] = jnp.where(incl[d][:CH], aq[2 * CH:] * dec, 0.0).astype(BF16)
    t_inv = {cd: eye - a[cd] for cd in pairs}
    p = {cd: _dot(a[cd].astype(BF16), _bd(a[cd], bd)) for cd in pairs}
    for it in range(5):
        for cd in pairs:
            pbd = _bd(p[cd], bd)
            if it < 4:
                res = _dot(jnp.concatenate([t_inv[cd], p[cd]], axis=0).astype(BF16), pbd)
                t_inv[cd] = t_inv[cd] + res[:CH]
                p[cd] = res[CH:]
            else:
                t_inv[cd] = t_inv[cd] + _dot(t_inv[cd].astype(BF16), pbd)

    def split(m):
        hi = m.astype(BF16)
        return hi, (m - hi.astype(F32)).astype(BF16)

    for cd in pairs:
        x0 = t_inv[cd]
        ah, al = split(a[cd])
        xh, xl = split(x0)
        hx = _dot(jnp.concatenate([ah, al], axis=0), _bd(xh, bd))
        resid = eye - x0 - (hx[:CH] + hx[CH:] + _dot(ah, _bd(xl, bd)))
        t_inv[cd] = x0 + _dot(xh, _bd(resid, bd))
    for c, d in pairs:
        r = rows[c]
        tb = t_inv[c, d].astype(BF16)
        u_out[0, d, r, :] = _dot(tb, _bd(v[r] * beta[d][r], bd))
        w_out[0, d, r, :] = _dot(tb, _bd(kb[d][r] * egc[d][r], bd)).astype(BF16)
        last = (c + 1) * CH - 1 if d == 0 else c * CH
        gtot = gcum[d][last:last + 1]
        k_end = (k[r] * jnp.exp(gtot - gcum[d][r])).astype(BF16)
        ket_out[0, d, c] = _dot_nt(eye_bf, k_end).astype(BF16)
        ge_out[0, d, c] = jnp.exp(gtot)


def _dn_local(pc, pcs, conv_w, a_log, dt_bias):
    b, t, w = pc.shape
    nt = t // TM
    left, right = _halo_specs(w, DN_HALO, nt)
    row = lambda bb, i: (bb, i, 0)
    pad_vec = lambda a: jnp.zeros((1, LANES), F32).at[0, 2 * HEADS:4 * HEADS].set(a.reshape(-1))
    vec = pl.BlockSpec((1, LANES), lambda bb, i: (0, 0))
    drow = pl.BlockSpec((1, 2, TM, GW), lambda bb, i: (bb, 0, i, 0))
    return _Part(
        _dn_local_kernel,
        (pc, pc, pc, pcs, conv_w, pad_vec(a_log), pad_vec(dt_bias)),
        [pl.BlockSpec((1, TM, w), row), left, right, pl.BlockSpec((1, TM, LANES), row),
         pl.BlockSpec((DN_CONV, w), lambda bb, i: (0, 0)), vec, vec],
        [drow, drow, drow, drow,
         pl.BlockSpec((1, 2, CPT, GW, CH), lambda bb, i: (bb, 0, i, 0, 0)),
         pl.BlockSpec((1, 2, CPT, 1, GW), lambda bb, i: (bb, 0, i, 0, 0))],
        [jax.ShapeDtypeStruct((b, 2, t, GW), F32)] + [jax.ShapeDtypeStruct((b, 2, t, GW), BF16)] * 3
        + [jax.ShapeDtypeStruct((b, 2, t // CH, GW, CH), BF16), jax.ShapeDtypeStruct((b, 2, t // CH, 1, GW), F32)],
        [pltpu.VMEM((TM + 2 * DN_HALO, w), F32)])


def _dn_chunk(refs, o_ref, s_scr, d, cc, bd):
    u, w, qi, a, ket, ge = refs
    rows = slice(cc * CH, (cc + 1) * CH)
    s = s_scr[d]
    wq = _dot(jnp.concatenate([w[0, 0, rows, :], qi[0, 0, rows, :]], axis=0), s.astype(BF16))
    v_new = u[0, 0, rows, :] - wq[:CH]
    o_ref[0, rows, :] = wq[CH:] + _dot(a[0, 0, rows, :], _bd(v_new, bd))
    s_scr[d] = s * ge[0, 0, cc] + jnp.where(bd, _dot(ket[0, 0, cc], v_new.astype(BF16)), 0.0)


def _dir_specs(shape_tail, n_tiles, chunked):
    blk = (1, 1, CPT if chunked else TM) + shape_tail
    zeros = (0,) * len(shape_tail)
    fwd = pl.BlockSpec(blk, lambda b, s: (b, 0, _fwd_tile(s, n_tiles)) + zeros)
    rev = pl.BlockSpec(blk, lambda b, s: (b, 1, _rev_tile(s, n_tiles)) + zeros)
    return fwd, rev


GLA_QK = HEADS * GLA_K


def _gla_local_kernel(pd_ref, pdl_ref, w2_ref, b2_ref, qi_out, a_out, ke_out, vt_out, vb_out, de_out):
    _, incl, _, _, _ = _chunk_masks()
    rr = _iota((TM, TM), 0)
    cc = _iota((TM, TM), 1)
    same_chunk = (rr >> 6) == (cc >> 6)
    eye_bf = (rr == cc).astype(BF16)
    bdk = (_iota((GW, GLA_QK), 0) >> 6) == (_iota((GW, GLA_QK), 1) >> 5)
    z = _dot(pdl_ref[0], w2_ref[...], HI) + b2_ref[...]
    gk_all = -_softplus(-z) * (1.0 / GLA_TAU)
    bcs_all = []
    for d in range(2):
        cum_bd = jnp.where(jnp.logical_and(same_chunk, cc <= rr if d == 0 else cc >= rr), 1.0, 0.0).astype(BF16)
        gparts = _bf16_parts(gk_all[:, d * GLA_QK:(d + 1) * GLA_QK], 2)
        cs = _dot(cum_bd, jnp.concatenate(gparts, axis=1))
        bcs_all.append(cs[:, :GLA_QK] + cs[:, GLA_QK:])
    for c in range(CPT):
        rows = slice(c * CH, (c + 1) * CH)
        p = pd_ref[0, rows, :]
        q = p[:, :GLA_QK] * (GLA_K ** -0.5)
        k = p[:, GLA_QK:2 * GLA_QK]
        vb = p[:, 2 * GLA_QK:].astype(BF16)
        vb_out[0, rows, :] = vb
        vt_out[0, c] = _dot_nt(eye_bf, vb).astype(BF16)
        for d in range(2):
            bcs = bcs_all[d][rows]
            bend = bcs[CH - 1:CH] if d == 0 else bcs[0:1]
            q_in = (q * jnp.exp(bcs)).astype(BF16)
            kdec = jnp.where(bdk, _tile4(k * jnp.exp(-bcs)), 0.0).astype(BF16)
            a_out[0, d, rows, :] = jnp.where(incl[d], _dot_nt(q_in, kdec), 0.0).astype(BF16)
            qi_out[0, d, rows, :] = q_in
            ke_out[0, d, rows, :] = (k * jnp.exp(bend - bcs)).astype(BF16)
            de_out[0, d, c] = jnp.exp(bend)


def _gla_local(pd, pdl, w2bd, b2):
    b, t, w = pd.shape
    nt = t // TM
    row = lambda bb, i: (bb, i, 0)
    return _Part(
        _gla_local_kernel,
        (pd, pdl, w2bd, b2),
        [pl.BlockSpec((1, TM, w), row), pl.BlockSpec((1, TM, LANES), row),
         pl.BlockSpec((LANES, GW), lambda bb, i: (0, 0)), pl.BlockSpec((1, GW), lambda bb, i: (0, 0))],
        [pl.BlockSpec((1, 2, TM, GLA_QK), lambda bb, i: (bb, 0, i, 0)),
         pl.BlockSpec((1, 2, TM, GW), lambda bb, i: (bb, 0, i, 0)),
         pl.BlockSpec((1, 2, TM, GLA_QK), lambda bb, i: (bb, 0, i, 0)),
         pl.BlockSpec((1, CPT, GW, CH), lambda bb, i: (bb, i, 0, 0)),
         pl.BlockSpec((1, TM, GW), row),
         pl.BlockSpec((1, 2, CPT, 1, GLA_QK), lambda bb, i: (bb, 0, i, 0, 0))],
        [jax.ShapeDtypeStruct((b, 2, t, GLA_QK), BF16), jax.ShapeDtypeStruct((b, 2, t, GW), BF16),
         jax.ShapeDtypeStruct((b, 2, t, GLA_QK), BF16), jax.ShapeDtypeStruct((b, t // CH, GW, CH), BF16),
         jax.ShapeDtypeStruct((b, t, GW), BF16), jax.ShapeDtypeStruct((b, 2, t // CH, 1, GLA_QK), F32)],
        [])


def _gla_chunk(refs, o_ref, s_scr, d, cc, bd, bdt):
    qi, a, ke, de, vt, v = refs
    rows = slice(cc * CH, (cc + 1) * CH)
    st = s_scr[d]
    vbd = jnp.where(bd, _tile4(v[0, rows, :]), jnp.zeros((), BF16))
    o_ref[0, rows, :] = _dot_nt(qi[0, 0, rows, :], st.astype(BF16)) + _dot(a[0, 0, rows, :], vbd)
    s_scr[d] = st * de[0, 0, cc] + jnp.where(bdt, _dot(vt[0, cc], ke[0, 0, rows, :]), 0.0)


def _scan_kernel(*refs):
    dn_in = (refs[0:6], refs[6:12])
    gla_in = (refs[12:18], refs[18:24])
    dn_out = refs[24:26]
    gla_out = refs[26:28]
    dn_s, gla_s = refs[28:30]

    @pl.when(pl.program_id(1) == 0)
    def _():
        dn_s[...] = jnp.zeros(dn_s.shape, F32)
        gla_s[...] = jnp.zeros(gla_s.shape, F32)

    bd = (_iota((GW, GW), 0) >> 6) == (_iota((GW, GW), 1) >> 6)
    bdt = (_iota((GW, GLA_QK), 0) >> 6) == (_iota((GW, GLA_QK), 1) >> 5)
    for c in range(CPT):
        for d in range(2):
            cc = c if d == 0 else CPT - 1 - c
            _dn_chunk(dn_in[d], dn_out[d], dn_s, d, cc, bd)
            _gla_chunk(gla_in[d], gla_out[d], gla_s, d, cc, bd, bdt)


def _scans(u, w, dqi, da, ket, ge, qi, a, ke, de, vt, vb):
    b, _, t, _ = u.shape
    nt = t // TM
    rowf, rowr = _dir_specs((GW,), nt, False)
    ketf, ketr = _dir_specs((GW, CH), nt, True)
    gef, ger = _dir_specs((1, GW), nt, True)
    qf, qr = _dir_specs((GLA_QK,), nt, False)
    df, dr = _dir_specs((1, GLA_QK), nt, True)
    vtf = pl.BlockSpec((1, CPT, GW, CH), lambda bb, s: (bb, _fwd_tile(s, nt), 0, 0))
    vtr = pl.BlockSpec((1, CPT, GW, CH), lambda bb, s: (bb, _rev_tile(s, nt), 0, 0))
    vf = pl.BlockSpec((1, TM, GW), lambda bb, s: (bb, _fwd_tile(s, nt), 0))
    vr = pl.BlockSpec((1, TM, GW), lambda bb, s: (bb, _rev_tile(s, nt), 0))
    dn_args = (u, w, dqi, da, ket, ge)
    gla_args = (qi, a, ke, de, vt, vb)
    return pl.pallas_call(
        _scan_kernel,
        grid=(b, nt),
        in_specs=[rowf, rowf, rowf, rowf, ketf, gef, rowr, rowr, rowr, rowr, ketr, ger,
                  qf, rowf, qf, df, vtf, vf, qr, rowr, qr, dr, vtr, vr],
        out_specs=[vf, vr, vf, vr],
        out_shape=[jax.ShapeDtypeStruct((b, t, GW), F32)] * 4,
        scratch_shapes=[pltpu.VMEM((2, GW, GW), F32), pltpu.VMEM((2, GW, GLA_QK), F32)],
        compiler_params=_cp(("parallel", "arbitrary")),
    )(*dn_args, *dn_args, *gla_args, *gla_args)


def _outproj_kernel(x_ref, ya_ref, yb_ref, cf_ref, cr_ref, df_ref, dr_ref, pg_ref, g1b_ref, g1c_ref, gc_ref, gd_ref,
                    w_ref, o_ref, *, lat_rows):
    ones64 = _group_ones(GW, 6)

    def fin(o, g, gate):
        ms = _dot_sel(o * o, ones64) * (1.0 / HEAD_DIM)
        return (o * lax.rsqrt(ms + EPS) * g * _silu(gate)).astype(BF16)

    pg = pg_ref[0]
    yc = fin(cf_ref[0] + cr_ref[0], gc_ref[...], pg[:, :GW])
    yd = fin(df_ref[0] + dr_ref[0], gd_ref[...], pg[:, GW:])
    res = (_dot(ya_ref[0].astype(BF16), w_ref[0:GW, :]) + _dot(yb_ref[0].astype(BF16), w_ref[GW:2 * GW, :])
           + _dot(yc, w_ref[2 * GW:3 * GW, :]) + _dot(yd, w_ref[3 * GW:, :]))
    g1 = jnp.where(_ctx_rows(res.shape[0], lat_rows), g1c_ref[0], g1b_ref[0])
    o_ref[0] = x_ref[0] + g1 * res


def _outproj(xs, ya, yb, ocf, ocr, odf, odr, pg, mod3, dn_g, gla_g, w_out_bf, n_batch, rows, n_blk):
    b, t, d = xs.shape
    reps = GW // HEAD_DIM
    tb = rows // n_blk
    row = lambda bb, i: (bb, i, 0)
    g256 = pl.BlockSpec((1, tb, GW), row)
    in_specs = ([pl.BlockSpec((1, tb, d), row), g256, g256, g256, g256, g256, g256, pl.BlockSpec((1, tb, 2 * GW), row)]
                + _mod_specs(2, n_batch) + [_resident((1, GW)), _resident((1, GW)), _resident((d, d))])
    args = (xs, ya, yb, ocf, ocr, odf, odr, pg, mod3, mod3, jnp.tile(dn_g, reps).reshape(1, GW),
            jnp.tile(gla_g, reps).reshape(1, GW), w_out_bf)
    kern = functools.partial(_outproj_kernel, lat_rows=t - TM)
    return _row_call(kern, n_batch, rows, args, in_specs, [d], n_blk=n_blk)[0]


FF_HALO = 8


def _ffn_kernel(xm_ref, xl_ref, xr_ref, shb_ref, shc_ref, scb_ref, scc_ref, g2b_ref, g2c_ref, ng_ref,
                wu_ref, cw_ref, wd_ref, o_ref, ext_ref, *, lat_rows):
    tb = xm_ref.shape[1]
    i = pl.program_id(1)
    rows_ext = _iota((tb + 2 * FF_HALO, 1), 0)
    left_ok = jnp.logical_and(i > 0, i * tb != lat_rows)
    right_ok = jnp.logical_and(i < pl.num_programs(1) - 1, (i + 1) * tb != lat_rows)
    keep = jnp.logical_and(jnp.logical_or(rows_ext >= FF_HALO, left_ok),
                           jnp.logical_or(rows_ext < FF_HALO + tb, right_ok))
    grow = i * tb - FF_HALO + rows_ext
    ctx_ext = grow >= lat_rows
    x = jnp.concatenate([xl_ref[0], xm_ref[0], xr_ref[0]], axis=0)
    y = x * lax.rsqrt(jnp.mean(x * x, axis=-1, keepdims=True) + EPS) * ng_ref[...]
    h = y * (1.0 + jnp.where(ctx_ext, scc_ref[0], scb_ref[0])) + jnp.where(ctx_ext, shc_ref[0], shb_ref[0])
    h = jnp.where(keep, h, 0.0).astype(BF16)
    inner_boundary = lat_rows % tb != 0
    if inner_boundary:
        row = grow[FF_HALO:FF_HALO + tb]
        m_prev = jnp.broadcast_to(jnp.where(row == lat_rows, 0.0, 1.0), (tb, FF_BLK))
        m_next = jnp.broadcast_to(jnp.where(row == lat_rows - 1, 0.0, 1.0), (tb, FF_BLK))
    acc = jnp.zeros((tb, D_MODEL), F32)
    for j in range(N_FF_BLK):
        def conv(col0, half):
            cols = slice(col0, col0 + FF_BLK)
            ext_ref[half] = _dot(h, wu_ref[:, cols])
            cw = cw_ref[:, cols]
            prev = ext_ref[half, pl.ds(FF_HALO - 1, tb), :]
            nxt = ext_ref[half, pl.ds(FF_HALO + 1, tb), :]
            if inner_boundary:
                prev = m_prev * prev
                nxt = m_next * nxt
            return cw[0:1] * prev + cw[1:2] * ext_ref[half, pl.ds(FF_HALO, tb), :] + cw[2:3] * nxt
        a = conv(j * FF_BLK, 0)
        g = conv(D_FF + j * FF_BLK, 1)
        acc = acc + _dot((_silu(g) * a).astype(BF16), wd_ref[j * FF_BLK:(j + 1) * FF_BLK, :])
    g2 = jnp.where(ctx_ext[FF_HALO:FF_HALO + tb], g2c_ref[0], g2b_ref[0])
    o_ref[0] = xm_ref[0] + g2 * acc


def _ffn(x1, mod3, norm_g, w_up, cw, w_down, n_batch, lat_rows, rows, n_blk):
    b, t, d = x1.shape
    tb = rows // n_blk
    per = tb // FF_HALO
    last = t // FF_HALO - 1
    in_specs = ([pl.BlockSpec((1, tb, d), lambda bb, i: (bb, i, 0)),
                 pl.BlockSpec((1, FF_HALO, d), lambda bb, i: (bb, jnp.maximum(i * per - 1, 0), 0)),
                 pl.BlockSpec((1, FF_HALO, d), lambda bb, i: (bb, jnp.minimum((i + 1) * per, last), 0))]
                + _mod_specs(3, n_batch) + _mod_specs(4, n_batch) + _mod_specs(5, n_batch)
                + [_resident((1, d)), _resident(w_up.shape), _resident(cw.shape), _resident(w_down.shape)])
    args = (x1, x1, x1, mod3, mod3, mod3, mod3, mod3, mod3, norm_g.reshape(1, d), w_up, cw, w_down)
    kern = functools.partial(_ffn_kernel, lat_rows=lat_rows)
    return _row_call(kern, n_batch, rows, args, in_specs, [d],
                     scratch=[pltpu.VMEM((2, tb + 2 * FF_HALO, FF_BLK), F32)], n_blk=n_blk)[0]


def _rope_tables(seq, ctx_len):
    rows = seq // GRID_W
    row = jnp.repeat(jnp.arange(rows, dtype=F32), GRID_W)
    col = jnp.tile(jnp.arange(GRID_W, dtype=F32), rows)
    nf = QK_DIM // 4
    inv = ROPE_THETA ** (-jnp.arange(nf, dtype=F32) / nf)
    ang = jnp.concatenate([row[:, None] * inv, col[:, None] * inv], axis=-1)
    cos = jnp.concatenate([jnp.cos(ang), jnp.ones((ctx_len, QK_DIM // 2), F32)], axis=0)
    sin = jnp.concatenate([jnp.sin(ang), jnp.zeros((ctx_len, QK_DIM // 2), F32)], axis=0)
    reps = GW // QK_DIM
    return (jnp.tile(jnp.concatenate([cos, cos], axis=-1), (1, reps)),
            jnp.tile(jnp.concatenate([-sin, sin], axis=-1), (1, reps)))


def _regroup_w_in(w):
    d = w.shape[0]
    z = lambda n: jnp.zeros((d, n), w.dtype)
    return jnp.concatenate([w[:, :2048], w[:, 2048:2064], z(LANES - 16), w[:, 2320:2832], w[:, 2832:2864],
                            z(LANES - 32), w[:, 2064:2320], w[:, 2864:3120]], axis=1).astype(BF16)


def _gla_w2_blockdiag(w2):
    out = jnp.zeros((LANES, GW), F32)
    out = out.at[0:GLA_RANK, 0:GLA_QK].set(w2[0])
    return out.at[GLA_RANK:2 * GLA_RANK, GLA_QK:].set(w2[1])


def _layer(xs, mod3, lp, cos_t, sin_t, layer_idx, last, n_batch):
    b, t, d = xs.shape
    nt = t // TM
    pa, pb, pc, pcs, pd, pdl, pg = _inproj(xs, mod3, lp["norm1_g"], _regroup_w_in(lp["w_in"]), n_batch)

    lam_init = 0.8 - 0.6 * math.exp(-0.3 * layer_idx)
    q, kt, v, qn, kn = _attn_prep(pb, lp["da_qnorm_g"], lp["da_knorm_g"], cos_t, sin_t)
    yb = _attention(q, kt, v, qn, kn, lp["da_lambda"], lp["da_subln_g"], lam_init)

    (ya,), dn_parts, (qi, a, ke, vt, vb, de) = _fused_call(
        [_conv_module(pa, lp["cm_conv_w"], lp["cm_conv_b"], lp["cm_ln_g"], lp["cm_ln_b"]),
         _dn_local(pc, pcs, lp["dn_conv_w"], lp["dn_a_log"], lp["dn_dt_bias"]),
         _gla_local(pd, pdl, _gla_w2_blockdiag(lp["gla_w2"]), lp["gla_b2"].reshape(1, GW))],
        (b, nt))
    ocf, ocr, odf, odr = _scans(*dn_parts, qi, a, ke, de, vt, vb)

    lat_rows = t - TM
    rows = lat_rows if last else t
    x1 = _outproj(xs, ya, yb, ocf, ocr, odf, odr, pg, mod3, lp["dn_onorm_g"], lp["gla_onorm_g"],
                  lp["w_out"].astype(BF16), n_batch, rows, N_BLK_FFN)
    return _ffn(x1, mod3, lp["norm2_g"], lp["ffn_w_up"].astype(BF16), lp["ffn_conv_w"],
                lp["ffn_w_down"].astype(BF16), n_batch, lat_rows, rows, N_BLK_FFN)


def kernel(x, c, ctx, c_ctx, w_mod, b_mod, norm1_g, norm2_g, w_in, w_out, cm_conv_w, cm_conv_b, cm_ln_g, cm_ln_b, da_qnorm_g, da_knorm_g, da_lambda, da_subln_g, dn_conv_w, dn_a_log, dn_dt_bias, dn_onorm_g, gla_w2, gla_b2, gla_onorm_g, ffn_w_up, ffn_conv_w, ffn_w_down):
    n_batch, seq, d = x.shape
    ctx_len = ctx.shape[1]
    assert ctx_len == TM and seq % TM == 0 and (seq + ctx_len) % (8 * N_BLK) == 0 and d == D_MODEL
    depth = w_mod.shape[0]
    cos_t, sin_t = _rope_tables(seq, ctx_len)
    xs = jnp.concatenate([x, ctx], axis=1)
    mod_rows = 16
    c_rows = jnp.zeros((mod_rows, d), F32).at[:n_batch].set(c).at[n_batch].set(c_ctx)
    params = dict(w_mod=w_mod, b_mod=b_mod, norm1_g=norm1_g, norm2_g=norm2_g, w_in=w_in, w_out=w_out,
                  cm_conv_w=cm_conv_w, cm_conv_b=cm_conv_b, cm_ln_g=cm_ln_g, cm_ln_b=cm_ln_b,
                  da_qnorm_g=da_qnorm_g, da_knorm_g=da_knorm_g, da_lambda=da_lambda, da_subln_g=da_subln_g,
                  dn_conv_w=dn_conv_w, dn_a_log=dn_a_log, dn_dt_bias=dn_dt_bias, dn_onorm_g=dn_onorm_g,
                  gla_w2=gla_w2, gla_b2=gla_b2, gla_onorm_g=gla_onorm_g,
                  ffn_w_up=ffn_w_up, ffn_conv_w=ffn_conv_w, ffn_w_down=ffn_w_down)
    for l in range(depth):
        lp = {k: v[l] for k, v in params.items()}
        mod3 = _modulation(c_rows, lp["w_mod"], lp["b_mod"]).reshape(mod_rows * 6, 1, d)
        xs = _layer(xs, mod3, lp, cos_t, sin_t, l, l == depth - 1, n_batch)
    return xs
```

```python
import functools
import math
from typing import Any, NamedTuple

import jax
import jax.numpy as jnp
from jax import lax
from jax.experimental import pallas as pl
from jax.experimental.pallas import tpu as pltpu

F32 = jnp.float32
BF16 = jnp.bfloat16
HI = lax.Precision.HIGHEST
EPS = 1e-6

D_MODEL = 1024
GRID_W = 64
HEADS = 4
HEAD_DIM = 64
GW = 256
QK_DIM = 32
GLA_K = 32
GLA_RANK = 16
GLA_TAU = 16.0
CM_KERNEL = 31
DN_CONV = 5
ROPE_THETA = 10000.0
CH = 64
TM = 256
CPT = TM // CH
D_FF = 2816
FF_BLK = 256
N_FF_BLK = D_FF // FF_BLK
LANES = 128

IN_GROUPS = (("pa", 512), ("pb", 768), ("pc", 768), ("pcs", LANES), ("pd", 512), ("pdl", LANES), ("pg", 512))
IN_COLS_PAD = sum(w for _, w in IN_GROUPS)

VMEM_LIMIT = 56 * 1024 * 1024


def _cp(sem):
    return pltpu.CompilerParams(dimension_semantics=sem, vmem_limit_bytes=VMEM_LIMIT)


def _dot(a, b, prec=None):
    return jnp.dot(a, b, preferred_element_type=F32, precision=prec)


def _dot_nt(a, b, prec=None):
    return lax.dot_general(a, b, (((1,), (1,)), ((), ())), preferred_element_type=F32, precision=prec)


def _sigmoid(x):
    return 1.0 / (1.0 + jnp.exp(-x))


def _silu(x):
    return x * _sigmoid(x)


def _softplus(x):
    return jnp.maximum(x, 0.0) + jnp.log(1.0 + jnp.exp(-jnp.abs(x)))


def _iota(shape, dim):
    return lax.broadcasted_iota(jnp.int32, shape, dim)


def _group_ones(n, shift):
    return ((_iota((n, n), 0) >> shift) == (_iota((n, n), 1) >> shift)).astype(BF16)


def _bf16_parts(m, n):
    out = []
    for _ in range(n):
        hi = m.astype(BF16)
        out.append(hi)
        m = m - hi.astype(F32)
    return out


def _dot_sel(x, sel, n_parts=2):
    rows = x.shape[0]
    r = _dot(jnp.concatenate(_bf16_parts(x, n_parts), axis=0), sel)
    out = r[:rows]
    for k in range(1, n_parts):
        out = out + r[k * rows:(k + 1) * rows]
    return out


def _tile4(y):
    return jnp.concatenate([y, y, y, y], axis=0)


def _mod_kernel(c_ref, w_ref, b_ref, o_ref):
    o_ref[...] = _dot(_silu(c_ref[...]), w_ref[...], HI) + b_ref[...]


def _modulation(c_rows, w_mod, b_mod):
    r, d = c_rows.shape
    n = w_mod.shape[1] // d
    return pl.pallas_call(
        _mod_kernel,
        grid=(n,),
        in_specs=[pl.BlockSpec((r, d), lambda j: (0, 0)),
                  pl.BlockSpec((d, d), lambda j: (0, j)),
                  pl.BlockSpec((1, d), lambda j: (0, j))],
        out_specs=pl.BlockSpec((r, d), lambda j: (0, j)),
        out_shape=jax.ShapeDtypeStruct((r, n * d), F32),
        compiler_params=_cp(("arbitrary",)),
    )(c_rows, w_mod, b_mod.reshape(1, -1))


N_BLK = 4
N_BLK_FFN = 8


def _mod_specs(k, n_batch):
    return [pl.BlockSpec((1, 1, D_MODEL), lambda b, i: (b * 6 + k, 0, 0)),
            pl.BlockSpec((1, 1, D_MODEL), lambda b, i: (n_batch * 6 + k, 0, 0))]


def _ctx_rows(tb, lat_rows):
    return pl.program_id(1) * tb + _iota((tb, 1), 0) >= lat_rows


def _resident(shape):
    zeros = (0,) * len(shape)
    return pl.BlockSpec(shape, lambda b, i: zeros, pipeline_mode=pl.Buffered(1))


def _row_call(kern, n_batch, rows, args, in_specs, out_widths, scratch=(), n_blk=N_BLK):
    tb = rows // n_blk
    return pl.pallas_call(
        kern,
        grid=(n_batch, n_blk),
        in_specs=in_specs,
        out_specs=[pl.BlockSpec((1, tb, w), lambda bb, i: (bb, i, 0)) for w in out_widths],
        out_shape=[jax.ShapeDtypeStruct((n_batch, rows, w), F32) for w in out_widths],
        scratch_shapes=list(scratch),
        compiler_params=_cp(("parallel", "arbitrary")),
    )(*args)


def _inproj_kernel(x_ref, shb_ref, shc_ref, scb_ref, scc_ref, g_ref, w_ref, *outs, lat_rows):
    x = x_ref[0]
    ctx = _ctx_rows(x.shape[0], lat_rows)
    y = x * lax.rsqrt(jnp.mean(x * x, axis=-1, keepdims=True) + EPS) * g_ref[...]
    h = y * (1.0 + jnp.where(ctx, scc_ref[0], scb_ref[0])) + jnp.where(ctx, shc_ref[0], shb_ref[0])
    p = _dot(h.astype(BF16), w_ref[...])
    off = 0
    for o_ref, (_, width) in zip(outs, IN_GROUPS):
        o_ref[0] = p[:, off:off + width]
        off += width


def _inproj(xs, mod3, norm_g, w_in_r, n_batch):
    b, t, d = xs.shape
    in_specs = ([pl.BlockSpec((1, t // N_BLK, d), lambda bb, i: (bb, i, 0))] + _mod_specs(0, n_batch)
                + _mod_specs(1, n_batch) + [_resident((1, d)), _resident((d, IN_COLS_PAD))])
    kern = functools.partial(_inproj_kernel, lat_rows=t - TM)
    return _row_call(kern, n_batch, t, (xs, mod3, mod3, mod3, mod3, norm_g.reshape(1, d), w_in_r), in_specs,
                     [w for _, w in IN_GROUPS])


class _Part(NamedTuple):
    body: Any
    args: Any
    in_specs: Any
    out_specs: Any
    out_shapes: Any
    scratch: Any


def _fused_call(parts, grid):
    n_in = [len(p.args) for p in parts]
    n_out = [len(p.out_specs) for p in parts]
    n_scr = [len(p.scratch) for p in parts]

    def kern(*refs):
        ins, outs, scr = refs[:sum(n_in)], refs[sum(n_in):sum(n_in) + sum(n_out)], refs[sum(n_in) + sum(n_out):]
        i = o = s = 0
        for p, ni, no, ns in zip(parts, n_in, n_out, n_scr):
            p.body(*ins[i:i + ni], *outs[o:o + no], *scr[s:s + ns])
            i, o, s = i + ni, o + no, s + ns

    res = pl.pallas_call(
        kern,
        grid=grid,
        in_specs=[sp for p in parts for sp in p.in_specs],
        out_specs=[sp for p in parts for sp in p.out_specs],
        out_shape=[sh for p in parts for sh in p.out_shapes],
        scratch_shapes=[sc for p in parts for sc in p.scratch],
        compiler_params=_cp(("parallel", "arbitrary")),
    )(*[a for p in parts for a in p.args])
    out, o = [], 0
    for no in n_out:
        out.append(res[o:o + no])
        o += no
    return out


def _halo_specs(width, halo, n_tiles):
    per = TM // halo
    left = pl.BlockSpec((1, halo, width), lambda b, i: (b, jnp.maximum(i * per - 1, 0), 0))
    right = pl.BlockSpec((1, halo, width), lambda b, i: (b, jnp.minimum((i + 1) * per, n_tiles * per - 1), 0))
    return left, right


def _halo_ok(i, n_tiles):
    return jnp.logical_and(i >= 1, i < n_tiles - 1), i < n_tiles - 2


CM_HALO = 16


def _convmod_kernel(pm_ref, pl_ref, pr_ref, cw_ref, cb_ref, lg_ref, lb_ref, o_ref, ext_ref):
    left_ok, right_ok = _halo_ok(pl.program_id(1), pl.num_programs(1))

    def glu(p):
        return p[:, :GW] * _sigmoid(p[:, GW:])

    ext_ref[0:CM_HALO] = jnp.where(left_ok, glu(pl_ref[0]), 0.0)
    ext_ref[CM_HALO:CM_HALO + TM] = glu(pm_ref[0])
    ext_ref[CM_HALO + TM:] = jnp.where(right_ok, glu(pr_ref[0]), 0.0)
    pad = CM_KERNEL // 2
    sub = 8
    y = cb_ref[...]
    for r in range(sub):
        acc = None
        for j in range(CM_KERNEL):
            off = CM_HALO - pad + j
            if off % sub == r:
                term = cw_ref[j:j + 1, :] * ext_ref[pl.ds(off - r, TM + sub), :]
                acc = term if acc is None else acc + term
        if acc is not None:
            y = y + acc[r:r + TM]
    mu = jnp.mean(y, axis=-1, keepdims=True)
    yc = y - mu
    var = jnp.mean(yc * yc, axis=-1, keepdims=True)
    o_ref[0] = _silu(yc * lax.rsqrt(var + EPS) * lg_ref[...] + lb_ref[...])


def _conv_module(pa, conv_w, conv_b, ln_g, ln_b):
    b, t, w = pa.shape
    nt = t // TM
    left, right = _halo_specs(w, CM_HALO, nt)
    vec = pl.BlockSpec((1, GW), lambda bb, i: (0, 0))
    return _Part(
        _convmod_kernel,
        (pa, pa, pa, conv_w, conv_b.reshape(1, GW), ln_g.reshape(1, GW), ln_b.reshape(1, GW)),
        [pl.BlockSpec((1, TM, w), lambda bb, i: (bb, i, 0)), left, right,
         pl.BlockSpec((CM_KERNEL, GW), lambda bb, i: (0, 0)), vec, vec, vec],
        [pl.BlockSpec((1, TM, GW), lambda bb, i: (bb, i, 0))],
        [jax.ShapeDtypeStruct((b, t, GW), F32)],
        [pltpu.VMEM((TM + 2 * CM_HALO, GW), F32)])


N_MAPS = 2 * HEADS
QK_PAD = 2 * QK_DIM
V_EXT = 2 * HEAD_DIM
MASK_BIG = 8192.0
SHIFT_MAX = 40.0


def _attn_prep_kernel(pb_ref, qg_ref, kg_ref, cos_ref, sin_ref, q_out, kt_out, v_out, qn_out, kn_out):
    p = pb_ref[0]
    ones32 = _group_ones(GW, 5)
    first = (_iota((TM, GW), 1) & (QK_DIM - 1)) < QK_DIM // 2
    cos = cos_ref[...]
    sin = sin_ref[...]

    def norm_rope(t, g):
        ms = _dot_sel(t * t, ones32) * (1.0 / QK_DIM)
        tn = t * lax.rsqrt(ms + EPS) * g
        partner = jnp.where(first, pltpu.roll(tn, GW - QK_DIM // 2, 1), pltpu.roll(tn, QK_DIM // 2, 1))
        return tn * cos + partner * sin

    qf = norm_rope(p[:, :GW], qg_ref[...]) * (QK_DIM ** -0.5)
    kf = norm_rope(p[:, GW:2 * GW], kg_ref[...])
    map_sel = ((_iota((GW, N_MAPS), 0) >> 5) == _iota((GW, N_MAPS), 1)).astype(BF16)
    qn_out[0] = _dot_sel(qf * qf, map_sel)
    kn_out[0] = _dot_sel(kf * kf, map_sel)
    q = qf.astype(BF16)
    kt = kf.T.astype(BF16)
    is_ctx = pl.program_id(1) == pl.num_programs(1) - 1
    k_row = _iota((QK_DIM, TM), 0)
    k_tail = jnp.where(k_row == 0, 1.0, jnp.where(jnp.logical_and(k_row == 1, jnp.logical_not(is_ctx)), 1.0, 0.0))
    k_tail = k_tail.astype(BF16)
    q_tail = jnp.where(jnp.logical_and(_iota((TM, QK_DIM), 1) == 1, is_ctx), -MASK_BIG, 0.0).astype(BF16)
    for g in range(N_MAPS):
        q_out[0, g, :, 0:QK_DIM] = q[:, g * QK_DIM:(g + 1) * QK_DIM]
        q_out[0, g, :, QK_DIM:] = q_tail
        kt_out[0, 0, g, 0:QK_DIM, :] = kt[g * QK_DIM:(g + 1) * QK_DIM, :]
        kt_out[0, 0, g, QK_DIM:, :] = k_tail
    v = p[:, 2 * GW:].astype(BF16)
    ones = jnp.ones((TM, HEAD_DIM), BF16)
    for h in range(HEADS):
        v_out[0, 0, h, :, 0:HEAD_DIM] = v[:, h * HEAD_DIM:(h + 1) * HEAD_DIM]
        v_out[0, 0, h, :, HEAD_DIM:] = ones


def _attn_prep(pb, qn_g, kn_g, cos_t, sin_t):
    b, t, w = pb.shape
    nt = t // TM
    vec = pl.BlockSpec((1, GW), lambda bb, i: (0, 0))
    tab = pl.BlockSpec((TM, GW), lambda bb, i: (i, 0))
    reps = GW // QK_DIM
    n_spec = pl.BlockSpec((1, TM, N_MAPS), lambda bb, i: (bb, i, 0))
    n_shape = jax.ShapeDtypeStruct((b, t, N_MAPS), F32)
    return _Part(
        _attn_prep_kernel,
        (pb, jnp.tile(qn_g, reps).reshape(1, GW), jnp.tile(kn_g, reps).reshape(1, GW), cos_t, sin_t),
        [pl.BlockSpec((1, TM, w), lambda bb, i: (bb, i, 0)), vec, vec, tab, tab],
        [pl.BlockSpec((1, N_MAPS, TM, QK_PAD), lambda bb, i: (bb, 0, i, 0)),
         pl.BlockSpec((1, 1, N_MAPS, QK_PAD, TM), lambda bb, i: (bb, i, 0, 0, 0)),
         pl.BlockSpec((1, 1, HEADS, TM, V_EXT), lambda bb, i: (bb, i, 0, 0, 0)), n_spec, n_spec],
        [jax.ShapeDtypeStruct((b, N_MAPS, t, QK_PAD), BF16),
         jax.ShapeDtypeStruct((b, nt, N_MAPS, QK_PAD, TM), BF16),
         jax.ShapeDtypeStruct((b, nt, HEADS, TM, V_EXT), BF16), n_shape, n_shape],
        [])


def _attn_kernel(q_ref, kt_ref, v_ref, qn_ref, kn_ref, lam_ref, sg_ref, o_ref, qa_scr, acc_scr, s_scr, m_scr,
                 *, n_chunks, lam_init):
    tq = o_ref.shape[1]
    acc_scr[...] = jnp.zeros(acc_scr.shape, F32)
    k2 = jnp.max(kn_ref[0], axis=0, keepdims=True)
    q2 = jnp.max(qn_ref[0], axis=0, keepdims=True)
    bound = jnp.sqrt(q2 * k2)
    safe = jnp.max(bound) <= SHIFT_MAX
    shift = jnp.where(safe, bound, 0.0)
    shift_lane = _iota((tq, QK_PAD), 1) == QK_DIM
    for g in range(N_MAPS):
        qa_scr[g] = jnp.where(shift_lane, (-shift[:, g:g + 1]).astype(BF16), q_ref[0, g])

    @pl.when(safe)
    def _():
        s_scr[0] = _dot(qa_scr[0], kt_ref[0, 0, 0])

        def body(c, carry):
            nxt = jnp.minimum(c + 1, n_chunks - 1)
            for g in range(N_MAPS):
                if g + 1 < N_MAPS:
                    s_next = _dot(qa_scr[g + 1], kt_ref[0, c, g + 1])
                else:
                    s_next = _dot(qa_scr[0], kt_ref[0, nxt, 0])
                p = jnp.exp(s_scr[g & 1]).astype(BF16)
                acc_scr[g] += _dot(p, v_ref[0, c, g // 2])
                s_scr[(g + 1) & 1] = s_next
            return carry

        lax.fori_loop(0, n_chunks, body, 0)

    @pl.when(jnp.logical_not(safe))
    def _():
        m_scr[...] = jnp.full(m_scr.shape, -jnp.inf, F32)

        def body(c, carry):
            for g in range(N_MAPS):
                s = _dot(qa_scr[g], kt_ref[0, c, g])
                m_old = m_scr[:, g:g + 1]
                m_new = jnp.maximum(m_old, jnp.max(s, axis=-1, keepdims=True))
                p = jnp.exp(s - m_new).astype(BF16)
                acc_scr[g] = jnp.exp(m_old - m_new) * acc_scr[g] + _dot(p, v_ref[0, c, g // 2])
                m_scr[:, g:g + 1] = m_new
            return carry

        lax.fori_loop(0, n_chunks, body, 0)

    lp = lam_ref[...]
    lam = (jnp.exp(jnp.sum(lp[0:1] * lp[1:2], axis=-1, keepdims=True))
           - jnp.exp(jnp.sum(lp[2:3] * lp[3:4], axis=-1, keepdims=True)) + lam_init)
    for h in range(HEADS):
        a0 = acc_scr[2 * h]
        a1 = acc_scr[2 * h + 1]
        o = (a0[:, :HEAD_DIM] / a0[:, HEAD_DIM:HEAD_DIM + 1]
             - lam * (a1[:, :HEAD_DIM] / a1[:, HEAD_DIM:HEAD_DIM + 1]))
        y = o * lax.rsqrt(jnp.mean(o * o, axis=-1, keepdims=True) + EPS) * sg_ref[...] * (1.0 - lam_init)
        o_ref[0, :, h * HEAD_DIM:(h + 1) * HEAD_DIM] = y


def _attention(q, kt, v, qn, kn, lam_p, subln_g, lam_init):
    b, _, t, _ = q.shape
    tq = t // N_BLK
    nt = t // TM
    kern = functools.partial(_attn_kernel, n_chunks=nt, lam_init=lam_init)
    return pl.pallas_call(
        kern,
        grid=(b, N_BLK),
        in_specs=[pl.BlockSpec((1, N_MAPS, tq, QK_PAD), lambda bb, i: (bb, 0, i, 0)),
                  pl.BlockSpec((1, nt, N_MAPS, QK_PAD, TM), lambda bb, i: (bb, 0, 0, 0, 0)),
                  pl.BlockSpec((1, nt, HEADS, TM, V_EXT), lambda bb, i: (bb, 0, 0, 0, 0)),
                  pl.BlockSpec((1, tq, N_MAPS), lambda bb, i: (bb, i, 0)),
                  pl.BlockSpec((1, t, N_MAPS), lambda bb, i: (bb, 0, 0)),
                  _resident((4, QK_DIM)), _resident((1, HEAD_DIM))],
        out_specs=pl.BlockSpec((1, tq, GW), lambda bb, i: (bb, i, 0)),
        out_shape=jax.ShapeDtypeStruct((b, t, GW), F32),
        scratch_shapes=[pltpu.VMEM((N_MAPS, tq, QK_PAD), BF16), pltpu.VMEM((N_MAPS, tq, V_EXT), F32),
                        pltpu.VMEM((2, tq, TM), F32), pltpu.VMEM((tq, N_MAPS), F32)],
        compiler_params=_cp(("parallel", "arbitrary")),
    )(q, kt, v, qn, kn, lam_p, subln_g.reshape(1, HEAD_DIM))


def _chunk_masks():
    i = _iota((CH, GW), 0)
    j = _iota((CH, GW), 1) & (CH - 1)
    eye = (i == j).astype(F32)
    incl = (j <= i, j >= i)
    strict = (j < i, j > i)
    r = _iota((CH, CH), 0)
    c = _iota((CH, CH), 1)
    cum = ((c <= r).astype(F32), (c >= r).astype(F32))
    bd = (_iota((GW, GW), 0) >> 6) == (_iota((GW, GW), 1) >> 6)
    return eye, incl, strict, cum, bd


def _bd(y, bd):
    return jnp.where(bd, _tile4(y), 0.0).astype(BF16)


def _fwd_tile(s, n_tiles):
    return jnp.where(s == 0, n_tiles - 1, s - 1)


def _rev_tile(s, n_tiles):
    return jnp.where(s == 0, n_tiles - 1, n_tiles - 1 - s)


DN_HALO = 8


def _dn_prep(pm_ref, pl_ref, pr_ref, pcs_ref, cw_ref, alog_ref, dtb_ref, ext_ref):
    left_ok, right_ok = _halo_ok(pl.program_id(1), pl.num_programs(1))
    ext_ref[0:DN_HALO] = jnp.where(left_ok, pl_ref[0], 0.0)
    ext_ref[DN_HALO:DN_HALO + TM] = pm_ref[0]
    ext_ref[DN_HALO + TM:] = jnp.where(right_ok, pr_ref[0], 0.0)
    pad = DN_CONV // 2
    acc = jnp.zeros((TM, 3 * GW), F32)
    for j in range(DN_CONV):
        acc = acc + cw_ref[j:j + 1, :] * ext_ref[pl.ds(DN_HALO - pad + j, TM), :]
    qkv = _silu(acc)
    ones64 = _group_ones(GW, 6)

    def l2n(t):
        return t * lax.rsqrt(_dot_sel(t * t, ones64) + EPS)

    s = pcs_ref[0]
    col = _iota(s.shape, 1)
    gate = -jnp.exp(alog_ref[...]) * _softplus(s + dtb_ref[...])
    bg = jnp.where(col < 2 * HEADS, _sigmoid(s), jnp.where(col < 4 * HEADS, gate, 0.0))
    return l2n(qkv[:, :GW]) * (HEAD_DIM ** -0.5), l2n(qkv[:, GW:2 * GW]), qkv[:, 2 * GW:], bg


def _dn_local_kernel(pm_ref, pl_ref, pr_ref, pcs_ref, cw_ref, alog_ref, dtb_ref,
                     u_out, w_out, qi_out, a_out, ket_out, ge_out, ext_ref):
    rr = _iota((TM, GW), 0)
    cc = _iota((TM, GW), 1)
    i_in = rr & (CH - 1)
    j_in = cc & (CH - 1)
    bd = (rr >> 6) == (cc >> 6)
    eye_t = (i_in == j_in).astype(F32)
    incl = (j_in <= i_in, j_in >= i_in)
    strict = (j_in < i_in, j_in > i_in)
    ones_bd = bd.astype(F32)
    eye_bf = (rr == cc).astype(BF16)
    head_of_lane = _iota((LANES, GW), 1) >> 6
    src = _iota((LANES, GW), 0)
    q, k, v, bg = _dn_prep(pm_ref, pl_ref, pr_ref, pcs_ref, cw_ref, alog_ref, dtb_ref, ext_ref)
    def parts(m, n):
        out = []
        for _ in range(n):
            hi = m.astype(BF16)
            out.append(hi)
            m = m - hi.astype(F32)
        return out

    ones_bf = ones_bd.astype(BF16)
    eye_f = eye_t
    beta, gcum, decay, kb, egc = [], [], [], [], []
    for d in range(2):
        sel_b = (src == d * HEADS + head_of_lane).astype(BF16)
        sel_g = (src == 2 * HEADS + d * HEADS + head_of_lane).astype(BF16)
        cum_bd = jnp.where(jnp.logical_and(bd, incl[d]), 1.0, 0.0).astype(BF16)
        g_exp = _dot(jnp.concatenate(parts(bg, 2), axis=0), sel_g)
        beta.append(_dot(bg.astype(BF16), sel_b))
        cs = _dot(cum_bd, jnp.concatenate([g_exp[:TM].astype(BF16), g_exp[TM:].astype(BF16)], axis=1))
        gcum.append(cs[:, :GW] + cs[:, GW:])
        gparts = parts(gcum[d] * eye_f, 3)
        tr = _dot(ones_bf, jnp.concatenate(gparts, axis=1))
        grow = tr[:, :GW] + tr[:, GW:2 * GW] + tr[:, 2 * GW:]
        decay.append(jnp.where(incl[d], jnp.exp(jnp.where(incl[d], gcum[d] - grow, 0.0)), 0.0))
        kb.append(k * beta[d])
        egc.append(jnp.exp(gcum[d]))
        qi_out[0, d] = (q * egc[d]).astype(BF16)
    pairs = [(c, d) for c in range(CPT) for d in range(2)]
    rows = [slice(c * CH, (c + 1) * CH) for c in range(CPT)]
    eye = eye_t[:CH]
    a = {}
    for c in range(CPT):
        r = rows[c]
        lhs = jnp.concatenate([kb[0][r], kb[1][r], q[r]], axis=0).astype(BF16)
        aq = _dot_nt(lhs, _bd(k[r], bd))
        for d in range(2):
            dec = decay[d][r]
            a[c, d] = jnp.where(strict[d][:CH], aq[d * CH:(d + 1) * CH] * dec, 0.0)
            a_out[0, d, r, :] = jnp.where(incl[d][:CH], aq[2 * CH:] * dec, 0.0).astype(BF16)
    t_inv = {cd: eye - a[cd] for cd in pairs}
    p = {cd: _dot(a[cd].astype(BF16), _bd(a[cd], bd)) for cd in pairs}
    for it in range(5):
        for cd in pairs:
            pbd = _bd(p[cd], bd)
            if it < 4:
                res = _dot(jnp.concatenate([t_inv[cd], p[cd]], axis=0).astype(BF16), pbd)
                t_inv[cd] = t_inv[cd] + res[:CH]
                p[cd] = res[CH:]
            else:
                t_inv[cd] = t_inv[cd] + _dot(t_inv[cd].astype(BF16), pbd)

    def split(m):
        hi = m.astype(BF16)
        return hi, (m - hi.astype(F32)).astype(BF16)

    for cd in pairs:
        x0 = t_inv[cd]
        ah, al = split(a[cd])
        xh, xl = split(x0)
        hx = _dot(jnp.concatenate([ah, al], axis=0), _bd(xh, bd))
        resid = eye - x0 - (hx[:CH] + hx[CH:] + _dot(ah, _bd(xl, bd)))
        t_inv[cd] = x0 + _dot(xh, _bd(resid, bd))
    for c, d in pairs:
        r = rows[c]
        tb = t_inv[c, d].astype(BF16)
        u_out[0, d, r, :] = _dot(tb, _bd(v[r] * beta[d][r], bd))
        w_out[0, d, r, :] = _dot(tb, _bd(kb[d][r] * egc[d][r], bd)).astype(BF16)
        last = (c + 1) * CH - 1 if d == 0 else c * CH
        gtot = gcum[d][last:last + 1]
        k_end = (k[r] * jnp.exp(gtot - gcum[d][r])).astype(BF16)
        ket_out[0, d, c] = _dot_nt(eye_bf, k_end).astype(BF16)
        ge_out[0, d, c] = jnp.exp(gtot)


def _dn_local(pc, pcs, conv_w, a_log, dt_bias):
    b, t, w = pc.shape
    nt = t // TM
    left, right = _halo_specs(w, DN_HALO, nt)
    row = lambda bb, i: (bb, i, 0)
    pad_vec = lambda a: jnp.zeros((1, LANES), F32).at[0, 2 * HEADS:4 * HEADS].set(a.reshape(-1))
    vec = pl.BlockSpec((1, LANES), lambda bb, i: (0, 0))
    drow = pl.BlockSpec((1, 2, TM, GW), lambda bb, i: (bb, 0, i, 0))
    return _Part(
        _dn_local_kernel,
        (pc, pc, pc, pcs, conv_w, pad_vec(a_log), pad_vec(dt_bias)),
        [pl.BlockSpec((1, TM, w), row), left, right, pl.BlockSpec((1, TM, LANES), row),
         pl.BlockSpec((DN_CONV, w), lambda bb, i: (0, 0)), vec, vec],
        [drow, drow, drow, drow,
         pl.BlockSpec((1, 2, CPT, GW, CH), lambda bb, i: (bb, 0, i, 0, 0)),
         pl.BlockSpec((1, 2, CPT, 1, GW), lambda bb, i: (bb, 0, i, 0, 0))],
        [jax.ShapeDtypeStruct((b, 2, t, GW), F32)] + [jax.ShapeDtypeStruct((b, 2, t, GW), BF16)] * 3
        + [jax.ShapeDtypeStruct((b, 2, t // CH, GW, CH), BF16), jax.ShapeDtypeStruct((b, 2, t // CH, 1, GW), F32)],
        [pltpu.VMEM((TM + 2 * DN_HALO, w), F32)])


def _dn_chunk(refs, o_ref, s_scr, d, cc, bd):
    u, w, qi, a, ket, ge = refs
    rows = slice(cc * CH, (cc + 1) * CH)
    s = s_scr[d]
    wq = _dot(jnp.concatenate([w[0, 0, rows, :], qi[0, 0, rows, :]], axis=0), s.astype(BF16))
    v_new = u[0, 0, rows, :] - wq[:CH]
    o_ref[0, rows, :] = wq[CH:] + _dot(a[0, 0, rows, :], _bd(v_new, bd))
    s_scr[d] = s * ge[0, 0, cc] + jnp.where(bd, _dot(ket[0, 0, cc], v_new.astype(BF16)), 0.0)


def _dir_specs(shape_tail, n_tiles, chunked):
    blk = (1, 1, CPT if chunked else TM) + shape_tail
    zeros = (0,) * len(shape_tail)
    fwd = pl.BlockSpec(blk, lambda b, s: (b, 0, _fwd_tile(s, n_tiles)) + zeros)
    rev = pl.BlockSpec(blk, lambda b, s: (b, 1, _rev_tile(s, n_tiles)) + zeros)
    return fwd, rev


GLA_QK = HEADS * GLA_K


def _gla_local_kernel(pd_ref, pdl_ref, w2_ref, b2_ref, qi_out, a_out, ke_out, vt_out, vb_out, de_out):
    _, incl, _, _, _ = _chunk_masks()
    rr = _iota((TM, TM), 0)
    cc = _iota((TM, TM), 1)
    same_chunk = (rr >> 6) == (cc >> 6)
    eye_bf = (rr == cc).astype(BF16)
    bdk = (_iota((GW, GLA_QK), 0) >> 6) == (_iota((GW, GLA_QK), 1) >> 5)
    z = _dot(pdl_ref[0], w2_ref[...], HI) + b2_ref[...]
    gk_all = -_softplus(-z) * (1.0 / GLA_TAU)
    bcs_all = []
    for d in range(2):
        cum_bd = jnp.where(jnp.logical_and(same_chunk, cc <= rr if d == 0 else cc >= rr), 1.0, 0.0).astype(BF16)
        gparts = _bf16_parts(gk_all[:, d * GLA_QK:(d + 1) * GLA_QK], 2)
        cs = _dot(cum_bd, jnp.concatenate(gparts, axis=1))
        bcs_all.append(cs[:, :GLA_QK] + cs[:, GLA_QK:])
    for c in range(CPT):
        rows = slice(c * CH, (c + 1) * CH)
        p = pd_ref[0, rows, :]
        q = p[:, :GLA_QK] * (GLA_K ** -0.5)
        k = p[:, GLA_QK:2 * GLA_QK]
        vb = p[:, 2 * GLA_QK:].astype(BF16)
        vb_out[0, rows, :] = vb
        vt_out[0, c] = _dot_nt(eye_bf, vb).astype(BF16)
        for d in range(2):
            bcs = bcs_all[d][rows]
            bend = bcs[CH - 1:CH] if d == 0 else bcs[0:1]
            q_in = (q * jnp.exp(bcs)).astype(BF16)
            kdec = jnp.where(bdk, _tile4(k * jnp.exp(-bcs)), 0.0).astype(BF16)
            a_out[0, d, rows, :] = jnp.where(incl[d], _dot_nt(q_in, kdec), 0.0).astype(BF16)
            qi_out[0, d, rows, :] = q_in
            ke_out[0, d, rows, :] = (k * jnp.exp(bend - bcs)).astype(BF16)
            de_out[0, d, c] = jnp.exp(bend)


def _gla_local(pd, pdl, w2bd, b2):
    b, t, w = pd.shape
    nt = t // TM
    row = lambda bb, i: (bb, i, 0)
    return _Part(
        _gla_local_kernel,
        (pd, pdl, w2bd, b2),
        [pl.BlockSpec((1, TM, w), row), pl.BlockSpec((1, TM, LANES), row),
         pl.BlockSpec((LANES, GW), lambda bb, i: (0, 0)), pl.BlockSpec((1, GW), lambda bb, i: (0, 0))],
        [pl.BlockSpec((1, 2, TM, GLA_QK), lambda bb, i: (bb, 0, i, 0)),
         pl.BlockSpec((1, 2, TM, GW), lambda bb, i: (bb, 0, i, 0)),
         pl.BlockSpec((1, 2, TM, GLA_QK), lambda bb, i: (bb, 0, i, 0)),
         pl.BlockSpec((1, CPT, GW, CH), lambda bb, i: (bb, i, 0, 0)),
         pl.BlockSpec((1, TM, GW), row),
         pl.BlockSpec((1, 2, CPT, 1, GLA_QK), lambda bb, i: (bb, 0, i, 0, 0))],
        [jax.ShapeDtypeStruct((b, 2, t, GLA_QK), BF16), jax.ShapeDtypeStruct((b, 2, t, GW), BF16),
         jax.ShapeDtypeStruct((b, 2, t, GLA_QK), BF16), jax.ShapeDtypeStruct((b, t // CH, GW, CH), BF16),
         jax.ShapeDtypeStruct((b, t, GW), BF16), jax.ShapeDtypeStruct((b, 2, t // CH, 1, GLA_QK), F32)],
        [])


def _gla_chunk(refs, o_ref, s_scr, d, cc, bd, bdt):
    qi, a, ke, de, vt, v = refs
    rows = slice(cc * CH, (cc + 1) * CH)
    st = s_scr[d]
    vbd = jnp.where(bd, _tile4(v[0, rows, :]), jnp.zeros((), BF16))
    o_ref[0, rows, :] = _dot_nt(qi[0, 0, rows, :], st.astype(BF16)) + _dot(a[0, 0, rows, :], vbd)
    s_scr[d] = st * de[0, 0, cc] + jnp.where(bdt, _dot(vt[0, cc], ke[0, 0, rows, :]), 0.0)


def _scan_kernel(*refs):
    dn_in = (refs[0:6], refs[6:12])
    gla_in = (refs[12:18], refs[18:24])
    dn_out = refs[24:26]
    gla_out = refs[26:28]
    dn_s, gla_s = refs[28:30]

    @pl.when(pl.program_id(1) == 0)
    def _():
        dn_s[...] = jnp.zeros(dn_s.shape, F32)
        gla_s[...] = jnp.zeros(gla_s.shape, F32)

    bd = (_iota((GW, GW), 0) >> 6) == (_iota((GW, GW), 1) >> 6)
    bdt = (_iota((GW, GLA_QK), 0) >> 6) == (_iota((GW, GLA_QK), 1) >> 5)
    for c in range(CPT):
        for d in range(2):
            cc = c if d == 0 else CPT - 1 - c
            _dn_chunk(dn_in[d], dn_out[d], dn_s, d, cc, bd)
            _gla_chunk(gla_in[d], gla_out[d], gla_s, d, cc, bd, bdt)


def _scans(u, w, dqi, da, ket, ge, qi, a, ke, de, vt, vb):
    b, _, t, _ = u.shape
    nt = t // TM
    rowf, rowr = _dir_specs((GW,), nt, False)
    ketf, ketr = _dir_specs((GW, CH), nt, True)
    gef, ger = _dir_specs((1, GW), nt, True)
    qf, qr = _dir_specs((GLA_QK,), nt, False)
    df, dr = _dir_specs((1, GLA_QK), nt, True)
    vtf = pl.BlockSpec((1, CPT, GW, CH), lambda bb, s: (bb, _fwd_tile(s, nt), 0, 0))
    vtr = pl.BlockSpec((1, CPT, GW, CH), lambda bb, s: (bb, _rev_tile(s, nt), 0, 0))
    vf = pl.BlockSpec((1, TM, GW), lambda bb, s: (bb, _fwd_tile(s, nt), 0))
    vr = pl.BlockSpec((1, TM, GW), lambda bb, s: (bb, _rev_tile(s, nt), 0))
    dn_args = (u, w, dqi, da, ket, ge)
    gla_args = (qi, a, ke, de, vt, vb)
    return pl.pallas_call(
        _scan_kernel,
        grid=(b, nt),
        in_specs=[rowf, rowf, rowf, rowf, ketf, gef, rowr, rowr, rowr, rowr, ketr, ger,
                  qf, rowf, qf, df, vtf, vf, qr, rowr, qr, dr, vtr, vr],
        out_specs=[vf, vr, vf, vr],
        out_shape=[jax.ShapeDtypeStruct((b, t, GW), F32)] * 4,
        scratch_shapes=[pltpu.VMEM((2, GW, GW), F32), pltpu.VMEM((2, GW, GLA_QK), F32)],
        compiler_params=_cp(("parallel", "arbitrary")),
    )(*dn_args, *dn_args, *gla_args, *gla_args)


def _outproj_kernel(x_ref, ya_ref, yb_ref, cf_ref, cr_ref, df_ref, dr_ref, pg_ref, g1b_ref, g1c_ref, gc_ref, gd_ref,
                    w_ref, o_ref, *, lat_rows):
    ones64 = _group_ones(GW, 6)

    def fin(o, g, gate):
        ms = _dot_sel(o * o, ones64) * (1.0 / HEAD_DIM)
        return (o * lax.rsqrt(ms + EPS) * g * _silu(gate)).astype(BF16)

    pg = pg_ref[0]
    yc = fin(cf_ref[0] + cr_ref[0], gc_ref[...], pg[:, :GW])
    yd = fin(df_ref[0] + dr_ref[0], gd_ref[...], pg[:, GW:])
    res = (_dot(ya_ref[0].astype(BF16), w_ref[0:GW, :]) + _dot(yb_ref[0].astype(BF16), w_ref[GW:2 * GW, :])
           + _dot(yc, w_ref[2 * GW:3 * GW, :]) + _dot(yd, w_ref[3 * GW:, :]))
    g1 = jnp.where(_ctx_rows(res.shape[0], lat_rows), g1c_ref[0], g1b_ref[0])
    o_ref[0] = x_ref[0] + g1 * res


def _outproj(xs, ya, yb, ocf, ocr, odf, odr, pg, mod3, dn_g, gla_g, w_out_bf, n_batch, rows, n_blk):
    b, t, d = xs.shape
    reps = GW // HEAD_DIM
    tb = rows // n_blk
    row = lambda bb, i: (bb, i, 0)
    g256 = pl.BlockSpec((1, tb, GW), row)
    in_specs = ([pl.BlockSpec((1, tb, d), row), g256, g256, g256, g256, g256, g256, pl.BlockSpec((1, tb, 2 * GW), row)]
                + _mod_specs(2, n_batch) + [_resident((1, GW)), _resident((1, GW)), _resident((d, d))])
    args = (xs, ya, yb, ocf, ocr, odf, odr, pg, mod3, mod3, jnp.tile(dn_g, reps).reshape(1, GW),
            jnp.tile(gla_g, reps).reshape(1, GW), w_out_bf)
    kern = functools.partial(_outproj_kernel, lat_rows=t - TM)
    return _row_call(kern, n_batch, rows, args, in_specs, [d], n_blk=n_blk)[0]


FF_HALO = 8


def _ffn_kernel(xm_ref, xl_ref, xr_ref, shb_ref, shc_ref, scb_ref, scc_ref, g2b_ref, g2c_ref, ng_ref,
                wu_ref, cw_ref, wd_ref, o_ref, ext_ref, *, lat_rows):
    tb = xm_ref.shape[1]
    i = pl.program_id(1)
    rows_ext = _iota((tb + 2 * FF_HALO, 1), 0)
    left_ok = jnp.logical_and(i > 0, i * tb != lat_rows)
    right_ok = jnp.logical_and(i < pl.num_programs(1) - 1, (i + 1) * tb != lat_rows)
    keep = jnp.logical_and(jnp.logical_or(rows_ext >= FF_HALO, left_ok),
                           jnp.logical_or(rows_ext < FF_HALO + tb, right_ok))
    grow = i * tb - FF_HALO + rows_ext
    ctx_ext = grow >= lat_rows
    x = jnp.concatenate([xl_ref[0], xm_ref[0], xr_ref[0]], axis=0)
    y = x * lax.rsqrt(jnp.mean(x * x, axis=-1, keepdims=True) + EPS) * ng_ref[...]
    h = y * (1.0 + jnp.where(ctx_ext, scc_ref[0], scb_ref[0])) + jnp.where(ctx_ext, shc_ref[0], shb_ref[0])
    h = jnp.where(keep, h, 0.0).astype(BF16)
    inner_boundary = lat_rows % tb != 0
    if inner_boundary:
        row = grow[FF_HALO:FF_HALO + tb]
        m_prev = jnp.broadcast_to(jnp.where(row == lat_rows, 0.0, 1.0), (tb, FF_BLK))
        m_next = jnp.broadcast_to(jnp.where(row == lat_rows - 1, 0.0, 1.0), (tb, FF_BLK))
    acc = jnp.zeros((tb, D_MODEL), F32)
    for j in range(N_FF_BLK):
        def conv(col0, half):
            cols = slice(col0, col0 + FF_BLK)
            ext_ref[half] = _dot(h, wu_ref[:, cols])
            cw = cw_ref[:, cols]
            prev = ext_ref[half, pl.ds(FF_HALO - 1, tb), :]
            nxt = ext_ref[half, pl.ds(FF_HALO + 1, tb), :]
            if inner_boundary:
                prev = m_prev * prev
                nxt = m_next * nxt
            return cw[0:1] * prev + cw[1:2] * ext_ref[half, pl.ds(FF_HALO, tb), :] + cw[2:3] * nxt
        a = conv(j * FF_BLK, 0)
        g = conv(D_FF + j * FF_BLK, 1)
        acc = acc + _dot((_silu(g) * a).astype(BF16), wd_ref[j * FF_BLK:(j + 1) * FF_BLK, :])
    g2 = jnp.where(ctx_ext[FF_HALO:FF_HALO + tb], g2c_ref[0], g2b_ref[0])
    o_ref[0] = xm_ref[0] + g2 * acc


def _ffn(x1, mod3, norm_g, w_up, cw, w_down, n_batch, lat_rows, rows, n_blk):
    b, t, d = x1.shape
    tb = rows // n_blk
    per = tb // FF_HALO
    last = t // FF_HALO - 1
    in_specs = ([pl.BlockSpec((1, tb, d), lambda bb, i: (bb, i, 0)),
                 pl.BlockSpec((1, FF_HALO, d), lambda bb, i: (bb, jnp.maximum(i * per - 1, 0), 0)),
                 pl.BlockSpec((1, FF_HALO, d), lambda bb, i: (bb, jnp.minimum((i + 1) * per, last), 0))]
                + _mod_specs(3, n_batch) + _mod_specs(4, n_batch) + _mod_specs(5, n_batch)
                + [_resident((1, d)), _resident(w_up.shape), _resident(cw.shape), _resident(w_down.shape)])
    args = (x1, x1, x1, mod3, mod3, mod3, mod3, mod3, mod3, norm_g.reshape(1, d), w_up, cw, w_down)
    kern = functools.partial(_ffn_kernel, lat_rows=lat_rows)
    return _row_call(kern, n_batch, rows, args, in_specs, [d],
                     scratch=[pltpu.VMEM((2, tb + 2 * FF_HALO, FF_BLK), F32)], n_blk=n_blk)[0]


def _rope_tables(seq, ctx_len):
    rows = seq // GRID_W
    row = jnp.repeat(jnp.arange(rows, dtype=F32), GRID_W)
    col = jnp.tile(jnp.arange(GRID_W, dtype=F32), rows)
    nf = QK_DIM // 4
    inv = ROPE_THETA ** (-jnp.arange(nf, dtype=F32) / nf)
    ang = jnp.concatenate([row[:, None] * inv, col[:, None] * inv], axis=-1)
    cos = jnp.concatenate([jnp.cos(ang), jnp.ones((ctx_len, QK_DIM // 2), F32)], axis=0)
    sin = jnp.concatenate([jnp.sin(ang), jnp.zeros((ctx_len, QK_DIM // 2), F32)], axis=0)
    reps = GW // QK_DIM
    return (jnp.tile(jnp.concatenate([cos, cos], axis=-1), (1, reps)),
            jnp.tile(jnp.concatenate([-sin, sin], axis=-1), (1, reps)))


def _regroup_w_in(w):
    d = w.shape[0]
    z = lambda n: jnp.zeros((d, n), w.dtype)
    return jnp.concatenate([w[:, :2048], w[:, 2048:2064], z(LANES - 16), w[:, 2320:2832], w[:, 2832:2864],
                            z(LANES - 32), w[:, 2064:2320], w[:, 2864:3120]], axis=1).astype(BF16)


def _gla_w2_blockdiag(w2):
    out = jnp.zeros((LANES, GW), F32)
    out = out.at[0:GLA_RANK, 0:GLA_QK].set(w2[0])
    return out.at[GLA_RANK:2 * GLA_RANK, GLA_QK:].set(w2[1])


def _layer(xs, mod3, lp, cos_t, sin_t, layer_idx, last, n_batch):
    b, t, d = xs.shape
    nt = t // TM
    pa, pb, pc, pcs, pd, pdl, pg = _inproj(xs, mod3, lp["norm1_g"], _regroup_w_in(lp["w_in"]), n_batch)

    lam_init = 0.8 - 0.6 * math.exp(-0.3 * layer_idx)
    (q, kt, v, qn, kn), (ya,), dn_parts, (qi, a, ke, vt, vb, de) = _fused_call(
        [_attn_prep(pb, lp["da_qnorm_g"], lp["da_knorm_g"], cos_t, sin_t),
         _conv_module(pa, lp["cm_conv_w"], lp["cm_conv_b"], lp["cm_ln_g"], lp["cm_ln_b"]),
         _dn_local(pc, pcs, lp["dn_conv_w"], lp["dn_a_log"], lp["dn_dt_bias"]),
         _gla_local(pd, pdl, _gla_w2_blockdiag(lp["gla_w2"]), lp["gla_b2"].reshape(1, GW))],
        (b, nt))
    yb = _attention(q, kt, v, qn, kn, lp["da_lambda"], lp["da_subln_g"], lam_init)
    ocf, ocr, odf, odr = _scans(*dn_parts, qi, a, ke, de, vt, vb)

    lat_rows = t - TM
    rows = lat_rows if last else t
    x1 = _outproj(xs, ya, yb, ocf, ocr, odf, odr, pg, mod3, lp["dn_onorm_g"], lp["gla_onorm_g"],
                  lp["w_out"].astype(BF16), n_batch, rows, N_BLK_FFN)
    return _ffn(x1, mod3, lp["norm2_g"], lp["ffn_w_up"].astype(BF16), lp["ffn_conv_w"],
                lp["ffn_w_down"].astype(BF16), n_batch, lat_rows, rows, N_BLK_FFN)


def kernel(x, c, ctx, c_ctx, w_mod, b_mod, norm1_g, norm2_g, w_in, w_out, cm_conv_w, cm_conv_b, cm_ln_g, cm_ln_b, da_qnorm_g, da_knorm_g, da_lambda, da_subln_g, dn_conv_w, dn_a_log, dn_dt_bias, dn_onorm_g, gla_w2, gla_b2, gla_onorm_g, ffn_w_up, ffn_conv_w, ffn_w_down):
    n_batch, seq, d = x.shape
    ctx_len = ctx.shape[1]
    assert ctx_len == TM and seq % TM == 0 and (seq + ctx_len) % (8 * N_BLK) == 0 and d == D_MODEL
    depth = w_mod.shape[0]
    cos_t, sin_t = _rope_tables(seq, ctx_len)
    xs = jnp.concatenate([x, ctx], axis=1)
    mod_rows = 16
    c_rows = jnp.zeros((mod_rows, d), F32).at[:n_batch].set(c).at[n_batch].set(c_ctx)
    params = dict(w_mod=w_mod, b_mod=b_mod, norm1_g=norm1_g, norm2_g=norm2_g, w_in=w_in, w_out=w_out,
                  cm_conv_w=cm_conv_w, cm_conv_b=cm_conv_b, cm_ln_g=cm_ln_g, cm_ln_b=cm_ln_b,
                  da_qnorm_g=da_qnorm_g, da_knorm_g=da_knorm_g, da_lambda=da_lambda, da_subln_g=da_subln_g,
                  dn_conv_w=dn_conv_w, dn_a_log=dn_a_log, dn_dt_bias=dn_dt_bias, dn_onorm_g=dn_onorm_g,
                  gla_w2=gla_w2, gla_b2=gla_b2, gla_onorm_g=gla_onorm_g,
                  ffn_w_up=ffn_w_up, ffn_conv_w=ffn_conv_w, ffn_w_down=ffn_w_down)
    for l in range(depth):
        lp = {k: v[l] for k, v in params.items()}
        mod3 = _modulation(c_rows, lp["w_mod"], lp["b_mod"]).reshape(mod_rows * 6, 1, d)
        xs = _layer(xs, mod3, lp, cos_t, sin_t, l, l == depth - 1, n_batch)
    return xs
```

```python
import functools
import math
from typing import Any, NamedTuple

import jax
import jax.numpy as jnp
from jax import lax
from jax.experimental import pallas as pl
from jax.experimental.pallas import tpu as pltpu

F32 = jnp.float32
BF16 = jnp.bfloat16
HI = lax.Precision.HIGHEST
EPS = 1e-6

D_MODEL = 1024
GRID_W = 64
HEADS = 4
HEAD_DIM = 64
GW = 256
QK_DIM = 32
GLA_K = 32
GLA_RANK = 16
GLA_TAU = 16.0
CM_KERNEL = 31
DN_CONV = 5
ROPE_THETA = 10000.0
CH = 64
TM = 256
CPT = TM // CH
D_FF = 2816
FF_BLK = 256
N_FF_BLK = D_FF // FF_BLK
LANES = 128

IN_GROUPS = (("pa", 512), ("pb", 768), ("pc", 768), ("pcs", LANES), ("pd", 512), ("pdl", LANES), ("pg", 512))
IN_COLS_PAD = sum(w for _, w in IN_GROUPS)

VMEM_LIMIT = 56 * 1024 * 1024


def _cp(sem):
    return pltpu.CompilerParams(dimension_semantics=sem, vmem_limit_bytes=VMEM_LIMIT)


def _dot(a, b, prec=None):
    return jnp.dot(a, b, preferred_element_type=F32, precision=prec)


def _dot_nt(a, b, prec=None):
    return lax.dot_general(a, b, (((1,), (1,)), ((), ())), preferred_element_type=F32, precision=prec)


def _sigmoid(x):
    return 1.0 / (1.0 + jnp.exp(-x))


def _silu(x):
    return x * _sigmoid(x)


def _softplus(x):
    return jnp.maximum(x, 0.0) + jnp.log(1.0 + jnp.exp(-jnp.abs(x)))


def _iota(shape, dim):
    return lax.broadcasted_iota(jnp.int32, shape, dim)


def _group_ones(n, shift):
    return ((_iota((n, n), 0) >> shift) == (_iota((n, n), 1) >> shift)).astype(BF16)


def _bf16_parts(m, n):
    out = []
    for _ in range(n):
        hi = m.astype(BF16)
        out.append(hi)
        m = m - hi.astype(F32)
    return out


def _dot_sel(x, sel, n_parts=2):
    rows = x.shape[0]
    r = _dot(jnp.concatenate(_bf16_parts(x, n_parts), axis=0), sel)
    out = r[:rows]
    for k in range(1, n_parts):
        out = out + r[k * rows:(k + 1) * rows]
    return out


def _tile4(y):
    return jnp.concatenate([y, y, y, y], axis=0)


def _mod_kernel(c_ref, w_ref, b_ref, o_ref):
    o_ref[...] = _dot(_silu(c_ref[...]), w_ref[...], HI) + b_ref[...]


def _modulation(c_rows, w_mod, b_mod):
    r, d = c_rows.shape
    n = w_mod.shape[1] // d
    return pl.pallas_call(
        _mod_kernel,
        grid=(n,),
        in_specs=[pl.BlockSpec((r, d), lambda j: (0, 0)),
                  pl.BlockSpec((d, d), lambda j: (0, j)),
                  pl.BlockSpec((1, d), lambda j: (0, j))],
        out_specs=pl.BlockSpec((r, d), lambda j: (0, j)),
        out_shape=jax.ShapeDtypeStruct((r, n * d), F32),
        compiler_params=_cp(("arbitrary",)),
    )(c_rows, w_mod, b_mod.reshape(1, -1))


N_BLK = 4
N_BLK_FFN = 8


def _mod_specs(k, n_batch):
    return [pl.BlockSpec((1, 1, D_MODEL), lambda b, i: (b * 6 + k, 0, 0)),
            pl.BlockSpec((1, 1, D_MODEL), lambda b, i: (n_batch * 6 + k, 0, 0))]


def _ctx_rows(tb, lat_rows):
    return pl.program_id(1) * tb + _iota((tb, 1), 0) >= lat_rows


def _resident(shape):
    zeros = (0,) * len(shape)
    return pl.BlockSpec(shape, lambda b, i: zeros, pipeline_mode=pl.Buffered(1))


def _row_call(kern, n_batch, rows, args, in_specs, out_widths, scratch=(), n_blk=N_BLK):
    tb = rows // n_blk
    return pl.pallas_call(
        kern,
        grid=(n_batch, n_blk),
        in_specs=in_specs,
        out_specs=[pl.BlockSpec((1, tb, w), lambda bb, i: (bb, i, 0)) for w in out_widths],
        out_shape=[jax.ShapeDtypeStruct((n_batch, rows, w), F32) for w in out_widths],
        scratch_shapes=list(scratch),
        compiler_params=_cp(("parallel", "arbitrary")),
    )(*args)


def _inproj_kernel(x_ref, shb_ref, shc_ref, scb_ref, scc_ref, g_ref, w_ref, *outs, lat_rows):
    x = x_ref[0]
    ctx = _ctx_rows(x.shape[0], lat_rows)
    y = x * lax.rsqrt(jnp.mean(x * x, axis=-1, keepdims=True) + EPS) * g_ref[...]
    h = y * (1.0 + jnp.where(ctx, scc_ref[0], scb_ref[0])) + jnp.where(ctx, shc_ref[0], shb_ref[0])
    p = _dot(h.astype(BF16), w_ref[...])
    off = 0
    for o_ref, (_, width) in zip(outs, IN_GROUPS):
        o_ref[0] = p[:, off:off + width]
        off += width


def _inproj(xs, mod3, norm_g, w_in_r, n_batch):
    b, t, d = xs.shape
    in_specs = ([pl.BlockSpec((1, t // N_BLK, d), lambda bb, i: (bb, i, 0))] + _mod_specs(0, n_batch)
                + _mod_specs(1, n_batch) + [_resident((1, d)), _resident((d, IN_COLS_PAD))])
    kern = functools.partial(_inproj_kernel, lat_rows=t - TM)
    return _row_call(kern, n_batch, t, (xs, mod3, mod3, mod3, mod3, norm_g.reshape(1, d), w_in_r), in_specs,
                     [w for _, w in IN_GROUPS])


class _Part(NamedTuple):
    body: Any
    args: Any
    in_specs: Any
    out_specs: Any
    out_shapes: Any
    scratch: Any


def _fused_call(parts, grid):
    n_in = [len(p.args) for p in parts]
    n_out = [len(p.out_specs) for p in parts]
    n_scr = [len(p.scratch) for p in parts]

    def kern(*refs):
        ins, outs, scr = refs[:sum(n_in)], refs[sum(n_in):sum(n_in) + sum(n_out)], refs[sum(n_in) + sum(n_out):]
        i = o = s = 0
        for p, ni, no, ns in zip(parts, n_in, n_out, n_scr):
            p.body(*ins[i:i + ni], *outs[o:o + no], *scr[s:s + ns])
            i, o, s = i + ni, o + no, s + ns

    res = pl.pallas_call(
        kern,
        grid=grid,
        in_specs=[sp for p in parts for sp in p.in_specs],
        out_specs=[sp for p in parts for sp in p.out_specs],
        out_shape=[sh for p in parts for sh in p.out_shapes],
        scratch_shapes=[sc for p in parts for sc in p.scratch],
        compiler_params=_cp(("parallel", "arbitrary")),
    )(*[a for p in parts for a in p.args])
    out, o = [], 0
    for no in n_out:
        out.append(res[o:o + no])
        o += no
    return out


def _halo_specs(width, halo, n_tiles):
    per = TM // halo
    left = pl.BlockSpec((1, halo, width), lambda b, i: (b, jnp.maximum(i * per - 1, 0), 0))
    right = pl.BlockSpec((1, halo, width), lambda b, i: (b, jnp.minimum((i + 1) * per, n_tiles * per - 1), 0))
    return left, right


def _halo_ok(i, n_tiles):
    return jnp.logical_and(i >= 1, i < n_tiles - 1), i < n_tiles - 2


CM_HALO = 16


def _convmod_kernel(pm_ref, pl_ref, pr_ref, cw_ref, cb_ref, lg_ref, lb_ref, o_ref, ext_ref):
    left_ok, right_ok = _halo_ok(pl.program_id(1), pl.num_programs(1))

    def glu(p):
        return p[:, :GW] * _sigmoid(p[:, GW:])

    ext_ref[0:CM_HALO] = jnp.where(left_ok, glu(pl_ref[0]), 0.0)
    ext_ref[CM_HALO:CM_HALO + TM] = glu(pm_ref[0])
    ext_ref[CM_HALO + TM:] = jnp.where(right_ok, glu(pr_ref[0]), 0.0)
    pad = CM_KERNEL // 2
    sub = 8
    y = cb_ref[...]
    for r in range(sub):
        acc = None
        for j in range(CM_KERNEL):
            off = CM_HALO - pad + j
            if off % sub == r:
                term = cw_ref[j:j + 1, :] * ext_ref[pl.ds(off - r, TM + sub), :]
                acc = term if acc is None else acc + term
        if acc is not None:
            y = y + acc[r:r + TM]
    mu = jnp.mean(y, axis=-1, keepdims=True)
    yc = y - mu
    var = jnp.mean(yc * yc, axis=-1, keepdims=True)
    o_ref[0] = _silu(yc * lax.rsqrt(var + EPS) * lg_ref[...] + lb_ref[...])


def _conv_module(pa, conv_w, conv_b, ln_g, ln_b):
    b, t, w = pa.shape
    nt = t // TM
    left, right = _halo_specs(w, CM_HALO, nt)
    vec = pl.BlockSpec((1, GW), lambda bb, i: (0, 0))
    return _Part(
        _convmod_kernel,
        (pa, pa, pa, conv_w, conv_b.reshape(1, GW), ln_g.reshape(1, GW), ln_b.reshape(1, GW)),
        [pl.BlockSpec((1, TM, w), lambda bb, i: (bb, i, 0)), left, right,
         pl.BlockSpec((CM_KERNEL, GW), lambda bb, i: (0, 0)), vec, vec, vec],
        [pl.BlockSpec((1, TM, GW), lambda bb, i: (bb, i, 0))],
        [jax.ShapeDtypeStruct((b, t, GW), F32)],
        [pltpu.VMEM((TM + 2 * CM_HALO, GW), F32)])


N_MAPS = 2 * HEADS
QK_PAD = 2 * QK_DIM
V_EXT = 2 * HEAD_DIM
MASK_BIG = 8192.0
SHIFT_MAX = 40.0


def _attn_prep_kernel(pb_ref, qg_ref, kg_ref, cos_ref, sin_ref, q_out, kt_out, v_out, qn_out, kn_out):
    p = pb_ref[0]
    ones32 = _group_ones(GW, 5)
    first = (_iota((TM, GW), 1) & (QK_DIM - 1)) < QK_DIM // 2
    cos = cos_ref[...]
    sin = sin_ref[...]

    def norm_rope(t, g):
        ms = _dot_sel(t * t, ones32) * (1.0 / QK_DIM)
        tn = t * lax.rsqrt(ms + EPS) * g
        partner = jnp.where(first, pltpu.roll(tn, GW - QK_DIM // 2, 1), pltpu.roll(tn, QK_DIM // 2, 1))
        return tn * cos + partner * sin

    qf = norm_rope(p[:, :GW], qg_ref[...]) * (QK_DIM ** -0.5)
    kf = norm_rope(p[:, GW:2 * GW], kg_ref[...])
    map_sel = ((_iota((GW, N_MAPS), 0) >> 5) == _iota((GW, N_MAPS), 1)).astype(BF16)
    qn_out[0] = _dot_sel(qf * qf, map_sel)
    kn_out[0] = _dot_sel(kf * kf, map_sel)
    q = qf.astype(BF16)
    kt = kf.T.astype(BF16)
    is_ctx = pl.program_id(1) == pl.num_programs(1) - 1
    k_row = _iota((QK_DIM, TM), 0)
    k_tail = jnp.where(k_row == 0, 1.0, jnp.where(jnp.logical_and(k_row == 1, jnp.logical_not(is_ctx)), 1.0, 0.0))
    k_tail = k_tail.astype(BF16)
    q_tail = jnp.where(jnp.logical_and(_iota((TM, QK_DIM), 1) == 1, is_ctx), -MASK_BIG, 0.0).astype(BF16)
    for g in range(N_MAPS):
        q_out[0, g, :, 0:QK_DIM] = q[:, g * QK_DIM:(g + 1) * QK_DIM]
        q_out[0, g, :, QK_DIM:] = q_tail
        kt_out[0, 0, g, 0:QK_DIM, :] = kt[g * QK_DIM:(g + 1) * QK_DIM, :]
        kt_out[0, 0, g, QK_DIM:, :] = k_tail
    v = p[:, 2 * GW:].astype(BF16)
    ones = jnp.ones((TM, HEAD_DIM), BF16)
    for h in range(HEADS):
        v_out[0, 0, h, :, 0:HEAD_DIM] = v[:, h * HEAD_DIM:(h + 1) * HEAD_DIM]
        v_out[0, 0, h, :, HEAD_DIM:] = ones


def _attn_prep(pb, qn_g, kn_g, cos_t, sin_t):
    b, t, w = pb.shape
    nt = t // TM
    vec = pl.BlockSpec((1, GW), lambda bb, i: (0, 0))
    tab = pl.BlockSpec((TM, GW), lambda bb, i: (i, 0))
    reps = GW // QK_DIM
    n_spec = pl.BlockSpec((1, TM, N_MAPS), lambda bb, i: (bb, i, 0))
    n_shape = jax.ShapeDtypeStruct((b, t, N_MAPS), F32)
    return _Part(
        _attn_prep_kernel,
        (pb, jnp.tile(qn_g, reps).reshape(1, GW), jnp.tile(kn_g, reps).reshape(1, GW), cos_t, sin_t),
        [pl.BlockSpec((1, TM, w), lambda bb, i: (bb, i, 0)), vec, vec, tab, tab],
        [pl.BlockSpec((1, N_MAPS, TM, QK_PAD), lambda bb, i: (bb, 0, i, 0)),
         pl.BlockSpec((1, 1, N_MAPS, QK_PAD, TM), lambda bb, i: (bb, i, 0, 0, 0)),
         pl.BlockSpec((1, 1, HEADS, TM, V_EXT), lambda bb, i: (bb, i, 0, 0, 0)), n_spec, n_spec],
        [jax.ShapeDtypeStruct((b, N_MAPS, t, QK_PAD), BF16),
         jax.ShapeDtypeStruct((b, nt, N_MAPS, QK_PAD, TM), BF16),
         jax.ShapeDtypeStruct((b, nt, HEADS, TM, V_EXT), BF16), n_shape, n_shape],
        [])


def _attn_kernel(q_ref, kt_ref, v_ref, qn_ref, kn_ref, lam_ref, sg_ref, o_ref, qa_scr, acc_scr, s_scr, m_scr,
                 *, n_chunks, lam_init):
    tq = o_ref.shape[1]
    acc_scr[...] = jnp.zeros(acc_scr.shape, F32)
    k2 = jnp.max(kn_ref[0], axis=0, keepdims=True)
    q2 = jnp.max(qn_ref[0], axis=0, keepdims=True)
    bound = jnp.sqrt(q2 * k2)
    safe = jnp.max(bound) <= SHIFT_MAX
    shift = jnp.where(safe, bound, 0.0)
    shift_lane = _iota((tq, QK_PAD), 1) == QK_DIM
    for g in range(N_MAPS):
        qa_scr[g] = jnp.where(shift_lane, (-shift[:, g:g + 1]).astype(BF16), q_ref[0, g])

    @pl.when(safe)
    def _():
        s_scr[0] = _dot(qa_scr[0], kt_ref[0, 0, 0])

        def body(c, carry):
            nxt = jnp.minimum(c + 1, n_chunks - 1)
            for g in range(N_MAPS):
                if g + 1 < N_MAPS:
                    s_next = _dot(qa_scr[g + 1], kt_ref[0, c, g + 1])
                else:
                    s_next = _dot(qa_scr[0], kt_ref[0, nxt, 0])
                p = jnp.exp(s_scr[g & 1]).astype(BF16)
                acc_scr[g] += _dot(p, v_ref[0, c, g // 2])
                s_scr[(g + 1) & 1] = s_next
            return carry

        lax.fori_loop(0, n_chunks, body, 0)

    @pl.when(jnp.logical_not(safe))
    def _():
        m_scr[...] = jnp.full(m_scr.shape, -jnp.inf, F32)

        def body(c, carry):
            for g in range(N_MAPS):
                s = _dot(qa_scr[g], kt_ref[0, c, g])
                m_old = m_scr[:, g:g + 1]
                m_new = jnp.maximum(m_old, jnp.max(s, axis=-1, keepdims=True))
                p = jnp.exp(s - m_new).astype(BF16)
                acc_scr[g] = jnp.exp(m_old - m_new) * acc_scr[g] + _dot(p, v_ref[0, c, g // 2])
                m_scr[:, g:g + 1] = m_new
            return carry

        lax.fori_loop(0, n_chunks, body, 0)

    lp = lam_ref[...]
    lam = (jnp.exp(jnp.sum(lp[0:1] * lp[1:2], axis=-1, keepdims=True))
           - jnp.exp(jnp.sum(lp[2:3] * lp[3:4], axis=-1, keepdims=True)) + lam_init)
    for h in range(HEADS):
        a0 = acc_scr[2 * h]
        a1 = acc_scr[2 * h + 1]
        o = (a0[:, :HEAD_DIM] / a0[:, HEAD_DIM:HEAD_DIM + 1]
             - lam * (a1[:, :HEAD_DIM] / a1[:, HEAD_DIM:HEAD_DIM + 1]))
        y = o * lax.rsqrt(jnp.mean(o * o, axis=-1, keepdims=True) + EPS) * sg_ref[...] * (1.0 - lam_init)
        o_ref[0, :, h * HEAD_DIM:(h + 1) * HEAD_DIM] = y


def _attention(q, kt, v, qn, kn, lam_p, subln_g, lam_init):
    b, _, t, _ = q.shape
    tq = t // N_BLK
    nt = t // TM
    kern = functools.partial(_attn_kernel, n_chunks=nt, lam_init=lam_init)
    return pl.pallas_call(
        kern,
        grid=(b, N_BLK),
        in_specs=[pl.BlockSpec((1, N_MAPS, tq, QK_PAD), lambda bb, i: (bb, 0, i, 0)),
                  pl.BlockSpec((1, nt, N_MAPS, QK_PAD, TM), lambda bb, i: (bb, 0, 0, 0, 0)),
                  pl.BlockSpec((1, nt, HEADS, TM, V_EXT), lambda bb, i: (bb, 0, 0, 0, 0)),
                  pl.BlockSpec((1, tq, N_MAPS), lambda bb, i: (bb, i, 0)),
                  pl.BlockSpec((1, t, N_MAPS), lambda bb, i: (bb, 0, 0)),
                  _resident((4, QK_DIM)), _resident((1, HEAD_DIM))],
        out_specs=pl.BlockSpec((1, tq, GW), lambda bb, i: (bb, i, 0)),
        out_shape=jax.ShapeDtypeStruct((b, t, GW), F32),
        scratch_shapes=[pltpu.VMEM((N_MAPS, tq, QK_PAD), BF16), pltpu.VMEM((N_MAPS, tq, V_EXT), F32),
                        pltpu.VMEM((2, tq, TM), F32), pltpu.VMEM((tq, N_MAPS), F32)],
        compiler_params=_cp(("parallel", "arbitrary")),
    )(q, kt, v, qn, kn, lam_p, subln_g.reshape(1, HEAD_DIM))


def _bd(y, bd):
    return jnp.where(bd, _tile4(y), 0.0).astype(BF16)


def _fwd_tile(s, n_tiles):
    return jnp.where(s == 0, n_tiles - 1, s - 1)


def _rev_tile(s, n_tiles):
    return jnp.where(s == 0, n_tiles - 1, n_tiles - 1 - s)


DN_HALO = 8


def _dn_prep(pm_ref, pl_ref, pr_ref, pcs_ref, cw_ref, alog_ref, dtb_ref, ext_ref):
    left_ok, right_ok = _halo_ok(pl.program_id(1), pl.num_programs(1))
    ext_ref[0:DN_HALO] = jnp.where(left_ok, pl_ref[0], 0.0)
    ext_ref[DN_HALO:DN_HALO + TM] = pm_ref[0]
    ext_ref[DN_HALO + TM:] = jnp.where(right_ok, pr_ref[0], 0.0)
    pad = DN_CONV // 2
    acc = jnp.zeros((TM, 3 * GW), F32)
    for j in range(DN_CONV):
        acc = acc + cw_ref[j:j + 1, :] * ext_ref[pl.ds(DN_HALO - pad + j, TM), :]
    qkv = _silu(acc)
    ones64 = _group_ones(GW, 6)

    def l2n(t):
        return t * lax.rsqrt(_dot_sel(t * t, ones64) + EPS)

    s = pcs_ref[0]
    col = _iota(s.shape, 1)
    gate = -jnp.exp(alog_ref[...]) * _softplus(s + dtb_ref[...])
    bg = jnp.where(col < 2 * HEADS, _sigmoid(s), jnp.where(col < 4 * HEADS, gate, 0.0))
    return l2n(qkv[:, :GW]) * (HEAD_DIM ** -0.5), l2n(qkv[:, GW:2 * GW]), qkv[:, 2 * GW:], bg


def _dn_local_kernel(pm_ref, pl_ref, pr_ref, pcs_ref, cw_ref, alog_ref, dtb_ref,
                     u_out, w_out, qi_out, a_out, ket_out, ge_out, ext_ref):
    rr = _iota((TM, GW), 0)
    cc = _iota((TM, GW), 1)
    i_in = rr & (CH - 1)
    j_in = cc & (CH - 1)
    bd = (rr >> 6) == (cc >> 6)
    eye_t = (i_in == j_in).astype(F32)
    incl = (j_in <= i_in, j_in >= i_in)
    strict = (j_in < i_in, j_in > i_in)
    eye_bf = (rr == cc).astype(BF16)
    head_of_lane = _iota((LANES, GW), 1) >> 6
    src = _iota((LANES, GW), 0)
    q, k, v, bg = _dn_prep(pm_ref, pl_ref, pr_ref, pcs_ref, cw_ref, alog_ref, dtb_ref, ext_ref)
    ones_bf = bd.astype(BF16)
    beta, gcum, decay, kb, egc = [], [], [], [], []
    for d in range(2):
        sel_b = (src == d * HEADS + head_of_lane).astype(BF16)
        sel_g = (src == 2 * HEADS + d * HEADS + head_of_lane).astype(BF16)
        cum_bd = jnp.where(jnp.logical_and(bd, incl[d]), 1.0, 0.0).astype(BF16)
        g_exp = _dot(jnp.concatenate(_bf16_parts(bg, 2), axis=0), sel_g)
        beta.append(_dot(bg.astype(BF16), sel_b))
        cs = _dot(cum_bd, jnp.concatenate([g_exp[:TM].astype(BF16), g_exp[TM:].astype(BF16)], axis=1))
        gcum.append(cs[:, :GW] + cs[:, GW:])
        gparts = _bf16_parts(gcum[d] * eye_t, 3)
        tr = _dot(ones_bf, jnp.concatenate(gparts, axis=1))
        grow = tr[:, :GW] + tr[:, GW:2 * GW] + tr[:, 2 * GW:]
        decay.append(jnp.where(incl[d], jnp.exp(jnp.where(incl[d], gcum[d] - grow, 0.0)), 0.0))
        kb.append(k * beta[d])
        egc.append(jnp.exp(gcum[d]))
        qi_out[0, d] = (q * egc[d]).astype(BF16)
    pairs = [(c, d) for c in range(CPT) for d in range(2)]
    rows = [slice(c * CH, (c + 1) * CH) for c in range(CPT)]
    eye = eye_t[:CH]
    a = {}
    for c in range(CPT):
        r = rows[c]
        lhs = jnp.concatenate([kb[0][r], kb[1][r], q[r]], axis=0).astype(BF16)
        aq = _dot_nt(lhs, _bd(k[r], bd))
        for d in range(2):
            dec = decay[d][r]
            a[c, d] = jnp.where(strict[d][:CH], aq[d * CH:(d + 1) * CH] * dec, 0.0)
            a_out[0, d, r, :] = jnp.where(incl[d][:CH], aq[2 * CH:] * dec, 0.0).astype(BF16)
    t_inv = {cd: eye - a[cd] for cd in pairs}
    p = {cd: _dot(a[cd].astype(BF16), _bd(a[cd], bd)) for cd in pairs}
    for it in range(5):
        for cd in pairs:
            pbd = _bd(p[cd], bd)
            if it < 4:
                res = _dot(jnp.concatenate([t_inv[cd], p[cd]], axis=0).astype(BF16), pbd)
                t_inv[cd] = t_inv[cd] + res[:CH]
                p[cd] = res[CH:]
            else:
                t_inv[cd] = t_inv[cd] + _dot(t_inv[cd].astype(BF16), pbd)

    def split(m):
        hi = m.astype(BF16)
        return hi, (m - hi.astype(F32)).astype(BF16)

    for cd in pairs:
        x0 = t_inv[cd]
        ah, al = split(a[cd])
        xh, xl = split(x0)
        hx = _dot(jnp.concatenate([ah, al], axis=0), _bd(xh, bd))
        resid = eye - x0 - (hx[:CH] + hx[CH:] + _dot(ah, _bd(xl, bd)))
        t_inv[cd] = x0 + _dot(xh, _bd(resid, bd))
    for c, d in pairs:
        r = rows[c]
        tb = t_inv[c, d].astype(BF16)
        u_out[0, d, r, :] = _dot(tb, _bd(v[r] * beta[d][r], bd))
        w_out[0, d, r, :] = _dot(tb, _bd(kb[d][r] * egc[d][r], bd)).astype(BF16)
        last = (c + 1) * CH - 1 if d == 0 else c * CH
        gtot = gcum[d][last:last + 1]
        k_end = (k[r] * jnp.exp(gtot - gcum[d][r])).astype(BF16)
        ket_out[0, d, c] = _dot_nt(eye_bf, k_end).astype(BF16)
        ge_out[0, d, c] = jnp.exp(gtot)


def _dn_local(pc, pcs, conv_w, a_log, dt_bias):
    b, t, w = pc.shape
    nt = t // TM
    left, right = _halo_specs(w, DN_HALO, nt)
    row = lambda bb, i: (bb, i, 0)
    pad_vec = lambda a: jnp.zeros((1, LANES), F32).at[0, 2 * HEADS:4 * HEADS].set(a.reshape(-1))
    vec = pl.BlockSpec((1, LANES), lambda bb, i: (0, 0))
    drow = pl.BlockSpec((1, 2, TM, GW), lambda bb, i: (bb, 0, i, 0))
    return _Part(
        _dn_local_kernel,
        (pc, pc, pc, pcs, conv_w, pad_vec(a_log), pad_vec(dt_bias)),
        [pl.BlockSpec((1, TM, w), row), left, right, pl.BlockSpec((1, TM, LANES), row),
         pl.BlockSpec((DN_CONV, w), lambda bb, i: (0, 0)), vec, vec],
        [drow, drow, drow, drow,
         pl.BlockSpec((1, 2, CPT, GW, CH), lambda bb, i: (bb, 0, i, 0, 0)),
         pl.BlockSpec((1, 2, CPT, 1, GW), lambda bb, i: (bb, 0, i, 0, 0))],
        [jax.ShapeDtypeStruct((b, 2, t, GW), F32)] + [jax.ShapeDtypeStruct((b, 2, t, GW), BF16)] * 3
        + [jax.ShapeDtypeStruct((b, 2, t // CH, GW, CH), BF16), jax.ShapeDtypeStruct((b, 2, t // CH, 1, GW), F32)],
        [pltpu.VMEM((TM + 2 * DN_HALO, w), F32)])


def _dn_chunk(refs, o_ref, s_scr, d, cc, bd):
    u, w, qi, a, ket, ge = refs
    rows = slice(cc * CH, (cc + 1) * CH)
    s = s_scr[d]
    wq = _dot(jnp.concatenate([w[0, 0, rows, :], qi[0, 0, rows, :]], axis=0), s.astype(BF16))
    v_new = u[0, 0, rows, :] - wq[:CH]
    o_ref[0, rows, :] = wq[CH:] + _dot(a[0, 0, rows, :], _bd(v_new, bd))
    s_scr[d] = s * ge[0, 0, cc] + jnp.where(bd, _dot(ket[0, 0, cc], v_new.astype(BF16)), 0.0)


def _dir_specs(shape_tail, n_tiles, chunked):
    blk = (1, 1, CPT if chunked else TM) + shape_tail
    zeros = (0,) * len(shape_tail)
    fwd = pl.BlockSpec(blk, lambda b, s: (b, 0, _fwd_tile(s, n_tiles)) + zeros)
    rev = pl.BlockSpec(blk, lambda b, s: (b, 1, _rev_tile(s, n_tiles)) + zeros)
    return fwd, rev


GLA_QK = HEADS * GLA_K


def _gla_local_kernel(pd_ref, pdl_ref, w2_ref, b2_ref, qi_out, a_out, ke_out, vt_out, vb_out, de_out):
    j_in = _iota((CH, GW), 1) & (CH - 1)
    i_in = _iota((CH, GW), 0)
    incl = (j_in <= i_in, j_in >= i_in)
    rr = _iota((TM, TM), 0)
    cc = _iota((TM, TM), 1)
    same_chunk = (rr >> 6) == (cc >> 6)
    eye_bf = (rr == cc).astype(BF16)
    bdk = (_iota((GW, GLA_QK), 0) >> 6) == (_iota((GW, GLA_QK), 1) >> 5)
    z = _dot(pdl_ref[0], w2_ref[...], HI) + b2_ref[...]
    gk_all = -_softplus(-z) * (1.0 / GLA_TAU)
    bcs_all = []
    for d in range(2):
        cum_bd = jnp.where(jnp.logical_and(same_chunk, cc <= rr if d == 0 else cc >= rr), 1.0, 0.0).astype(BF16)
        gparts = _bf16_parts(gk_all[:, d * GLA_QK:(d + 1) * GLA_QK], 2)
        cs = _dot(cum_bd, jnp.concatenate(gparts, axis=1))
        bcs_all.append(cs[:, :GLA_QK] + cs[:, GLA_QK:])
    for c in range(CPT):
        rows = slice(c * CH, (c + 1) * CH)
        p = pd_ref[0, rows, :]
        q = p[:, :GLA_QK] * (GLA_K ** -0.5)
        k = p[:, GLA_QK:2 * GLA_QK]
        vb = p[:, 2 * GLA_QK:].astype(BF16)
        vb_out[0, rows, :] = vb
        vt_out[0, c] = _dot_nt(eye_bf, vb).astype(BF16)
        for d in range(2):
            bcs = bcs_all[d][rows]
            bend = bcs[CH - 1:CH] if d == 0 else bcs[0:1]
            q_in = (q * jnp.exp(bcs)).astype(BF16)
            kdec = jnp.where(bdk, _tile4(k * jnp.exp(-bcs)), 0.0).astype(BF16)
            a_out[0, d, rows, :] = jnp.where(incl[d], _dot_nt(q_in, kdec), 0.0).astype(BF16)
            qi_out[0, d, rows, :] = q_in
            ke_out[0, d, rows, :] = (k * jnp.exp(bend - bcs)).astype(BF16)
            de_out[0, d, c] = jnp.exp(bend)


def _gla_local(pd, pdl, w2bd, b2):
    b, t, w = pd.shape
    nt = t // TM
    row = lambda bb, i: (bb, i, 0)
    return _Part(
        _gla_local_kernel,
        (pd, pdl, w2bd, b2),
        [pl.BlockSpec((1, TM, w), row), pl.BlockSpec((1, TM, LANES), row),
         pl.BlockSpec((LANES, GW), lambda bb, i: (0, 0)), pl.BlockSpec((1, GW), lambda bb, i: (0, 0))],
        [pl.BlockSpec((1, 2, TM, GLA_QK), lambda bb, i: (bb, 0, i, 0)),
         pl.BlockSpec((1, 2, TM, GW), lambda bb, i: (bb, 0, i, 0)),
         pl.BlockSpec((1, 2, TM, GLA_QK), lambda bb, i: (bb, 0, i, 0)),
         pl.BlockSpec((1, CPT, GW, CH), lambda bb, i: (bb, i, 0, 0)),
         pl.BlockSpec((1, TM, GW), row),
         pl.BlockSpec((1, 2, CPT, 1, GLA_QK), lambda bb, i: (bb, 0, i, 0, 0))],
        [jax.ShapeDtypeStruct((b, 2, t, GLA_QK), BF16), jax.ShapeDtypeStruct((b, 2, t, GW), BF16),
         jax.ShapeDtypeStruct((b, 2, t, GLA_QK), BF16), jax.ShapeDtypeStruct((b, t // CH, GW, CH), BF16),
         jax.ShapeDtypeStruct((b, t, GW), BF16), jax.ShapeDtypeStruct((b, 2, t // CH, 1, GLA_QK), F32)],
        [])


def _gla_chunk(refs, o_ref, s_scr, d, cc, bd, bdt):
    qi, a, ke, de, vt, v = refs
    rows = slice(cc * CH, (cc + 1) * CH)
    st = s_scr[d]
    vbd = jnp.where(bd, _tile4(v[0, rows, :]), jnp.zeros((), BF16))
    o_ref[0, rows, :] = _dot_nt(qi[0, 0, rows, :], st.astype(BF16)) + _dot(a[0, 0, rows, :], vbd)
    s_scr[d] = st * de[0, 0, cc] + jnp.where(bdt, _dot(vt[0, cc], ke[0, 0, rows, :]), 0.0)


def _scan_kernel(*refs):
    dn_in = (refs[0:6], refs[6:12])
    gla_in = (refs[12:18], refs[18:24])
    dn_out = refs[24:26]
    gla_out = refs[26:28]
    dn_s, gla_s = refs[28:30]

    @pl.when(pl.program_id(1) == 0)
    def _():
        dn_s[...] = jnp.zeros(dn_s.shape, F32)
        gla_s[...] = jnp.zeros(gla_s.shape, F32)

    bd = (_iota((GW, GW), 0) >> 6) == (_iota((GW, GW), 1) >> 6)
    bdt = (_iota((GW, GLA_QK), 0) >> 6) == (_iota((GW, GLA_QK), 1) >> 5)
    for c in range(CPT):
        for d in range(2):
            cc = c if d == 0 else CPT - 1 - c
            _dn_chunk(dn_in[d], dn_out[d], dn_s, d, cc, bd)
            _gla_chunk(gla_in[d], gla_out[d], gla_s, d, cc, bd, bdt)


def _scans(u, w, dqi, da, ket, ge, qi, a, ke, de, vt, vb):
    b, _, t, _ = u.shape
    nt = t // TM
    rowf, rowr = _dir_specs((GW,), nt, False)
    ketf, ketr = _dir_specs((GW, CH), nt, True)
    gef, ger = _dir_specs((1, GW), nt, True)
    qf, qr = _dir_specs((GLA_QK,), nt, False)
    df, dr = _dir_specs((1, GLA_QK), nt, True)
    vtf = pl.BlockSpec((1, CPT, GW, CH), lambda bb, s: (bb, _fwd_tile(s, nt), 0, 0))
    vtr = pl.BlockSpec((1, CPT, GW, CH), lambda bb, s: (bb, _rev_tile(s, nt), 0, 0))
    vf = pl.BlockSpec((1, TM, GW), lambda bb, s: (bb, _fwd_tile(s, nt), 0))
    vr = pl.BlockSpec((1, TM, GW), lambda bb, s: (bb, _rev_tile(s, nt), 0))
    dn_args = (u, w, dqi, da, ket, ge)
    gla_args = (qi, a, ke, de, vt, vb)
    return pl.pallas_call(
        _scan_kernel,
        grid=(b, nt),
        in_specs=[rowf, rowf, rowf, rowf, ketf, gef, rowr, rowr, rowr, rowr, ketr, ger,
                  qf, rowf, qf, df, vtf, vf, qr, rowr, qr, dr, vtr, vr],
        out_specs=[vf, vr, vf, vr],
        out_shape=[jax.ShapeDtypeStruct((b, t, GW), F32)] * 4,
        scratch_shapes=[pltpu.VMEM((2, GW, GW), F32), pltpu.VMEM((2, GW, GLA_QK), F32)],
        compiler_params=_cp(("parallel", "arbitrary")),
    )(*dn_args, *dn_args, *gla_args, *gla_args)


def _outproj_kernel(x_ref, ya_ref, yb_ref, cf_ref, cr_ref, df_ref, dr_ref, pg_ref, g1b_ref, g1c_ref, gc_ref, gd_ref,
                    w_ref, o_ref, *, lat_rows):
    ones64 = _group_ones(GW, 6)

    def fin(o, g, gate):
        ms = _dot_sel(o * o, ones64) * (1.0 / HEAD_DIM)
        return (o * lax.rsqrt(ms + EPS) * g * _silu(gate)).astype(BF16)

    pg = pg_ref[0]
    yc = fin(cf_ref[0] + cr_ref[0], gc_ref[...], pg[:, :GW])
    yd = fin(df_ref[0] + dr_ref[0], gd_ref[...], pg[:, GW:])
    res = (_dot(ya_ref[0].astype(BF16), w_ref[0:GW, :]) + _dot(yb_ref[0].astype(BF16), w_ref[GW:2 * GW, :])
           + _dot(yc, w_ref[2 * GW:3 * GW, :]) + _dot(yd, w_ref[3 * GW:, :]))
    g1 = jnp.where(_ctx_rows(res.shape[0], lat_rows), g1c_ref[0], g1b_ref[0])
    o_ref[0] = x_ref[0] + g1 * res


def _outproj(xs, ya, yb, ocf, ocr, odf, odr, pg, mod3, dn_g, gla_g, w_out_bf, n_batch, rows, n_blk):
    b, t, d = xs.shape
    reps = GW // HEAD_DIM
    tb = rows // n_blk
    row = lambda bb, i: (bb, i, 0)
    g256 = pl.BlockSpec((1, tb, GW), row)
    in_specs = ([pl.BlockSpec((1, tb, d), row), g256, g256, g256, g256, g256, g256, pl.BlockSpec((1, tb, 2 * GW), row)]
                + _mod_specs(2, n_batch) + [_resident((1, GW)), _resident((1, GW)), _resident((d, d))])
    args = (xs, ya, yb, ocf, ocr, odf, odr, pg, mod3, mod3, jnp.tile(dn_g, reps).reshape(1, GW),
            jnp.tile(gla_g, reps).reshape(1, GW), w_out_bf)
    kern = functools.partial(_outproj_kernel, lat_rows=t - TM)
    return _row_call(kern, n_batch, rows, args, in_specs, [d], n_blk=n_blk)[0]


FF_HALO = 8


def _ffn_kernel(xm_ref, xl_ref, xr_ref, shb_ref, shc_ref, scb_ref, scc_ref, g2b_ref, g2c_ref, ng_ref,
                wu_ref, cw_ref, wd_ref, o_ref, ext_ref, *, lat_rows):
    tb = xm_ref.shape[1]
    i = pl.program_id(1)
    rows_ext = _iota((tb + 2 * FF_HALO, 1), 0)
    left_ok = jnp.logical_and(i > 0, i * tb != lat_rows)
    right_ok = jnp.logical_and(i < pl.num_programs(1) - 1, (i + 1) * tb != lat_rows)
    keep = jnp.logical_and(jnp.logical_or(rows_ext >= FF_HALO, left_ok),
                           jnp.logical_or(rows_ext < FF_HALO + tb, right_ok))
    grow = i * tb - FF_HALO + rows_ext
    ctx_ext = grow >= lat_rows
    x = jnp.concatenate([xl_ref[0], xm_ref[0], xr_ref[0]], axis=0)
    y = x * lax.rsqrt(jnp.mean(x * x, axis=-1, keepdims=True) + EPS) * ng_ref[...]
    h = y * (1.0 + jnp.where(ctx_ext, scc_ref[0], scb_ref[0])) + jnp.where(ctx_ext, shc_ref[0], shb_ref[0])
    h = jnp.where(keep, h, 0.0).astype(BF16)
    inner_boundary = lat_rows % tb != 0
    if inner_boundary:
        row = grow[FF_HALO:FF_HALO + tb]
        m_prev = jnp.broadcast_to(jnp.where(row == lat_rows, 0.0, 1.0), (tb, FF_BLK))
        m_next = jnp.broadcast_to(jnp.where(row == lat_rows - 1, 0.0, 1.0), (tb, FF_BLK))
    acc = jnp.zeros((tb, D_MODEL), F32)
    for j in range(N_FF_BLK):
        def conv(col0, half):
            cols = slice(col0, col0 + FF_BLK)
            ext_ref[half] = _dot(h, wu_ref[:, cols])
            cw = cw_ref[:, cols]
            prev = ext_ref[half, pl.ds(FF_HALO - 1, tb), :]
            nxt = ext_ref[half, pl.ds(FF_HALO + 1, tb), :]
            if inner_boundary:
                prev = m_prev * prev
                nxt = m_next * nxt
            return cw[0:1] * prev + cw[1:2] * ext_ref[half, pl.ds(FF_HALO, tb), :] + cw[2:3] * nxt
        a = conv(j * FF_BLK, 0)
        g = conv(D_FF + j * FF_BLK, 1)
        acc = acc + _dot((_silu(g) * a).astype(BF16), wd_ref[j * FF_BLK:(j + 1) * FF_BLK, :])
    g2 = jnp.where(ctx_ext[FF_HALO:FF_HALO + tb], g2c_ref[0], g2b_ref[0])
    o_ref[0] = xm_ref[0] + g2 * acc


def _ffn(x1, mod3, norm_g, w_up, cw, w_down, n_batch, lat_rows, rows, n_blk):
    b, t, d = x1.shape
    tb = rows // n_blk
    per = tb // FF_HALO
    last = t // FF_HALO - 1
    in_specs = ([pl.BlockSpec((1, tb, d), lambda bb, i: (bb, i, 0)),
                 pl.BlockSpec((1, FF_HALO, d), lambda bb, i: (bb, jnp.maximum(i * per - 1, 0), 0)),
                 pl.BlockSpec((1, FF_HALO, d), lambda bb, i: (bb, jnp.minimum((i + 1) * per, last), 0))]
                + _mod_specs(3, n_batch) + _mod_specs(4, n_batch) + _mod_specs(5, n_batch)
                + [_resident((1, d)), _resident(w_up.shape), _resident(cw.shape), _resident(w_down.shape)])
    args = (x1, x1, x1, mod3, mod3, mod3, mod3, mod3, mod3, norm_g.reshape(1, d), w_up, cw, w_down)
    kern = functools.partial(_ffn_kernel, lat_rows=lat_rows)
    return _row_call(kern, n_batch, rows, args, in_specs, [d],
                     scratch=[pltpu.VMEM((2, tb + 2 * FF_HALO, FF_BLK), F32)], n_blk=n_blk)[0]


def _rope_tables(seq, ctx_len):
    rows = seq // GRID_W
    row = jnp.repeat(jnp.arange(rows, dtype=F32), GRID_W)
    col = jnp.tile(jnp.arange(GRID_W, dtype=F32), rows)
    nf = QK_DIM // 4
    inv = ROPE_THETA ** (-jnp.arange(nf, dtype=F32) / nf)
    ang = jnp.concatenate([row[:, None] * inv, col[:, None] * inv], axis=-1)
    cos = jnp.concatenate([jnp.cos(ang), jnp.ones((ctx_len, QK_DIM // 2), F32)], axis=0)
    sin = jnp.concatenate([jnp.sin(ang), jnp.zeros((ctx_len, QK_DIM // 2), F32)], axis=0)
    reps = GW // QK_DIM
    return (jnp.tile(jnp.concatenate([cos, cos], axis=-1), (1, reps)),
            jnp.tile(jnp.concatenate([-sin, sin], axis=-1), (1, reps)))


def _regroup_w_in(w):
    d = w.shape[0]
    z = lambda n: jnp.zeros((d, n), w.dtype)
    return jnp.concatenate([w[:, :2048], w[:, 2048:2064], z(LANES - 16), w[:, 2320:2832], w[:, 2832:2864],
                            z(LANES - 32), w[:, 2064:2320], w[:, 2864:3120]], axis=1).astype(BF16)


def _gla_w2_blockdiag(w2):
    out = jnp.zeros((LANES, GW), F32)
    out = out.at[0:GLA_RANK, 0:GLA_QK].set(w2[0])
    return out.at[GLA_RANK:2 * GLA_RANK, GLA_QK:].set(w2[1])


def _layer(xs, mod3, lp, cos_t, sin_t, layer_idx, last, n_batch):
    b, t, d = xs.shape
    nt = t // TM
    pa, pb, pc, pcs, pd, pdl, pg = _inproj(xs, mod3, lp["norm1_g"], _regroup_w_in(lp["w_in"]), n_batch)

    lam_init = 0.8 - 0.6 * math.exp(-0.3 * layer_idx)
    (q, kt, v, qn, kn), (ya,), dn_parts, (qi, a, ke, vt, vb, de) = _fused_call(
        [_attn_prep(pb, lp["da_qnorm_g"], lp["da_knorm_g"], cos_t, sin_t),
         _conv_module(pa, lp["cm_conv_w"], lp["cm_conv_b"], lp["cm_ln_g"], lp["cm_ln_b"]),
         _dn_local(pc, pcs, lp["dn_conv_w"], lp["dn_a_log"], lp["dn_dt_bias"]),
         _gla_local(pd, pdl, _gla_w2_blockdiag(lp["gla_w2"]), lp["gla_b2"].reshape(1, GW))],
        (b, nt))
    yb = _attention(q, kt, v, qn, kn, lp["da_lambda"], lp["da_subln_g"], lam_init)
    ocf, ocr, odf, odr = _scans(*dn_parts, qi, a, ke, de, vt, vb)

    lat_rows = t - TM
    rows = lat_rows if last else t
    x1 = _outproj(xs, ya, yb, ocf, ocr, odf, odr, pg, mod3, lp["dn_onorm_g"], lp["gla_onorm_g"],
                  lp["w_out"].astype(BF16), n_batch, rows, N_BLK_FFN)
    return _ffn(x1, mod3, lp["norm2_g"], lp["ffn_w_up"].astype(BF16), lp["ffn_conv_w"],
                lp["ffn_w_down"].astype(BF16), n_batch, lat_rows, rows, N_BLK_FFN)


def kernel(x, c, ctx, c_ctx, w_mod, b_mod, norm1_g, norm2_g, w_in, w_out, cm_conv_w, cm_conv_b, cm_ln_g, cm_ln_b, da_qnorm_g, da_knorm_g, da_lambda, da_subln_g, dn_conv_w, dn_a_log, dn_dt_bias, dn_onorm_g, gla_w2, gla_b2, gla_onorm_g, ffn_w_up, ffn_conv_w, ffn_w_down):
    n_batch, seq, d = x.shape
    ctx_len = ctx.shape[1]
    assert ctx_len == TM and seq % TM == 0 and (seq + ctx_len) % (8 * N_BLK) == 0 and d == D_MODEL
    depth = w_mod.shape[0]
    cos_t, sin_t = _rope_tables(seq, ctx_len)
    xs = jnp.concatenate([x, ctx], axis=1)
    mod_rows = 16
    c_rows = jnp.zeros((mod_rows, d), F32).at[:n_batch].set(c).at[n_batch].set(c_ctx)
    params = dict(w_mod=w_mod, b_mod=b_mod, norm1_g=norm1_g, norm2_g=norm2_g, w_in=w_in, w_out=w_out,
                  cm_conv_w=cm_conv_w, cm_conv_b=cm_conv_b, cm_ln_g=cm_ln_g, cm_ln_b=cm_ln_b,
                  da_qnorm_g=da_qnorm_g, da_knorm_g=da_knorm_g, da_lambda=da_lambda, da_subln_g=da_subln_g,
                  dn_conv_w=dn_conv_w, dn_a_log=dn_a_log, dn_dt_bias=dn_dt_bias, dn_onorm_g=dn_onorm_g,
                  gla_w2=gla_w2, gla_b2=gla_b2, gla_onorm_g=gla_onorm_g,
                  ffn_w_up=ffn_w_up, ffn_conv_w=ffn_conv_w, ffn_w_down=ffn_w_down)
    for l in range(depth):
        lp = {k: v[l] for k, v in params.items()}
        mod3 = _modulation(c_rows, lp["w_mod"], lp["b_mod"]).reshape(mod_rows * 6, 1, d)
        xs = _layer(xs, mod3, lp, cos_t, sin_t, l, l == depth - 1, n_batch)
    return xs
```

```python
import functools
import math
from typing import Any, NamedTuple

import jax
import jax.numpy as jnp
from jax import lax
from jax.experimental import pallas as pl
from jax.experimental.pallas import tpu as pltpu

F32 = jnp.float32
BF16 = jnp.bfloat16
HI = lax.Precision.HIGHEST
EPS = 1e-6

D_MODEL = 1024
GRID_W = 64
HEADS = 4
HEAD_DIM = 64
GW = 256
QK_DIM = 32
GLA_K = 32
GLA_RANK = 16
GLA_TAU = 16.0
CM_KERNEL = 31
DN_CONV = 5
ROPE_THETA = 10000.0
CH = 64
TM = 256
CPT = TM // CH
D_FF = 2816
FF_BLK = 256
N_FF_BLK = D_FF // FF_BLK
LANES = 128

IN_GROUPS = (("pa", 512), ("pb", 768), ("pc", 768), ("pcs", LANES), ("pd", 512), ("pdl", LANES), ("pg", 512))
IN_COLS_PAD = sum(w for _, w in IN_GROUPS)

VMEM_LIMIT = 56 * 1024 * 1024


def _cp(sem):
    return pltpu.CompilerParams(dimension_semantics=sem, vmem_limit_bytes=VMEM_LIMIT)


def _dot(a, b, prec=None):
    return jnp.dot(a, b, preferred_element_type=F32, precision=prec)


def _dot_nt(a, b, prec=None):
    return lax.dot_general(a, b, (((1,), (1,)), ((), ())), preferred_element_type=F32, precision=prec)


def _sigmoid(x):
    return 1.0 / (1.0 + jnp.exp(-x))


def _silu(x):
    return x * _sigmoid(x)


def _softplus(x):
    return jnp.maximum(x, 0.0) + jnp.log(1.0 + jnp.exp(-jnp.abs(x)))


def _iota(shape, dim):
    return lax.broadcasted_iota(jnp.int32, shape, dim)


def _group_ones(n, shift):
    return ((_iota((n, n), 0) >> shift) == (_iota((n, n), 1) >> shift)).astype(BF16)


def _bf16_parts(m, n):
    out = []
    for _ in range(n):
        hi = m.astype(BF16)
        out.append(hi)
        m = m - hi.astype(F32)
    return out


def _dot_sel(x, sel, n_parts=2):
    rows = x.shape[0]
    r = _dot(jnp.concatenate(_bf16_parts(x, n_parts), axis=0), sel)
    out = r[:rows]
    for k in range(1, n_parts):
        out = out + r[k * rows:(k + 1) * rows]
    return out


def _tile4(y):
    return jnp.concatenate([y, y, y, y], axis=0)


def _mod_kernel(c_ref, w_ref, b_ref, o_ref):
    o_ref[...] = _dot(_silu(c_ref[...]), w_ref[...], HI) + b_ref[...]


def _modulation(c_rows, w_mod, b_mod):
    r, d = c_rows.shape
    n = w_mod.shape[1] // d
    return pl.pallas_call(
        _mod_kernel,
        grid=(n,),
        in_specs=[pl.BlockSpec((r, d), lambda j: (0, 0)),
                  pl.BlockSpec((d, d), lambda j: (0, j)),
                  pl.BlockSpec((1, d), lambda j: (0, j))],
        out_specs=pl.BlockSpec((r, d), lambda j: (0, j)),
        out_shape=jax.ShapeDtypeStruct((r, n * d), F32),
        compiler_params=_cp(("arbitrary",)),
    )(c_rows, w_mod, b_mod.reshape(1, -1))


N_BLK = 4
N_BLK_FFN = 8


def _mod_specs(k, n_batch):
    return [pl.BlockSpec((1, 1, D_MODEL), lambda b, i: (b * 6 + k, 0, 0)),
            pl.BlockSpec((1, 1, D_MODEL), lambda b, i: (n_batch * 6 + k, 0, 0))]


def _ctx_rows(tb, lat_rows):
    return pl.program_id(1) * tb + _iota((tb, 1), 0) >= lat_rows


def _resident(shape):
    zeros = (0,) * len(shape)
    return pl.BlockSpec(shape, lambda b, i: zeros, pipeline_mode=pl.Buffered(1))


def _row_call(kern, n_batch, rows, args, in_specs, out_widths, scratch=(), n_blk=N_BLK):
    tb = rows // n_blk
    return pl.pallas_call(
        kern,
        grid=(n_batch, n_blk),
        in_specs=in_specs,
        out_specs=[pl.BlockSpec((1, tb, w), lambda bb, i: (bb, i, 0)) for w in out_widths],
        out_shape=[jax.ShapeDtypeStruct((n_batch, rows, w), F32) for w in out_widths],
        scratch_shapes=list(scratch),
        compiler_params=_cp(("parallel", "arbitrary")),
    )(*args)


def _inproj_kernel(x_ref, shb_ref, shc_ref, scb_ref, scc_ref, g_ref, w_ref, *outs, lat_rows):
    x = x_ref[0]
    ctx = _ctx_rows(x.shape[0], lat_rows)
    y = x * lax.rsqrt(jnp.mean(x * x, axis=-1, keepdims=True) + EPS) * g_ref[...]
    h = y * (1.0 + jnp.where(ctx, scc_ref[0], scb_ref[0])) + jnp.where(ctx, shc_ref[0], shb_ref[0])
    p = _dot(h.astype(BF16), w_ref[...])
    off = 0
    for o_ref, (_, width) in zip(outs, IN_GROUPS):
        o_ref[0] = p[:, off:off + width]
        off += width


def _inproj(xs, mod3, norm_g, w_in_r, n_batch):
    b, t, d = xs.shape
    in_specs = ([pl.BlockSpec((1, t // N_BLK, d), lambda bb, i: (bb, i, 0))] + _mod_specs(0, n_batch)
                + _mod_specs(1, n_batch) + [_resident((1, d)), _resident((d, IN_COLS_PAD))])
    kern = functools.partial(_inproj_kernel, lat_rows=t - TM)
    return _row_call(kern, n_batch, t, (xs, mod3, mod3, mod3, mod3, norm_g.reshape(1, d), w_in_r), in_specs,
                     [w for _, w in IN_GROUPS])


class _Part(NamedTuple):
    body: Any
    args: Any
    in_specs: Any
    out_specs: Any
    out_shapes: Any
    scratch: Any


def _fused_call(parts, grid):
    n_in = [len(p.args) for p in parts]
    n_out = [len(p.out_specs) for p in parts]
    n_scr = [len(p.scratch) for p in parts]

    def kern(*refs):
        ins, outs, scr = refs[:sum(n_in)], refs[sum(n_in):sum(n_in) + sum(n_out)], refs[sum(n_in) + sum(n_out):]
        i = o = s = 0
        for p, ni, no, ns in zip(parts, n_in, n_out, n_scr):
            p.body(*ins[i:i + ni], *outs[o:o + no], *scr[s:s + ns])
            i, o, s = i + ni, o + no, s + ns

    res = pl.pallas_call(
        kern,
        grid=grid,
        in_specs=[sp for p in parts for sp in p.in_specs],
        out_specs=[sp for p in parts for sp in p.out_specs],
        out_shape=[sh for p in parts for sh in p.out_shapes],
        scratch_shapes=[sc for p in parts for sc in p.scratch],
        compiler_params=_cp(("parallel", "arbitrary")),
    )(*[a for p in parts for a in p.args])
    out, o = [], 0
    for no in n_out:
        out.append(res[o:o + no])
        o += no
    return out


def _halo_specs(width, halo, n_tiles):
    per = TM // halo
    left = pl.BlockSpec((1, halo, width), lambda b, i: (b, jnp.maximum(i * per - 1, 0), 0))
    right = pl.BlockSpec((1, halo, width), lambda b, i: (b, jnp.minimum((i + 1) * per, n_tiles * per - 1), 0))
    return left, right


def _halo_ok(i, n_tiles):
    return jnp.logical_and(i >= 1, i < n_tiles - 1), i < n_tiles - 2


CM_HALO = 16


def _convmod_kernel(pm_ref, pl_ref, pr_ref, cw_ref, cb_ref, lg_ref, lb_ref, o_ref, ext_ref):
    left_ok, right_ok = _halo_ok(pl.program_id(1), pl.num_programs(1))

    def glu(p):
        return p[:, :GW] * _sigmoid(p[:, GW:])

    ext_ref[0:CM_HALO] = jnp.where(left_ok, glu(pl_ref[0]), 0.0)
    ext_ref[CM_HALO:CM_HALO + TM] = glu(pm_ref[0])
    ext_ref[CM_HALO + TM:] = jnp.where(right_ok, glu(pr_ref[0]), 0.0)
    pad = CM_KERNEL // 2
    sub = 8
    y = cb_ref[...]
    for r in range(sub):
        acc = None
        for j in range(CM_KERNEL):
            off = CM_HALO - pad + j
            if off % sub == r:
                term = cw_ref[j:j + 1, :] * ext_ref[pl.ds(off - r, TM + sub), :]
                acc = term if acc is None else acc + term
        if acc is not None:
            y = y + acc[r:r + TM]
    mu = jnp.mean(y, axis=-1, keepdims=True)
    yc = y - mu
    var = jnp.mean(yc * yc, axis=-1, keepdims=True)
    o_ref[0] = _silu(yc * lax.rsqrt(var + EPS) * lg_ref[...] + lb_ref[...])


def _conv_module(pa, conv_w, conv_b, ln_g, ln_b):
    b, t, w = pa.shape
    nt = t // TM
    left, right = _halo_specs(w, CM_HALO, nt)
    vec = pl.BlockSpec((1, GW), lambda bb, i: (0, 0))
    return _Part(
        _convmod_kernel,
        (pa, pa, pa, conv_w, conv_b.reshape(1, GW), ln_g.reshape(1, GW), ln_b.reshape(1, GW)),
        [pl.BlockSpec((1, TM, w), lambda bb, i: (bb, i, 0)), left, right,
         pl.BlockSpec((CM_KERNEL, GW), lambda bb, i: (0, 0)), vec, vec, vec],
        [pl.BlockSpec((1, TM, GW), lambda bb, i: (bb, i, 0))],
        [jax.ShapeDtypeStruct((b, t, GW), F32)],
        [pltpu.VMEM((TM + 2 * CM_HALO, GW), F32)])


N_MAPS = 2 * HEADS
QK_PAD = 2 * QK_DIM
V_EXT = 2 * HEAD_DIM
MASK_BIG = 8192.0
SHIFT_MAX = 40.0


def _attn_prep_kernel(pb_ref, qg_ref, kg_ref, cos_ref, sin_ref, q_out, kt_out, v_out, qn_out, kn_out):
    p = pb_ref[0]
    ones32 = _group_ones(GW, 5)
    first = (_iota((TM, GW), 1) & (QK_DIM - 1)) < QK_DIM // 2
    cos = cos_ref[...]
    sin = sin_ref[...]

    def norm_rope(t, g):
        ms = _dot_sel(t * t, ones32) * (1.0 / QK_DIM)
        tn = t * lax.rsqrt(ms + EPS) * g
        partner = jnp.where(first, pltpu.roll(tn, GW - QK_DIM // 2, 1), pltpu.roll(tn, QK_DIM // 2, 1))
        return tn * cos + partner * sin

    qf = norm_rope(p[:, :GW], qg_ref[...]) * (QK_DIM ** -0.5)
    kf = norm_rope(p[:, GW:2 * GW], kg_ref[...])
    map_sel = ((_iota((GW, N_MAPS), 0) >> 5) == _iota((GW, N_MAPS), 1)).astype(BF16)
    qn_out[0] = _dot_sel(qf * qf, map_sel)
    kn_out[0] = _dot_sel(kf * kf, map_sel)
    q = qf.astype(BF16)
    kt = kf.T.astype(BF16)
    is_ctx = pl.program_id(1) == pl.num_programs(1) - 1
    k_row = _iota((QK_DIM, TM), 0)
    k_tail = jnp.where(k_row == 0, 1.0, jnp.where(jnp.logical_and(k_row == 1, jnp.logical_not(is_ctx)), 1.0, 0.0))
    k_tail = k_tail.astype(BF16)
    q_tail = jnp.where(jnp.logical_and(_iota((TM, QK_DIM), 1) == 1, is_ctx), -MASK_BIG, 0.0).astype(BF16)
    for g in range(N_MAPS):
        q_out[0, g, :, 0:QK_DIM] = q[:, g * QK_DIM:(g + 1) * QK_DIM]
        q_out[0, g, :, QK_DIM:] = q_tail
        kt_out[0, 0, g, 0:QK_DIM, :] = kt[g * QK_DIM:(g + 1) * QK_DIM, :]
        kt_out[0, 0, g, QK_DIM:, :] = k_tail
    v = p[:, 2 * GW:].astype(BF16)
    ones = jnp.ones((TM, HEAD_DIM), BF16)
    for h in range(HEADS):
        v_out[0, 0, h, :, 0:HEAD_DIM] = v[:, h * HEAD_DIM:(h + 1) * HEAD_DIM]
        v_out[0, 0, h, :, HEAD_DIM:] = ones


def _attn_prep(pb, qn_g, kn_g, cos_t, sin_t):
    b, t, w = pb.shape
    nt = t // TM
    vec = pl.BlockSpec((1, GW), lambda bb, i: (0, 0))
    tab = pl.BlockSpec((TM, GW), lambda bb, i: (i, 0))
    reps = GW // QK_DIM
    n_spec = pl.BlockSpec((1, TM, N_MAPS), lambda bb, i: (bb, i, 0))
    n_shape = jax.ShapeDtypeStruct((b, t, N_MAPS), F32)
    return _Part(
        _attn_prep_kernel,
        (pb, jnp.tile(qn_g, reps).reshape(1, GW), jnp.tile(kn_g, reps).reshape(1, GW), cos_t, sin_t),
        [pl.BlockSpec((1, TM, w), lambda bb, i: (bb, i, 0)), vec, vec, tab, tab],
        [pl.BlockSpec((1, N_MAPS, TM, QK_PAD), lambda bb, i: (bb, 0, i, 0)),
         pl.BlockSpec((1, 1, N_MAPS, QK_PAD, TM), lambda bb, i: (bb, i, 0, 0, 0)),
         pl.BlockSpec((1, 1, HEADS, TM, V_EXT), lambda bb, i: (bb, i, 0, 0, 0)), n_spec, n_spec],
        [jax.ShapeDtypeStruct((b, N_MAPS, t, QK_PAD), BF16),
         jax.ShapeDtypeStruct((b, nt, N_MAPS, QK_PAD, TM), BF16),
         jax.ShapeDtypeStruct((b, nt, HEADS, TM, V_EXT), BF16), n_shape, n_shape],
        [])


def _attn_kernel(q_ref, kt_ref, v_ref, qn_ref, kn_ref, lam_ref, sg_ref, o_ref, qa_scr, acc_scr, s_scr, m_scr,
                 *, n_chunks, lam_init):
    tq = o_ref.shape[1]
    acc_scr[...] = jnp.zeros(acc_scr.shape, F32)
    k2 = jnp.max(kn_ref[0], axis=0, keepdims=True)
    q2 = jnp.max(qn_ref[0], axis=0, keepdims=True)
    bound = jnp.sqrt(q2 * k2)
    safe = jnp.max(bound) <= SHIFT_MAX
    shift = jnp.where(safe, bound, 0.0)
    shift_lane = _iota((tq, QK_PAD), 1) == QK_DIM
    for g in range(N_MAPS):
        qa_scr[g] = jnp.where(shift_lane, (-shift[:, g:g + 1]).astype(BF16), q_ref[0, g])

    @pl.when(safe)
    def _():
        s_scr[0] = _dot(qa_scr[0], kt_ref[0, 0, 0])

        def body(c, carry):
            nxt = jnp.minimum(c + 1, n_chunks - 1)
            for g in range(N_MAPS):
                if g + 1 < N_MAPS:
                    s_next = _dot(qa_scr[g + 1], kt_ref[0, c, g + 1])
                else:
                    s_next = _dot(qa_scr[0], kt_ref[0, nxt, 0])
                p = jnp.exp(s_scr[g & 1]).astype(BF16)
                acc_scr[g] += _dot(p, v_ref[0, c, g // 2])
                s_scr[(g + 1) & 1] = s_next
            return carry

        lax.fori_loop(0, n_chunks, body, 0)

    @pl.when(jnp.logical_not(safe))
    def _():
        m_scr[...] = jnp.full(m_scr.shape, -jnp.inf, F32)

        def body(c, carry):
            for g in range(N_MAPS):
                s = _dot(qa_scr[g], kt_ref[0, c, g])
                m_old = m_scr[:, g:g + 1]
                m_new = jnp.maximum(m_old, jnp.max(s, axis=-1, keepdims=True))
                p = jnp.exp(s - m_new).astype(BF16)
                acc_scr[g] = jnp.exp(m_old - m_new) * acc_scr[g] + _dot(p, v_ref[0, c, g // 2])
                m_scr[:, g:g + 1] = m_new
            return carry

        lax.fori_loop(0, n_chunks, body, 0)

    lp = lam_ref[...]
    lam = (jnp.exp(jnp.sum(lp[0:1] * lp[1:2], axis=-1, keepdims=True))
           - jnp.exp(jnp.sum(lp[2:3] * lp[3:4], axis=-1, keepdims=True)) + lam_init)
    for h in range(HEADS):
        a0 = acc_scr[2 * h]
        a1 = acc_scr[2 * h + 1]
        o = (a0[:, :HEAD_DIM] / a0[:, HEAD_DIM:HEAD_DIM + 1]
             - lam * (a1[:, :HEAD_DIM] / a1[:, HEAD_DIM:HEAD_DIM + 1]))
        y = o * lax.rsqrt(jnp.mean(o * o, axis=-1, keepdims=True) + EPS) * sg_ref[...] * (1.0 - lam_init)
        o_ref[0, :, h * HEAD_DIM:(h + 1) * HEAD_DIM] = y


def _attention(q, kt, v, qn, kn, lam_p, subln_g, lam_init):
    b, _, t, _ = q.shape
    tq = t // N_BLK
    nt = t // TM
    kern = functools.partial(_attn_kernel, n_chunks=nt, lam_init=lam_init)
    return pl.pallas_call(
        kern,
        grid=(b, N_BLK),
        in_specs=[pl.BlockSpec((1, N_MAPS, tq, QK_PAD), lambda bb, i: (bb, 0, i, 0)),
                  pl.BlockSpec((1, nt, N_MAPS, QK_PAD, TM), lambda bb, i: (bb, 0, 0, 0, 0)),
                  pl.BlockSpec((1, nt, HEADS, TM, V_EXT), lambda bb, i: (bb, 0, 0, 0, 0)),
                  pl.BlockSpec((1, tq, N_MAPS), lambda bb, i: (bb, i, 0)),
                  pl.BlockSpec((1, t, N_MAPS), lambda bb, i: (bb, 0, 0)),
                  _resident((4, QK_DIM)), _resident((1, HEAD_DIM))],
        out_specs=pl.BlockSpec((1, tq, GW), lambda bb, i: (bb, i, 0)),
        out_shape=jax.ShapeDtypeStruct((b, t, GW), F32),
        scratch_shapes=[pltpu.VMEM((N_MAPS, tq, QK_PAD), BF16), pltpu.VMEM((N_MAPS, tq, V_EXT), F32),
                        pltpu.VMEM((2, tq, TM), F32), pltpu.VMEM((tq, N_MAPS), F32)],
        compiler_params=_cp(("parallel", "arbitrary")),
    )(q, kt, v, qn, kn, lam_p, subln_g.reshape(1, HEAD_DIM))


def _bd(y, bd):
    return jnp.where(bd, _tile4(y), 0.0).astype(BF16)


def _fwd_tile(s, n_tiles):
    return jnp.where(s == 0, n_tiles - 1, s - 1)


def _rev_tile(s, n_tiles):
    return jnp.where(s == 0, n_tiles - 1, n_tiles - 1 - s)


DN_HALO = 8


def _dn_prep(pm_ref, pl_ref, pr_ref, pcs_ref, cw_ref, alog_ref, dtb_ref, ext_ref):
    left_ok, right_ok = _halo_ok(pl.program_id(1), pl.num_programs(1))
    ext_ref[0:DN_HALO] = jnp.where(left_ok, pl_ref[0], 0.0)
    ext_ref[DN_HALO:DN_HALO + TM] = pm_ref[0]
    ext_ref[DN_HALO + TM:] = jnp.where(right_ok, pr_ref[0], 0.0)
    pad = DN_CONV // 2
    acc = jnp.zeros((TM, 3 * GW), F32)
    for j in range(DN_CONV):
        acc = acc + cw_ref[j:j + 1, :] * ext_ref[pl.ds(DN_HALO - pad + j, TM), :]
    qkv = _silu(acc)
    ones64 = _group_ones(GW, 6)

    def l2n(t):
        return t * lax.rsqrt(_dot_sel(t * t, ones64) + EPS)

    s = pcs_ref[0]
    col = _iota(s.shape, 1)
    gate = -jnp.exp(alog_ref[...]) * _softplus(s + dtb_ref[...])
    bg = jnp.where(col < 2 * HEADS, _sigmoid(s), jnp.where(col < 4 * HEADS, gate, 0.0))
    return l2n(qkv[:, :GW]) * (HEAD_DIM ** -0.5), l2n(qkv[:, GW:2 * GW]), qkv[:, 2 * GW:], bg


def _dn_local_kernel(pm_ref, pl_ref, pr_ref, pcs_ref, cw_ref, alog_ref, dtb_ref,
                     u_out, w_out, qi_out, a_out, ket_out, ge_out, ext_ref):
    rr = _iota((TM, GW), 0)
    cc = _iota((TM, GW), 1)
    i_in = rr & (CH - 1)
    j_in = cc & (CH - 1)
    bd = (rr >> 6) == (cc >> 6)
    eye_t = (i_in == j_in).astype(F32)
    incl = (j_in <= i_in, j_in >= i_in)
    strict = (j_in < i_in, j_in > i_in)
    eye_bf = (rr == cc).astype(BF16)
    head_of_lane = _iota((LANES, GW), 1) >> 6
    src = _iota((LANES, GW), 0)
    q, k, v, bg = _dn_prep(pm_ref, pl_ref, pr_ref, pcs_ref, cw_ref, alog_ref, dtb_ref, ext_ref)
    ones_bf = bd.astype(BF16)
    beta, gcum, decay, kb, egc = [], [], [], [], []
    for d in range(2):
        sel_b = (src == d * HEADS + head_of_lane).astype(BF16)
        sel_g = (src == 2 * HEADS + d * HEADS + head_of_lane).astype(BF16)
        cum_bd = jnp.where(jnp.logical_and(bd, incl[d]), 1.0, 0.0).astype(BF16)
        g_exp = _dot(jnp.concatenate(_bf16_parts(bg, 2), axis=0), sel_g)
        beta.append(_dot(bg.astype(BF16), sel_b))
        cs = _dot(cum_bd, jnp.concatenate([g_exp[:TM].astype(BF16), g_exp[TM:].astype(BF16)], axis=1))
        gcum.append(cs[:, :GW] + cs[:, GW:])
        gparts = _bf16_parts(gcum[d] * eye_t, 3)
        tr = _dot(ones_bf, jnp.concatenate(gparts, axis=1))
        grow = tr[:, :GW] + tr[:, GW:2 * GW] + tr[:, 2 * GW:]
        decay.append(jnp.where(incl[d], jnp.exp(jnp.where(incl[d], gcum[d] - grow, 0.0)), 0.0))
        kb.append(k * beta[d])
        egc.append(jnp.exp(gcum[d]))
        qi_out[0, d] = (q * egc[d]).astype(BF16)
    pairs = [(c, d) for c in range(CPT) for d in range(2)]
    rows = [slice(c * CH, (c + 1) * CH) for c in range(CPT)]
    eye = eye_t[:CH]
    a = {}
    for c in range(CPT):
        r = rows[c]
        lhs = jnp.concatenate([kb[0][r], kb[1][r], q[r]], axis=0).astype(BF16)
        aq = _dot_nt(lhs, _bd(k[r], bd))
        for d in range(2):
            dec = decay[d][r]
            a[c, d] = jnp.where(strict[d][:CH], aq[d * CH:(d + 1) * CH] * dec, 0.0)
            a_out[0, d, r, :] = jnp.where(incl[d][:CH], aq[2 * CH:] * dec, 0.0).astype(BF16)
    t_inv = {cd: eye - a[cd] for cd in pairs}
    p = {cd: _dot(a[cd].astype(BF16), _bd(a[cd], bd)) for cd in pairs}
    for it in range(5):
        for cd in pairs:
            pbd = _bd(p[cd], bd)
            if it < 4:
                res = _dot(jnp.concatenate([t_inv[cd], p[cd]], axis=0).astype(BF16), pbd)
                t_inv[cd] = t_inv[cd] + res[:CH]
                p[cd] = res[CH:]
            else:
                t_inv[cd] = t_inv[cd] + _dot(t_inv[cd].astype(BF16), pbd)

    def split(m):
        hi = m.astype(BF16)
        return hi, (m - hi.astype(F32)).astype(BF16)

    for cd in pairs:
        x0 = t_inv[cd]
        ah, al = split(a[cd])
        xh, xl = split(x0)
        hx = _dot(jnp.concatenate([ah, al], axis=0), _bd(xh, bd))
        resid = eye - x0 - (hx[:CH] + hx[CH:] + _dot(ah, _bd(xl, bd)))
        t_inv[cd] = x0 + _dot(xh, _bd(resid, bd))
    for c, d in pairs:
        r = rows[c]
        tb = t_inv[c, d].astype(BF16)
        u_out[0, d, r, :] = _dot(tb, _bd(v[r] * beta[d][r], bd))
        w_out[0, d, r, :] = _dot(tb, _bd(kb[d][r] * egc[d][r], bd)).astype(BF16)
        last = (c + 1) * CH - 1 if d == 0 else c * CH
        gtot = gcum[d][last:last + 1]
        k_end = (k[r] * jnp.exp(gtot - gcum[d][r])).astype(BF16)
        ket_out[0, d, c] = _dot_nt(eye_bf, k_end).astype(BF16)
        ge_out[0, d, c] = jnp.exp(gtot)


def _dn_local(pc, pcs, conv_w, a_log, dt_bias):
    b, t, w = pc.shape
    nt = t // TM
    left, right = _halo_specs(w, DN_HALO, nt)
    row = lambda bb, i: (bb, i, 0)
    pad_vec = lambda a: jnp.zeros((1, LANES), F32).at[0, 2 * HEADS:4 * HEADS].set(a.reshape(-1))
    vec = pl.BlockSpec((1, LANES), lambda bb, i: (0, 0))
    drow = pl.BlockSpec((1, 2, TM, GW), lambda bb, i: (bb, 0, i, 0))
    return _Part(
        _dn_local_kernel,
        (pc, pc, pc, pcs, conv_w, pad_vec(a_log), pad_vec(dt_bias)),
        [pl.BlockSpec((1, TM, w), row), left, right, pl.BlockSpec((1, TM, LANES), row),
         pl.BlockSpec((DN_CONV, w), lambda bb, i: (0, 0)), vec, vec],
        [drow, drow, drow, drow,
         pl.BlockSpec((1, 2, CPT, GW, CH), lambda bb, i: (bb, 0, i, 0, 0)),
         pl.BlockSpec((1, 2, CPT, 1, GW), lambda bb, i: (bb, 0, i, 0, 0))],
        [jax.ShapeDtypeStruct((b, 2, t, GW), F32)] + [jax.ShapeDtypeStruct((b, 2, t, GW), BF16)] * 3
        + [jax.ShapeDtypeStruct((b, 2, t // CH, GW, CH), BF16), jax.ShapeDtypeStruct((b, 2, t // CH, 1, GW), F32)],
        [pltpu.VMEM((TM + 2 * DN_HALO, w), F32)])


def _dn_chunk(refs, o_ref, s_scr, d, cc, bd):
    u, w, qi, a, ket, ge = refs
    rows = slice(cc * CH, (cc + 1) * CH)
    s = s_scr[d]
    wq = _dot(jnp.concatenate([w[0, 0, rows, :], qi[0, 0, rows, :]], axis=0), s.astype(BF16))
    v_new = u[0, 0, rows, :] - wq[:CH]
    o_ref[0, rows, :] = wq[CH:] + _dot(a[0, 0, rows, :], _bd(v_new, bd))
    s_scr[d] = s * ge[0, 0, cc] + jnp.where(bd, _dot(ket[0, 0, cc], v_new.astype(BF16)), 0.0)


def _dir_specs(shape_tail, n_tiles, chunked):
    blk = (1, 1, CPT if chunked else TM) + shape_tail
    zeros = (0,) * len(shape_tail)
    fwd = pl.BlockSpec(blk, lambda b, s: (b, 0, _fwd_tile(s, n_tiles)) + zeros)
    rev = pl.BlockSpec(blk, lambda b, s: (b, 1, _rev_tile(s, n_tiles)) + zeros)
    return fwd, rev


GLA_QK = HEADS * GLA_K


def _gla_local_kernel(pd_ref, pdl_ref, w2_ref, b2_ref, qi_out, a_out, ke_out, vt_out, vb_out, de_out):
    j_in = _iota((CH, GW), 1) & (CH - 1)
    i_in = _iota((CH, GW), 0)
    incl = (j_in <= i_in, j_in >= i_in)
    rr = _iota((TM, TM), 0)
    cc = _iota((TM, TM), 1)
    same_chunk = (rr >> 6) == (cc >> 6)
    eye_bf = (rr == cc).astype(BF16)
    bdk = (_iota((GW, GLA_QK), 0) >> 6) == (_iota((GW, GLA_QK), 1) >> 5)
    z = _dot(pdl_ref[0], w2_ref[...], HI) + b2_ref[...]
    gk_all = -_softplus(-z) * (1.0 / GLA_TAU)
    bcs_all = []
    for d in range(2):
        cum_bd = jnp.where(jnp.logical_and(same_chunk, cc <= rr if d == 0 else cc >= rr), 1.0, 0.0).astype(BF16)
        gparts = _bf16_parts(gk_all[:, d * GLA_QK:(d + 1) * GLA_QK], 2)
        cs = _dot(cum_bd, jnp.concatenate(gparts, axis=1))
        bcs_all.append(cs[:, :GLA_QK] + cs[:, GLA_QK:])
    for c in range(CPT):
        rows = slice(c * CH, (c + 1) * CH)
        p = pd_ref[0, rows, :]
        q = p[:, :GLA_QK] * (GLA_K ** -0.5)
        k = p[:, GLA_QK:2 * GLA_QK]
        vb = p[:, 2 * GLA_QK:].astype(BF16)
        vb_out[0, rows, :] = vb
        vt_out[0, c] = _dot_nt(eye_bf, vb).astype(BF16)
        for d in range(2):
            bcs = bcs_all[d][rows]
            bend = bcs[CH - 1:CH] if d == 0 else bcs[0:1]
            q_in = (q * jnp.exp(bcs)).astype(BF16)
            kdec = jnp.where(bdk, _tile4(k * jnp.exp(-bcs)), 0.0).astype(BF16)
            a_out[0, d, rows, :] = jnp.where(incl[d], _dot_nt(q_in, kdec), 0.0).astype(BF16)
            qi_out[0, d, rows, :] = q_in
            ke_out[0, d, rows, :] = (k * jnp.exp(bend - bcs)).astype(BF16)
            de_out[0, d, c] = jnp.exp(bend)


def _gla_local(pd, pdl, w2bd, b2):
    b, t, w = pd.shape
    nt = t // TM
    row = lambda bb, i: (bb, i, 0)
    return _Part(
        _gla_local_kernel,
        (pd, pdl, w2bd, b2),
        [pl.BlockSpec((1, TM, w), row), pl.BlockSpec((1, TM, LANES), row),
         pl.BlockSpec((LANES, GW), lambda bb, i: (0, 0)), pl.BlockSpec((1, GW), lambda bb, i: (0, 0))],
        [pl.BlockSpec((1, 2, TM, GLA_QK), lambda bb, i: (bb, 0, i, 0)),
         pl.BlockSpec((1, 2, TM, GW), lambda bb, i: (bb, 0, i, 0)),
         pl.BlockSpec((1, 2, TM, GLA_QK), lambda bb, i: (bb, 0, i, 0)),
         pl.BlockSpec((1, CPT, GW, CH), lambda bb, i: (bb, i, 0, 0)),
         pl.BlockSpec((1, TM, GW), row),
         pl.BlockSpec((1, 2, CPT, 1, GLA_QK), lambda bb, i: (bb, 0, i, 0, 0))],
        [jax.ShapeDtypeStruct((b, 2, t, GLA_QK), BF16), jax.ShapeDtypeStruct((b, 2, t, GW), BF16),
         jax.ShapeDtypeStruct((b, 2, t, GLA_QK), BF16), jax.ShapeDtypeStruct((b, t // CH, GW, CH), BF16),
         jax.ShapeDtypeStruct((b, t, GW), BF16), jax.ShapeDtypeStruct((b, 2, t // CH, 1, GLA_QK), F32)],
        [])


def _gla_chunk(refs, o_ref, s_scr, d, cc, bd, bdt):
    qi, a, ke, de, vt, v = refs
    rows = slice(cc * CH, (cc + 1) * CH)
    st = s_scr[d]
    vbd = jnp.where(bd, _tile4(v[0, rows, :]), jnp.zeros((), BF16))
    o_ref[0, rows, :] = _dot_nt(qi[0, 0, rows, :], st.astype(BF16)) + _dot(a[0, 0, rows, :], vbd)
    s_scr[d] = st * de[0, 0, cc] + jnp.where(bdt, _dot(vt[0, cc], ke[0, 0, rows, :]), 0.0)


def _scan_kernel(*refs):
    dn_in = (refs[0:6], refs[6:12])
    gla_in = (refs[12:18], refs[18:24])
    dn_out = refs[24:26]
    gla_out = refs[26:28]
    dn_s, gla_s = refs[28:30]

    @pl.when(pl.program_id(1) == 0)
    def _():
        dn_s[...] = jnp.zeros(dn_s.shape, F32)
        gla_s[...] = jnp.zeros(gla_s.shape, F32)

    bd = (_iota((GW, GW), 0) >> 6) == (_iota((GW, GW), 1) >> 6)
    bdt = (_iota((GW, GLA_QK), 0) >> 6) == (_iota((GW, GLA_QK), 1) >> 5)
    for c in range(CPT):
        for d in range(2):
            cc = c if d == 0 else CPT - 1 - c
            _dn_chunk(dn_in[d], dn_out[d], dn_s, d, cc, bd)
            _gla_chunk(gla_in[d], gla_out[d], gla_s, d, cc, bd, bdt)


def _scans(u, w, dqi, da, ket, ge, qi, a, ke, de, vt, vb):
    b, _, t, _ = u.shape
    nt = t // TM
    rowf, rowr = _dir_specs((GW,), nt, False)
    ketf, ketr = _dir_specs((GW, CH), nt, True)
    gef, ger = _dir_specs((1, GW), nt, True)
    qf, qr = _dir_specs((GLA_QK,), nt, False)
    df, dr = _dir_specs((1, GLA_QK), nt, True)
    vtf = pl.BlockSpec((1, CPT, GW, CH), lambda bb, s: (bb, _fwd_tile(s, nt), 0, 0))
    vtr = pl.BlockSpec((1, CPT, GW, CH), lambda bb, s: (bb, _rev_tile(s, nt), 0, 0))
    vf = pl.BlockSpec((1, TM, GW), lambda bb, s: (bb, _fwd_tile(s, nt), 0))
    vr = pl.BlockSpec((1, TM, GW), lambda bb, s: (bb, _rev_tile(s, nt), 0))
    dn_args = (u, w, dqi, da, ket, ge)
    gla_args = (qi, a, ke, de, vt, vb)
    return pl.pallas_call(
        _scan_kernel,
        grid=(b, nt),
        in_specs=[rowf, rowf, rowf, rowf, ketf, gef, rowr, rowr, rowr, rowr, ketr, ger,
                  qf, rowf, qf, df, vtf, vf, qr, rowr, qr, dr, vtr, vr],
        out_specs=[vf, vr, vf, vr],
        out_shape=[jax.ShapeDtypeStruct((b, t, GW), F32)] * 4,
        scratch_shapes=[pltpu.VMEM((2, GW, GW), F32), pltpu.VMEM((2, GW, GLA_QK), F32)],
        compiler_params=_cp(("parallel", "arbitrary")),
    )(*dn_args, *dn_args, *gla_args, *gla_args)


def _outproj_kernel(x_ref, ya_ref, yb_ref, cf_ref, cr_ref, df_ref, dr_ref, pg_ref, g1b_ref, g1c_ref, gc_ref, gd_ref,
                    w_ref, o_ref, *, lat_rows):
    ones64 = _group_ones(GW, 6)

    def fin(o, g, gate):
        ms = _dot_sel(o * o, ones64) * (1.0 / HEAD_DIM)
        return (o * lax.rsqrt(ms + EPS) * g * _silu(gate)).astype(BF16)

    pg = pg_ref[0]
    yc = fin(cf_ref[0] + cr_ref[0], gc_ref[...], pg[:, :GW])
    yd = fin(df_ref[0] + dr_ref[0], gd_ref[...], pg[:, GW:])
    res = (_dot(ya_ref[0].astype(BF16), w_ref[0:GW, :]) + _dot(yb_ref[0].astype(BF16), w_ref[GW:2 * GW, :])
           + _dot(yc, w_ref[2 * GW:3 * GW, :]) + _dot(yd, w_ref[3 * GW:, :]))
    g1 = jnp.where(_ctx_rows(res.shape[0], lat_rows), g1c_ref[0], g1b_ref[0])
    o_ref[0] = x_ref[0] + g1 * res


def _outproj(xs, ya, yb, ocf, ocr, odf, odr, pg, mod3, dn_g, gla_g, w_out_bf, n_batch, rows, n_blk):
    b, t, d = xs.shape
    reps = GW // HEAD_DIM
    tb = rows // n_blk
    row = lambda bb, i: (bb, i, 0)
    g256 = pl.BlockSpec((1, tb, GW), row)
    in_specs = ([pl.BlockSpec((1, tb, d), row), g256, g256, g256, g256, g256, g256, pl.BlockSpec((1, tb, 2 * GW), row)]
                + _mod_specs(2, n_batch) + [_resident((1, GW)), _resident((1, GW)), _resident((d, d))])
    args = (xs, ya, yb, ocf, ocr, odf, odr, pg, mod3, mod3, jnp.tile(dn_g, reps).reshape(1, GW),
            jnp.tile(gla_g, reps).reshape(1, GW), w_out_bf)
    kern = functools.partial(_outproj_kernel, lat_rows=t - TM)
    return _row_call(kern, n_batch, rows, args, in_specs, [d], n_blk=n_blk)[0]


FF_HALO = 8


def _ffn_kernel(xm_ref, xl_ref, xr_ref, shb_ref, shc_ref, scb_ref, scc_ref, g2b_ref, g2c_ref, ng_ref,
                wu_ref, cw_ref, wd_ref, o_ref, ext_ref, *, lat_rows):
    tb = xm_ref.shape[1]
    i = pl.program_id(1)
    rows_ext = _iota((tb + 2 * FF_HALO, 1), 0)
    left_ok = jnp.logical_and(i > 0, i * tb != lat_rows)
    right_ok = jnp.logical_and(i < pl.num_programs(1) - 1, (i + 1) * tb != lat_rows)
    keep = jnp.logical_and(jnp.logical_or(rows_ext >= FF_HALO, left_ok),
                           jnp.logical_or(rows_ext < FF_HALO + tb, right_ok))
    grow = i * tb - FF_HALO + rows_ext
    ctx_ext = grow >= lat_rows
    x = jnp.concatenate([xl_ref[0], xm_ref[0], xr_ref[0]], axis=0)
    y = x * lax.rsqrt(jnp.mean(x * x, axis=-1, keepdims=True) + EPS) * ng_ref[...]
    h = y * (1.0 + jnp.where(ctx_ext, scc_ref[0], scb_ref[0])) + jnp.where(ctx_ext, shc_ref[0], shb_ref[0])
    h = jnp.where(keep, h, 0.0).astype(BF16)
    inner_boundary = lat_rows % tb != 0
    if inner_boundary:
        row = grow[FF_HALO:FF_HALO + tb]
        m_prev = jnp.broadcast_to(jnp.where(row == lat_rows, 0.0, 1.0), (tb, FF_BLK))
        m_next = jnp.broadcast_to(jnp.where(row == lat_rows - 1, 0.0, 1.0), (tb, FF_BLK))
    acc = jnp.zeros((tb, D_MODEL), F32)
    for j in range(N_FF_BLK):
        def conv(col0, half):
            cols = slice(col0, col0 + FF_BLK)
            ext_ref[half] = _dot(h, wu_ref[:, cols])
            cw = cw_ref[:, cols]
            prev = ext_ref[half, pl.ds(FF_HALO - 1, tb), :]
            nxt = ext_ref[half, pl.ds(FF_HALO + 1, tb), :]
            if inner_boundary:
                prev = m_prev * prev
                nxt = m_next * nxt
            return cw[0:1] * prev + cw[1:2] * ext_ref[half, pl.ds(FF_HALO, tb), :] + cw[2:3] * nxt
        a = conv(j * FF_BLK, 0)
        g = conv(D_FF + j * FF_BLK, 1)
        acc = acc + _dot((_silu(g) * a).astype(BF16), wd_ref[j * FF_BLK:(j + 1) * FF_BLK, :])
    g2 = jnp.where(ctx_ext[FF_HALO:FF_HALO + tb], g2c_ref[0], g2b_ref[0])
    o_ref[0] = xm_ref[0] + g2 * acc


def _ffn(x1, mod3, norm_g, w_up, cw, w_down, n_batch, lat_rows, rows, n_blk):
    b, t, d = x1.shape
    tb = rows // n_blk
    per = tb // FF_HALO
    last = t // FF_HALO - 1
    in_specs = ([pl.BlockSpec((1, tb, d), lambda bb, i: (bb, i, 0)),
                 pl.BlockSpec((1, FF_HALO, d), lambda bb, i: (bb, jnp.maximum(i * per - 1, 0), 0)),
                 pl.BlockSpec((1, FF_HALO, d), lambda bb, i: (bb, jnp.minimum((i + 1) * per, last), 0))]
                + _mod_specs(3, n_batch) + _mod_specs(4, n_batch) + _mod_specs(5, n_batch)
                + [_resident((1, d)), _resident(w_up.shape), _resident(cw.shape), _resident(w_down.shape)])
    args = (x1, x1, x1, mod3, mod3, mod3, mod3, mod3, mod3, norm_g.reshape(1, d), w_up, cw, w_down)
    kern = functools.partial(_ffn_kernel, lat_rows=lat_rows)
    return _row_call(kern, n_batch, rows, args, in_specs, [d],
                     scratch=[pltpu.VMEM((2, tb + 2 * FF_HALO, FF_BLK), F32)], n_blk=n_blk)[0]


def _rope_tables(seq, ctx_len):
    rows = seq // GRID_W
    row = jnp.repeat(jnp.arange(rows, dtype=F32), GRID_W)
    col = jnp.tile(jnp.arange(GRID_W, dtype=F32), rows)
    nf = QK_DIM // 4
    inv = ROPE_THETA ** (-jnp.arange(nf, dtype=F32) / nf)
    ang = jnp.concatenate([row[:, None] * inv, col[:, None] * inv], axis=-1)
    cos = jnp.concatenate([jnp.cos(ang), jnp.ones((ctx_len, QK_DIM // 2), F32)], axis=0)
    sin = jnp.concatenate([jnp.sin(ang), jnp.zeros((ctx_len, QK_DIM // 2), F32)], axis=0)
    reps = GW // QK_DIM
    return (jnp.tile(jnp.concatenate([cos, cos], axis=-1), (1, reps)),
            jnp.tile(jnp.concatenate([-sin, sin], axis=-1), (1, reps)))


def _regroup_w_in(w):
    d = w.shape[0]
    z = lambda n: jnp.zeros((d, n), w.dtype)
    return jnp.concatenate([w[:, :2048], w[:, 2048:2064], z(LANES - 16), w[:, 2320:2832], w[:, 2832:2864],
                            z(LANES - 32), w[:, 2064:2320], w[:, 2864:3120]], axis=1).astype(BF16)


def _gla_w2_blockdiag(w2):
    out = jnp.zeros((LANES, GW), F32)
    out = out.at[0:GLA_RANK, 0:GLA_QK].set(w2[0])
    return out.at[GLA_RANK:2 * GLA_RANK, GLA_QK:].set(w2[1])


def _layer(xs, mod3, lp, cos_t, sin_t, layer_idx, last, n_batch):
    b, t, d = xs.shape
    nt = t // TM
    pa, pb, pc, pcs, pd, pdl, pg = _inproj(xs, mod3, lp["norm1_g"], _regroup_w_in(lp["w_in"]), n_batch)

    lam_init = 0.8 - 0.6 * math.exp(-0.3 * layer_idx)
    (q, kt, v, qn, kn), (ya,), dn_parts, (qi, a, ke, vt, vb, de) = _fused_call(
        [_attn_prep(pb, lp["da_qnorm_g"], lp["da_knorm_g"], cos_t, sin_t),
         _conv_module(pa, lp["cm_conv_w"], lp["cm_conv_b"], lp["cm_ln_g"], lp["cm_ln_b"]),
         _dn_local(pc, pcs, lp["dn_conv_w"], lp["dn_a_log"], lp["dn_dt_bias"]),
         _gla_local(pd, pdl, _gla_w2_blockdiag(lp["gla_w2"]), lp["gla_b2"].reshape(1, GW))],
        (b, nt))
    yb = _attention(q, kt, v, qn, kn, lp["da_lambda"], lp["da_subln_g"], lam_init)
    ocf, ocr, odf, odr = _scans(*dn_parts, qi, a, ke, de, vt, vb)

    lat_rows = t - TM
    rows = lat_rows if last else t
    x1 = _outproj(xs, ya, yb, ocf, ocr, odf, odr, pg, mod3, lp["dn_onorm_g"], lp["gla_onorm_g"],
                  lp["w_out"].astype(BF16), n_batch, rows, N_BLK_FFN)
    return _ffn(x1, mod3, lp["norm2_g"], lp["ffn_w_up"].astype(BF16), lp["ffn_conv_w"],
                lp["ffn_w_down"].astype(BF16), n_batch, lat_rows, rows, N_BLK_FFN)


def kernel(x, c, ctx, c_ctx, w_mod, b_mod, norm1_g, norm2_g, w_in, w_out, cm_conv_w, cm_conv_b, cm_ln_g, cm_ln_b, da_qnorm_g, da_knorm_g, da_lambda, da_subln_g, dn_conv_w, dn_a_log, dn_dt_bias, dn_onorm_g, gla_w2, gla_b2, gla_onorm_g, ffn_w_up, ffn_conv_w, ffn_w_down):
    n_batch, seq, d = x.shape
    ctx_len = ctx.shape[1]
    assert ctx_len == TM and seq % TM == 0 and d == D_MODEL
    assert (seq + ctx_len) % (8 * N_BLK) == 0 and (seq + ctx_len) % (8 * N_BLK_FFN) == 0 and seq % (8 * N_BLK_FFN) == 0
    depth = w_mod.shape[0]
    cos_t, sin_t = _rope_tables(seq, ctx_len)
    xs = jnp.concatenate([x, ctx], axis=1)
    mod_rows = 16
    c_rows = jnp.zeros((mod_rows, d), F32).at[:n_batch].set(c).at[n_batch].set(c_ctx)
    params = dict(w_mod=w_mod, b_mod=b_mod, norm1_g=norm1_g, norm2_g=norm2_g, w_in=w_in, w_out=w_out,
                  cm_conv_w=cm_conv_w, cm_conv_b=cm_conv_b, cm_ln_g=cm_ln_g, cm_ln_b=cm_ln_b,
                  da_qnorm_g=da_qnorm_g, da_knorm_g=da_knorm_g, da_lambda=da_lambda, da_subln_g=da_subln_g,
                  dn_conv_w=dn_conv_w, dn_a_log=dn_a_log, dn_dt_bias=dn_dt_bias, dn_onorm_g=dn_onorm_g,
                  gla_w2=gla_w2, gla_b2=gla_b2, gla_onorm_g=gla_onorm_g,
                  ffn_w_up=ffn_w_up, ffn_conv_w=ffn_conv_w, ffn_w_down=ffn_w_down)
    for l in range(depth):
        lp = {k: v[l] for k, v in params.items()}
        mod3 = _modulation(c_rows, lp["w_mod"], lp["b_mod"]).reshape(mod_rows * 6, 1, d)
        xs = _layer(xs, mod3, lp, cos_t, sin_t, l, l == depth - 1, n_batch)
    return xs
```

```python
import functools
import math
from typing import Any, NamedTuple

import jax
import jax.numpy as jnp
from jax import lax
from jax.experimental import pallas as pl
from jax.experimental.pallas import tpu as pltpu

F32 = jnp.float32
BF16 = jnp.bfloat16
HI = lax.Precision.HIGHEST
EPS = 1e-6

D_MODEL = 1024
GRID_W = 64
HEADS = 4
HEAD_DIM = 64
GW = 256
QK_DIM = 32
GLA_K = 32
GLA_RANK = 16
GLA_TAU = 16.0
CM_KERNEL = 31
DN_CONV = 5
ROPE_THETA = 10000.0
CH = 64
TM = 256
CPT = TM // CH
D_FF = 2816
FF_BLK = 256
N_FF_BLK = D_FF // FF_BLK
LANES = 128

IN_GROUPS = (("pa", 512), ("pb", 768), ("pc", 768), ("pcs", LANES), ("pd", 512), ("pdl", LANES), ("pg", 512))
IN_COLS_PAD = sum(w for _, w in IN_GROUPS)

VMEM_LIMIT = 56 * 1024 * 1024


def _cp(sem):
    return pltpu.CompilerParams(dimension_semantics=sem, vmem_limit_bytes=VMEM_LIMIT)


def _dot(a, b, prec=None):
    return jnp.dot(a, b, preferred_element_type=F32, precision=prec)


def _dot_nt(a, b, prec=None):
    return lax.dot_general(a, b, (((1,), (1,)), ((), ())), preferred_element_type=F32, precision=prec)


def _sigmoid(x):
    return 1.0 / (1.0 + jnp.exp(-x))


def _silu(x):
    return x * _sigmoid(x)


def _softplus(x):
    return jnp.maximum(x, 0.0) + jnp.log(1.0 + jnp.exp(-jnp.abs(x)))


def _iota(shape, dim):
    return lax.broadcasted_iota(jnp.int32, shape, dim)


def _group_ones(n, shift):
    return ((_iota((n, n), 0) >> shift) == (_iota((n, n), 1) >> shift)).astype(BF16)


def _bf16_parts(m, n):
    out = []
    for _ in range(n):
        hi = m.astype(BF16)
        out.append(hi)
        m = m - hi.astype(F32)
    return out


def _dot_sel(x, sel, n_parts=2):
    rows = x.shape[0]
    r = _dot(jnp.concatenate(_bf16_parts(x, n_parts), axis=0), sel)
    out = r[:rows]
    for k in range(1, n_parts):
        out = out + r[k * rows:(k + 1) * rows]
    return out


def _tile4(y):
    return jnp.concatenate([y, y, y, y], axis=0)


def _mod_kernel(c_ref, w_ref, b_ref, o_ref):
    o_ref[...] = _dot(_silu(c_ref[...]), w_ref[...], HI) + b_ref[...]


def _modulation(c_rows, w_mod, b_mod):
    r, d = c_rows.shape
    n = w_mod.shape[1] // d
    return pl.pallas_call(
        _mod_kernel,
        grid=(n,),
        in_specs=[pl.BlockSpec((r, d), lambda j: (0, 0)),
                  pl.BlockSpec((d, d), lambda j: (0, j)),
                  pl.BlockSpec((1, d), lambda j: (0, j))],
        out_specs=pl.BlockSpec((r, d), lambda j: (0, j)),
        out_shape=jax.ShapeDtypeStruct((r, n * d), F32),
        compiler_params=_cp(("arbitrary",)),
    )(c_rows, w_mod, b_mod.reshape(1, -1))


N_BLK = 4
N_BLK_FFN = 8


def _mod_specs(k, n_batch):
    return [pl.BlockSpec((1, 1, D_MODEL), lambda b, i: (b * 6 + k, 0, 0)),
            pl.BlockSpec((1, 1, D_MODEL), lambda b, i: (n_batch * 6 + k, 0, 0))]


def _ctx_rows(tb, lat_rows):
    return pl.program_id(1) * tb + _iota((tb, 1), 0) >= lat_rows


def _resident(shape):
    zeros = (0,) * len(shape)
    return pl.BlockSpec(shape, lambda b, i: zeros, pipeline_mode=pl.Buffered(1))


def _row_call(kern, n_batch, rows, args, in_specs, out_widths, scratch=(), n_blk=N_BLK):
    tb = rows // n_blk
    return pl.pallas_call(
        kern,
        grid=(n_batch, n_blk),
        in_specs=in_specs,
        out_specs=[pl.BlockSpec((1, tb, w), lambda bb, i: (bb, i, 0)) for w in out_widths],
        out_shape=[jax.ShapeDtypeStruct((n_batch, rows, w), F32) for w in out_widths],
        scratch_shapes=list(scratch),
        compiler_params=_cp(("parallel", "arbitrary")),
    )(*args)


def _inproj_kernel(x_ref, shb_ref, shc_ref, scb_ref, scc_ref, g_ref, w_ref, *outs, lat_rows):
    x = x_ref[0]
    ctx = _ctx_rows(x.shape[0], lat_rows)
    y = x * lax.rsqrt(jnp.mean(x * x, axis=-1, keepdims=True) + EPS) * g_ref[...]
    h = y * (1.0 + jnp.where(ctx, scc_ref[0], scb_ref[0])) + jnp.where(ctx, shc_ref[0], shb_ref[0])
    p = _dot(h.astype(BF16), w_ref[...])
    off = 0
    for o_ref, (_, width) in zip(outs, IN_GROUPS):
        o_ref[0] = p[:, off:off + width]
        off += width


def _inproj(xs, mod3, norm_g, w_in_r, n_batch):
    b, t, d = xs.shape
    in_specs = ([pl.BlockSpec((1, t // N_BLK, d), lambda bb, i: (bb, i, 0))] + _mod_specs(0, n_batch)
                + _mod_specs(1, n_batch) + [_resident((1, d)), _resident((d, IN_COLS_PAD))])
    kern = functools.partial(_inproj_kernel, lat_rows=t - TM)
    return _row_call(kern, n_batch, t, (xs, mod3, mod3, mod3, mod3, norm_g.reshape(1, d), w_in_r), in_specs,
                     [w for _, w in IN_GROUPS])


class _Part(NamedTuple):
    body: Any
    args: Any
    in_specs: Any
    out_specs: Any
    out_shapes: Any
    scratch: Any


def _fused_call(parts, grid):
    n_in = [len(p.args) for p in parts]
    n_out = [len(p.out_specs) for p in parts]
    n_scr = [len(p.scratch) for p in parts]

    def kern(*refs):
        ins, outs, scr = refs[:sum(n_in)], refs[sum(n_in):sum(n_in) + sum(n_out)], refs[sum(n_in) + sum(n_out):]
        i = o = s = 0
        for p, ni, no, ns in zip(parts, n_in, n_out, n_scr):
            p.body(*ins[i:i + ni], *outs[o:o + no], *scr[s:s + ns])
            i, o, s = i + ni, o + no, s + ns

    res = pl.pallas_call(
        kern,
        grid=grid,
        in_specs=[sp for p in parts for sp in p.in_specs],
        out_specs=[sp for p in parts for sp in p.out_specs],
        out_shape=[sh for p in parts for sh in p.out_shapes],
        scratch_shapes=[sc for p in parts for sc in p.scratch],
        compiler_params=_cp(("parallel", "arbitrary")),
    )(*[a for p in parts for a in p.args])
    out, o = [], 0
    for no in n_out:
        out.append(res[o:o + no])
        o += no
    return out


def _halo_specs(width, halo, n_tiles):
    per = TM // halo
    left = pl.BlockSpec((1, halo, width), lambda b, i: (b, jnp.maximum(i * per - 1, 0), 0))
    right = pl.BlockSpec((1, halo, width), lambda b, i: (b, jnp.minimum((i + 1) * per, n_tiles * per - 1), 0))
    return left, right


def _halo_ok(i, n_tiles):
    return jnp.logical_and(i >= 1, i < n_tiles - 1), i < n_tiles - 2


CM_HALO = 16


def _convmod_kernel(pm_ref, pl_ref, pr_ref, cw_ref, cb_ref, lg_ref, lb_ref, o_ref, ext_ref):
    left_ok, right_ok = _halo_ok(pl.program_id(1), pl.num_programs(1))

    def glu(p):
        return p[:, :GW] * _sigmoid(p[:, GW:])

    ext_ref[0:CM_HALO] = jnp.where(left_ok, glu(pl_ref[0]), 0.0)
    ext_ref[CM_HALO:CM_HALO + TM] = glu(pm_ref[0])
    ext_ref[CM_HALO + TM:] = jnp.where(right_ok, glu(pr_ref[0]), 0.0)
    pad = CM_KERNEL // 2
    sub = 8
    y = cb_ref[...]
    for r in range(sub):
        acc = None
        for j in range(CM_KERNEL):
            off = CM_HALO - pad + j
            if off % sub == r:
                term = cw_ref[j:j + 1, :] * ext_ref[pl.ds(off - r, TM + sub), :]
                acc = term if acc is None else acc + term
        if acc is not None:
            y = y + acc[r:r + TM]
    mu = jnp.mean(y, axis=-1, keepdims=True)
    yc = y - mu
    var = jnp.mean(yc * yc, axis=-1, keepdims=True)
    o_ref[0] = _silu(yc * lax.rsqrt(var + EPS) * lg_ref[...] + lb_ref[...])


def _conv_module(pa, conv_w, conv_b, ln_g, ln_b):
    b, t, w = pa.shape
    nt = t // TM
    left, right = _halo_specs(w, CM_HALO, nt)
    vec = pl.BlockSpec((1, GW), lambda bb, i: (0, 0))
    return _Part(
        _convmod_kernel,
        (pa, pa, pa, conv_w, conv_b.reshape(1, GW), ln_g.reshape(1, GW), ln_b.reshape(1, GW)),
        [pl.BlockSpec((1, TM, w), lambda bb, i: (bb, i, 0)), left, right,
         pl.BlockSpec((CM_KERNEL, GW), lambda bb, i: (0, 0)), vec, vec, vec],
        [pl.BlockSpec((1, TM, GW), lambda bb, i: (bb, i, 0))],
        [jax.ShapeDtypeStruct((b, t, GW), F32)],
        [pltpu.VMEM((TM + 2 * CM_HALO, GW), F32)])


N_MAPS = 2 * HEADS
QK_PAD = 2 * QK_DIM
V_EXT = 2 * HEAD_DIM
MASK_BIG = 8192.0
SHIFT_MAX = 40.0


def _attn_prep_kernel(pb_ref, qg_ref, kg_ref, cos_ref, sin_ref, q_out, kt_out, v_out, qn_out, kn_out):
    p = pb_ref[0]
    ones32 = _group_ones(GW, 5)
    first = (_iota((TM, GW), 1) & (QK_DIM - 1)) < QK_DIM // 2
    cos = cos_ref[...]
    sin = sin_ref[...]

    def norm_rope(t, g):
        ms = _dot_sel(t * t, ones32) * (1.0 / QK_DIM)
        tn = t * lax.rsqrt(ms + EPS) * g
        partner = jnp.where(first, pltpu.roll(tn, GW - QK_DIM // 2, 1), pltpu.roll(tn, QK_DIM // 2, 1))
        return tn * cos + partner * sin

    qf = norm_rope(p[:, :GW], qg_ref[...]) * (QK_DIM ** -0.5)
    kf = norm_rope(p[:, GW:2 * GW], kg_ref[...])
    map_sel = ((_iota((GW, N_MAPS), 0) >> 5) == _iota((GW, N_MAPS), 1)).astype(BF16)
    qn_out[0] = _dot_sel(qf * qf, map_sel)
    kn_out[0] = _dot_sel(kf * kf, map_sel)
    q = qf.astype(BF16)
    kt = kf.T.astype(BF16)
    is_ctx = pl.program_id(1) == pl.num_programs(1) - 1
    k_row = _iota((QK_DIM, TM), 0)
    k_tail = jnp.where(k_row == 0, 1.0, jnp.where(jnp.logical_and(k_row == 1, jnp.logical_not(is_ctx)), 1.0, 0.0))
    k_tail = k_tail.astype(BF16)
    q_tail = jnp.where(jnp.logical_and(_iota((TM, QK_DIM), 1) == 1, is_ctx), -MASK_BIG, 0.0).astype(BF16)
    for g in range(N_MAPS):
        q_out[0, g, :, 0:QK_DIM] = q[:, g * QK_DIM:(g + 1) * QK_DIM]
        q_out[0, g, :, QK_DIM:] = q_tail
        kt_out[0, 0, g, 0:QK_DIM, :] = kt[g * QK_DIM:(g + 1) * QK_DIM, :]
        kt_out[0, 0, g, QK_DIM:, :] = k_tail
    v = p[:, 2 * GW:].astype(BF16)
    ones = jnp.ones((TM, HEAD_DIM), BF16)
    for h in range(HEADS):
        v_out[0, 0, h, :, 0:HEAD_DIM] = v[:, h * HEAD_DIM:(h + 1) * HEAD_DIM]
        v_out[0, 0, h, :, HEAD_DIM:] = ones


def _attn_prep(pb, qn_g, kn_g, cos_t, sin_t):
    b, t, w = pb.shape
    nt = t // TM
    vec = pl.BlockSpec((1, GW), lambda bb, i: (0, 0))
    tab = pl.BlockSpec((TM, GW), lambda bb, i: (i, 0))
    reps = GW // QK_DIM
    n_spec = pl.BlockSpec((1, TM, N_MAPS), lambda bb, i: (bb, i, 0))
    n_shape = jax.ShapeDtypeStruct((b, t, N_MAPS), F32)
    return _Part(
        _attn_prep_kernel,
        (pb, jnp.tile(qn_g, reps).reshape(1, GW), jnp.tile(kn_g, reps).reshape(1, GW), cos_t, sin_t),
        [pl.BlockSpec((1, TM, w), lambda bb, i: (bb, i, 0)), vec, vec, tab, tab],
        [pl.BlockSpec((1, N_MAPS, TM, QK_PAD), lambda bb, i: (bb, 0, i, 0)),
         pl.BlockSpec((1, 1, N_MAPS, QK_PAD, TM), lambda bb, i: (bb, i, 0, 0, 0)),
         pl.BlockSpec((1, 1, HEADS, TM, V_EXT), lambda bb, i: (bb, i, 0, 0, 0)), n_spec, n_spec],
        [jax.ShapeDtypeStruct((b, N_MAPS, t, QK_PAD), BF16),
         jax.ShapeDtypeStruct((b, nt, N_MAPS, QK_PAD, TM), BF16),
         jax.ShapeDtypeStruct((b, nt, HEADS, TM, V_EXT), BF16), n_shape, n_shape],
        [])


def _attn_kernel(q_ref, kt_ref, v_ref, qn_ref, kn_ref, lam_ref, sg_ref, o_ref, qa_scr, acc_scr, s_scr, m_scr,
                 *, n_chunks, lam_init):
    tq = o_ref.shape[1]
    acc_scr[...] = jnp.zeros(acc_scr.shape, F32)
    k2 = jnp.max(kn_ref[0], axis=0, keepdims=True)
    q2 = jnp.max(qn_ref[0], axis=0, keepdims=True)
    bound = jnp.sqrt(q2 * k2)
    safe = jnp.max(bound) <= SHIFT_MAX
    shift = jnp.where(safe, bound, 0.0)
    shift_lane = _iota((tq, QK_PAD), 1) == QK_DIM
    for g in range(N_MAPS):
        qa_scr[g] = jnp.where(shift_lane, (-shift[:, g:g + 1]).astype(BF16), q_ref[0, g])

    @pl.when(safe)
    def _():
        s_scr[0] = _dot(qa_scr[0], kt_ref[0, 0, 0])

        def body(c, carry):
            nxt = jnp.minimum(c + 1, n_chunks - 1)
            for g in range(N_MAPS):
                if g + 1 < N_MAPS:
                    s_next = _dot(qa_scr[g + 1], kt_ref[0, c, g + 1])
                else:
                    s_next = _dot(qa_scr[0], kt_ref[0, nxt, 0])
                p = jnp.exp(s_scr[g & 1]).astype(BF16)
                acc_scr[g] += _dot(p, v_ref[0, c, g // 2])
                s_scr[(g + 1) & 1] = s_next
            return carry

        lax.fori_loop(0, n_chunks, body, 0)

    @pl.when(jnp.logical_not(safe))
    def _():
        m_scr[...] = jnp.full(m_scr.shape, -jnp.inf, F32)

        def body(c, carry):
            for g in range(N_MAPS):
                s = _dot(qa_scr[g], kt_ref[0, c, g])
                m_old = m_scr[:, g:g + 1]
                m_new = jnp.maximum(m_old, jnp.max(s, axis=-1, keepdims=True))
                p = jnp.exp(s - m_new).astype(BF16)
                acc_scr[g] = jnp.exp(m_old - m_new) * acc_scr[g] + _dot(p, v_ref[0, c, g // 2])
                m_scr[:, g:g + 1] = m_new
            return carry

        lax.fori_loop(0, n_chunks, body, 0)

    lp = lam_ref[...]
    lam = (jnp.exp(jnp.sum(lp[0:1] * lp[1:2], axis=-1, keepdims=True))
           - jnp.exp(jnp.sum(lp[2:3] * lp[3:4], axis=-1, keepdims=True)) + lam_init)
    for h in range(HEADS):
        a0 = acc_scr[2 * h]
        a1 = acc_scr[2 * h + 1]
        o = (a0[:, :HEAD_DIM] / a0[:, HEAD_DIM:HEAD_DIM + 1]
             - lam * (a1[:, :HEAD_DIM] / a1[:, HEAD_DIM:HEAD_DIM + 1]))
        y = o * lax.rsqrt(jnp.mean(o * o, axis=-1, keepdims=True) + EPS) * sg_ref[...] * (1.0 - lam_init)
        o_ref[0, :, h * HEAD_DIM:(h + 1) * HEAD_DIM] = y


def _attention(q, kt, v, qn, kn, lam_p, subln_g, lam_init, rows):
    b, _, t, _ = q.shape
    tq = rows // N_BLK
    nt = t // TM
    kern = functools.partial(_attn_kernel, n_chunks=nt, lam_init=lam_init)
    return pl.pallas_call(
        kern,
        grid=(b, N_BLK),
        in_specs=[pl.BlockSpec((1, N_MAPS, tq, QK_PAD), lambda bb, i: (bb, 0, i, 0)),
                  pl.BlockSpec((1, nt, N_MAPS, QK_PAD, TM), lambda bb, i: (bb, 0, 0, 0, 0)),
                  pl.BlockSpec((1, nt, HEADS, TM, V_EXT), lambda bb, i: (bb, 0, 0, 0, 0)),
                  pl.BlockSpec((1, tq, N_MAPS), lambda bb, i: (bb, i, 0)),
                  pl.BlockSpec((1, t, N_MAPS), lambda bb, i: (bb, 0, 0)),
                  _resident((4, QK_DIM)), _resident((1, HEAD_DIM))],
        out_specs=pl.BlockSpec((1, tq, GW), lambda bb, i: (bb, i, 0)),
        out_shape=jax.ShapeDtypeStruct((b, rows, GW), F32),
        scratch_shapes=[pltpu.VMEM((N_MAPS, tq, QK_PAD), BF16), pltpu.VMEM((N_MAPS, tq, V_EXT), F32),
                        pltpu.VMEM((2, tq, TM), F32), pltpu.VMEM((tq, N_MAPS), F32)],
        compiler_params=_cp(("parallel", "arbitrary")),
    )(q, kt, v, qn, kn, lam_p, subln_g.reshape(1, HEAD_DIM))


def _bd(y, bd):
    return jnp.where(bd, _tile4(y), 0.0).astype(BF16)


def _fwd_tile(s, n_tiles):
    return jnp.where(s == 0, n_tiles - 1, s - 1)


def _rev_tile(s, n_tiles):
    return jnp.where(s == 0, n_tiles - 1, n_tiles - 1 - s)


DN_HALO = 8


def _dn_prep(pm_ref, pl_ref, pr_ref, pcs_ref, cw_ref, alog_ref, dtb_ref, ext_ref):
    left_ok, right_ok = _halo_ok(pl.program_id(1), pl.num_programs(1))
    ext_ref[0:DN_HALO] = jnp.where(left_ok, pl_ref[0], 0.0)
    ext_ref[DN_HALO:DN_HALO + TM] = pm_ref[0]
    ext_ref[DN_HALO + TM:] = jnp.where(right_ok, pr_ref[0], 0.0)
    pad = DN_CONV // 2
    acc = jnp.zeros((TM, 3 * GW), F32)
    for j in range(DN_CONV):
        acc = acc + cw_ref[j:j + 1, :] * ext_ref[pl.ds(DN_HALO - pad + j, TM), :]
    qkv = _silu(acc)
    ones64 = _group_ones(GW, 6)

    def l2n(t):
        return t * lax.rsqrt(_dot_sel(t * t, ones64) + EPS)

    s = pcs_ref[0]
    col = _iota(s.shape, 1)
    gate = -jnp.exp(alog_ref[...]) * _softplus(s + dtb_ref[...])
    bg = jnp.where(col < 2 * HEADS, _sigmoid(s), jnp.where(col < 4 * HEADS, gate, 0.0))
    return l2n(qkv[:, :GW]) * (HEAD_DIM ** -0.5), l2n(qkv[:, GW:2 * GW]), qkv[:, 2 * GW:], bg


def _dn_local_kernel(pm_ref, pl_ref, pr_ref, pcs_ref, cw_ref, alog_ref, dtb_ref,
                     u_out, w_out, qi_out, a_out, ket_out, ge_out, ext_ref):
    rr = _iota((TM, GW), 0)
    cc = _iota((TM, GW), 1)
    i_in = rr & (CH - 1)
    j_in = cc & (CH - 1)
    bd = (rr >> 6) == (cc >> 6)
    eye_t = (i_in == j_in).astype(F32)
    incl = (j_in <= i_in, j_in >= i_in)
    strict = (j_in < i_in, j_in > i_in)
    eye_bf = (rr == cc).astype(BF16)
    head_of_lane = _iota((LANES, GW), 1) >> 6
    src = _iota((LANES, GW), 0)
    q, k, v, bg = _dn_prep(pm_ref, pl_ref, pr_ref, pcs_ref, cw_ref, alog_ref, dtb_ref, ext_ref)
    ones_bf = bd.astype(BF16)
    beta, gcum, decay, kb, egc = [], [], [], [], []
    for d in range(2):
        sel_b = (src == d * HEADS + head_of_lane).astype(BF16)
        sel_g = (src == 2 * HEADS + d * HEADS + head_of_lane).astype(BF16)
        cum_bd = jnp.where(jnp.logical_and(bd, incl[d]), 1.0, 0.0).astype(BF16)
        g_exp = _dot(jnp.concatenate(_bf16_parts(bg, 2), axis=0), sel_g)
        beta.append(_dot(bg.astype(BF16), sel_b))
        cs = _dot(cum_bd, jnp.concatenate([g_exp[:TM].astype(BF16), g_exp[TM:].astype(BF16)], axis=1))
        gcum.append(cs[:, :GW] + cs[:, GW:])
        gparts = _bf16_parts(gcum[d] * eye_t, 3)
        tr = _dot(ones_bf, jnp.concatenate(gparts, axis=1))
        grow = tr[:, :GW] + tr[:, GW:2 * GW] + tr[:, 2 * GW:]
        decay.append(jnp.where(incl[d], jnp.exp(jnp.where(incl[d], gcum[d] - grow, 0.0)), 0.0))
        kb.append(k * beta[d])
        egc.append(jnp.exp(gcum[d]))
        qi_out[0, d] = (q * egc[d]).astype(BF16)
    pairs = [(c, d) for c in range(CPT) for d in range(2)]
    rows = [slice(c * CH, (c + 1) * CH) for c in range(CPT)]
    eye = eye_t[:CH]
    a = {}
    for c in range(CPT):
        r = rows[c]
        lhs = jnp.concatenate([kb[0][r], kb[1][r], q[r]], axis=0).astype(BF16)
        aq = _dot_nt(lhs, _bd(k[r], bd))
        for d in range(2):
            dec = decay[d][r]
            a[c, d] = jnp.where(strict[d][:CH], aq[d * CH:(d + 1) * CH] * dec, 0.0)
            a_out[0, d, r, :] = jnp.where(incl[d][:CH], aq[2 * CH:] * dec, 0.0).astype(BF16)
    t_inv = {cd: eye - a[cd] for cd in pairs}
    p = {cd: _dot(a[cd].astype(BF16), _bd(a[cd], bd)) for cd in pairs}
    for it in range(5):
        for cd in pairs:
            pbd = _bd(p[cd], bd)
            if it < 4:
                res = _dot(jnp.concatenate([t_inv[cd], p[cd]], axis=0).astype(BF16), pbd)
                t_inv[cd] = t_inv[cd] + res[:CH]
                p[cd] = res[CH:]
            else:
                t_inv[cd] = t_inv[cd] + _dot(t_inv[cd].astype(BF16), pbd)

    def split(m):
        hi = m.astype(BF16)
        return hi, (m - hi.astype(F32)).astype(BF16)

    for cd in pairs:
        x0 = t_inv[cd]
        ah, al = split(a[cd])
        xh, xl = split(x0)
        hx = _dot(jnp.concatenate([ah, al], axis=0), _bd(xh, bd))
        resid = eye - x0 - (hx[:CH] + hx[CH:] + _dot(ah, _bd(xl, bd)))
        t_inv[cd] = x0 + _dot(xh, _bd(resid, bd))
    for c, d in pairs:
        r = rows[c]
        tb = t_inv[c, d].astype(BF16)
        u_out[0, d, r, :] = _dot(tb, _bd(v[r] * beta[d][r], bd))
        w_out[0, d, r, :] = _dot(tb, _bd(kb[d][r] * egc[d][r], bd)).astype(BF16)
        last = (c + 1) * CH - 1 if d == 0 else c * CH
        gtot = gcum[d][last:last + 1]
        k_end = (k[r] * jnp.exp(gtot - gcum[d][r])).astype(BF16)
        ket_out[0, d, c] = _dot_nt(eye_bf, k_end).astype(BF16)
        ge_out[0, d, c] = jnp.exp(gtot)


def _dn_local(pc, pcs, conv_w, a_log, dt_bias):
    b, t, w = pc.shape
    nt = t // TM
    left, right = _halo_specs(w, DN_HALO, nt)
    row = lambda bb, i: (bb, i, 0)
    pad_vec = lambda a: jnp.zeros((1, LANES), F32).at[0, 2 * HEADS:4 * HEADS].set(a.reshape(-1))
    vec = pl.BlockSpec((1, LANES), lambda bb, i: (0, 0))
    drow = pl.BlockSpec((1, 2, TM, GW), lambda bb, i: (bb, 0, i, 0))
    return _Part(
        _dn_local_kernel,
        (pc, pc, pc, pcs, conv_w, pad_vec(a_log), pad_vec(dt_bias)),
        [pl.BlockSpec((1, TM, w), row), left, right, pl.BlockSpec((1, TM, LANES), row),
         pl.BlockSpec((DN_CONV, w), lambda bb, i: (0, 0)), vec, vec],
        [drow, drow, drow, drow,
         pl.BlockSpec((1, 2, CPT, GW, CH), lambda bb, i: (bb, 0, i, 0, 0)),
         pl.BlockSpec((1, 2, CPT, 1, GW), lambda bb, i: (bb, 0, i, 0, 0))],
        [jax.ShapeDtypeStruct((b, 2, t, GW), F32)] + [jax.ShapeDtypeStruct((b, 2, t, GW), BF16)] * 3
        + [jax.ShapeDtypeStruct((b, 2, t // CH, GW, CH), BF16), jax.ShapeDtypeStruct((b, 2, t // CH, 1, GW), F32)],
        [pltpu.VMEM((TM + 2 * DN_HALO, w), F32)])


def _dn_chunk(refs, o_ref, s_scr, d, cc, bd):
    u, w, qi, a, ket, ge = refs
    rows = slice(cc * CH, (cc + 1) * CH)
    s = s_scr[d]
    wq = _dot(jnp.concatenate([w[0, 0, rows, :], qi[0, 0, rows, :]], axis=0), s.astype(BF16))
    v_new = u[0, 0, rows, :] - wq[:CH]
    o_ref[0, rows, :] = wq[CH:] + _dot(a[0, 0, rows, :], _bd(v_new, bd))
    s_scr[d] = s * ge[0, 0, cc] + jnp.where(bd, _dot(ket[0, 0, cc], v_new.astype(BF16)), 0.0)


def _dir_specs(shape_tail, n_tiles, chunked):
    blk = (1, 1, CPT if chunked else TM) + shape_tail
    zeros = (0,) * len(shape_tail)
    fwd = pl.BlockSpec(blk, lambda b, s: (b, 0, _fwd_tile(s, n_tiles)) + zeros)
    rev = pl.BlockSpec(blk, lambda b, s: (b, 1, _rev_tile(s, n_tiles)) + zeros)
    return fwd, rev


GLA_QK = HEADS * GLA_K


def _gla_local_kernel(pd_ref, pdl_ref, w2_ref, b2_ref, qi_out, a_out, ke_out, vt_out, vb_out, de_out):
    j_in = _iota((CH, GW), 1) & (CH - 1)
    i_in = _iota((CH, GW), 0)
    incl = (j_in <= i_in, j_in >= i_in)
    rr = _iota((TM, TM), 0)
    cc = _iota((TM, TM), 1)
    same_chunk = (rr >> 6) == (cc >> 6)
    eye_bf = (rr == cc).astype(BF16)
    bdk = (_iota((GW, GLA_QK), 0) >> 6) == (_iota((GW, GLA_QK), 1) >> 5)
    z = _dot(pdl_ref[0], w2_ref[...], HI) + b2_ref[...]
    gk_all = -_softplus(-z) * (1.0 / GLA_TAU)
    bcs_all = []
    for d in range(2):
        cum_bd = jnp.where(jnp.logical_and(same_chunk, cc <= rr if d == 0 else cc >= rr), 1.0, 0.0).astype(BF16)
        gparts = _bf16_parts(gk_all[:, d * GLA_QK:(d + 1) * GLA_QK], 2)
        cs = _dot(cum_bd, jnp.concatenate(gparts, axis=1))
        bcs_all.append(cs[:, :GLA_QK] + cs[:, GLA_QK:])
    for c in range(CPT):
        rows = slice(c * CH, (c + 1) * CH)
        p = pd_ref[0, rows, :]
        q = p[:, :GLA_QK] * (GLA_K ** -0.5)
        k = p[:, GLA_QK:2 * GLA_QK]
        vb = p[:, 2 * GLA_QK:].astype(BF16)
        vb_out[0, rows, :] = vb
        vt_out[0, c] = _dot_nt(eye_bf, vb).astype(BF16)
        for d in range(2):
            bcs = bcs_all[d][rows]
            bend = bcs[CH - 1:CH] if d == 0 else bcs[0:1]
            q_in = (q * jnp.exp(bcs)).astype(BF16)
            kdec = jnp.where(bdk, _tile4(k * jnp.exp(-bcs)), 0.0).astype(BF16)
            a_out[0, d, rows, :] = jnp.where(incl[d], _dot_nt(q_in, kdec), 0.0).astype(BF16)
            qi_out[0, d, rows, :] = q_in
            ke_out[0, d, rows, :] = (k * jnp.exp(bend - bcs)).astype(BF16)
            de_out[0, d, c] = jnp.exp(bend)


def _gla_local(pd, pdl, w2bd, b2):
    b, t, w = pd.shape
    nt = t // TM
    row = lambda bb, i: (bb, i, 0)
    return _Part(
        _gla_local_kernel,
        (pd, pdl, w2bd, b2),
        [pl.BlockSpec((1, TM, w), row), pl.BlockSpec((1, TM, LANES), row),
         pl.BlockSpec((LANES, GW), lambda bb, i: (0, 0)), pl.BlockSpec((1, GW), lambda bb, i: (0, 0))],
        [pl.BlockSpec((1, 2, TM, GLA_QK), lambda bb, i: (bb, 0, i, 0)),
         pl.BlockSpec((1, 2, TM, GW), lambda bb, i: (bb, 0, i, 0)),
         pl.BlockSpec((1, 2, TM, GLA_QK), lambda bb, i: (bb, 0, i, 0)),
         pl.BlockSpec((1, CPT, GW, CH), lambda bb, i: (bb, i, 0, 0)),
         pl.BlockSpec((1, TM, GW), row),
         pl.BlockSpec((1, 2, CPT, 1, GLA_QK), lambda bb, i: (bb, 0, i, 0, 0))],
        [jax.ShapeDtypeStruct((b, 2, t, GLA_QK), BF16), jax.ShapeDtypeStruct((b, 2, t, GW), BF16),
         jax.ShapeDtypeStruct((b, 2, t, GLA_QK), BF16), jax.ShapeDtypeStruct((b, t // CH, GW, CH), BF16),
         jax.ShapeDtypeStruct((b, t, GW), BF16), jax.ShapeDtypeStruct((b, 2, t // CH, 1, GLA_QK), F32)],
        [])


def _gla_chunk(refs, o_ref, s_scr, d, cc, bd, bdt):
    qi, a, ke, de, vt, v = refs
    rows = slice(cc * CH, (cc + 1) * CH)
    st = s_scr[d]
    vbd = jnp.where(bd, _tile4(v[0, rows, :]), jnp.zeros((), BF16))
    o_ref[0, rows, :] = _dot_nt(qi[0, 0, rows, :], st.astype(BF16)) + _dot(a[0, 0, rows, :], vbd)
    s_scr[d] = st * de[0, 0, cc] + jnp.where(bdt, _dot(vt[0, cc], ke[0, 0, rows, :]), 0.0)


def _scan_kernel(*refs):
    dn_in = (refs[0:6], refs[6:12])
    gla_in = (refs[12:18], refs[18:24])
    dn_out = refs[24:26]
    gla_out = refs[26:28]
    dn_s, gla_s = refs[28:30]

    @pl.when(pl.program_id(1) == 0)
    def _():
        dn_s[...] = jnp.zeros(dn_s.shape, F32)
        gla_s[...] = jnp.zeros(gla_s.shape, F32)

    bd = (_iota((GW, GW), 0) >> 6) == (_iota((GW, GW), 1) >> 6)
    bdt = (_iota((GW, GLA_QK), 0) >> 6) == (_iota((GW, GLA_QK), 1) >> 5)
    for c in range(CPT):
        for d in range(2):
            cc = c if d == 0 else CPT - 1 - c
            _dn_chunk(dn_in[d], dn_out[d], dn_s, d, cc, bd)
            _gla_chunk(gla_in[d], gla_out[d], gla_s, d, cc, bd, bdt)


def _scans(u, w, dqi, da, ket, ge, qi, a, ke, de, vt, vb):
    b, _, t, _ = u.shape
    nt = t // TM
    rowf, rowr = _dir_specs((GW,), nt, False)
    ketf, ketr = _dir_specs((GW, CH), nt, True)
    gef, ger = _dir_specs((1, GW), nt, True)
    qf, qr = _dir_specs((GLA_QK,), nt, False)
    df, dr = _dir_specs((1, GLA_QK), nt, True)
    vtf = pl.BlockSpec((1, CPT, GW, CH), lambda bb, s: (bb, _fwd_tile(s, nt), 0, 0))
    vtr = pl.BlockSpec((1, CPT, GW, CH), lambda bb, s: (bb, _rev_tile(s, nt), 0, 0))
    vf = pl.BlockSpec((1, TM, GW), lambda bb, s: (bb, _fwd_tile(s, nt), 0))
    vr = pl.BlockSpec((1, TM, GW), lambda bb, s: (bb, _rev_tile(s, nt), 0))
    dn_args = (u, w, dqi, da, ket, ge)
    gla_args = (qi, a, ke, de, vt, vb)
    return pl.pallas_call(
        _scan_kernel,
        grid=(b, nt),
        in_specs=[rowf, rowf, rowf, rowf, ketf, gef, rowr, rowr, rowr, rowr, ketr, ger,
                  qf, rowf, qf, df, vtf, vf, qr, rowr, qr, dr, vtr, vr],
        out_specs=[vf, vr, vf, vr],
        out_shape=[jax.ShapeDtypeStruct((b, t, GW), F32)] * 4,
        scratch_shapes=[pltpu.VMEM((2, GW, GW), F32), pltpu.VMEM((2, GW, GLA_QK), F32)],
        compiler_params=_cp(("parallel", "arbitrary")),
    )(*dn_args, *dn_args, *gla_args, *gla_args)


def _outproj_kernel(x_ref, ya_ref, yb_ref, cf_ref, cr_ref, df_ref, dr_ref, pg_ref, g1b_ref, g1c_ref, gc_ref, gd_ref,
                    w_ref, o_ref, *, lat_rows):
    ones64 = _group_ones(GW, 6)

    def fin(o, g, gate):
        ms = _dot_sel(o * o, ones64) * (1.0 / HEAD_DIM)
        return (o * lax.rsqrt(ms + EPS) * g * _silu(gate)).astype(BF16)

    pg = pg_ref[0]
    yc = fin(cf_ref[0] + cr_ref[0], gc_ref[...], pg[:, :GW])
    yd = fin(df_ref[0] + dr_ref[0], gd_ref[...], pg[:, GW:])
    res = (_dot(ya_ref[0].astype(BF16), w_ref[0:GW, :]) + _dot(yb_ref[0].astype(BF16), w_ref[GW:2 * GW, :])
           + _dot(yc, w_ref[2 * GW:3 * GW, :]) + _dot(yd, w_ref[3 * GW:, :]))
    g1 = jnp.where(_ctx_rows(res.shape[0], lat_rows), g1c_ref[0], g1b_ref[0])
    o_ref[0] = x_ref[0] + g1 * res


def _outproj(xs, ya, yb, ocf, ocr, odf, odr, pg, mod3, dn_g, gla_g, w_out_bf, n_batch, rows, n_blk):
    b, t, d = xs.shape
    reps = GW // HEAD_DIM
    tb = rows // n_blk
    row = lambda bb, i: (bb, i, 0)
    g256 = pl.BlockSpec((1, tb, GW), row)
    in_specs = ([pl.BlockSpec((1, tb, d), row), g256, g256, g256, g256, g256, g256, pl.BlockSpec((1, tb, 2 * GW), row)]
                + _mod_specs(2, n_batch) + [_resident((1, GW)), _resident((1, GW)), _resident((d, d))])
    args = (xs, ya, yb, ocf, ocr, odf, odr, pg, mod3, mod3, jnp.tile(dn_g, reps).reshape(1, GW),
            jnp.tile(gla_g, reps).reshape(1, GW), w_out_bf)
    kern = functools.partial(_outproj_kernel, lat_rows=t - TM)
    return _row_call(kern, n_batch, rows, args, in_specs, [d], n_blk=n_blk)[0]


FF_HALO = 8


def _ffn_kernel(xm_ref, xl_ref, xr_ref, shb_ref, shc_ref, scb_ref, scc_ref, g2b_ref, g2c_ref, ng_ref,
                wu_ref, cw_ref, wd_ref, o_ref, ext_ref, *, lat_rows):
    tb = xm_ref.shape[1]
    i = pl.program_id(1)
    rows_ext = _iota((tb + 2 * FF_HALO, 1), 0)
    left_ok = jnp.logical_and(i > 0, i * tb != lat_rows)
    right_ok = jnp.logical_and(i < pl.num_programs(1) - 1, (i + 1) * tb != lat_rows)
    keep = jnp.logical_and(jnp.logical_or(rows_ext >= FF_HALO, left_ok),
                           jnp.logical_or(rows_ext < FF_HALO + tb, right_ok))
    grow = i * tb - FF_HALO + rows_ext
    ctx_ext = grow >= lat_rows
    x = jnp.concatenate([xl_ref[0], xm_ref[0], xr_ref[0]], axis=0)
    y = x * lax.rsqrt(jnp.mean(x * x, axis=-1, keepdims=True) + EPS) * ng_ref[...]
    h = y * (1.0 + jnp.where(ctx_ext, scc_ref[0], scb_ref[0])) + jnp.where(ctx_ext, shc_ref[0], shb_ref[0])
    h = jnp.where(keep, h, 0.0).astype(BF16)
    inner_boundary = lat_rows % tb != 0
    if inner_boundary:
        row = grow[FF_HALO:FF_HALO + tb]
        m_prev = jnp.broadcast_to(jnp.where(row == lat_rows, 0.0, 1.0), (tb, FF_BLK))
        m_next = jnp.broadcast_to(jnp.where(row == lat_rows - 1, 0.0, 1.0), (tb, FF_BLK))
    acc = jnp.zeros((tb, D_MODEL), F32)
    for j in range(N_FF_BLK):
        def conv(col0, half):
            cols = slice(col0, col0 + FF_BLK)
            ext_ref[half] = _dot(h, wu_ref[:, cols])
            cw = cw_ref[:, cols]
            prev = ext_ref[half, pl.ds(FF_HALO - 1, tb), :]
            nxt = ext_ref[half, pl.ds(FF_HALO + 1, tb), :]
            if inner_boundary:
                prev = m_prev * prev
                nxt = m_next * nxt
            return cw[0:1] * prev + cw[1:2] * ext_ref[half, pl.ds(FF_HALO, tb), :] + cw[2:3] * nxt
        a = conv(j * FF_BLK, 0)
        g = conv(D_FF + j * FF_BLK, 1)
        acc = acc + _dot((_silu(g) * a).astype(BF16), wd_ref[j * FF_BLK:(j + 1) * FF_BLK, :])
    g2 = jnp.where(ctx_ext[FF_HALO:FF_HALO + tb], g2c_ref[0], g2b_ref[0])
    o_ref[0] = xm_ref[0] + g2 * acc


def _ffn(x1, mod3, norm_g, w_up, cw, w_down, n_batch, lat_rows, rows, n_blk):
    b, t, d = x1.shape
    tb = rows // n_blk
    per = tb // FF_HALO
    last = t // FF_HALO - 1
    in_specs = ([pl.BlockSpec((1, tb, d), lambda bb, i: (bb, i, 0)),
                 pl.BlockSpec((1, FF_HALO, d), lambda bb, i: (bb, jnp.maximum(i * per - 1, 0), 0)),
                 pl.BlockSpec((1, FF_HALO, d), lambda bb, i: (bb, jnp.minimum((i + 1) * per, last), 0))]
                + _mod_specs(3, n_batch) + _mod_specs(4, n_batch) + _mod_specs(5, n_batch)
                + [_resident((1, d)), _resident(w_up.shape), _resident(cw.shape), _resident(w_down.shape)])
    args = (x1, x1, x1, mod3, mod3, mod3, mod3, mod3, mod3, norm_g.reshape(1, d), w_up, cw, w_down)
    kern = functools.partial(_ffn_kernel, lat_rows=lat_rows)
    return _row_call(kern, n_batch, rows, args, in_specs, [d],
                     scratch=[pltpu.VMEM((2, tb + 2 * FF_HALO, FF_BLK), F32)], n_blk=n_blk)[0]


def _rope_tables(seq, ctx_len):
    rows = seq // GRID_W
    row = jnp.repeat(jnp.arange(rows, dtype=F32), GRID_W)
    col = jnp.tile(jnp.arange(GRID_W, dtype=F32), rows)
    nf = QK_DIM // 4
    inv = ROPE_THETA ** (-jnp.arange(nf, dtype=F32) / nf)
    ang = jnp.concatenate([row[:, None] * inv, col[:, None] * inv], axis=-1)
    cos = jnp.concatenate([jnp.cos(ang), jnp.ones((ctx_len, QK_DIM // 2), F32)], axis=0)
    sin = jnp.concatenate([jnp.sin(ang), jnp.zeros((ctx_len, QK_DIM // 2), F32)], axis=0)
    reps = GW // QK_DIM
    return (jnp.tile(jnp.concatenate([cos, cos], axis=-1), (1, reps)),
            jnp.tile(jnp.concatenate([-sin, sin], axis=-1), (1, reps)))


def _regroup_w_in(w):
    d = w.shape[0]
    z = lambda n: jnp.zeros((d, n), w.dtype)
    return jnp.concatenate([w[:, :2048], w[:, 2048:2064], z(LANES - 16), w[:, 2320:2832], w[:, 2832:2864],
                            z(LANES - 32), w[:, 2064:2320], w[:, 2864:3120]], axis=1).astype(BF16)


def _gla_w2_blockdiag(w2):
    out = jnp.zeros((LANES, GW), F32)
    out = out.at[0:GLA_RANK, 0:GLA_QK].set(w2[0])
    return out.at[GLA_RANK:2 * GLA_RANK, GLA_QK:].set(w2[1])


def _layer(xs, mod3, lp, cos_t, sin_t, layer_idx, last, n_batch):
    b, t, d = xs.shape
    nt = t // TM
    pa, pb, pc, pcs, pd, pdl, pg = _inproj(xs, mod3, lp["norm1_g"], _regroup_w_in(lp["w_in"]), n_batch)

    lam_init = 0.8 - 0.6 * math.exp(-0.3 * layer_idx)
    (q, kt, v, qn, kn), (ya,), dn_parts, (qi, a, ke, vt, vb, de) = _fused_call(
        [_attn_prep(pb, lp["da_qnorm_g"], lp["da_knorm_g"], cos_t, sin_t),
         _conv_module(pa, lp["cm_conv_w"], lp["cm_conv_b"], lp["cm_ln_g"], lp["cm_ln_b"]),
         _dn_local(pc, pcs, lp["dn_conv_w"], lp["dn_a_log"], lp["dn_dt_bias"]),
         _gla_local(pd, pdl, _gla_w2_blockdiag(lp["gla_w2"]), lp["gla_b2"].reshape(1, GW))],
        (b, nt))
    lat_rows = t - TM
    rows = lat_rows if last else t
    yb = _attention(q, kt, v, qn, kn, lp["da_lambda"], lp["da_subln_g"], lam_init, rows)
    ocf, ocr, odf, odr = _scans(*dn_parts, qi, a, ke, de, vt, vb)

    x1 = _outproj(xs, ya, yb, ocf, ocr, odf, odr, pg, mod3, lp["dn_onorm_g"], lp["gla_onorm_g"],
                  lp["w_out"].astype(BF16), n_batch, rows, N_BLK_FFN)
    return _ffn(x1, mod3, lp["norm2_g"], lp["ffn_w_up"].astype(BF16), lp["ffn_conv_w"],
                lp["ffn_w_down"].astype(BF16), n_batch, lat_rows, rows, N_BLK_FFN)


def kernel(x, c, ctx, c_ctx, w_mod, b_mod, norm1_g, norm2_g, w_in, w_out, cm_conv_w, cm_conv_b, cm_ln_g, cm_ln_b, da_qnorm_g, da_knorm_g, da_lambda, da_subln_g, dn_conv_w, dn_a_log, dn_dt_bias, dn_onorm_g, gla_w2, gla_b2, gla_onorm_g, ffn_w_up, ffn_conv_w, ffn_w_down):
    n_batch, seq, d = x.shape
    ctx_len = ctx.shape[1]
    assert ctx_len == TM and seq % TM == 0 and d == D_MODEL
    assert (seq + ctx_len) % (8 * N_BLK) == 0 and (seq + ctx_len) % (8 * N_BLK_FFN) == 0 and seq % (8 * N_BLK_FFN) == 0
    depth = w_mod.shape[0]
    cos_t, sin_t = _rope_tables(seq, ctx_len)
    xs = jnp.concatenate([x, ctx], axis=1)
    mod_rows = 16
    c_rows = jnp.zeros((mod_rows, d), F32).at[:n_batch].set(c).at[n_batch].set(c_ctx)
    params = dict(w_mod=w_mod, b_mod=b_mod, norm1_g=norm1_g, norm2_g=norm2_g, w_in=w_in, w_out=w_out,
                  cm_conv_w=cm_conv_w, cm_conv_b=cm_conv_b, cm_ln_g=cm_ln_g, cm_ln_b=cm_ln_b,
                  da_qnorm_g=da_qnorm_g, da_knorm_g=da_knorm_g, da_lambda=da_lambda, da_subln_g=da_subln_g,
                  dn_conv_w=dn_conv_w, dn_a_log=dn_a_log, dn_dt_bias=dn_dt_bias, dn_onorm_g=dn_onorm_g,
                  gla_w2=gla_w2, gla_b2=gla_b2, gla_onorm_g=gla_onorm_g,
                  ffn_w_up=ffn_w_up, ffn_conv_w=ffn_conv_w, ffn_w_down=ffn_w_down)
    for l in range(depth):
        lp = {k: v[l] for k, v in params.items()}
        mod3 = _modulation(c_rows, lp["w_mod"], lp["b_mod"]).reshape(mod_rows * 6, 1, d)
        xs = _layer(xs, mod3, lp, cos_t, sin_t, l, l == depth - 1, n_batch)
    return xs
```

```python
import functools
import math
from typing import Any, NamedTuple

import jax
import jax.numpy as jnp
from jax import lax
from jax.experimental import pallas as pl
from jax.experimental.pallas import tpu as pltpu

F32 = jnp.float32
BF16 = jnp.bfloat16
HI = lax.Precision.HIGHEST
EPS = 1e-6

D_MODEL = 1024
GRID_W = 64
HEADS = 4
HEAD_DIM = 64
GW = 256
QK_DIM = 32
GLA_K = 32
GLA_RANK = 16
GLA_TAU = 16.0
CM_KERNEL = 31
DN_CONV = 5
ROPE_THETA = 10000.0
CH = 64
TM = 256
CPT = TM // CH
D_FF = 2816
FF_BLK = 256
N_FF_BLK = D_FF // FF_BLK
LANES = 128

IN_GROUPS = (("pa", 512), ("pb", 768), ("pc", 768), ("pcs", LANES), ("pd", 512), ("pdl", LANES), ("pg", 512))
IN_COLS_PAD = sum(w for _, w in IN_GROUPS)

VMEM_LIMIT = 56 * 1024 * 1024


def _cp(sem):
    return pltpu.CompilerParams(dimension_semantics=sem, vmem_limit_bytes=VMEM_LIMIT)


def _dot(a, b, prec=None):
    return jnp.dot(a, b, preferred_element_type=F32, precision=prec)


def _dot_nt(a, b, prec=None):
    return lax.dot_general(a, b, (((1,), (1,)), ((), ())), preferred_element_type=F32, precision=prec)


def _sigmoid(x):
    return 1.0 / (1.0 + jnp.exp(-x))


def _silu(x):
    return x * _sigmoid(x)


def _softplus(x):
    return jnp.maximum(x, 0.0) + jnp.log(1.0 + jnp.exp(-jnp.abs(x)))


def _iota(shape, dim):
    return lax.broadcasted_iota(jnp.int32, shape, dim)


def _group_ones(n, shift):
    return ((_iota((n, n), 0) >> shift) == (_iota((n, n), 1) >> shift)).astype(BF16)


def _bf16_parts(m, n):
    out = []
    for _ in range(n):
        hi = m.astype(BF16)
        out.append(hi)
        m = m - hi.astype(F32)
    return out


def _dot_sel(x, sel, n_parts=2):
    rows = x.shape[0]
    r = _dot(jnp.concatenate(_bf16_parts(x, n_parts), axis=0), sel)
    out = r[:rows]
    for k in range(1, n_parts):
        out = out + r[k * rows:(k + 1) * rows]
    return out


def _tile4(y):
    return jnp.concatenate([y, y, y, y], axis=0)


def _mod_kernel(c_ref, w_ref, b_ref, o_ref):
    o_ref[...] = _dot(_silu(c_ref[...]), w_ref[...], HI) + b_ref[...]


def _modulation(c_rows, w_mod, b_mod):
    r, d = c_rows.shape
    depth = w_mod.shape[0]
    n = w_mod.shape[2] // d
    return pl.pallas_call(
        _mod_kernel,
        grid=(depth, n),
        in_specs=[pl.BlockSpec((r, d), lambda l, j: (0, 0)),
                  pl.BlockSpec((None, d, d), lambda l, j: (l, 0, j)),
                  pl.BlockSpec((None, 1, d), lambda l, j: (l, 0, j))],
        out_specs=pl.BlockSpec((None, r, d), lambda l, j: (l, 0, j)),
        out_shape=jax.ShapeDtypeStruct((depth, r, n * d), F32),
        compiler_params=_cp(("arbitrary", "arbitrary")),
    )(c_rows, w_mod, b_mod.reshape(depth, 1, -1))


N_BLK = 4
N_BLK_FFN = 8


def _mod_specs(k, n_batch):
    return [pl.BlockSpec((1, 1, D_MODEL), lambda b, i: (b * 6 + k, 0, 0)),
            pl.BlockSpec((1, 1, D_MODEL), lambda b, i: (n_batch * 6 + k, 0, 0))]


def _ctx_rows(tb, lat_rows):
    return pl.program_id(1) * tb + _iota((tb, 1), 0) >= lat_rows


def _resident(shape):
    zeros = (0,) * len(shape)
    return pl.BlockSpec(shape, lambda b, i: zeros, pipeline_mode=pl.Buffered(1))


def _row_call(kern, n_batch, rows, args, in_specs, out_widths, scratch=(), n_blk=N_BLK):
    tb = rows // n_blk
    return pl.pallas_call(
        kern,
        grid=(n_batch, n_blk),
        in_specs=in_specs,
        out_specs=[pl.BlockSpec((1, tb, w), lambda bb, i: (bb, i, 0)) for w in out_widths],
        out_shape=[jax.ShapeDtypeStruct((n_batch, rows, w), F32) for w in out_widths],
        scratch_shapes=list(scratch),
        compiler_params=_cp(("parallel", "arbitrary")),
    )(*args)


def _inproj_kernel(x_ref, shb_ref, shc_ref, scb_ref, scc_ref, g_ref, w_ref, *outs, lat_rows):
    x = x_ref[0]
    ctx = _ctx_rows(x.shape[0], lat_rows)
    y = x * lax.rsqrt(jnp.mean(x * x, axis=-1, keepdims=True) + EPS) * g_ref[...]
    h = y * (1.0 + jnp.where(ctx, scc_ref[0], scb_ref[0])) + jnp.where(ctx, shc_ref[0], shb_ref[0])
    p = _dot(h.astype(BF16), w_ref[...])
    off = 0
    for o_ref, (_, width) in zip(outs, IN_GROUPS):
        o_ref[0] = p[:, off:off + width]
        off += width


def _inproj(xs, mod3, norm_g, w_in_r, n_batch):
    b, t, d = xs.shape
    in_specs = ([pl.BlockSpec((1, t // N_BLK, d), lambda bb, i: (bb, i, 0))] + _mod_specs(0, n_batch)
                + _mod_specs(1, n_batch) + [_resident((1, d)), _resident((d, IN_COLS_PAD))])
    kern = functools.partial(_inproj_kernel, lat_rows=t - TM)
    return _row_call(kern, n_batch, t, (xs, mod3, mod3, mod3, mod3, norm_g.reshape(1, d), w_in_r), in_specs,
                     [w for _, w in IN_GROUPS])


class _Part(NamedTuple):
    body: Any
    args: Any
    in_specs: Any
    out_specs: Any
    out_shapes: Any
    scratch: Any


def _fused_call(parts, grid):
    n_in = [len(p.args) for p in parts]
    n_out = [len(p.out_specs) for p in parts]
    n_scr = [len(p.scratch) for p in parts]

    def kern(*refs):
        ins, outs, scr = refs[:sum(n_in)], refs[sum(n_in):sum(n_in) + sum(n_out)], refs[sum(n_in) + sum(n_out):]
        i = o = s = 0
        for p, ni, no, ns in zip(parts, n_in, n_out, n_scr):
            p.body(*ins[i:i + ni], *outs[o:o + no], *scr[s:s + ns])
            i, o, s = i + ni, o + no, s + ns

    res = pl.pallas_call(
        kern,
        grid=grid,
        in_specs=[sp for p in parts for sp in p.in_specs],
        out_specs=[sp for p in parts for sp in p.out_specs],
        out_shape=[sh for p in parts for sh in p.out_shapes],
        scratch_shapes=[sc for p in parts for sc in p.scratch],
        compiler_params=_cp(("parallel", "arbitrary")),
    )(*[a for p in parts for a in p.args])
    out, o = [], 0
    for no in n_out:
        out.append(res[o:o + no])
        o += no
    return out


def _halo_specs(width, halo, n_tiles):
    per = TM // halo
    left = pl.BlockSpec((1, halo, width), lambda b, i: (b, jnp.maximum(i * per - 1, 0), 0))
    right = pl.BlockSpec((1, halo, width), lambda b, i: (b, jnp.minimum((i + 1) * per, n_tiles * per - 1), 0))
    return left, right


def _halo_ok(i, n_tiles):
    return jnp.logical_and(i >= 1, i < n_tiles - 1), i < n_tiles - 2


CM_HALO = 16


def _convmod_kernel(pm_ref, pl_ref, pr_ref, cw_ref, cb_ref, lg_ref, lb_ref, o_ref, ext_ref):
    left_ok, right_ok = _halo_ok(pl.program_id(1), pl.num_programs(1))

    def glu(p):
        return p[:, :GW] * _sigmoid(p[:, GW:])

    ext_ref[0:CM_HALO] = jnp.where(left_ok, glu(pl_ref[0]), 0.0)
    ext_ref[CM_HALO:CM_HALO + TM] = glu(pm_ref[0])
    ext_ref[CM_HALO + TM:] = jnp.where(right_ok, glu(pr_ref[0]), 0.0)
    pad = CM_KERNEL // 2
    sub = 8
    y = cb_ref[...]
    for r in range(sub):
        acc = None
        for j in range(CM_KERNEL):
            off = CM_HALO - pad + j
            if off % sub == r:
                term = cw_ref[j:j + 1, :] * ext_ref[pl.ds(off - r, TM + sub), :]
                acc = term if acc is None else acc + term
        if acc is not None:
            y = y + acc[r:r + TM]
    mu = jnp.mean(y, axis=-1, keepdims=True)
    yc = y - mu
    var = jnp.mean(yc * yc, axis=-1, keepdims=True)
    o_ref[0] = _silu(yc * lax.rsqrt(var + EPS) * lg_ref[...] + lb_ref[...])


def _conv_module(pa, conv_w, conv_b, ln_g, ln_b):
    b, t, w = pa.shape
    nt = t // TM
    left, right = _halo_specs(w, CM_HALO, nt)
    vec = pl.BlockSpec((1, GW), lambda bb, i: (0, 0))
    return _Part(
        _convmod_kernel,
        (pa, pa, pa, conv_w, conv_b.reshape(1, GW), ln_g.reshape(1, GW), ln_b.reshape(1, GW)),
        [pl.BlockSpec((1, TM, w), lambda bb, i: (bb, i, 0)), left, right,
         pl.BlockSpec((CM_KERNEL, GW), lambda bb, i: (0, 0)), vec, vec, vec],
        [pl.BlockSpec((1, TM, GW), lambda bb, i: (bb, i, 0))],
        [jax.ShapeDtypeStruct((b, t, GW), F32)],
        [pltpu.VMEM((TM + 2 * CM_HALO, GW), F32)])


N_MAPS = 2 * HEADS
QK_PAD = 2 * QK_DIM
V_EXT = 2 * HEAD_DIM
MASK_BIG = 8192.0
SHIFT_MAX = 40.0


def _attn_prep_kernel(pb_ref, qg_ref, kg_ref, cos_ref, sin_ref, q_out, kt_out, v_out, qn_out, kn_out):
    p = pb_ref[0]
    ones32 = _group_ones(GW, 5)
    first = (_iota((TM, GW), 1) & (QK_DIM - 1)) < QK_DIM // 2
    cos = cos_ref[...]
    sin = sin_ref[...]

    def norm_rope(t, g):
        ms = _dot_sel(t * t, ones32) * (1.0 / QK_DIM)
        tn = t * lax.rsqrt(ms + EPS) * g
        partner = jnp.where(first, pltpu.roll(tn, GW - QK_DIM // 2, 1), pltpu.roll(tn, QK_DIM // 2, 1))
        return tn * cos + partner * sin

    qf = norm_rope(p[:, :GW], qg_ref[...]) * (QK_DIM ** -0.5)
    kf = norm_rope(p[:, GW:2 * GW], kg_ref[...])
    map_sel = ((_iota((GW, N_MAPS), 0) >> 5) == _iota((GW, N_MAPS), 1)).astype(BF16)
    qn_out[0] = _dot_sel(qf * qf, map_sel)
    kn_out[0] = _dot_sel(kf * kf, map_sel)
    q = qf.astype(BF16)
    kt = kf.T.astype(BF16)
    is_ctx = pl.program_id(1) == pl.num_programs(1) - 1
    k_row = _iota((QK_DIM, TM), 0)
    k_tail = jnp.where(k_row == 0, 1.0, jnp.where(jnp.logical_and(k_row == 1, jnp.logical_not(is_ctx)), 1.0, 0.0))
    k_tail = k_tail.astype(BF16)
    q_tail = jnp.where(jnp.logical_and(_iota((TM, QK_DIM), 1) == 1, is_ctx), -MASK_BIG, 0.0).astype(BF16)
    for g in range(N_MAPS):
        q_out[0, g, :, 0:QK_DIM] = q[:, g * QK_DIM:(g + 1) * QK_DIM]
        q_out[0, g, :, QK_DIM:] = q_tail
        kt_out[0, 0, g, 0:QK_DIM, :] = kt[g * QK_DIM:(g + 1) * QK_DIM, :]
        kt_out[0, 0, g, QK_DIM:, :] = k_tail
    v = p[:, 2 * GW:].astype(BF16)
    ones = jnp.ones((TM, HEAD_DIM), BF16)
    for h in range(HEADS):
        v_out[0, 0, h, :, 0:HEAD_DIM] = v[:, h * HEAD_DIM:(h + 1) * HEAD_DIM]
        v_out[0, 0, h, :, HEAD_DIM:] = ones


def _attn_prep(pb, qn_g, kn_g, cos_t, sin_t):
    b, t, w = pb.shape
    nt = t // TM
    vec = pl.BlockSpec((1, GW), lambda bb, i: (0, 0))
    tab = pl.BlockSpec((TM, GW), lambda bb, i: (i, 0))
    reps = GW // QK_DIM
    n_spec = pl.BlockSpec((1, TM, N_MAPS), lambda bb, i: (bb, i, 0))
    n_shape = jax.ShapeDtypeStruct((b, t, N_MAPS), F32)
    return _Part(
        _attn_prep_kernel,
        (pb, jnp.tile(qn_g, reps).reshape(1, GW), jnp.tile(kn_g, reps).reshape(1, GW), cos_t, sin_t),
        [pl.BlockSpec((1, TM, w), lambda bb, i: (bb, i, 0)), vec, vec, tab, tab],
        [pl.BlockSpec((1, N_MAPS, TM, QK_PAD), lambda bb, i: (bb, 0, i, 0)),
         pl.BlockSpec((1, 1, N_MAPS, QK_PAD, TM), lambda bb, i: (bb, i, 0, 0, 0)),
         pl.BlockSpec((1, 1, HEADS, TM, V_EXT), lambda bb, i: (bb, i, 0, 0, 0)), n_spec, n_spec],
        [jax.ShapeDtypeStruct((b, N_MAPS, t, QK_PAD), BF16),
         jax.ShapeDtypeStruct((b, nt, N_MAPS, QK_PAD, TM), BF16),
         jax.ShapeDtypeStruct((b, nt, HEADS, TM, V_EXT), BF16), n_shape, n_shape],
        [])


def _attn_kernel(q_ref, kt_ref, v_ref, qn_ref, kn_ref, lam_ref, sg_ref, o_ref, qa_scr, acc_scr, s_scr, m_scr,
                 *, n_chunks, lam_init):
    tq = o_ref.shape[1]
    acc_scr[...] = jnp.zeros(acc_scr.shape, F32)
    k2 = jnp.max(kn_ref[0], axis=0, keepdims=True)
    q2 = jnp.max(qn_ref[0], axis=0, keepdims=True)
    bound = jnp.sqrt(q2 * k2)
    safe = jnp.max(bound) <= SHIFT_MAX
    shift = jnp.where(safe, bound, 0.0)
    shift_lane = _iota((tq, QK_PAD), 1) == QK_DIM
    for g in range(N_MAPS):
        qa_scr[g] = jnp.where(shift_lane, (-shift[:, g:g + 1]).astype(BF16), q_ref[0, g])

    @pl.when(safe)
    def _():
        s_scr[0] = _dot(qa_scr[0], kt_ref[0, 0, 0])

        def body(c, carry):
            nxt = jnp.minimum(c + 1, n_chunks - 1)
            for g in range(N_MAPS):
                if g + 1 < N_MAPS:
                    s_next = _dot(qa_scr[g + 1], kt_ref[0, c, g + 1])
                else:
                    s_next = _dot(qa_scr[0], kt_ref[0, nxt, 0])
                p = jnp.exp(s_scr[g & 1]).astype(BF16)
                acc_scr[g] += _dot(p, v_ref[0, c, g // 2])
                s_scr[(g + 1) & 1] = s_next
            return carry

        lax.fori_loop(0, n_chunks, body, 0)

    @pl.when(jnp.logical_not(safe))
    def _():
        m_scr[...] = jnp.full(m_scr.shape, -jnp.inf, F32)

        def body(c, carry):
            for g in range(N_MAPS):
                s = _dot(qa_scr[g], kt_ref[0, c, g])
                m_old = m_scr[:, g:g + 1]
                m_new = jnp.maximum(m_old, jnp.max(s, axis=-1, keepdims=True))
                p = jnp.exp(s - m_new).astype(BF16)
                acc_scr[g] = jnp.exp(m_old - m_new) * acc_scr[g] + _dot(p, v_ref[0, c, g // 2])
                m_scr[:, g:g + 1] = m_new
            return carry

        lax.fori_loop(0, n_chunks, body, 0)

    lp = lam_ref[...]
    lam = (jnp.exp(jnp.sum(lp[0:1] * lp[1:2], axis=-1, keepdims=True))
           - jnp.exp(jnp.sum(lp[2:3] * lp[3:4], axis=-1, keepdims=True)) + lam_init)
    for h in range(HEADS):
        a0 = acc_scr[2 * h]
        a1 = acc_scr[2 * h + 1]
        o = (a0[:, :HEAD_DIM] / a0[:, HEAD_DIM:HEAD_DIM + 1]
             - lam * (a1[:, :HEAD_DIM] / a1[:, HEAD_DIM:HEAD_DIM + 1]))
        y = o * lax.rsqrt(jnp.mean(o * o, axis=-1, keepdims=True) + EPS) * sg_ref[...] * (1.0 - lam_init)
        o_ref[0, :, h * HEAD_DIM:(h + 1) * HEAD_DIM] = y


def _attention(q, kt, v, qn, kn, lam_p, subln_g, lam_init, rows):
    b, _, t, _ = q.shape
    tq = rows // N_BLK
    nt = t // TM
    kern = functools.partial(_attn_kernel, n_chunks=nt, lam_init=lam_init)
    return pl.pallas_call(
        kern,
        grid=(b, N_BLK),
        in_specs=[pl.BlockSpec((1, N_MAPS, tq, QK_PAD), lambda bb, i: (bb, 0, i, 0)),
                  pl.BlockSpec((1, nt, N_MAPS, QK_PAD, TM), lambda bb, i: (bb, 0, 0, 0, 0)),
                  pl.BlockSpec((1, nt, HEADS, TM, V_EXT), lambda bb, i: (bb, 0, 0, 0, 0)),
                  pl.BlockSpec((1, tq, N_MAPS), lambda bb, i: (bb, i, 0)),
                  pl.BlockSpec((1, t, N_MAPS), lambda bb, i: (bb, 0, 0)),
                  _resident((4, QK_DIM)), _resident((1, HEAD_DIM))],
        out_specs=pl.BlockSpec((1, tq, GW), lambda bb, i: (bb, i, 0)),
        out_shape=jax.ShapeDtypeStruct((b, rows, GW), F32),
        scratch_shapes=[pltpu.VMEM((N_MAPS, tq, QK_PAD), BF16), pltpu.VMEM((N_MAPS, tq, V_EXT), F32),
                        pltpu.VMEM((2, tq, TM), F32), pltpu.VMEM((tq, N_MAPS), F32)],
        compiler_params=_cp(("parallel", "arbitrary")),
    )(q, kt, v, qn, kn, lam_p, subln_g.reshape(1, HEAD_DIM))


def _bd(y, bd):
    return jnp.where(bd, _tile4(y), 0.0).astype(BF16)


def _fwd_tile(s, n_tiles):
    return jnp.where(s == 0, n_tiles - 1, s - 1)


def _rev_tile(s, n_tiles):
    return jnp.where(s == 0, n_tiles - 1, n_tiles - 1 - s)


DN_HALO = 8


def _dn_prep(pm_ref, pl_ref, pr_ref, pcs_ref, cw_ref, alog_ref, dtb_ref, ext_ref):
    left_ok, right_ok = _halo_ok(pl.program_id(1), pl.num_programs(1))
    ext_ref[0:DN_HALO] = jnp.where(left_ok, pl_ref[0], 0.0)
    ext_ref[DN_HALO:DN_HALO + TM] = pm_ref[0]
    ext_ref[DN_HALO + TM:] = jnp.where(right_ok, pr_ref[0], 0.0)
    pad = DN_CONV // 2
    acc = jnp.zeros((TM, 3 * GW), F32)
    for j in range(DN_CONV):
        acc = acc + cw_ref[j:j + 1, :] * ext_ref[pl.ds(DN_HALO - pad + j, TM), :]
    qkv = _silu(acc)
    ones64 = _group_ones(GW, 6)

    def l2n(t):
        return t * lax.rsqrt(_dot_sel(t * t, ones64) + EPS)

    s = pcs_ref[0]
    col = _iota(s.shape, 1)
    gate = -jnp.exp(alog_ref[...]) * _softplus(s + dtb_ref[...])
    bg = jnp.where(col < 2 * HEADS, _sigmoid(s), jnp.where(col < 4 * HEADS, gate, 0.0))
    return l2n(qkv[:, :GW]) * (HEAD_DIM ** -0.5), l2n(qkv[:, GW:2 * GW]), qkv[:, 2 * GW:], bg


def _dn_local_kernel(pm_ref, pl_ref, pr_ref, pcs_ref, cw_ref, alog_ref, dtb_ref,
                     u_out, w_out, qi_out, a_out, ket_out, ge_out, ext_ref):
    rr = _iota((TM, GW), 0)
    cc = _iota((TM, GW), 1)
    i_in = rr & (CH - 1)
    j_in = cc & (CH - 1)
    bd = (rr >> 6) == (cc >> 6)
    eye_t = (i_in == j_in).astype(F32)
    incl = (j_in <= i_in, j_in >= i_in)
    strict = (j_in < i_in, j_in > i_in)
    eye_bf = (rr == cc).astype(BF16)
    head_of_lane = _iota((LANES, GW), 1) >> 6
    src = _iota((LANES, GW), 0)
    q, k, v, bg = _dn_prep(pm_ref, pl_ref, pr_ref, pcs_ref, cw_ref, alog_ref, dtb_ref, ext_ref)
    ones_bf = bd.astype(BF16)
    beta, gcum, decay, kb, egc = [], [], [], [], []
    for d in range(2):
        sel_b = (src == d * HEADS + head_of_lane).astype(BF16)
        sel_g = (src == 2 * HEADS + d * HEADS + head_of_lane).astype(BF16)
        cum_bd = jnp.where(jnp.logical_and(bd, incl[d]), 1.0, 0.0).astype(BF16)
        g_exp = _dot(jnp.concatenate(_bf16_parts(bg, 2), axis=0), sel_g)
        beta.append(_dot(bg.astype(BF16), sel_b))
        cs = _dot(cum_bd, jnp.concatenate([g_exp[:TM].astype(BF16), g_exp[TM:].astype(BF16)], axis=1))
        gcum.append(cs[:, :GW] + cs[:, GW:])
        gparts = _bf16_parts(gcum[d] * eye_t, 3)
        tr = _dot(ones_bf, jnp.concatenate(gparts, axis=1))
        grow = tr[:, :GW] + tr[:, GW:2 * GW] + tr[:, 2 * GW:]
        decay.append(jnp.where(incl[d], jnp.exp(jnp.where(incl[d], gcum[d] - grow, 0.0)), 0.0))
        kb.append(k * beta[d])
        egc.append(jnp.exp(gcum[d]))
        qi_out[0, d] = (q * egc[d]).astype(BF16)
    pairs = [(c, d) for c in range(CPT) for d in range(2)]
    rows = [slice(c * CH, (c + 1) * CH) for c in range(CPT)]
    eye = eye_t[:CH]
    a = {}
    for c in range(CPT):
        r = rows[c]
        lhs = jnp.concatenate([kb[0][r], kb[1][r], q[r]], axis=0).astype(BF16)
        aq = _dot_nt(lhs, _bd(k[r], bd))
        for d in range(2):
            dec = decay[d][r]
            a[c, d] = jnp.where(strict[d][:CH], aq[d * CH:(d + 1) * CH] * dec, 0.0)
            a_out[0, d, r, :] = jnp.where(incl[d][:CH], aq[2 * CH:] * dec, 0.0).astype(BF16)
    t_inv = {cd: eye - a[cd] for cd in pairs}
    p = {cd: _dot(a[cd].astype(BF16), _bd(a[cd], bd)) for cd in pairs}
    for it in range(5):
        for cd in pairs:
            pbd = _bd(p[cd], bd)
            if it < 4:
                res = _dot(jnp.concatenate([t_inv[cd], p[cd]], axis=0).astype(BF16), pbd)
                t_inv[cd] = t_inv[cd] + res[:CH]
                p[cd] = res[CH:]
            else:
                t_inv[cd] = t_inv[cd] + _dot(t_inv[cd].astype(BF16), pbd)

    def split(m):
        hi = m.astype(BF16)
        return hi, (m - hi.astype(F32)).astype(BF16)

    for cd in pairs:
        x0 = t_inv[cd]
        ah, al = split(a[cd])
        xh, xl = split(x0)
        hx = _dot(jnp.concatenate([ah, al], axis=0), _bd(xh, bd))
        resid = eye - x0 - (hx[:CH] + hx[CH:] + _dot(ah, _bd(xl, bd)))
        t_inv[cd] = x0 + _dot(xh, _bd(resid, bd))
    for c, d in pairs:
        r = rows[c]
        tb = t_inv[c, d].astype(BF16)
        u_out[0, d, r, :] = _dot(tb, _bd(v[r] * beta[d][r], bd))
        w_out[0, d, r, :] = _dot(tb, _bd(kb[d][r] * egc[d][r], bd)).astype(BF16)
        last = (c + 1) * CH - 1 if d == 0 else c * CH
        gtot = gcum[d][last:last + 1]
        k_end = (k[r] * jnp.exp(gtot - gcum[d][r])).astype(BF16)
        ket_out[0, d, c] = _dot_nt(eye_bf, k_end).astype(BF16)
        ge_out[0, d, c] = jnp.exp(gtot)


def _dn_local(pc, pcs, conv_w, a_log, dt_bias):
    b, t, w = pc.shape
    nt = t // TM
    left, right = _halo_specs(w, DN_HALO, nt)
    row = lambda bb, i: (bb, i, 0)
    pad_vec = lambda a: jnp.zeros((1, LANES), F32).at[0, 2 * HEADS:4 * HEADS].set(a.reshape(-1))
    vec = pl.BlockSpec((1, LANES), lambda bb, i: (0, 0))
    drow = pl.BlockSpec((1, 2, TM, GW), lambda bb, i: (bb, 0, i, 0))
    return _Part(
        _dn_local_kernel,
        (pc, pc, pc, pcs, conv_w, pad_vec(a_log), pad_vec(dt_bias)),
        [pl.BlockSpec((1, TM, w), row), left, right, pl.BlockSpec((1, TM, LANES), row),
         pl.BlockSpec((DN_CONV, w), lambda bb, i: (0, 0)), vec, vec],
        [drow, drow, drow, drow,
         pl.BlockSpec((1, 2, CPT, GW, CH), lambda bb, i: (bb, 0, i, 0, 0)),
         pl.BlockSpec((1, 2, CPT, 1, GW), lambda bb, i: (bb, 0, i, 0, 0))],
        [jax.ShapeDtypeStruct((b, 2, t, GW), F32)] + [jax.ShapeDtypeStruct((b, 2, t, GW), BF16)] * 3
        + [jax.ShapeDtypeStruct((b, 2, t // CH, GW, CH), BF16), jax.ShapeDtypeStruct((b, 2, t // CH, 1, GW), F32)],
        [pltpu.VMEM((TM + 2 * DN_HALO, w), F32)])


def _dn_chunk(refs, o_ref, s_scr, d, cc, bd):
    u, w, qi, a, ket, ge = refs
    rows = slice(cc * CH, (cc + 1) * CH)
    s = s_scr[d]
    wq = _dot(jnp.concatenate([w[0, 0, rows, :], qi[0, 0, rows, :]], axis=0), s.astype(BF16))
    v_new = u[0, 0, rows, :] - wq[:CH]
    o_ref[0, rows, :] = wq[CH:] + _dot(a[0, 0, rows, :], _bd(v_new, bd))
    s_scr[d] = s * ge[0, 0, cc] + jnp.where(bd, _dot(ket[0, 0, cc], v_new.astype(BF16)), 0.0)


def _dir_specs(shape_tail, n_tiles, chunked):
    blk = (1, 1, CPT if chunked else TM) + shape_tail
    zeros = (0,) * len(shape_tail)
    fwd = pl.BlockSpec(blk, lambda b, s: (b, 0, _fwd_tile(s, n_tiles)) + zeros)
    rev = pl.BlockSpec(blk, lambda b, s: (b, 1, _rev_tile(s, n_tiles)) + zeros)
    return fwd, rev


GLA_QK = HEADS * GLA_K


def _gla_local_kernel(pd_ref, pdl_ref, w2_ref, b2_ref, qi_out, a_out, ke_out, vt_out, vb_out, de_out):
    j_in = _iota((CH, GW), 1) & (CH - 1)
    i_in = _iota((CH, GW), 0)
    incl = (j_in <= i_in, j_in >= i_in)
    rr = _iota((TM, TM), 0)
    cc = _iota((TM, TM), 1)
    same_chunk = (rr >> 6) == (cc >> 6)
    eye_bf = (rr == cc).astype(BF16)
    bdk = (_iota((GW, GLA_QK), 0) >> 6) == (_iota((GW, GLA_QK), 1) >> 5)
    z = _dot(pdl_ref[0], w2_ref[...], HI) + b2_ref[...]
    gk_all = -_softplus(-z) * (1.0 / GLA_TAU)
    bcs_all = []
    for d in range(2):
        cum_bd = jnp.where(jnp.logical_and(same_chunk, cc <= rr if d == 0 else cc >= rr), 1.0, 0.0).astype(BF16)
        gparts = _bf16_parts(gk_all[:, d * GLA_QK:(d + 1) * GLA_QK], 2)
        cs = _dot(cum_bd, jnp.concatenate(gparts, axis=1))
        bcs_all.append(cs[:, :GLA_QK] + cs[:, GLA_QK:])
    for c in range(CPT):
        rows = slice(c * CH, (c + 1) * CH)
        p = pd_ref[0, rows, :]
        q = p[:, :GLA_QK] * (GLA_K ** -0.5)
        k = p[:, GLA_QK:2 * GLA_QK]
        vb = p[:, 2 * GLA_QK:].astype(BF16)
        vb_out[0, rows, :] = vb
        vt_out[0, c] = _dot_nt(eye_bf, vb).astype(BF16)
        for d in range(2):
            bcs = bcs_all[d][rows]
            bend = bcs[CH - 1:CH] if d == 0 else bcs[0:1]
            q_in = (q * jnp.exp(bcs)).astype(BF16)
            kdec = jnp.where(bdk, _tile4(k * jnp.exp(-bcs)), 0.0).astype(BF16)
            a_out[0, d, rows, :] = jnp.where(incl[d], _dot_nt(q_in, kdec), 0.0).astype(BF16)
            qi_out[0, d, rows, :] = q_in
            ke_out[0, d, rows, :] = (k * jnp.exp(bend - bcs)).astype(BF16)
            de_out[0, d, c] = jnp.exp(bend)


def _gla_local(pd, pdl, w2bd, b2):
    b, t, w = pd.shape
    nt = t // TM
    row = lambda bb, i: (bb, i, 0)
    return _Part(
        _gla_local_kernel,
        (pd, pdl, w2bd, b2),
        [pl.BlockSpec((1, TM, w), row), pl.BlockSpec((1, TM, LANES), row),
         pl.BlockSpec((LANES, GW), lambda bb, i: (0, 0)), pl.BlockSpec((1, GW), lambda bb, i: (0, 0))],
        [pl.BlockSpec((1, 2, TM, GLA_QK), lambda bb, i: (bb, 0, i, 0)),
         pl.BlockSpec((1, 2, TM, GW), lambda bb, i: (bb, 0, i, 0)),
         pl.BlockSpec((1, 2, TM, GLA_QK), lambda bb, i: (bb, 0, i, 0)),
         pl.BlockSpec((1, CPT, GW, CH), lambda bb, i: (bb, i, 0, 0)),
         pl.BlockSpec((1, TM, GW), row),
         pl.BlockSpec((1, 2, CPT, 1, GLA_QK), lambda bb, i: (bb, 0, i, 0, 0))],
        [jax.ShapeDtypeStruct((b, 2, t, GLA_QK), BF16), jax.ShapeDtypeStruct((b, 2, t, GW), BF16),
         jax.ShapeDtypeStruct((b, 2, t, GLA_QK), BF16), jax.ShapeDtypeStruct((b, t // CH, GW, CH), BF16),
         jax.ShapeDtypeStruct((b, t, GW), BF16), jax.ShapeDtypeStruct((b, 2, t // CH, 1, GLA_QK), F32)],
        [])


def _gla_chunk(refs, o_ref, s_scr, d, cc, bd, bdt):
    qi, a, ke, de, vt, v = refs
    rows = slice(cc * CH, (cc + 1) * CH)
    st = s_scr[d]
    vbd = jnp.where(bd, _tile4(v[0, rows, :]), jnp.zeros((), BF16))
    o_ref[0, rows, :] = _dot_nt(qi[0, 0, rows, :], st.astype(BF16)) + _dot(a[0, 0, rows, :], vbd)
    s_scr[d] = st * de[0, 0, cc] + jnp.where(bdt, _dot(vt[0, cc], ke[0, 0, rows, :]), 0.0)


def _scan_kernel(*refs):
    dn_in = (refs[0:6], refs[6:12])
    gla_in = (refs[12:18], refs[18:24])
    dn_out = refs[24:26]
    gla_out = refs[26:28]
    dn_s, gla_s = refs[28:30]

    @pl.when(pl.program_id(1) == 0)
    def _():
        dn_s[...] = jnp.zeros(dn_s.shape, F32)
        gla_s[...] = jnp.zeros(gla_s.shape, F32)

    bd = (_iota((GW, GW), 0) >> 6) == (_iota((GW, GW), 1) >> 6)
    bdt = (_iota((GW, GLA_QK), 0) >> 6) == (_iota((GW, GLA_QK), 1) >> 5)
    for c in range(CPT):
        for d in range(2):
            cc = c if d == 0 else CPT - 1 - c
            _dn_chunk(dn_in[d], dn_out[d], dn_s, d, cc, bd)
            _gla_chunk(gla_in[d], gla_out[d], gla_s, d, cc, bd, bdt)


def _scans(u, w, dqi, da, ket, ge, qi, a, ke, de, vt, vb):
    b, _, t, _ = u.shape
    nt = t // TM
    rowf, rowr = _dir_specs((GW,), nt, False)
    ketf, ketr = _dir_specs((GW, CH), nt, True)
    gef, ger = _dir_specs((1, GW), nt, True)
    qf, qr = _dir_specs((GLA_QK,), nt, False)
    df, dr = _dir_specs((1, GLA_QK), nt, True)
    vtf = pl.BlockSpec((1, CPT, GW, CH), lambda bb, s: (bb, _fwd_tile(s, nt), 0, 0))
    vtr = pl.BlockSpec((1, CPT, GW, CH), lambda bb, s: (bb, _rev_tile(s, nt), 0, 0))
    vf = pl.BlockSpec((1, TM, GW), lambda bb, s: (bb, _fwd_tile(s, nt), 0))
    vr = pl.BlockSpec((1, TM, GW), lambda bb, s: (bb, _rev_tile(s, nt), 0))
    dn_args = (u, w, dqi, da, ket, ge)
    gla_args = (qi, a, ke, de, vt, vb)
    return pl.pallas_call(
        _scan_kernel,
        grid=(b, nt),
        in_specs=[rowf, rowf, rowf, rowf, ketf, gef, rowr, rowr, rowr, rowr, ketr, ger,
                  qf, rowf, qf, df, vtf, vf, qr, rowr, qr, dr, vtr, vr],
        out_specs=[vf, vr, vf, vr],
        out_shape=[jax.ShapeDtypeStruct((b, t, GW), F32)] * 4,
        scratch_shapes=[pltpu.VMEM((2, GW, GW), F32), pltpu.VMEM((2, GW, GLA_QK), F32)],
        compiler_params=_cp(("parallel", "arbitrary")),
    )(*dn_args, *dn_args, *gla_args, *gla_args)


def _outproj_kernel(x_ref, ya_ref, yb_ref, cf_ref, cr_ref, df_ref, dr_ref, pg_ref, g1b_ref, g1c_ref, gc_ref, gd_ref,
                    w_ref, o_ref, *, lat_rows):
    ones64 = _group_ones(GW, 6)

    def fin(o, g, gate):
        ms = _dot_sel(o * o, ones64) * (1.0 / HEAD_DIM)
        return (o * lax.rsqrt(ms + EPS) * g * _silu(gate)).astype(BF16)

    pg = pg_ref[0]
    yc = fin(cf_ref[0] + cr_ref[0], gc_ref[...], pg[:, :GW])
    yd = fin(df_ref[0] + dr_ref[0], gd_ref[...], pg[:, GW:])
    res = (_dot(ya_ref[0].astype(BF16), w_ref[0:GW, :]) + _dot(yb_ref[0].astype(BF16), w_ref[GW:2 * GW, :])
           + _dot(yc, w_ref[2 * GW:3 * GW, :]) + _dot(yd, w_ref[3 * GW:, :]))
    g1 = jnp.where(_ctx_rows(res.shape[0], lat_rows), g1c_ref[0], g1b_ref[0])
    o_ref[0] = x_ref[0] + g1 * res


def _outproj(xs, ya, yb, ocf, ocr, odf, odr, pg, mod3, dn_g, gla_g, w_out_bf, n_batch, rows, n_blk):
    b, t, d = xs.shape
    reps = GW // HEAD_DIM
    tb = rows // n_blk
    row = lambda bb, i: (bb, i, 0)
    g256 = pl.BlockSpec((1, tb, GW), row)
    in_specs = ([pl.BlockSpec((1, tb, d), row), g256, g256, g256, g256, g256, g256, pl.BlockSpec((1, tb, 2 * GW), row)]
                + _mod_specs(2, n_batch) + [_resident((1, GW)), _resident((1, GW)), _resident((d, d))])
    args = (xs, ya, yb, ocf, ocr, odf, odr, pg, mod3, mod3, jnp.tile(dn_g, reps).reshape(1, GW),
            jnp.tile(gla_g, reps).reshape(1, GW), w_out_bf)
    kern = functools.partial(_outproj_kernel, lat_rows=t - TM)
    return _row_call(kern, n_batch, rows, args, in_specs, [d], n_blk=n_blk)[0]


FF_HALO = 8


def _ffn_kernel(xm_ref, xl_ref, xr_ref, shb_ref, shc_ref, scb_ref, scc_ref, g2b_ref, g2c_ref, ng_ref,
                wu_ref, cw_ref, wd_ref, o_ref, ext_ref, *, lat_rows):
    tb = xm_ref.shape[1]
    i = pl.program_id(1)
    rows_ext = _iota((tb + 2 * FF_HALO, 1), 0)
    left_ok = jnp.logical_and(i > 0, i * tb != lat_rows)
    right_ok = jnp.logical_and(i < pl.num_programs(1) - 1, (i + 1) * tb != lat_rows)
    keep = jnp.logical_and(jnp.logical_or(rows_ext >= FF_HALO, left_ok),
                           jnp.logical_or(rows_ext < FF_HALO + tb, right_ok))
    grow = i * tb - FF_HALO + rows_ext
    ctx_ext = grow >= lat_rows
    x = jnp.concatenate([xl_ref[0], xm_ref[0], xr_ref[0]], axis=0)
    y = x * lax.rsqrt(jnp.mean(x * x, axis=-1, keepdims=True) + EPS) * ng_ref[...]
    h = y * (1.0 + jnp.where(ctx_ext, scc_ref[0], scb_ref[0])) + jnp.where(ctx_ext, shc_ref[0], shb_ref[0])
    h = jnp.where(keep, h, 0.0).astype(BF16)
    inner_boundary = lat_rows % tb != 0
    if inner_boundary:
        row = grow[FF_HALO:FF_HALO + tb]
        m_prev = jnp.broadcast_to(jnp.where(row == lat_rows, 0.0, 1.0), (tb, FF_BLK))
        m_next = jnp.broadcast_to(jnp.where(row == lat_rows - 1, 0.0, 1.0), (tb, FF_BLK))
    acc = jnp.zeros((tb, D_MODEL), F32)
    for j in range(N_FF_BLK):
        def conv(col0, half):
            cols = slice(col0, col0 + FF_BLK)
            ext_ref[half] = _dot(h, wu_ref[:, cols])
            cw = cw_ref[:, cols]
            prev = ext_ref[half, pl.ds(FF_HALO - 1, tb), :]
            nxt = ext_ref[half, pl.ds(FF_HALO + 1, tb), :]
            if inner_boundary:
                prev = m_prev * prev
                nxt = m_next * nxt
            return cw[0:1] * prev + cw[1:2] * ext_ref[half, pl.ds(FF_HALO, tb), :] + cw[2:3] * nxt
        a = conv(j * FF_BLK, 0)
        g = conv(D_FF + j * FF_BLK, 1)
        acc = acc + _dot((_silu(g) * a).astype(BF16), wd_ref[j * FF_BLK:(j + 1) * FF_BLK, :])
    g2 = jnp.where(ctx_ext[FF_HALO:FF_HALO + tb], g2c_ref[0], g2b_ref[0])
    o_ref[0] = xm_ref[0] + g2 * acc


def _ffn(x1, mod3, norm_g, w_up, cw, w_down, n_batch, lat_rows, rows, n_blk):
    b, t, d = x1.shape
    tb = rows // n_blk
    per = tb // FF_HALO
    last = t // FF_HALO - 1
    in_specs = ([pl.BlockSpec((1, tb, d), lambda bb, i: (bb, i, 0)),
                 pl.BlockSpec((1, FF_HALO, d), lambda bb, i: (bb, jnp.maximum(i * per - 1, 0), 0)),
                 pl.BlockSpec((1, FF_HALO, d), lambda bb, i: (bb, jnp.minimum((i + 1) * per, last), 0))]
                + _mod_specs(3, n_batch) + _mod_specs(4, n_batch) + _mod_specs(5, n_batch)
                + [_resident((1, d)), _resident(w_up.shape), _resident(cw.shape), _resident(w_down.shape)])
    args = (x1, x1, x1, mod3, mod3, mod3, mod3, mod3, mod3, norm_g.reshape(1, d), w_up, cw, w_down)
    kern = functools.partial(_ffn_kernel, lat_rows=lat_rows)
    return _row_call(kern, n_batch, rows, args, in_specs, [d],
                     scratch=[pltpu.VMEM((2, tb + 2 * FF_HALO, FF_BLK), F32)], n_blk=n_blk)[0]


def _rope_tables(seq, ctx_len):
    rows = seq // GRID_W
    row = jnp.repeat(jnp.arange(rows, dtype=F32), GRID_W)
    col = jnp.tile(jnp.arange(GRID_W, dtype=F32), rows)
    nf = QK_DIM // 4
    inv = ROPE_THETA ** (-jnp.arange(nf, dtype=F32) / nf)
    ang = jnp.concatenate([row[:, None] * inv, col[:, None] * inv], axis=-1)
    cos = jnp.concatenate([jnp.cos(ang), jnp.ones((ctx_len, QK_DIM // 2), F32)], axis=0)
    sin = jnp.concatenate([jnp.sin(ang), jnp.zeros((ctx_len, QK_DIM // 2), F32)], axis=0)
    reps = GW // QK_DIM
    return (jnp.tile(jnp.concatenate([cos, cos], axis=-1), (1, reps)),
            jnp.tile(jnp.concatenate([-sin, sin], axis=-1), (1, reps)))


def _regroup_w_in(w):
    d = w.shape[0]
    z = lambda n: jnp.zeros((d, n), w.dtype)
    return jnp.concatenate([w[:, :2048], w[:, 2048:2064], z(LANES - 16), w[:, 2320:2832], w[:, 2832:2864],
                            z(LANES - 32), w[:, 2064:2320], w[:, 2864:3120]], axis=1).astype(BF16)


def _gla_w2_blockdiag(w2):
    out = jnp.zeros((LANES, GW), F32)
    out = out.at[0:GLA_RANK, 0:GLA_QK].set(w2[0])
    return out.at[GLA_RANK:2 * GLA_RANK, GLA_QK:].set(w2[1])


def _layer(xs, mod3, lp, cos_t, sin_t, layer_idx, last, n_batch):
    b, t, d = xs.shape
    nt = t // TM
    pa, pb, pc, pcs, pd, pdl, pg = _inproj(xs, mod3, lp["norm1_g"], _regroup_w_in(lp["w_in"]), n_batch)

    lam_init = 0.8 - 0.6 * math.exp(-0.3 * layer_idx)
    (q, kt, v, qn, kn), (ya,), dn_parts, (qi, a, ke, vt, vb, de) = _fused_call(
        [_attn_prep(pb, lp["da_qnorm_g"], lp["da_knorm_g"], cos_t, sin_t),
         _conv_module(pa, lp["cm_conv_w"], lp["cm_conv_b"], lp["cm_ln_g"], lp["cm_ln_b"]),
         _dn_local(pc, pcs, lp["dn_conv_w"], lp["dn_a_log"], lp["dn_dt_bias"]),
         _gla_local(pd, pdl, _gla_w2_blockdiag(lp["gla_w2"]), lp["gla_b2"].reshape(1, GW))],
        (b, nt))
    lat_rows = t - TM
    rows = lat_rows if last else t
    yb = _attention(q, kt, v, qn, kn, lp["da_lambda"], lp["da_subln_g"], lam_init, rows)
    ocf, ocr, odf, odr = _scans(*dn_parts, qi, a, ke, de, vt, vb)

    x1 = _outproj(xs, ya, yb, ocf, ocr, odf, odr, pg, mod3, lp["dn_onorm_g"], lp["gla_onorm_g"],
                  lp["w_out"].astype(BF16), n_batch, rows, N_BLK_FFN)
    return _ffn(x1, mod3, lp["norm2_g"], lp["ffn_w_up"].astype(BF16), lp["ffn_conv_w"],
                lp["ffn_w_down"].astype(BF16), n_batch, lat_rows, rows, N_BLK_FFN)


def kernel(x, c, ctx, c_ctx, w_mod, b_mod, norm1_g, norm2_g, w_in, w_out, cm_conv_w, cm_conv_b, cm_ln_g, cm_ln_b, da_qnorm_g, da_knorm_g, da_lambda, da_subln_g, dn_conv_w, dn_a_log, dn_dt_bias, dn_onorm_g, gla_w2, gla_b2, gla_onorm_g, ffn_w_up, ffn_conv_w, ffn_w_down):
    n_batch, seq, d = x.shape
    ctx_len = ctx.shape[1]
    assert ctx_len == TM and seq % TM == 0 and d == D_MODEL
    assert (seq + ctx_len) % (8 * N_BLK) == 0 and (seq + ctx_len) % (8 * N_BLK_FFN) == 0 and seq % (8 * N_BLK_FFN) == 0
    depth = w_mod.shape[0]
    cos_t, sin_t = _rope_tables(seq, ctx_len)
    xs = jnp.concatenate([x, ctx], axis=1)
    mod_rows = 16
    c_rows = jnp.zeros((mod_rows, d), F32).at[:n_batch].set(c).at[n_batch].set(c_ctx)
    mod_all = _modulation(c_rows, w_mod, b_mod).reshape(depth, mod_rows * 6, 1, d)
    params = dict(norm1_g=norm1_g, norm2_g=norm2_g, w_in=w_in, w_out=w_out,
                  cm_conv_w=cm_conv_w, cm_conv_b=cm_conv_b, cm_ln_g=cm_ln_g, cm_ln_b=cm_ln_b,
                  da_qnorm_g=da_qnorm_g, da_knorm_g=da_knorm_g, da_lambda=da_lambda, da_subln_g=da_subln_g,
                  dn_conv_w=dn_conv_w, dn_a_log=dn_a_log, dn_dt_bias=dn_dt_bias, dn_onorm_g=dn_onorm_g,
                  gla_w2=gla_w2, gla_b2=gla_b2, gla_onorm_g=gla_onorm_g,
                  ffn_w_up=ffn_w_up, ffn_conv_w=ffn_conv_w, ffn_w_down=ffn_w_down)
    for l in range(depth):
        lp = {k: v[l] for k, v in params.items()}
        xs = _layer(xs, mod_all[l], lp, cos_t, sin_t, l, l == depth - 1, n_batch)
    return xs
```
